```python
import math
import jax
import jax.numpy as jnp
from jax import lax
import numpy as np

D_MODEL = 1024
BATCH = 8
SEQ = 4096
DEPTH = 4

N_EVEN = (DEPTH + 1) // 2
N_ODD = DEPTH // 2
Q_BLOCK = 128
RMS_EPS = 1e-6
NEG_INF = -1e30
FOX_HEADS = 8
FOX_HEAD_DIM = 64
FOX_WIDTH = FOX_HEADS * FOX_HEAD_DIM
FORGET_BIAS_MEAN = 3.0
MLA_HEADS = 8
MLA_NOPE_DIM = 64
MLA_ROPE_DIM = 32
MLA_V_DIM = 64
MLA_Q_RANK = 384
MLA_KV_RANK = 256
ROPE_THETA = 10000.0
EVEN_IN_WIDTH = 3 * FOX_WIDTH + FOX_HEADS + MLA_Q_RANK + MLA_KV_RANK + MLA_ROPE_DIM
EVEN_MIX_WIDTH = FOX_WIDTH + MLA_HEADS * MLA_V_DIM
DIFF_HEADS = 8
DIFF_HEAD_DIM = 64
DIFF_V_DIM = 2 * DIFF_HEAD_DIM
ODD_IN_WIDTH = 4 * DIFF_HEADS * DIFF_HEAD_DIM + DIFF_HEADS * DIFF_V_DIM
ODD_MIX_WIDTH = DIFF_HEADS * DIFF_V_DIM
D_FF = 2816
CONV_WIDTH = 3

kernel_name = 'hybrid_fox_mla_diffattn_convffn_trunk'


def _rms_norm(x, g):
    xf = x.astype(jnp.float32)
    y = xf * lax.rsqrt(jnp.mean(xf * xf, axis=-1, keepdims=True) + RMS_EPS)
    return (y * g.astype(jnp.float32)).astype(x.dtype)


def _rope_tables(seq, dim, dtype):
    inv = ROPE_THETA ** (-jnp.arange(0, dim, 2, dtype=jnp.float32) / dim)
    ang = jnp.arange(seq, dtype=jnp.float32)[:, None] * inv[None, :]
    return jnp.cos(ang).astype(dtype), jnp.sin(ang).astype(dtype)


def _apply_rope(x, cos, sin):
    x1, x2 = jnp.split(x, 2, axis=-1)
    return jnp.concatenate([x1 * cos - x2 * sin, x1 * sin + x2 * cos], axis=-1)


def _alibi_slopes(n):
    return 2.0 ** (-8.0 * jnp.arange(1, n + 1, dtype=jnp.float32) / n)


def _causal_mask(s0, seq):
    qpos = s0 + jnp.arange(Q_BLOCK)
    kpos = jnp.arange(seq)
    return qpos[:, None] >= kpos[None, :], (qpos[:, None] - kpos[None, :]).astype(jnp.float32)


def _merge_blocks(o):
    o = jnp.moveaxis(o, 0, 1)
    return o.reshape(o.shape[0], -1, o.shape[3] * o.shape[4])


def _fox_attention(q, k, v, log_f):
    seq, dh = q.shape[1], q.shape[3]
    c = jnp.cumsum(log_f, axis=1).transpose(0, 2, 1)
    scale = dh ** -0.5

    def block(i):
        s0 = i * Q_BLOCK
        qb = lax.dynamic_slice_in_dim(q, s0, Q_BLOCK, axis=1)
        cq = lax.dynamic_slice_in_dim(c, s0, Q_BLOCK, axis=2)
        mask, _ = _causal_mask(s0, seq)
        logits = jnp.einsum('bqhd,bkhd->bhqk', qb, k).astype(jnp.float32) * scale
        logits = logits + cq[..., :, None] - c[..., None, :]
        logits = jnp.where(mask, logits, NEG_INF)
        p = jax.nn.softmax(logits, axis=-1).astype(v.dtype)
        return jnp.einsum('bhqk,bkhd->bqhd', p, v)

    return _merge_blocks(lax.map(block, jnp.arange(seq // Q_BLOCK)))


def _mla_attention(q_nope, q_rope, k_nope, k_rope, v):
    seq = q_nope.shape[1]
    scale = (MLA_NOPE_DIM + MLA_ROPE_DIM) ** -0.5

    def block(i):
        s0 = i * Q_BLOCK
        qn = lax.dynamic_slice_in_dim(q_nope, s0, Q_BLOCK, axis=1)
        qr = lax.dynamic_slice_in_dim(q_rope, s0, Q_BLOCK, axis=1)
        mask, _ = _causal_mask(s0, seq)
        logits = (jnp.einsum('bqhd,bkhd->bhqk', qn, k_nope)
                  + jnp.einsum('bqhr,bkr->bhqk', qr, k_rope)).astype(jnp.float32) * scale
        logits = jnp.where(mask, logits, NEG_INF)
        p = jax.nn.softmax(logits, axis=-1).astype(v.dtype)
        return jnp.einsum('bhqk,bkhd->bqhd', p, v)

    return _merge_blocks(lax.map(block, jnp.arange(seq // Q_BLOCK)))


def _diff_attention(q1, q2, k1, k2, v, lam, slopes):
    seq, dh = q1.shape[1], q1.shape[3]
    scale = dh ** -0.5

    def block(i):
        s0 = i * Q_BLOCK
        q1b = lax.dynamic_slice_in_dim(q1, s0, Q_BLOCK, axis=1)
        q2b = lax.dynamic_slice_in_dim(q2, s0, Q_BLOCK, axis=1)
        mask, dist = _causal_mask(s0, seq)
        alibi = -slopes[:, None, None] * dist[None]
        l1 = jnp.einsum('bqhd,bkhd->bhqk', q1b, k1).astype(jnp.float32) * scale + alibi
        l2 = jnp.einsum('bqhd,bkhd->bhqk', q2b, k2).astype(jnp.float32) * scale + alibi
        p = (jax.nn.softmax(jnp.where(mask, l1, NEG_INF), axis=-1)
             - lam * jax.nn.softmax(jnp.where(mask, l2, NEG_INF), axis=-1)).astype(v.dtype)
        return jnp.einsum('bhqk,bkhd->bqhd', p, v)

    return _merge_blocks(lax.map(block, jnp.arange(seq // Q_BLOCK)))


def _even_mixer(h, w_in, b_forget, q_norm, w_uq, kv_norm, w_ukv, w_out, cos, sin):
    bsz, seq, _ = h.shape
    proj = h @ w_in
    cuts = np.cumsum([FOX_WIDTH, FOX_WIDTH, FOX_WIDTH, FOX_HEADS, MLA_Q_RANK, MLA_KV_RANK]).tolist()
    fq, fk, fv, fg, c_q, c_kv, k_r = jnp.split(proj, cuts, axis=-1)
    shp = (bsz, seq, FOX_HEADS, FOX_HEAD_DIM)
    log_f = jax.nn.log_sigmoid((fg + b_forget).astype(jnp.float32))
    fox_out = _fox_attention(fq.reshape(shp), fk.reshape(shp), fv.reshape(shp), log_f)
    q = (_rms_norm(c_q, q_norm) @ w_uq).reshape(bsz, seq, MLA_HEADS, MLA_NOPE_DIM + MLA_ROPE_DIM)
    q_nope, q_rope = jnp.split(q, [MLA_NOPE_DIM], axis=-1)
    q_rope = _apply_rope(q_rope, cos[:, None, :], sin[:, None, :])
    kv = (_rms_norm(c_kv, kv_norm) @ w_ukv).reshape(bsz, seq, MLA_HEADS, MLA_NOPE_DIM + MLA_V_DIM)
    k_nope, v = jnp.split(kv, [MLA_NOPE_DIM], axis=-1)
    k_rope = _apply_rope(k_r, cos, sin)
    mla_out = _mla_attention(q_nope, q_rope, k_nope, k_rope, v)
    return jnp.concatenate([fox_out, mla_out], axis=-1) @ w_out


def _odd_mixer(h, w_in, lq1, lk1, lq2, lk2, subln, w_out, slopes, lambda_init):
    bsz, seq, _ = h.shape
    qk_w = DIFF_HEADS * 2 * DIFF_HEAD_DIM
    q, k, v = jnp.split(h @ w_in, [qk_w, 2 * qk_w], axis=-1)
    q = q.reshape(bsz, seq, DIFF_HEADS, 2, DIFF_HEAD_DIM)
    k = k.reshape(bsz, seq, DIFF_HEADS, 2, DIFF_HEAD_DIM)
    v = v.reshape(bsz, seq, DIFF_HEADS, DIFF_V_DIM)
    f32 = jnp.float32
    lam = (jnp.exp(jnp.sum(lq1.astype(f32) * lk1.astype(f32)))
           - jnp.exp(jnp.sum(lq2.astype(f32) * lk2.astype(f32))) + lambda_init)
    o = _diff_attention(q[..., 0, :], q[..., 1, :], k[..., 0, :], k[..., 1, :], v, lam, slopes)
    o = o.reshape(bsz, seq, DIFF_HEADS, DIFF_V_DIM)
    o = _rms_norm(o, subln) * (1.0 - lambda_init)
    return o.reshape(bsz, seq, ODD_MIX_WIDTH) @ w_out


def _causal_dwconv(a, w, b):
    seq = a.shape[1]
    ap = jnp.pad(a, ((0, 0), (CONV_WIDTH - 1, 0), (0, 0)))
    out = b
    for tap in range(CONV_WIDTH):
        out = out + ap[:, tap:tap + seq] * w[tap]
    return out


def _conv_ffn(h, w_up, conv_w, conv_b, w_down):
    gate, val = jnp.split(h @ w_up, [D_FF], axis=-1)
    act = jax.nn.gelu(_causal_dwconv(gate, conv_w, conv_b), approximate=True)
    return (act * val) @ w_down


def setup_inputs(seed: int = 0) -> dict:
    key = jax.random.key(seed)
    ks = jax.random.split(key, 23)
    f32 = jnp.float32

    def nrm(k, shape, scale):
        return jax.random.normal(k, shape, f32) * scale

    def gain(k, shape):
        return 1.0 + nrm(k, shape, 0.1)

    return {
        'x': nrm(ks[0], (BATCH, SEQ, D_MODEL), 1.0),
        'norm_mix_pre': gain(ks[1], (DEPTH, D_MODEL)),
        'norm_mix_post': gain(ks[2], (DEPTH, D_MODEL)),
        'norm_ffn_pre': gain(ks[3], (DEPTH, D_MODEL)),
        'norm_ffn_post': gain(ks[4], (DEPTH, D_MODEL)),
        'even_w_in': nrm(ks[5], (N_EVEN, D_MODEL, EVEN_IN_WIDTH), D_MODEL ** -0.5),
        'even_b_forget': FORGET_BIAS_MEAN + nrm(ks[6], (N_EVEN, FOX_HEADS), 0.5),
        'even_q_norm': gain(ks[7], (N_EVEN, MLA_Q_RANK)),
        'even_w_uq': nrm(ks[8], (N_EVEN, MLA_Q_RANK, MLA_HEADS * (MLA_NOPE_DIM + MLA_ROPE_DIM)), MLA_Q_RANK ** -0.5),
        'even_kv_norm': gain(ks[9], (N_EVEN, MLA_KV_RANK)),
        'even_w_ukv': nrm(ks[10], (N_EVEN, MLA_KV_RANK, MLA_HEADS * (MLA_NOPE_DIM + MLA_V_DIM)), MLA_KV_RANK ** -0.5),
        'even_w_out': nrm(ks[11], (N_EVEN, EVEN_MIX_WIDTH, D_MODEL), EVEN_MIX_WIDTH ** -0.5),
        'odd_w_in': nrm(ks[12], (N_ODD, D_MODEL, ODD_IN_WIDTH), D_MODEL ** -0.5),
        'odd_lambda_q1': nrm(ks[13], (N_ODD, DIFF_HEAD_DIM), 0.1),
        'odd_lambda_k1': nrm(ks[14], (N_ODD, DIFF_HEAD_DIM), 0.1),
        'odd_lambda_q2': nrm(ks[15], (N_ODD, DIFF_HEAD_DIM), 0.1),
        'odd_lambda_k2': nrm(ks[16], (N_ODD, DIFF_HEAD_DIM), 0.1),
        'odd_subln': gain(ks[17], (N_ODD, DIFF_V_DIM)),
        'odd_w_out': nrm(ks[18], (N_ODD, ODD_MIX_WIDTH, D_MODEL), ODD_MIX_WIDTH ** -0.5),
        'ffn_w_up': nrm(ks[19], (DEPTH, D_MODEL, 2 * D_FF), D_MODEL ** -0.5),
        'ffn_conv_w': nrm(ks[20], (DEPTH, CONV_WIDTH, D_FF), CONV_WIDTH ** -0.5),
        'ffn_conv_b': nrm(ks[21], (DEPTH, D_FF), 0.02),
        'ffn_w_down': nrm(ks[22], (DEPTH, D_FF, D_MODEL), D_FF ** -0.5),
    }


def reference(x, norm_mix_pre, norm_mix_post, norm_ffn_pre, norm_ffn_post,
              even_w_in, even_b_forget, even_q_norm, even_w_uq, even_kv_norm, even_w_ukv, even_w_out,
              odd_w_in, odd_lambda_q1, odd_lambda_k1, odd_lambda_q2, odd_lambda_k2, odd_subln, odd_w_out,
              ffn_w_up, ffn_conv_w, ffn_conv_b, ffn_w_down):
    cos, sin = _rope_tables(x.shape[1], MLA_ROPE_DIM, x.dtype)
    slopes = _alibi_slopes(DIFF_HEADS)
    for layer in range(DEPTH):
        i = layer // 2
        h = _rms_norm(x, norm_mix_pre[layer])
        if layer % 2 == 0:
            m = _even_mixer(h, even_w_in[i], even_b_forget[i], even_q_norm[i], even_w_uq[i],
                            even_kv_norm[i], even_w_ukv[i], even_w_out[i], cos, sin)
        else:
            lambda_init = 0.8 - 0.6 * math.exp(-0.3 * layer)
            m = _odd_mixer(h, odd_w_in[i], odd_lambda_q1[i], odd_lambda_k1[i], odd_lambda_q2[i],
                           odd_lambda_k2[i], odd_subln[i], odd_w_out[i], slopes, lambda_init)
        x = x + _rms_norm(m, norm_mix_post[layer])
        h = _rms_norm(x, norm_ffn_pre[layer])
        f = _conv_ffn(h, ffn_w_up[layer], ffn_conv_w[layer], ffn_conv_b[layer], ffn_w_down[layer])
        x = x + _rms_norm(f, norm_ffn_post[layer])
    return x
```

```python
import functools
import math

import numpy as np
import jax
import jax.numpy as jnp
from jax import lax
from jax.experimental import pallas as pl
from jax.experimental.pallas import tpu as pltpu

D_MODEL = 1024
DEPTH = 4
RMS_EPS = 1e-6
NEG_INF = -1e30
HEADS = 8
FOX_HEAD_DIM = 64
FOX_WIDTH = HEADS * FOX_HEAD_DIM
MLA_NOPE_DIM = 64
MLA_ROPE_DIM = 32
MLA_V_DIM = 64
MLA_Q_RANK = 384
MLA_KV_RANK = 256
ROPE_THETA = 10000.0
DIFF_HEAD_DIM = 64
DIFF_V_DIM = 2 * DIFF_HEAD_DIM
D_FF = 2816
CONV_WIDTH = 3

LANES = 128
SUBLANES = 8
LOG2E = 1.4426950408889634
HEAD_COLS = HEADS * LANES
VMEM_LIMIT_BYTES = 56 * 1024 * 1024

PROJ_ROWS = 256
OUT_ROWS = 512
FFN_ROWS = 512
FFN_CHUNK = 256
ATTN_TILE = 512

F32 = jnp.float32
BF16 = jnp.bfloat16


def _params(*sem):
    return pltpu.CompilerParams(dimension_semantics=sem, vmem_limit_bytes=VMEM_LIMIT_BYTES)


def _const_spec(shape):
    nd = len(shape)
    return pl.BlockSpec(shape, lambda *_: (0,) * nd)


def _rms(x, g):
    return x * lax.rsqrt(jnp.mean(x * x, axis=-1, keepdims=True) + RMS_EPS) * g


def _split3(x):
    hi = x.astype(BF16)
    r1 = x - hi.astype(F32)
    mid = r1.astype(BF16)
    lo = (r1 - mid.astype(F32)).astype(BF16)
    return hi, mid, lo


def _rope(blk, ra, rb, rc):
    return blk * ra + pltpu.roll(blk, 16, 1) * rb + pltpu.roll(blk, LANES - 16, 1) * rc


_EV_Q, _EV_K, _EV_V = 0, HEAD_COLS, 2 * HEAD_COLS
_EV_G = 3 * HEAD_COLS
_EV_CQ = _EV_G + LANES
_EV_CKV = _EV_CQ + MLA_Q_RANK
_EV_KR = _EV_CKV + MLA_KV_RANK
_EV_WIDTH = _EV_KR + LANES


def _even_proj_kernel(x_ref, g_ref, w_ref, bf_ref, qn_ref, wuq_ref, kvn_ref, wuk_ref, wuv_ref,
                      ec_ref, ra_ref, rb_ref, rc_ref, cst_ref,
                      fq_ref, fk_ref, fv_ref, mq_ref, mk_ref, mv_ref, carry_ref,
                      *, tiles_per_seq):
    tm = x_ref.shape[0]
    i = pl.program_id(0)

    @pl.when(i % tiles_per_seq == 0)
    def _():
        carry_ref[...] = jnp.zeros_like(carry_ref)

    h = _rms(x_ref[...], g_ref[...]).astype(BF16)
    q_aug = cst_ref[0:1, :]
    v_one = cst_ref[1:2, :]

    def proj(lo, hi):
        return jnp.dot(h, w_ref[:, lo:hi], preferred_element_type=F32)

    fox_scale = FOX_HEAD_DIM ** -0.5 * LOG2E
    fq_ref[...] = (proj(_EV_Q, _EV_K) * fox_scale + q_aug).astype(BF16)
    fv_ref[...] = (proj(_EV_V, _EV_G) + v_one).astype(BF16)

    z = proj(_EV_G, _EV_CQ) + bf_ref[...]
    logf = (jnp.minimum(z, 0.0) - jnp.log1p(jnp.exp(-jnp.abs(z)))) * LOG2E
    row = lax.broadcasted_iota(jnp.int32, (tm, tm), 0)
    col = lax.broadcasted_iota(jnp.int32, (tm, tm), 1)
    tril = jnp.where(row >= col, 1.0, 0.0).astype(BF16)
    parts = jnp.dot(tril, jnp.concatenate(_split3(logf), axis=1), preferred_element_type=F32)
    c = (parts[:, :LANES] + parts[:, LANES:2 * LANES] + parts[:, 2 * LANES:]
         + carry_ref[SUBLANES - 1:SUBLANES, :])
    carry_ref[...] = c[tm - SUBLANES:, :]
    c_aug = jnp.dot(jnp.concatenate(_split3(c), axis=1), ec_ref[...], preferred_element_type=F32)
    fk_ref[...] = (proj(_EV_K, _EV_V) + c_aug).astype(BF16)

    ra, rb, rc = ra_ref[...], rb_ref[...], rc_ref[...]

    cq = _rms(proj(_EV_CQ, _EV_CKV), qn_ref[...]).astype(BF16)
    mla_scale = (MLA_NOPE_DIM + MLA_ROPE_DIM) ** -0.5 * LOG2E
    q = jnp.dot(cq, wuq_ref[...], preferred_element_type=F32) * mla_scale
    for hd in range(HEADS):
        sl = slice(hd * LANES, (hd + 1) * LANES)
        mq_ref[:, sl] = _rope(q[:, sl], ra, rb, rc).astype(BF16)

    ckv = _rms(proj(_EV_CKV, _EV_KR), kvn_ref[...]).astype(BF16)
    kr = _rope(proj(_EV_KR, _EV_WIDTH), ra, rb, rc)
    kn = jnp.dot(ckv, wuk_ref[...], preferred_element_type=F32)
    for hd in range(HEADS):
        sl = slice(hd * LANES, (hd + 1) * LANES)
        mk_ref[:, sl] = (kn[:, sl] + kr).astype(BF16)
    mv_ref[...] = (jnp.dot(ckv, wuv_ref[...], preferred_element_type=F32) + v_one).astype(BF16)


def _pad_heads(w, dh):
    k = w.shape[0]
    w = w.reshape(k, HEADS, dh)
    return jnp.pad(w, ((0, 0), (0, 0), (0, LANES - dh))).reshape(k, HEAD_COLS)


def _even_consts():
    cst = np.zeros((SUBLANES, HEAD_COLS), np.float32)
    ec = np.zeros((3 * LANES, HEAD_COLS), np.float32)
    for hd in range(HEADS):
        cst[0, hd * LANES + 64: hd * LANES + 67] = 1.0
        cst[1, hd * LANES + 64] = 1.0
        for p in range(3):
            ec[p * LANES + hd, hd * LANES + 64 + p] = -1.0
    return jnp.asarray(cst), jnp.asarray(ec, dtype=BF16)


def _rope_tables(seq):
    half = MLA_ROPE_DIM // 2
    inv = ROPE_THETA ** (-jnp.arange(0, MLA_ROPE_DIM, 2, dtype=F32) / MLA_ROPE_DIM)
    ang = jnp.arange(seq, dtype=F32)[:, None] * inv[None, :]
    cos, sin = jnp.cos(ang), jnp.sin(ang)
    zeros = jnp.zeros((seq, half), F32)
    ra = jnp.concatenate([jnp.ones((seq, 64), F32), cos, cos, jnp.zeros((seq, 32), F32)], axis=1)
    rb = jnp.concatenate([jnp.zeros((seq, 64), F32), zeros, sin, jnp.zeros((seq, 32), F32)], axis=1)
    rc = jnp.concatenate([jnp.zeros((seq, 64), F32), -sin, zeros, jnp.zeros((seq, 32), F32)], axis=1)
    return ra, rb, rc


def _even_proj(x, g, w_in, b_forget, q_norm, w_uq, kv_norm, w_ukv, rope, seq):
    t = x.shape[0]
    tm = PROJ_ROWS
    cuts = np.cumsum([FOX_WIDTH, FOX_WIDTH, FOX_WIDTH, HEADS, MLA_Q_RANK, MLA_KV_RANK]).tolist()
    wfq, wfk, wfv, wg, wcq, wckv, wkr = jnp.split(w_in, cuts, axis=1)
    w1 = jnp.concatenate([
        _pad_heads(wfq, FOX_HEAD_DIM), _pad_heads(wfk, FOX_HEAD_DIM), _pad_heads(wfv, FOX_HEAD_DIM),
        jnp.pad(wg, ((0, 0), (0, LANES - HEADS))), wcq, wckv,
        jnp.pad(wkr, ((0, 0), (64, LANES - 64 - MLA_ROPE_DIM)))], axis=1).astype(BF16)
    assert w1.shape[1] == _EV_WIDTH
    bf = jnp.pad(b_forget, (0, LANES - HEADS)).reshape(1, LANES)
    wuq = _pad_heads(w_uq, MLA_NOPE_DIM + MLA_ROPE_DIM).astype(BF16)
    wukv = w_ukv.reshape(MLA_KV_RANK, HEADS, MLA_NOPE_DIM + MLA_V_DIM)
    wuk = _pad_heads(wukv[:, :, :MLA_NOPE_DIM].reshape(MLA_KV_RANK, -1), MLA_NOPE_DIM).astype(BF16)
    wuv = _pad_heads(wukv[:, :, MLA_NOPE_DIM:].reshape(MLA_KV_RANK, -1), MLA_V_DIM).astype(BF16)
    cst, ec = _even_consts()
    ra, rb, rc = rope
    tiles_per_seq = seq // tm
    rope_spec = pl.BlockSpec((tm, LANES), lambda i: (i % tiles_per_seq, 0))
    row_spec = pl.BlockSpec((tm, HEAD_COLS), lambda i: (i, 0))
    out_sds = jax.ShapeDtypeStruct((t, HEAD_COLS), BF16)
    return pl.pallas_call(
        functools.partial(_even_proj_kernel, tiles_per_seq=tiles_per_seq),
        grid=(t // tm,),
        in_specs=[pl.BlockSpec((tm, D_MODEL), lambda i: (i, 0)),
                  _const_spec((1, D_MODEL)), _const_spec(w1.shape), _const_spec((1, LANES)),
                  _const_spec((1, MLA_Q_RANK)), _const_spec(wuq.shape),
                  _const_spec((1, MLA_KV_RANK)), _const_spec(wuk.shape), _const_spec(wuv.shape),
                  _const_spec(ec.shape), rope_spec, rope_spec, rope_spec, _const_spec(cst.shape)],
        out_specs=[row_spec] * 6,
        out_shape=[out_sds] * 6,
        scratch_shapes=[pltpu.VMEM((SUBLANES, LANES), F32)],
        compiler_params=_params("arbitrary"),
        name="even_proj",
    )(x, g.reshape(1, -1), w1, bf, q_norm.reshape(1, -1), wuq, kv_norm.reshape(1, -1), wuk, wuv,
      ec, ra, rb, rc, cst)


def _alibi_slope(hd):
    return 2.0 ** (-8.0 * (hd + 1) / HEADS)


def _odd_proj_kernel(x_ref, g_ref, w_ref, q_ref, k_ref, v_ref, *, tiles_per_seq):
    tm = x_ref.shape[0]
    i = pl.program_id(0)
    h = _rms(x_ref[...], g_ref[...]).astype(BF16)
    scale = DIFF_HEAD_DIM ** -0.5 * LOG2E
    q_ref[...] = (jnp.dot(h, w_ref[:, :HEAD_COLS], preferred_element_type=F32) * scale).astype(BF16)
    k = jnp.dot(h, w_ref[:, HEAD_COLS:2 * HEAD_COLS], preferred_element_type=F32)
    v = jnp.dot(h, w_ref[:, 2 * HEAD_COLS:], preferred_element_type=F32)
    pos = ((i % tiles_per_seq) * tm + lax.broadcasted_iota(jnp.int32, (tm, 1), 0)).astype(F32)
    lane = lax.broadcasted_iota(jnp.int32, (tm, LANES), 1)
    one_col = jnp.where(lane == 0, 1.0, 0.0).astype(BF16)
    for hd in range(HEADS):
        sl = slice(hd * LANES, (hd + 1) * LANES)
        b_hi, b_mid, b_lo = (p.astype(F32) for p in _split3(pos * (_alibi_slope(hd) * LOG2E)))
        aug = jnp.where(lane == 0, b_hi, jnp.where(lane == 1, b_mid, jnp.where(lane == 2, b_lo, 0.0)))
        k_ref[:, 2 * hd * LANES:(2 * hd + 1) * LANES] = k[:, sl].astype(BF16)
        k_ref[:, (2 * hd + 1) * LANES:(2 * hd + 2) * LANES] = aug.astype(BF16)
        v_ref[:, 2 * hd * LANES:(2 * hd + 1) * LANES] = v[:, sl].astype(BF16)
        v_ref[:, (2 * hd + 1) * LANES:(2 * hd + 2) * LANES] = one_col


def _odd_proj(x, g, w_in, seq):
    t = x.shape[0]
    tm = PROJ_ROWS
    w = w_in.astype(BF16)
    return pl.pallas_call(
        functools.partial(_odd_proj_kernel, tiles_per_seq=seq // tm),
        grid=(t // tm,),
        in_specs=[pl.BlockSpec((tm, D_MODEL), lambda i: (i, 0)),
                  _const_spec((1, D_MODEL)), _const_spec(w.shape)],
        out_specs=[pl.BlockSpec((tm, HEAD_COLS), lambda i: (i, 0)),
                   pl.BlockSpec((tm, 2 * HEAD_COLS), lambda i: (i, 0)),
                   pl.BlockSpec((tm, 2 * HEAD_COLS), lambda i: (i, 0))],
        out_shape=[jax.ShapeDtypeStruct((t, HEAD_COLS), BF16),
                   jax.ShapeDtypeStruct((t, 2 * HEAD_COLS), BF16),
                   jax.ShapeDtypeStruct((t, 2 * HEAD_COLS), BF16)],
        compiler_params=_params("arbitrary"),
        name="odd_proj",
    )(x, g.reshape(1, -1), w)


def _softmax_step(s, m, acc, vc):
    m_new = jnp.maximum(m, jnp.max(s, axis=1, keepdims=True))
    p = jnp.exp2(s - m_new).astype(BF16)
    acc = jnp.exp2(m - m_new) * acc + jnp.dot(p, vc, preferred_element_type=F32)
    return m_new, acc


def _causal(s):
    row = lax.broadcasted_iota(jnp.int32, s.shape, 0)
    col = lax.broadcasted_iota(jnp.int32, s.shape, 1)
    return jnp.where(row >= col, s, NEG_INF)


_NT = (((1,), (1,)), ((), ()))


def _flash_kernel(q_ref, k_ref, v_ref, o_ref):
    tq = q_ref.shape[1]
    qi = pl.program_id(2)
    q = q_ref[0]

    def chunk(j, carry, masked):
        start = pl.multiple_of(j * tq, tq)
        s = lax.dot_general(q, k_ref[0, pl.ds(start, tq), :], _NT, preferred_element_type=F32)
        if masked:
            s = _causal(s)
        return _softmax_step(s, *carry, v_ref[0, pl.ds(start, tq), :])

    init = (jnp.full((tq, 1), NEG_INF, F32), jnp.zeros((tq, LANES), F32))
    carry = lax.fori_loop(0, qi, lambda j, c: chunk(j, c, False), init)
    _, acc = chunk(qi, carry, True)
    o_ref[0] = (acc / acc[:, 64:65]).astype(BF16)


def _flash_attention(q, k, v, tq):
    b, s, _ = q.shape
    return pl.pallas_call(
        _flash_kernel,
        grid=(b, HEADS, s // tq),
        in_specs=[pl.BlockSpec((1, tq, LANES), lambda bi, hd, i: (bi, i, hd)),
                  pl.BlockSpec((1, s, LANES), lambda bi, hd, i: (bi, 0, hd)),
                  pl.BlockSpec((1, s, LANES), lambda bi, hd, i: (bi, 0, hd))],
        out_specs=pl.BlockSpec((1, tq, LANES), lambda bi, hd, i: (bi, i, hd)),
        out_shape=jax.ShapeDtypeStruct(q.shape, BF16),
        compiler_params=_params("arbitrary", "arbitrary", "arbitrary"),
        name="flash_attention",
    )(q, k, v)


def _diff_kernel(q_ref, k_ref, v_ref, lq1_ref, lk1_ref, lq2_ref, lk2_ref, sub_ref, o_ref,
                 *, lambda_init):
    tq = q_ref.shape[1]
    qi = pl.program_id(2)
    q = q_ref[0]
    lane = lax.broadcasted_iota(jnp.int32, (tq, LANES), 1)
    zero = jnp.zeros_like(q)
    aug = jnp.where(lane < 3, 1.0, 0.0).astype(BF16)
    q1 = jnp.concatenate([jnp.where(lane < DIFF_HEAD_DIM, q, zero), aug], axis=1)
    q2 = jnp.concatenate([jnp.where(lane >= DIFF_HEAD_DIM, q, zero), aug], axis=1)

    def chunk(j, carry, masked):
        m1, a1, m2, a2 = carry
        start = pl.multiple_of(j * tq, tq)
        kc = k_ref[0, pl.ds(start, tq), :]
        vc = v_ref[0, pl.ds(start, tq), :]
        s1 = lax.dot_general(q1, kc, _NT, preferred_element_type=F32)
        s2 = lax.dot_general(q2, kc, _NT, preferred_element_type=F32)
        if masked:
            s1, s2 = _causal(s1), _causal(s2)
        m1, a1 = _softmax_step(s1, m1, a1, vc)
        m2, a2 = _softmax_step(s2, m2, a2, vc)
        return m1, a1, m2, a2

    m0 = jnp.full((tq, 1), NEG_INF, F32)
    a0 = jnp.zeros((tq, 2 * LANES), F32)
    carry = lax.fori_loop(0, qi, lambda j, c: chunk(j, c, False), (m0, a0, m0, a0))
    _, a1, _, a2 = chunk(qi, carry, True)
    lam = (jnp.exp(jnp.sum(lq1_ref[...] * lk1_ref[...], axis=1, keepdims=True))
           - jnp.exp(jnp.sum(lq2_ref[...] * lk2_ref[...], axis=1, keepdims=True)) + lambda_init)
    o = a1[:, :LANES] / a1[:, LANES:LANES + 1] - lam * (a2[:, :LANES] / a2[:, LANES:LANES + 1])
    o_ref[0] = (_rms(o, sub_ref[...]) * (1.0 - lambda_init)).astype(BF16)


def _diff_attention(q, k, v, lq1, lk1, lq2, lk2, subln, lambda_init, tq):
    b, s, _ = q.shape
    vec = lambda a: a.reshape(1, -1)
    return pl.pallas_call(
        functools.partial(_diff_kernel, lambda_init=lambda_init),
        grid=(b, HEADS, s // tq),
        in_specs=[pl.BlockSpec((1, tq, LANES), lambda bi, hd, i: (bi, i, hd)),
                  pl.BlockSpec((1, s, 2 * LANES), lambda bi, hd, i: (bi, 0, hd)),
                  pl.BlockSpec((1, s, 2 * LANES), lambda bi, hd, i: (bi, 0, hd)),
                  _const_spec((1, DIFF_HEAD_DIM)), _const_spec((1, DIFF_HEAD_DIM)),
                  _const_spec((1, DIFF_HEAD_DIM)), _const_spec((1, DIFF_HEAD_DIM)),
                  _const_spec((1, DIFF_V_DIM))],
        out_specs=pl.BlockSpec((1, tq, LANES), lambda bi, hd, i: (bi, i, hd)),
        out_shape=jax.ShapeDtypeStruct(q.shape, BF16),
        compiler_params=_params("arbitrary", "arbitrary", "arbitrary"),
        name="diff_attention",
    )(q, k, v, vec(lq1), vec(lk1), vec(lq2), vec(lk2), vec(subln))


def _out_proj_kernel(*refs, n_in):
    a_refs, w_refs = refs[:n_in], refs[n_in:2 * n_in]
    g_ref, x_ref, o_ref = refs[2 * n_in:]
    m = jnp.dot(a_refs[0][...], w_refs[0][...], preferred_element_type=F32)
    for a_ref, w_ref in zip(a_refs[1:], w_refs[1:]):
        m = m + jnp.dot(a_ref[...], w_ref[...], preferred_element_type=F32)
    o_ref[...] = x_ref[...] + _rms(m, g_ref[...])


def _out_proj(acts, weights, g, x):
    t = x.shape[0]
    tm = OUT_ROWS
    n_in = len(acts)
    x_spec = pl.BlockSpec((tm, D_MODEL), lambda i: (i, 0))
    return pl.pallas_call(
        functools.partial(_out_proj_kernel, n_in=n_in),
        grid=(t // tm,),
        in_specs=([pl.BlockSpec((tm, a.shape[1]), lambda i: (i, 0)) for a in acts]
                  + [_const_spec(w.shape) for w in weights]
                  + [_const_spec((1, D_MODEL)), x_spec]),
        out_specs=x_spec,
        out_shape=jax.ShapeDtypeStruct(x.shape, F32),
        compiler_params=_params("arbitrary"),
        name="out_proj",
    )(*acts, *weights, g.reshape(1, -1), x)


_N_FFN_CHUNKS = D_FF // FFN_CHUNK
_GELU_C = math.sqrt(2.0 / math.pi)


def _ffn_kernel(x_ref, gpre_ref, wup_ref, cw_ref, wdn_ref, gpost_ref, o_ref,
                h_ref, gate_ref, acc_ref, *, tiles_per_seq):
    tm = x_ref.shape[0]
    i = pl.program_id(0)

    @pl.when(i % tiles_per_seq == 0)
    def _():
        gate_ref[:, 0:SUBLANES, :] = jnp.zeros((_N_FFN_CHUNKS, SUBLANES, FFN_CHUNK), F32)

    h_ref[...] = _rms(x_ref[...], gpre_ref[...]).astype(BF16)
    acc_ref[...] = jnp.zeros_like(acc_ref)

    def body(c, _):
        gv = jnp.dot(h_ref[...], wup_ref[c], preferred_element_type=F32)
        gate, val = gv[:, :FFN_CHUNK], gv[:, FFN_CHUNK:]
        gate_ref[c, SUBLANES:, :] = gate
        cw = cw_ref[c]
        pre = (gate_ref[c, SUBLANES - 2:SUBLANES - 2 + tm, :] * cw[0:1, :]
               + gate_ref[c, SUBLANES - 1:SUBLANES - 1 + tm, :] * cw[1:2, :]
               + gate * cw[2:3, :] + cw[3:4, :])
        act = 0.5 * pre * (1.0 + jnp.tanh(_GELU_C * (pre + 0.044715 * (pre * pre * pre))))
        acc_ref[...] += jnp.dot((act * val).astype(BF16), wdn_ref[c], preferred_element_type=F32)
        gate_ref[c, 0:SUBLANES, :] = gate[tm - SUBLANES:, :]
        return 0

    lax.fori_loop(0, _N_FFN_CHUNKS, body, 0)
    o_ref[...] = x_ref[...] + _rms(acc_ref[...], gpost_ref[...])


def _ffn(x, g_pre, w_up, conv_w, conv_b, w_down, g_post, seq):
    t = x.shape[0]
    tm = FFN_ROWS
    nc = _N_FFN_CHUNKS
    wg = w_up[:, :D_FF].reshape(D_MODEL, nc, FFN_CHUNK)
    wv = w_up[:, D_FF:].reshape(D_MODEL, nc, FFN_CHUNK)
    wup = jnp.concatenate([wg, wv], axis=2).transpose(1, 0, 2).astype(BF16)
    wdn = w_down.reshape(nc, FFN_CHUNK, D_MODEL).astype(BF16)
    cw = jnp.concatenate([conv_w, conv_b[None, :],
                          jnp.zeros((SUBLANES - CONV_WIDTH - 1, D_FF), F32)], axis=0)
    cw = cw.reshape(SUBLANES, nc, FFN_CHUNK).transpose(1, 0, 2)
    x_spec = pl.BlockSpec((tm, D_MODEL), lambda i: (i, 0))
    return pl.pallas_call(
        functools.partial(_ffn_kernel, tiles_per_seq=seq // tm),
        grid=(t // tm,),
        in_specs=[x_spec, _const_spec((1, D_MODEL)), _const_spec(wup.shape), _const_spec(cw.shape),
                  _const_spec(wdn.shape), _const_spec((1, D_MODEL))],
        out_specs=x_spec,
        out_shape=jax.ShapeDtypeStruct(x.shape, F32),
        scratch_shapes=[pltpu.VMEM((tm, D_MODEL), BF16),
                        pltpu.VMEM((nc, tm + SUBLANES, FFN_CHUNK), F32),
                        pltpu.VMEM((tm, D_MODEL), F32)],
        compiler_params=_params("arbitrary"),
        name="conv_ffn",
    )(x, g_pre.reshape(1, -1), wup, cw, wdn, g_post.reshape(1, -1))


def _pad_head_rows(w, dh):
    n = w.shape[1]
    w = w.reshape(HEADS, dh, n)
    return jnp.pad(w, ((0, 0), (0, LANES - dh), (0, 0))).reshape(HEAD_COLS, n)


def kernel(x, norm_mix_pre, norm_mix_post, norm_ffn_pre, norm_ffn_post, even_w_in, even_b_forget, even_q_norm, even_w_uq, even_kv_norm, even_w_ukv, even_w_out, odd_w_in, odd_lambda_q1, odd_lambda_k1, odd_lambda_q2, odd_lambda_k2, odd_subln, odd_w_out, ffn_w_up, ffn_conv_w, ffn_conv_b, ffn_w_down):
    bsz, seq, d = x.shape
    assert d == D_MODEL and seq % ATTN_TILE == 0 and seq % FFN_ROWS == 0
    t = bsz * seq
    tq = ATTN_TILE
    rope = _rope_tables(seq)
    xf = x.reshape(t, d)
    for layer in range(DEPTH):
        i = layer // 2
        if layer % 2 == 0:
            fq, fk, fv, mq, mk, mv = _even_proj(
                xf, norm_mix_pre[layer], even_w_in[i], even_b_forget[i], even_q_norm[i],
                even_w_uq[i], even_kv_norm[i], even_w_ukv[i], rope, seq)
            shp = (bsz, seq, HEAD_COLS)
            fo = _flash_attention(fq.reshape(shp), fk.reshape(shp), fv.reshape(shp), tq)
            mo = _flash_attention(mq.reshape(shp), mk.reshape(shp), mv.reshape(shp), tq)
            w_out = even_w_out[i]
            wo_f = _pad_head_rows(w_out[:FOX_WIDTH], FOX_HEAD_DIM).astype(BF16)
            wo_m = _pad_head_rows(w_out[FOX_WIDTH:], MLA_V_DIM).astype(BF16)
            xf = _out_proj([fo.reshape(t, -1), mo.reshape(t, -1)], [wo_f, wo_m],
                           norm_mix_post[layer], xf)
        else:
            lambda_init = 0.8 - 0.6 * math.exp(-0.3 * layer)
            q, k, v = _odd_proj(xf, norm_mix_pre[layer], odd_w_in[i], seq)
            o = _diff_attention(q.reshape(bsz, seq, -1), k.reshape(bsz, seq, -1),
                                v.reshape(bsz, seq, -1), odd_lambda_q1[i], odd_lambda_k1[i],
                                odd_lambda_q2[i], odd_lambda_k2[i], odd_subln[i], lambda_init, tq)
            xf = _out_proj([o.reshape(t, -1)], [odd_w_out[i].astype(BF16)],
                           norm_mix_post[layer], xf)
        xf = _ffn(xf, norm_ffn_pre[layer], ffn_w_up[layer], ffn_conv_w[layer], ffn_conv_b[layer],
                  ffn_w_down[layer], norm_ffn_post[layer], seq)
    return xf.reshape(bsz, seq, d)
```

```python
import functools
import math

import numpy as np
import jax
import jax.numpy as jnp
from jax import lax
from jax.experimental import pallas as pl
from jax.experimental.pallas import tpu as pltpu

D_MODEL = 1024
DEPTH = 4
RMS_EPS = 1e-6
NEG_INF = -1e30
HEADS = 8
FOX_HEAD_DIM = 64
FOX_WIDTH = HEADS * FOX_HEAD_DIM
MLA_NOPE_DIM = 64
MLA_ROPE_DIM = 32
MLA_V_DIM = 64
MLA_Q_RANK = 384
MLA_KV_RANK = 256
ROPE_THETA = 10000.0
DIFF_HEAD_DIM = 64
DIFF_V_DIM = 2 * DIFF_HEAD_DIM
D_FF = 2816
CONV_WIDTH = 3

LANES = 128
SUBLANES = 8
BF16_ROWS = 16
LOG2E = 1.4426950408889634
HEAD_COLS = HEADS * LANES
VMEM_LIMIT_BYTES = 56 * 1024 * 1024

PROJ_ROWS = 256
OUT_ROWS = 512
FFN_ROWS = 512
FFN_CHUNK = 256
ATTN_TILE = 512
F32 = jnp.float32
BF16 = jnp.bfloat16


def _params(*sem):
    return pltpu.CompilerParams(dimension_semantics=sem, vmem_limit_bytes=VMEM_LIMIT_BYTES)


def _const_spec(shape):
    nd = len(shape)
    return pl.BlockSpec(shape, lambda *_: (0,) * nd)


def _rms(x, g):
    return x * lax.rsqrt(jnp.mean(x * x, axis=-1, keepdims=True) + RMS_EPS) * g


def _split3(x):
    hi = x.astype(BF16)
    r1 = x - hi.astype(F32)
    mid = r1.astype(BF16)
    lo = (r1 - mid.astype(F32)).astype(BF16)
    return hi, mid, lo


def _rope(blk, ra, rb, rc):
    return blk * ra + pltpu.roll(blk, 16, 1) * rb + pltpu.roll(blk, LANES - 16, 1) * rc


def _vt_rows(dv):
    return dv + BF16_ROWS


def _store_vt(vt_ref, v, dv):
    tm = v.shape[0]
    per = LANES // dv
    for blk in range(v.shape[1] // LANES):
        vt = v[:, blk * LANES:(blk + 1) * LANES].T.astype(BF16)
        for s in range(per):
            vt_ref[0, blk * per + s, 0, 0:dv, :] = vt[s * dv:(s + 1) * dv, :]
    row = lax.broadcasted_iota(jnp.int32, (BF16_ROWS, tm), 0)
    ones_row = jnp.where(row == 0, 1.0, 0.0).astype(BF16)
    for hd in range(HEADS):
        vt_ref[0, hd, 0, dv:dv + BF16_ROWS, :] = ones_row


def _vt_spec(dv, tm, tk, tiles_per_seq):
    per_chunk = tk // tm
    return pl.BlockSpec(
        (1, HEADS, 1, _vt_rows(dv), tm),
        lambda i: (i // tiles_per_seq, 0, (i % tiles_per_seq) // per_chunk, 0,
                   (i % tiles_per_seq) % per_chunk))


_EV_Q, _EV_K, _EV_V = 0, HEAD_COLS, 2 * HEAD_COLS
_EV_G = _EV_V + FOX_WIDTH
_EV_CQ = _EV_G + LANES
_EV_CKV = _EV_CQ + MLA_Q_RANK
_EV_KR = _EV_CKV + MLA_KV_RANK
_EV_WIDTH = _EV_KR + LANES


def _even_proj_kernel(x_ref, g_ref, w_ref, bf_ref, qn_ref, wuq_ref, kvn_ref, wuk_ref, wuv_ref,
                      ec_ref, ra_ref, rb_ref, rc_ref, cst_ref,
                      fq_ref, fk_ref, fv_ref, mq_ref, mk_ref, mv_ref, carry_ref,
                      *, tiles_per_seq):
    tm = x_ref.shape[0]
    i = pl.program_id(0)

    @pl.when(i % tiles_per_seq == 0)
    def _():
        carry_ref[...] = jnp.zeros_like(carry_ref)

    h = _rms(x_ref[...], g_ref[...]).astype(BF16)
    q_aug = cst_ref[0:1, :]

    def proj(lo, hi):
        return jnp.dot(h, w_ref[:, lo:hi], preferred_element_type=F32)

    fox_scale = FOX_HEAD_DIM ** -0.5 * LOG2E
    fq_ref[...] = (proj(_EV_Q, _EV_K) * fox_scale + q_aug).astype(BF16)
    _store_vt(fv_ref, proj(_EV_V, _EV_G), FOX_HEAD_DIM)

    z = proj(_EV_G, _EV_CQ) + bf_ref[...]
    logf = (jnp.minimum(z, 0.0) - jnp.log1p(jnp.exp(-jnp.abs(z)))) * LOG2E
    row = lax.broadcasted_iota(jnp.int32, (tm, tm), 0)
    col = lax.broadcasted_iota(jnp.int32, (tm, tm), 1)
    tril = jnp.where(row >= col, 1.0, 0.0).astype(BF16)
    parts = jnp.dot(tril, jnp.concatenate(_split3(logf), axis=1), preferred_element_type=F32)
    c = (parts[:, :LANES] + parts[:, LANES:2 * LANES] + parts[:, 2 * LANES:]
         + carry_ref[SUBLANES - 1:SUBLANES, :])
    carry_ref[...] = c[tm - SUBLANES:, :]
    c_aug = jnp.dot(jnp.concatenate(_split3(c), axis=1), ec_ref[...], preferred_element_type=F32)
    fk_ref[...] = (proj(_EV_K, _EV_V) + c_aug).astype(BF16)

    ra, rb, rc = ra_ref[...], rb_ref[...], rc_ref[...]

    cq = _rms(proj(_EV_CQ, _EV_CKV), qn_ref[...]).astype(BF16)
    mla_scale = (MLA_NOPE_DIM + MLA_ROPE_DIM) ** -0.5 * LOG2E
    q = jnp.dot(cq, wuq_ref[...], preferred_element_type=F32) * mla_scale
    for hd in range(HEADS):
        sl = slice(hd * LANES, (hd + 1) * LANES)
        mq_ref[:, sl] = _rope(q[:, sl], ra, rb, rc).astype(BF16)

    ckv = _rms(proj(_EV_CKV, _EV_KR), kvn_ref[...]).astype(BF16)
    kr = _rope(proj(_EV_KR, _EV_WIDTH), ra, rb, rc)
    kn = jnp.dot(ckv, wuk_ref[...], preferred_element_type=F32)
    for hd in range(HEADS):
        sl = slice(hd * LANES, (hd + 1) * LANES)
        mk_ref[:, sl] = (kn[:, sl] + kr).astype(BF16)
    _store_vt(mv_ref, jnp.dot(ckv, wuv_ref[...], preferred_element_type=F32), MLA_V_DIM)


def _pad_heads(w, dh):
    k = w.shape[0]
    w = w.reshape(k, HEADS, dh)
    return jnp.pad(w, ((0, 0), (0, 0), (0, LANES - dh))).reshape(k, HEAD_COLS)


def _even_consts():
    cst = np.zeros((SUBLANES, HEAD_COLS), np.float32)
    ec = np.zeros((3 * LANES, HEAD_COLS), np.float32)
    for hd in range(HEADS):
        cst[0, hd * LANES + 64: hd * LANES + 67] = 1.0
        for p in range(3):
            ec[p * LANES + hd, hd * LANES + 64 + p] = -1.0
    return jnp.asarray(cst), jnp.asarray(ec, dtype=BF16)


def _rope_tables(seq):
    half = MLA_ROPE_DIM // 2
    inv = ROPE_THETA ** (-jnp.arange(0, MLA_ROPE_DIM, 2, dtype=F32) / MLA_ROPE_DIM)
    ang = jnp.arange(seq, dtype=F32)[:, None] * inv[None, :]
    cos, sin = jnp.cos(ang), jnp.sin(ang)
    zeros = jnp.zeros((seq, half), F32)
    ra = jnp.concatenate([jnp.ones((seq, 64), F32), cos, cos, jnp.zeros((seq, 32), F32)], axis=1)
    rb = jnp.concatenate([jnp.zeros((seq, 64), F32), zeros, sin, jnp.zeros((seq, 32), F32)], axis=1)
    rc = jnp.concatenate([jnp.zeros((seq, 64), F32), -sin, zeros, jnp.zeros((seq, 32), F32)], axis=1)
    return ra, rb, rc


def _even_proj(x, g, w_in, b_forget, q_norm, w_uq, kv_norm, w_ukv, rope, bsz, seq):
    t = x.shape[0]
    tm, tk = PROJ_ROWS, ATTN_TILE
    cuts = np.cumsum([FOX_WIDTH, FOX_WIDTH, FOX_WIDTH, HEADS, MLA_Q_RANK, MLA_KV_RANK]).tolist()
    wfq, wfk, wfv, wg, wcq, wckv, wkr = jnp.split(w_in, cuts, axis=1)
    w1 = jnp.concatenate([
        _pad_heads(wfq, FOX_HEAD_DIM), _pad_heads(wfk, FOX_HEAD_DIM), wfv,
        jnp.pad(wg, ((0, 0), (0, LANES - HEADS))), wcq, wckv,
        jnp.pad(wkr, ((0, 0), (64, LANES - 64 - MLA_ROPE_DIM)))], axis=1).astype(BF16)
    assert w1.shape[1] == _EV_WIDTH
    bf = jnp.pad(b_forget, (0, LANES - HEADS)).reshape(1, LANES)
    wuq = _pad_heads(w_uq, MLA_NOPE_DIM + MLA_ROPE_DIM).astype(BF16)
    wukv = w_ukv.reshape(MLA_KV_RANK, HEADS, MLA_NOPE_DIM + MLA_V_DIM)
    wuk = _pad_heads(wukv[:, :, :MLA_NOPE_DIM].reshape(MLA_KV_RANK, -1), MLA_NOPE_DIM).astype(BF16)
    wuv = wukv[:, :, MLA_NOPE_DIM:].reshape(MLA_KV_RANK, -1).astype(BF16)
    cst, ec = _even_consts()
    ra, rb, rc = rope
    tiles_per_seq = seq // tm
    rope_spec = pl.BlockSpec((tm, LANES), lambda i: (i % tiles_per_seq, 0))
    row_spec = pl.BlockSpec((tm, HEAD_COLS), lambda i: (i, 0))
    out_sds = jax.ShapeDtypeStruct((t, HEAD_COLS), BF16)
    vt_spec = _vt_spec(FOX_HEAD_DIM, tm, tk, tiles_per_seq)
    vt_sds = jax.ShapeDtypeStruct((bsz, HEADS, seq // tk, _vt_rows(FOX_HEAD_DIM), tk), BF16)
    return pl.pallas_call(
        functools.partial(_even_proj_kernel, tiles_per_seq=tiles_per_seq),
        grid=(t // tm,),
        in_specs=[pl.BlockSpec((tm, D_MODEL), lambda i: (i, 0)),
                  _const_spec((1, D_MODEL)), _const_spec(w1.shape), _const_spec((1, LANES)),
                  _const_spec((1, MLA_Q_RANK)), _const_spec(wuq.shape),
                  _const_spec((1, MLA_KV_RANK)), _const_spec(wuk.shape), _const_spec(wuv.shape),
                  _const_spec(ec.shape), rope_spec, rope_spec, rope_spec, _const_spec(cst.shape)],
        out_specs=[row_spec, row_spec, vt_spec, row_spec, row_spec, vt_spec],
        out_shape=[out_sds, out_sds, vt_sds, out_sds, out_sds, vt_sds],
        scratch_shapes=[pltpu.VMEM((SUBLANES, LANES), F32)],
        compiler_params=_params("arbitrary"),
        name="even_proj",
    )(x, g.reshape(1, -1), w1, bf, q_norm.reshape(1, -1), wuq, kv_norm.reshape(1, -1), wuk, wuv,
      ec, ra, rb, rc, cst)


def _alibi_slope(hd):
    return 2.0 ** (-8.0 * (hd + 1) / HEADS)


def _odd_proj_kernel(x_ref, g_ref, w_ref, q_ref, k_ref, v_ref, *, tiles_per_seq):
    tm = x_ref.shape[0]
    i = pl.program_id(0)
    h = _rms(x_ref[...], g_ref[...]).astype(BF16)
    scale = DIFF_HEAD_DIM ** -0.5 * LOG2E
    q_ref[...] = (jnp.dot(h, w_ref[:, :HEAD_COLS], preferred_element_type=F32) * scale).astype(BF16)
    k = jnp.dot(h, w_ref[:, HEAD_COLS:2 * HEAD_COLS], preferred_element_type=F32)
    v = jnp.dot(h, w_ref[:, 2 * HEAD_COLS:], preferred_element_type=F32)
    pos = ((i % tiles_per_seq) * tm + lax.broadcasted_iota(jnp.int32, (tm, 1), 0)).astype(F32)
    lane = lax.broadcasted_iota(jnp.int32, (tm, LANES), 1)
    for hd in range(HEADS):
        sl = slice(hd * LANES, (hd + 1) * LANES)
        b_hi, b_mid, b_lo = (p.astype(F32) for p in _split3(pos * (_alibi_slope(hd) * LOG2E)))
        aug = jnp.where(lane == 0, b_hi, jnp.where(lane == 1, b_mid, jnp.where(lane == 2, b_lo, 0.0)))
        k_ref[:, 2 * hd * LANES:(2 * hd + 1) * LANES] = k[:, sl].astype(BF16)
        k_ref[:, (2 * hd + 1) * LANES:(2 * hd + 2) * LANES] = aug.astype(BF16)
    _store_vt(v_ref, v, DIFF_V_DIM)


def _odd_proj(x, g, w_in, bsz, seq):
    t = x.shape[0]
    tm, tk = PROJ_ROWS, ATTN_TILE
    w = w_in.astype(BF16)
    tiles_per_seq = seq // tm
    return pl.pallas_call(
        functools.partial(_odd_proj_kernel, tiles_per_seq=tiles_per_seq),
        grid=(t // tm,),
        in_specs=[pl.BlockSpec((tm, D_MODEL), lambda i: (i, 0)),
                  _const_spec((1, D_MODEL)), _const_spec(w.shape)],
        out_specs=[pl.BlockSpec((tm, HEAD_COLS), lambda i: (i, 0)),
                   pl.BlockSpec((tm, 2 * HEAD_COLS), lambda i: (i, 0)),
                   _vt_spec(DIFF_V_DIM, tm, tk, tiles_per_seq)],
        out_shape=[jax.ShapeDtypeStruct((t, HEAD_COLS), BF16),
                   jax.ShapeDtypeStruct((t, 2 * HEAD_COLS), BF16),
                   jax.ShapeDtypeStruct((bsz, HEADS, seq // tk, _vt_rows(DIFF_V_DIM), tk), BF16)],
        compiler_params=_params("arbitrary"),
        name="odd_proj",
    )(x, g.reshape(1, -1), w)


_NT = (((1,), (1,)), ((), ()))


def _attend(s, vt, m, acc, masked):
    if masked:
        key = lax.broadcasted_iota(jnp.int32, s.shape, 0)
        qry = lax.broadcasted_iota(jnp.int32, s.shape, 1)
        s = jnp.where(key <= qry, s, NEG_INF)
    m_new = jnp.maximum(m, jnp.max(s, axis=0, keepdims=True))
    p = jnp.exp2(s - m_new).astype(BF16)
    acc = jnp.exp2(m - m_new) * acc + jnp.dot(vt, p, preferred_element_type=F32)
    return m_new, acc


def _two_stream_attention(qk, vts, s_ref, qi, nq, rows):
    s_ref[...] = qk(0, 0)

    def body(j, carry):
        m0, a0, m1, a1 = carry
        s1 = qk(1, j)
        m0, a0 = _attend(s_ref[...], vts(0, j), m0, a0, False)
        s_ref[...] = qk(0, j + 1)
        m1, a1 = _attend(s1, vts(1, j), m1, a1, False)
        return m0, a0, m1, a1

    init = (jnp.full((1, nq), NEG_INF, F32), jnp.zeros((rows, nq), F32)) * 2
    m0, a0, m1, a1 = lax.fori_loop(0, qi, body, init)
    s1 = qk(1, qi)
    _, a0 = _attend(s_ref[...], vts(0, qi), m0, a0, True)
    _, a1 = _attend(s1, vts(1, qi), m1, a1, True)
    return a0, a1


def _flash_kernel(q_ref, k_ref, vt_ref, o_ref, s_ref):
    tq = q_ref.shape[1]
    rows, tk = vt_ref.shape[3], vt_ref.shape[4]
    dv = rows - BF16_ROWS
    qs = [q_ref[0, :, hd * LANES:(hd + 1) * LANES] for hd in range(2)]

    def qk(hd, j):
        start = pl.multiple_of(j * tk, tk)
        kc = k_ref[0, pl.ds(start, tk), hd * LANES:(hd + 1) * LANES]
        return lax.dot_general(kc, qs[hd], _NT, preferred_element_type=F32)

    a0, a1 = _two_stream_attention(qk, lambda hd, j: vt_ref[0, hd, j], s_ref, pl.program_id(2),
                                   tq, rows)
    o_t = jnp.concatenate([a0[:dv] / a0[dv:dv + 1], a1[:dv] / a1[dv:dv + 1]], axis=0)
    o_ref[0] = o_t.T.astype(BF16)


def _flash_attention(q, k, vt, tq):
    b, s, _ = q.shape
    _, _, chunks, rows, tk = vt.shape
    assert tq == tk
    return pl.pallas_call(
        _flash_kernel,
        grid=(b, HEADS // 2, s // tq),
        in_specs=[pl.BlockSpec((1, tq, 2 * LANES), lambda bi, hp, i: (bi, i, hp)),
                  pl.BlockSpec((1, s, 2 * LANES), lambda bi, hp, i: (bi, 0, hp)),
                  pl.BlockSpec((1, 2, chunks, rows, tk), lambda bi, hp, i: (bi, hp, 0, 0, 0))],
        out_specs=pl.BlockSpec((1, tq, LANES), lambda bi, hp, i: (bi, i, hp)),
        out_shape=jax.ShapeDtypeStruct((b, s, HEADS * (rows - BF16_ROWS)), BF16),
        scratch_shapes=[pltpu.VMEM((tk, tq), F32)],
        compiler_params=_params("arbitrary", "arbitrary", "arbitrary"),
        name="flash_attention",
    )(q, k, vt)


def _diff_kernel(q_ref, k_ref, vt_ref, lq1_ref, lk1_ref, lq2_ref, lk2_ref, sub_ref, o_ref, s_ref,
                 *, lambda_init):
    tq = q_ref.shape[1]
    rows, tk = vt_ref.shape[3], vt_ref.shape[4]
    dv = rows - BF16_ROWS
    q = q_ref[0]
    lane = lax.broadcasted_iota(jnp.int32, (tq, LANES), 1)
    zero = jnp.zeros_like(q)
    aug = jnp.where(lane < 3, 1.0, 0.0).astype(BF16)
    q1 = jnp.concatenate([jnp.where(lane < DIFF_HEAD_DIM, q, zero), aug], axis=1)
    q2 = jnp.concatenate([jnp.where(lane >= DIFF_HEAD_DIM, q, zero), aug], axis=1)
    qs = [q1, q2]

    def qk(n, j):
        start = pl.multiple_of(j * tk, tk)
        return lax.dot_general(k_ref[0, pl.ds(start, tk), :], qs[n], _NT,
                               preferred_element_type=F32)

    a1, a2 = _two_stream_attention(qk, lambda n, j: vt_ref[0, 0, j], s_ref, pl.program_id(2),
                                   tq, rows)
    lam = (jnp.exp(jnp.sum(lq1_ref[...] * lk1_ref[...], axis=1, keepdims=True))
           - jnp.exp(jnp.sum(lq2_ref[...] * lk2_ref[...], axis=1, keepdims=True)) + lambda_init)
    o = (a1[:dv] / a1[dv:dv + 1] - lam * (a2[:dv] / a2[dv:dv + 1])).T
    o_ref[0] = (_rms(o, sub_ref[...]) * (1.0 - lambda_init)).astype(BF16)


def _diff_attention(q, k, vt, lq1, lk1, lq2, lk2, subln, lambda_init, tq):
    b, s, _ = q.shape
    _, _, chunks, rows, tk = vt.shape
    assert tq == tk
    vec = lambda a: a.reshape(1, -1)
    return pl.pallas_call(
        functools.partial(_diff_kernel, lambda_init=lambda_init),
        grid=(b, HEADS, s // tq),
        in_specs=[pl.BlockSpec((1, tq, LANES), lambda bi, hd, i: (bi, i, hd)),
                  pl.BlockSpec((1, s, 2 * LANES), lambda bi, hd, i: (bi, 0, hd)),
                  pl.BlockSpec((1, 1, chunks, rows, tk), lambda bi, hd, i: (bi, hd, 0, 0, 0)),
                  _const_spec((1, DIFF_HEAD_DIM)), _const_spec((1, DIFF_HEAD_DIM)),
                  _const_spec((1, DIFF_HEAD_DIM)), _const_spec((1, DIFF_HEAD_DIM)),
                  _const_spec((1, DIFF_V_DIM))],
        out_specs=pl.BlockSpec((1, tq, LANES), lambda bi, hd, i: (bi, i, hd)),
        out_shape=jax.ShapeDtypeStruct(q.shape, BF16),
        scratch_shapes=[pltpu.VMEM((tk, tq), F32)],
        compiler_params=_params("arbitrary", "arbitrary", "arbitrary"),
        name="diff_attention",
    )(q, k, vt, vec(lq1), vec(lk1), vec(lq2), vec(lk2), vec(subln))


def _out_proj_kernel(*refs, n_in):
    a_refs, w_refs = refs[:n_in], refs[n_in:2 * n_in]
    g_ref, x_ref, o_ref = refs[2 * n_in:]
    m = jnp.dot(a_refs[0][...], w_refs[0][...], preferred_element_type=F32)
    for a_ref, w_ref in zip(a_refs[1:], w_refs[1:]):
        m = m + jnp.dot(a_ref[...], w_ref[...], preferred_element_type=F32)
    o_ref[...] = x_ref[...] + _rms(m, g_ref[...])


def _out_proj(acts, weights, g, x):
    t = x.shape[0]
    tm = OUT_ROWS
    n_in = len(acts)
    x_spec = pl.BlockSpec((tm, D_MODEL), lambda i: (i, 0))
    return pl.pallas_call(
        functools.partial(_out_proj_kernel, n_in=n_in),
        grid=(t // tm,),
        in_specs=([pl.BlockSpec((tm, a.shape[1]), lambda i: (i, 0)) for a in acts]
                  + [_const_spec(w.shape) for w in weights]
                  + [_const_spec((1, D_MODEL)), x_spec]),
        out_specs=x_spec,
        out_shape=jax.ShapeDtypeStruct(x.shape, F32),
        compiler_params=_params("arbitrary"),
        name="out_proj",
    )(*acts, *weights, g.reshape(1, -1), x)


_N_FFN_CHUNKS = D_FF // FFN_CHUNK
_GELU_C = math.sqrt(2.0 / math.pi)


def _ffn_kernel(x_ref, gpre_ref, wup_ref, cw_ref, wdn_ref, gpost_ref, o_ref,
                h_ref, gate_ref, acc_ref, *, tiles_per_seq):
    tm = x_ref.shape[0]
    i = pl.program_id(0)

    @pl.when(i % tiles_per_seq == 0)
    def _():
        gate_ref[:, 0:SUBLANES, :] = jnp.zeros((_N_FFN_CHUNKS, SUBLANES, FFN_CHUNK), F32)

    h_ref[...] = _rms(x_ref[...], gpre_ref[...]).astype(BF16)
    acc_ref[...] = jnp.zeros_like(acc_ref)

    def body(c, _):
        gv = jnp.dot(h_ref[...], wup_ref[c], preferred_element_type=F32)
        gate, val = gv[:, :FFN_CHUNK], gv[:, FFN_CHUNK:]
        gate_ref[c, SUBLANES:, :] = gate
        cw = cw_ref[c]
        pre = (gate_ref[c, SUBLANES - 2:SUBLANES - 2 + tm, :] * cw[0:1, :]
               + gate_ref[c, SUBLANES - 1:SUBLANES - 1 + tm, :] * cw[1:2, :]
               + gate * cw[2:3, :] + cw[3:4, :])
        act = 0.5 * pre * (1.0 + jnp.tanh(_GELU_C * (pre + 0.044715 * (pre * pre * pre))))
        acc_ref[...] += jnp.dot((act * val).astype(BF16), wdn_ref[c], preferred_element_type=F32)
        gate_ref[c, 0:SUBLANES, :] = gate[tm - SUBLANES:, :]
        return 0

    lax.fori_loop(0, _N_FFN_CHUNKS, body, 0)
    o_ref[...] = x_ref[...] + _rms(acc_ref[...], gpost_ref[...])


def _ffn(x, g_pre, w_up, conv_w, conv_b, w_down, g_post, seq):
    t = x.shape[0]
    tm = FFN_ROWS
    nc = _N_FFN_CHUNKS
    wg = w_up[:, :D_FF].reshape(D_MODEL, nc, FFN_CHUNK)
    wv = w_up[:, D_FF:].reshape(D_MODEL, nc, FFN_CHUNK)
    wup = jnp.concatenate([wg, wv], axis=2).transpose(1, 0, 2).astype(BF16)
    wdn = w_down.reshape(nc, FFN_CHUNK, D_MODEL).astype(BF16)
    cw = jnp.concatenate([conv_w, conv_b[None, :],
                          jnp.zeros((SUBLANES - CONV_WIDTH - 1, D_FF), F32)], axis=0)
    cw = cw.reshape(SUBLANES, nc, FFN_CHUNK).transpose(1, 0, 2)
    x_spec = pl.BlockSpec((tm, D_MODEL), lambda i: (i, 0))
    return pl.pallas_call(
        functools.partial(_ffn_kernel, tiles_per_seq=seq // tm),
        grid=(t // tm,),
        in_specs=[x_spec, _const_spec((1, D_MODEL)), _const_spec(wup.shape), _const_spec(cw.shape),
                  _const_spec(wdn.shape), _const_spec((1, D_MODEL))],
        out_specs=x_spec,
        out_shape=jax.ShapeDtypeStruct(x.shape, F32),
        scratch_shapes=[pltpu.VMEM((tm, D_MODEL), BF16),
                        pltpu.VMEM((nc, tm + SUBLANES, FFN_CHUNK), F32),
                        pltpu.VMEM((tm, D_MODEL), F32)],
        compiler_params=_params("arbitrary"),
        name="conv_ffn",
    )(x, g_pre.reshape(1, -1), wup, cw, wdn, g_post.reshape(1, -1))


def kernel(x, norm_mix_pre, norm_mix_post, norm_ffn_pre, norm_ffn_post, even_w_in, even_b_forget, even_q_norm, even_w_uq, even_kv_norm, even_w_ukv, even_w_out, odd_w_in, odd_lambda_q1, odd_lambda_k1, odd_lambda_q2, odd_lambda_k2, odd_subln, odd_w_out, ffn_w_up, ffn_conv_w, ffn_conv_b, ffn_w_down):
    bsz, seq, d = x.shape
    assert d == D_MODEL and seq % ATTN_TILE == 0 and seq % FFN_ROWS == 0
    t = bsz * seq
    tq = ATTN_TILE
    rope = _rope_tables(seq)
    xf = x.reshape(t, d)
    for layer in range(DEPTH):
        i = layer // 2
        if layer % 2 == 0:
            fq, fk, fvt, mq, mk, mvt = _even_proj(
                xf, norm_mix_pre[layer], even_w_in[i], even_b_forget[i], even_q_norm[i],
                even_w_uq[i], even_kv_norm[i], even_w_ukv[i], rope, bsz, seq)
            shp = (bsz, seq, HEAD_COLS)
            fo = _flash_attention(fq.reshape(shp), fk.reshape(shp), fvt, tq)
            mo = _flash_attention(mq.reshape(shp), mk.reshape(shp), mvt, tq)
            w_out = even_w_out[i].astype(BF16)
            xf = _out_proj([fo.reshape(t, -1), mo.reshape(t, -1)],
                           [w_out[:FOX_WIDTH], w_out[FOX_WIDTH:]], norm_mix_post[layer], xf)
        else:
            lambda_init = 0.8 - 0.6 * math.exp(-0.3 * layer)
            q, k, vt = _odd_proj(xf, norm_mix_pre[layer], odd_w_in[i], bsz, seq)
            o = _diff_attention(q.reshape(bsz, seq, -1), k.reshape(bsz, seq, -1), vt,
                                odd_lambda_q1[i], odd_lambda_k1[i],
                                odd_lambda_q2[i], odd_lambda_k2[i], odd_subln[i], lambda_init, tq)
            xf = _out_proj([o.reshape(t, -1)], [odd_w_out[i].astype(BF16)],
                           norm_mix_post[layer], xf)
        xf = _ffn(xf, norm_ffn_pre[layer], ffn_w_up[layer], ffn_conv_w[layer], ffn_conv_b[layer],
                  ffn_w_down[layer], norm_ffn_post[layer], seq)
    return xf.reshape(bsz, seq, d)
```

```python
import functools
import math

import numpy as np
import jax
import jax.numpy as jnp
from jax import lax
from jax.experimental import pallas as pl
from jax.experimental.pallas import tpu as pltpu

D_MODEL = 1024
DEPTH = 4
RMS_EPS = 1e-6
NEG_INF = -1e30
HEADS = 8
FOX_HEAD_DIM = 64
FOX_WIDTH = HEADS * FOX_HEAD_DIM
MLA_NOPE_DIM = 64
MLA_ROPE_DIM = 32
MLA_V_DIM = 64
MLA_Q_RANK = 384
MLA_KV_RANK = 256
ROPE_THETA = 10000.0
DIFF_HEAD_DIM = 64
DIFF_V_DIM = 2 * DIFF_HEAD_DIM
D_FF = 2816
CONV_WIDTH = 3

LANES = 128
SUBLANES = 8
BF16_ROWS = 16
LOG2E = 1.4426950408889634
HEAD_COLS = HEADS * LANES
VMEM_LIMIT_BYTES = 56 * 1024 * 1024

PROJ_ROWS = 256
OUT_ROWS = 512
FFN_ROWS = 512
FFN_CHUNK = 256
ATTN_TILE = 512
FLASH_HEADS_PER_STEP = 4
DIFF_HEADS_PER_STEP = 2
F32 = jnp.float32
BF16 = jnp.bfloat16


def _params(*sem):
    return pltpu.CompilerParams(dimension_semantics=sem, vmem_limit_bytes=VMEM_LIMIT_BYTES)


def _const_spec(shape):
    nd = len(shape)
    return pl.BlockSpec(shape, lambda *_: (0,) * nd)


def _rms(x, g):
    return x * lax.rsqrt(jnp.mean(x * x, axis=-1, keepdims=True) + RMS_EPS) * g


def _split3(x):
    hi = x.astype(BF16)
    r1 = x - hi.astype(F32)
    mid = r1.astype(BF16)
    lo = (r1 - mid.astype(F32)).astype(BF16)
    return hi, mid, lo


def _rope(blk, ra, rb, rc):
    return blk * ra + pltpu.roll(blk, 16, 1) * rb + pltpu.roll(blk, LANES - 16, 1) * rc


def _vt_rows(dv):
    return dv + BF16_ROWS


def _store_vt(vt_ref, v, dv):
    tm = v.shape[0]
    per = LANES // dv
    for blk in range(v.shape[1] // LANES):
        vt = v[:, blk * LANES:(blk + 1) * LANES].T.astype(BF16)
        for s in range(per):
            vt_ref[0, blk * per + s, 0, 0:dv, :] = vt[s * dv:(s + 1) * dv, :]
    row = lax.broadcasted_iota(jnp.int32, (BF16_ROWS, tm), 0)
    ones_row = jnp.where(row == 0, 1.0, 0.0).astype(BF16)
    for hd in range(HEADS):
        vt_ref[0, hd, 0, dv:dv + BF16_ROWS, :] = ones_row


def _vt_spec(dv, tm, tk, tiles_per_seq):
    per_chunk = tk // tm
    return pl.BlockSpec(
        (1, HEADS, 1, _vt_rows(dv), tm),
        lambda i: (i // tiles_per_seq, 0, (i % tiles_per_seq) // per_chunk, 0,
                   (i % tiles_per_seq) % per_chunk))


_EV_Q, _EV_K, _EV_V = 0, HEAD_COLS, 2 * HEAD_COLS
_EV_G = _EV_V + FOX_WIDTH
_EV_CQ = _EV_G + LANES
_EV_CKV = _EV_CQ + MLA_Q_RANK
_EV_KR = _EV_CKV + MLA_KV_RANK
_EV_WIDTH = _EV_KR + LANES


def _even_proj_kernel(x_ref, g_ref, w_ref, bf_ref, qn_ref, wuq_ref, kvn_ref, wuk_ref, wuv_ref,
                      ec_ref, ra_ref, rb_ref, rc_ref, cst_ref,
                      fq_ref, fk_ref, fv_ref, mq_ref, mk_ref, mv_ref, carry_ref,
                      *, tiles_per_seq):
    tm = x_ref.shape[0]
    i = pl.program_id(0)

    @pl.when(i % tiles_per_seq == 0)
    def _():
        carry_ref[...] = jnp.zeros_like(carry_ref)

    h = _rms(x_ref[...], g_ref[...]).astype(BF16)
    q_aug = cst_ref[0:1, :]

    def proj(lo, hi):
        return jnp.dot(h, w_ref[:, lo:hi], preferred_element_type=F32)

    fox_scale = FOX_HEAD_DIM ** -0.5 * LOG2E
    fq_ref[...] = (proj(_EV_Q, _EV_K) * fox_scale + q_aug).astype(BF16)
    _store_vt(fv_ref, proj(_EV_V, _EV_G), FOX_HEAD_DIM)

    z = proj(_EV_G, _EV_CQ) + bf_ref[...]
    logf = (jnp.minimum(z, 0.0) - jnp.log1p(jnp.exp(-jnp.abs(z)))) * LOG2E
    row = lax.broadcasted_iota(jnp.int32, (tm, tm), 0)
    col = lax.broadcasted_iota(jnp.int32, (tm, tm), 1)
    tril = jnp.where(row >= col, 1.0, 0.0).astype(BF16)
    parts = jnp.dot(tril, jnp.concatenate(_split3(logf), axis=1), preferred_element_type=F32)
    c = (parts[:, :LANES] + parts[:, LANES:2 * LANES] + parts[:, 2 * LANES:]
         + carry_ref[SUBLANES - 1:SUBLANES, :])
    carry_ref[...] = c[tm - SUBLANES:, :]
    c_aug = jnp.dot(jnp.concatenate(_split3(c), axis=1), ec_ref[...], preferred_element_type=F32)
    fk_ref[...] = (proj(_EV_K, _EV_V) + c_aug).astype(BF16)

    ra, rb, rc = ra_ref[...], rb_ref[...], rc_ref[...]

    cq = _rms(proj(_EV_CQ, _EV_CKV), qn_ref[...]).astype(BF16)
    mla_scale = (MLA_NOPE_DIM + MLA_ROPE_DIM) ** -0.5 * LOG2E
    q = jnp.dot(cq, wuq_ref[...], preferred_element_type=F32) * mla_scale
    for hd in range(HEADS):
        sl = slice(hd * LANES, (hd + 1) * LANES)
        mq_ref[:, sl] = _rope(q[:, sl], ra, rb, rc).astype(BF16)

    ckv = _rms(proj(_EV_CKV, _EV_KR), kvn_ref[...]).astype(BF16)
    kr = _rope(proj(_EV_KR, _EV_WIDTH), ra, rb, rc)
    kn = jnp.dot(ckv, wuk_ref[...], preferred_element_type=F32)
    for hd in range(HEADS):
        sl = slice(hd * LANES, (hd + 1) * LANES)
        mk_ref[:, sl] = (kn[:, sl] + kr).astype(BF16)
    _store_vt(mv_ref, jnp.dot(ckv, wuv_ref[...], preferred_element_type=F32), MLA_V_DIM)


def _pad_heads(w, dh):
    k = w.shape[0]
    w = w.reshape(k, HEADS, dh)
    return jnp.pad(w, ((0, 0), (0, 0), (0, LANES - dh))).reshape(k, HEAD_COLS)


def _even_consts():
    cst = np.zeros((SUBLANES, HEAD_COLS), np.float32)
    ec = np.zeros((3 * LANES, HEAD_COLS), np.float32)
    for hd in range(HEADS):
        cst[0, hd * LANES + 64: hd * LANES + 67] = 1.0
        for p in range(3):
            ec[p * LANES + hd, hd * LANES + 64 + p] = -1.0
    return jnp.asarray(cst), jnp.asarray(ec, dtype=BF16)


def _rope_tables(seq):
    half = MLA_ROPE_DIM // 2
    inv = ROPE_THETA ** (-jnp.arange(0, MLA_ROPE_DIM, 2, dtype=F32) / MLA_ROPE_DIM)
    ang = jnp.arange(seq, dtype=F32)[:, None] * inv[None, :]
    cos, sin = jnp.cos(ang), jnp.sin(ang)
    zeros = jnp.zeros((seq, half), F32)
    ra = jnp.concatenate([jnp.ones((seq, 64), F32), cos, cos, jnp.zeros((seq, 32), F32)], axis=1)
    rb = jnp.concatenate([jnp.zeros((seq, 64), F32), zeros, sin, jnp.zeros((seq, 32), F32)], axis=1)
    rc = jnp.concatenate([jnp.zeros((seq, 64), F32), -sin, zeros, jnp.zeros((seq, 32), F32)], axis=1)
    return ra, rb, rc


def _even_proj(x, g, w_in, b_forget, q_norm, w_uq, kv_norm, w_ukv, rope, bsz, seq):
    t = x.shape[0]
    tm, tk = PROJ_ROWS, ATTN_TILE
    cuts = np.cumsum([FOX_WIDTH, FOX_WIDTH, FOX_WIDTH, HEADS, MLA_Q_RANK, MLA_KV_RANK]).tolist()
    wfq, wfk, wfv, wg, wcq, wckv, wkr = jnp.split(w_in, cuts, axis=1)
    w1 = jnp.concatenate([
        _pad_heads(wfq, FOX_HEAD_DIM), _pad_heads(wfk, FOX_HEAD_DIM), wfv,
        jnp.pad(wg, ((0, 0), (0, LANES - HEADS))), wcq, wckv,
        jnp.pad(wkr, ((0, 0), (64, LANES - 64 - MLA_ROPE_DIM)))], axis=1).astype(BF16)
    assert w1.shape[1] == _EV_WIDTH
    bf = jnp.pad(b_forget, (0, LANES - HEADS)).reshape(1, LANES)
    wuq = _pad_heads(w_uq, MLA_NOPE_DIM + MLA_ROPE_DIM).astype(BF16)
    wukv = w_ukv.reshape(MLA_KV_RANK, HEADS, MLA_NOPE_DIM + MLA_V_DIM)
    wuk = _pad_heads(wukv[:, :, :MLA_NOPE_DIM].reshape(MLA_KV_RANK, -1), MLA_NOPE_DIM).astype(BF16)
    wuv = wukv[:, :, MLA_NOPE_DIM:].reshape(MLA_KV_RANK, -1).astype(BF16)
    cst, ec = _even_consts()
    ra, rb, rc = rope
    tiles_per_seq = seq // tm
    rope_spec = pl.BlockSpec((tm, LANES), lambda i: (i % tiles_per_seq, 0))
    row_spec = pl.BlockSpec((tm, HEAD_COLS), lambda i: (i, 0))
    out_sds = jax.ShapeDtypeStruct((t, HEAD_COLS), BF16)
    vt_spec = _vt_spec(FOX_HEAD_DIM, tm, tk, tiles_per_seq)
    vt_sds = jax.ShapeDtypeStruct((bsz, HEADS, seq // tk, _vt_rows(FOX_HEAD_DIM), tk), BF16)
    return pl.pallas_call(
        functools.partial(_even_proj_kernel, tiles_per_seq=tiles_per_seq),
        grid=(t // tm,),
        in_specs=[pl.BlockSpec((tm, D_MODEL), lambda i: (i, 0)),
                  _const_spec((1, D_MODEL)), _const_spec(w1.shape), _const_spec((1, LANES)),
                  _const_spec((1, MLA_Q_RANK)), _const_spec(wuq.shape),
                  _const_spec((1, MLA_KV_RANK)), _const_spec(wuk.shape), _const_spec(wuv.shape),
                  _const_spec(ec.shape), rope_spec, rope_spec, rope_spec, _const_spec(cst.shape)],
        out_specs=[row_spec, row_spec, vt_spec, row_spec, row_spec, vt_spec],
        out_shape=[out_sds, out_sds, vt_sds, out_sds, out_sds, vt_sds],
        scratch_shapes=[pltpu.VMEM((SUBLANES, LANES), F32)],
        compiler_params=_params("arbitrary"),
        name="even_proj",
    )(x, g.reshape(1, -1), w1, bf, q_norm.reshape(1, -1), wuq, kv_norm.reshape(1, -1), wuk, wuv,
      ec, ra, rb, rc, cst)


def _alibi_slope(hd):
    return 2.0 ** (-8.0 * (hd + 1) / HEADS)


def _odd_proj_kernel(x_ref, g_ref, w_ref, q_ref, k_ref, v_ref, *, tiles_per_seq):
    tm = x_ref.shape[0]
    i = pl.program_id(0)
    h = _rms(x_ref[...], g_ref[...]).astype(BF16)
    scale = DIFF_HEAD_DIM ** -0.5 * LOG2E
    q_ref[...] = (jnp.dot(h, w_ref[:, :HEAD_COLS], preferred_element_type=F32) * scale).astype(BF16)
    k = jnp.dot(h, w_ref[:, HEAD_COLS:2 * HEAD_COLS], preferred_element_type=F32)
    v = jnp.dot(h, w_ref[:, 2 * HEAD_COLS:], preferred_element_type=F32)
    pos = ((i % tiles_per_seq) * tm + lax.broadcasted_iota(jnp.int32, (tm, 1), 0)).astype(F32)
    lane = lax.broadcasted_iota(jnp.int32, (tm, LANES), 1)
    for hd in range(HEADS):
        sl = slice(hd * LANES, (hd + 1) * LANES)
        b_hi, b_mid, b_lo = (p.astype(F32) for p in _split3(pos * (_alibi_slope(hd) * LOG2E)))
        aug = jnp.where(lane == 0, b_hi, jnp.where(lane == 1, b_mid, jnp.where(lane == 2, b_lo, 0.0)))
        k_ref[:, 2 * hd * LANES:(2 * hd + 1) * LANES] = k[:, sl].astype(BF16)
        k_ref[:, (2 * hd + 1) * LANES:(2 * hd + 2) * LANES] = aug.astype(BF16)
    _store_vt(v_ref, v, DIFF_V_DIM)


def _odd_proj(x, g, w_in, bsz, seq):
    t = x.shape[0]
    tm, tk = PROJ_ROWS, ATTN_TILE
    w = w_in.astype(BF16)
    tiles_per_seq = seq // tm
    return pl.pallas_call(
        functools.partial(_odd_proj_kernel, tiles_per_seq=tiles_per_seq),
        grid=(t // tm,),
        in_specs=[pl.BlockSpec((tm, D_MODEL), lambda i: (i, 0)),
                  _const_spec((1, D_MODEL)), _const_spec(w.shape)],
        out_specs=[pl.BlockSpec((tm, HEAD_COLS), lambda i: (i, 0)),
                   pl.BlockSpec((tm, 2 * HEAD_COLS), lambda i: (i, 0)),
                   _vt_spec(DIFF_V_DIM, tm, tk, tiles_per_seq)],
        out_shape=[jax.ShapeDtypeStruct((t, HEAD_COLS), BF16),
                   jax.ShapeDtypeStruct((t, 2 * HEAD_COLS), BF16),
                   jax.ShapeDtypeStruct((bsz, HEADS, seq // tk, _vt_rows(DIFF_V_DIM), tk), BF16)],
        compiler_params=_params("arbitrary"),
        name="odd_proj",
    )(x, g.reshape(1, -1), w)


_NT = (((1,), (1,)), ((), ()))


def _attend(s, vt, m, acc, masked):
    if masked:
        key = lax.broadcasted_iota(jnp.int32, s.shape, 0)
        qry = lax.broadcasted_iota(jnp.int32, s.shape, 1)
        s = jnp.where(key <= qry, s, NEG_INF)
    m_new = jnp.maximum(m, jnp.max(s, axis=0, keepdims=True))
    p = jnp.exp2(s - m_new).astype(BF16)
    acc = jnp.exp2(m - m_new) * acc + jnp.dot(vt, p, preferred_element_type=F32)
    return m_new, acc


def _stream_attention(qk, vts, s_ref, qi, nq, rows, n):
    s_ref[...] = qk(0, 0)

    def sweep(j, carry, diagonal):
        carry = list(carry)
        s_cur = s_ref[...]
        for k in range(n):
            if k + 1 < n:
                s_next = qk(k + 1, j)
            elif not diagonal:
                s_next = qk(0, j + 1)
            m, acc = _attend(s_cur, vts(k, j), carry[2 * k], carry[2 * k + 1], diagonal)
            carry[2 * k], carry[2 * k + 1] = m, acc
            if k + 1 == n and not diagonal:
                s_ref[...] = s_next
            s_cur = s_next
        return tuple(carry)

    init = (jnp.full((1, nq), NEG_INF, F32), jnp.zeros((rows, nq), F32)) * n
    carry = lax.fori_loop(0, qi, lambda j, c: sweep(j, c, False), init)
    return sweep(qi, carry, True)[1::2]


def _flash_kernel(q_ref, k_ref, vt_ref, o_ref, s_ref):
    tq = q_ref.shape[1]
    n, rows, tk = vt_ref.shape[1], vt_ref.shape[3], vt_ref.shape[4]
    dv = rows - BF16_ROWS
    qs = [q_ref[0, :, hd * LANES:(hd + 1) * LANES] for hd in range(n)]

    def qk(hd, j):
        start = pl.multiple_of(j * tk, tk)
        kc = k_ref[0, pl.ds(start, tk), hd * LANES:(hd + 1) * LANES]
        return lax.dot_general(kc, qs[hd], _NT, preferred_element_type=F32)

    accs = _stream_attention(qk, lambda hd, j: vt_ref[0, hd, j], s_ref, pl.program_id(2),
                             tq, rows, n)
    o_t = jnp.concatenate([a[:dv] / a[dv:dv + 1] for a in accs], axis=0)
    o_ref[0] = o_t.T.astype(BF16)


def _flash_attention(q, k, vt, tq):
    b, s, _ = q.shape
    _, _, chunks, rows, tk = vt.shape
    assert tq == tk
    n = FLASH_HEADS_PER_STEP
    dv = rows - BF16_ROWS
    return pl.pallas_call(
        _flash_kernel,
        grid=(b, HEADS // n, s // tq),
        in_specs=[pl.BlockSpec((1, tq, n * LANES), lambda bi, hp, i: (bi, i, hp)),
                  pl.BlockSpec((1, s, n * LANES), lambda bi, hp, i: (bi, 0, hp)),
                  pl.BlockSpec((1, n, chunks, rows, tk), lambda bi, hp, i: (bi, hp, 0, 0, 0))],
        out_specs=pl.BlockSpec((1, tq, n * dv), lambda bi, hp, i: (bi, i, hp)),
        out_shape=jax.ShapeDtypeStruct((b, s, HEADS * (rows - BF16_ROWS)), BF16),
        scratch_shapes=[pltpu.VMEM((tk, tq), F32)],
        compiler_params=_params("arbitrary", "arbitrary", "arbitrary"),
        name="flash_attention",
    )(q, k, vt)


def _diff_kernel(q_ref, k_ref, vt_ref, lq1_ref, lk1_ref, lq2_ref, lk2_ref, sub_ref, o_ref, s_ref,
                 *, lambda_init):
    tq = q_ref.shape[1]
    n, rows, tk = vt_ref.shape[1], vt_ref.shape[3], vt_ref.shape[4]
    dv = rows - BF16_ROWS
    lane = lax.broadcasted_iota(jnp.int32, (tq, LANES), 1)
    aug = jnp.where(lane < 3, 1.0, 0.0).astype(BF16)
    qs = []
    for hd in range(n):
        q = q_ref[0, :, hd * LANES:(hd + 1) * LANES]
        zero = jnp.zeros_like(q)
        qs.append(jnp.concatenate([jnp.where(lane < DIFF_HEAD_DIM, q, zero), aug], axis=1))
        qs.append(jnp.concatenate([jnp.where(lane >= DIFF_HEAD_DIM, q, zero), aug], axis=1))

    def qk(st, j):
        hd = st // 2
        start = pl.multiple_of(j * tk, tk)
        kc = k_ref[0, pl.ds(start, tk), 2 * hd * LANES:2 * (hd + 1) * LANES]
        return lax.dot_general(kc, qs[st], _NT, preferred_element_type=F32)

    accs = _stream_attention(qk, lambda st, j: vt_ref[0, st // 2, j], s_ref, pl.program_id(2),
                             tq, rows, 2 * n)
    lam = (jnp.exp(jnp.sum(lq1_ref[...] * lk1_ref[...], axis=1, keepdims=True))
           - jnp.exp(jnp.sum(lq2_ref[...] * lk2_ref[...], axis=1, keepdims=True)) + lambda_init)
    for hd in range(n):
        a1, a2 = accs[2 * hd], accs[2 * hd + 1]
        o = (a1[:dv] / a1[dv:dv + 1] - lam * (a2[:dv] / a2[dv:dv + 1])).T
        o_ref[0, :, hd * dv:(hd + 1) * dv] = (
            _rms(o, sub_ref[...]) * (1.0 - lambda_init)).astype(BF16)


def _diff_attention(q, k, vt, lq1, lk1, lq2, lk2, subln, lambda_init, tq):
    b, s, _ = q.shape
    _, _, chunks, rows, tk = vt.shape
    assert tq == tk
    vec = lambda a: a.reshape(1, -1)
    n = DIFF_HEADS_PER_STEP
    return pl.pallas_call(
        functools.partial(_diff_kernel, lambda_init=lambda_init),
        grid=(b, HEADS // n, s // tq),
        in_specs=[pl.BlockSpec((1, tq, n * LANES), lambda bi, hd, i: (bi, i, hd)),
                  pl.BlockSpec((1, s, 2 * n * LANES), lambda bi, hd, i: (bi, 0, hd)),
                  pl.BlockSpec((1, n, chunks, rows, tk), lambda bi, hd, i: (bi, hd, 0, 0, 0)),
                  _const_spec((1, DIFF_HEAD_DIM)), _const_spec((1, DIFF_HEAD_DIM)),
                  _const_spec((1, DIFF_HEAD_DIM)), _const_spec((1, DIFF_HEAD_DIM)),
                  _const_spec((1, DIFF_V_DIM))],
        out_specs=pl.BlockSpec((1, tq, n * LANES), lambda bi, hd, i: (bi, i, hd)),
        out_shape=jax.ShapeDtypeStruct(q.shape, BF16),
        scratch_shapes=[pltpu.VMEM((tk, tq), F32)],
        compiler_params=_params("arbitrary", "arbitrary", "arbitrary"),
        name="diff_attention",
    )(q, k, vt, vec(lq1), vec(lk1), vec(lq2), vec(lk2), vec(subln))


def _out_proj_kernel(*refs, n_in):
    a_refs, w_refs = refs[:n_in], refs[n_in:2 * n_in]
    g_ref, x_ref, o_ref = refs[2 * n_in:]
    m = jnp.dot(a_refs[0][...], w_refs[0][...], preferred_element_type=F32)
    for a_ref, w_ref in zip(a_refs[1:], w_refs[1:]):
        m = m + jnp.dot(a_ref[...], w_ref[...], preferred_element_type=F32)
    o_ref[...] = x_ref[...] + _rms(m, g_ref[...])


def _out_proj(acts, weights, g, x):
    t = x.shape[0]
    tm = OUT_ROWS
    n_in = len(acts)
    x_spec = pl.BlockSpec((tm, D_MODEL), lambda i: (i, 0))
    return pl.pallas_call(
        functools.partial(_out_proj_kernel, n_in=n_in),
        grid=(t // tm,),
        in_specs=([pl.BlockSpec((tm, a.shape[1]), lambda i: (i, 0)) for a in acts]
                  + [_const_spec(w.shape) for w in weights]
                  + [_const_spec((1, D_MODEL)), x_spec]),
        out_specs=x_spec,
        out_shape=jax.ShapeDtypeStruct(x.shape, F32),
        compiler_params=_params("arbitrary"),
        name="out_proj",
    )(*acts, *weights, g.reshape(1, -1), x)


_N_FFN_CHUNKS = D_FF // FFN_CHUNK
_GELU_C = math.sqrt(2.0 / math.pi)


def _ffn_kernel(x_ref, gpre_ref, wup_ref, cw_ref, wdn_ref, gpost_ref, o_ref,
                h_ref, hist_ref, gbuf_ref, gva_ref, gvb_ref, mid_ref, *, tiles_per_seq):
    tm = x_ref.shape[0]
    nc = _N_FFN_CHUNKS
    i = pl.program_id(0)

    @pl.when(i % tiles_per_seq == 0)
    def _():
        hist_ref[...] = jnp.zeros_like(hist_ref)

    h_ref[...] = _rms(x_ref[...], gpre_ref[...]).astype(BF16)

    def up(c):
        return jnp.dot(h_ref[...], wup_ref[c], preferred_element_type=F32)

    def gated(c, gv_ref):
        gate, val = gv_ref[:, :FFN_CHUNK], gv_ref[:, FFN_CHUNK:]
        gbuf_ref[0:SUBLANES, :] = hist_ref[c]
        gbuf_ref[SUBLANES:, :] = gate
        hist_ref[c] = gate[tm - SUBLANES:, :]
        cw = cw_ref[c]
        pre = (gbuf_ref[SUBLANES - 2:SUBLANES - 2 + tm, :] * cw[0:1, :]
               + gbuf_ref[SUBLANES - 1:SUBLANES - 1 + tm, :] * cw[1:2, :]
               + gate * cw[2:3, :] + cw[3:4, :])
        act = 0.5 * pre * (1.0 + jnp.tanh(_GELU_C * (pre + 0.044715 * (pre * pre * pre))))
        mid_ref[c] = (act * val).astype(BF16)

    assert nc % 2 == 1
    gva_ref[...] = up(0)

    def body(k, _):
        gated(2 * k, gva_ref)
        gvb_ref[...] = up(2 * k + 1)
        gated(2 * k + 1, gvb_ref)
        gva_ref[...] = up(2 * k + 2)
        return 0

    lax.fori_loop(0, nc // 2, body, 0)
    gated(nc - 1, gva_ref)
    f = jnp.dot(mid_ref[0], wdn_ref[0], preferred_element_type=F32)
    for c in range(1, nc):
        f = f + jnp.dot(mid_ref[c], wdn_ref[c], preferred_element_type=F32)
    o_ref[...] = x_ref[...] + _rms(f, gpost_ref[...])


def _ffn(x, g_pre, w_up, conv_w, conv_b, w_down, g_post, seq):
    t = x.shape[0]
    tm = FFN_ROWS
    nc = _N_FFN_CHUNKS
    wg = w_up[:, :D_FF].reshape(D_MODEL, nc, FFN_CHUNK)
    wv = w_up[:, D_FF:].reshape(D_MODEL, nc, FFN_CHUNK)
    wup = jnp.concatenate([wg, wv], axis=2).transpose(1, 0, 2).astype(BF16)
    wdn = w_down.reshape(nc, FFN_CHUNK, D_MODEL).astype(BF16)
    cw = jnp.concatenate([conv_w, conv_b[None, :],
                          jnp.zeros((SUBLANES - CONV_WIDTH - 1, D_FF), F32)], axis=0)
    cw = cw.reshape(SUBLANES, nc, FFN_CHUNK).transpose(1, 0, 2)
    x_spec = pl.BlockSpec((tm, D_MODEL), lambda i: (i, 0))
    return pl.pallas_call(
        functools.partial(_ffn_kernel, tiles_per_seq=seq // tm),
        grid=(t // tm,),
        in_specs=[x_spec, _const_spec((1, D_MODEL)), _const_spec(wup.shape), _const_spec(cw.shape),
                  _const_spec(wdn.shape), _const_spec((1, D_MODEL))],
        out_specs=x_spec,
        out_shape=jax.ShapeDtypeStruct(x.shape, F32),
        scratch_shapes=[pltpu.VMEM((tm, D_MODEL), BF16),
                        pltpu.VMEM((nc, SUBLANES, FFN_CHUNK), F32),
                        pltpu.VMEM((tm + SUBLANES, FFN_CHUNK), F32),
                        pltpu.VMEM((tm, 2 * FFN_CHUNK), F32),
                        pltpu.VMEM((tm, 2 * FFN_CHUNK), F32),
                        pltpu.VMEM((nc, tm, FFN_CHUNK), BF16)],
        compiler_params=_params("arbitrary"),
        name="conv_ffn",
    )(x, g_pre.reshape(1, -1), wup, cw, wdn, g_post.reshape(1, -1))


def kernel(x, norm_mix_pre, norm_mix_post, norm_ffn_pre, norm_ffn_post, even_w_in, even_b_forget, even_q_norm, even_w_uq, even_kv_norm, even_w_ukv, even_w_out, odd_w_in, odd_lambda_q1, odd_lambda_k1, odd_lambda_q2, odd_lambda_k2, odd_subln, odd_w_out, ffn_w_up, ffn_conv_w, ffn_conv_b, ffn_w_down):
    bsz, seq, d = x.shape
    assert d == D_MODEL and seq % ATTN_TILE == 0 and seq % FFN_ROWS == 0
    t = bsz * seq
    tq = ATTN_TILE
    rope = _rope_tables(seq)
    xf = x.reshape(t, d)
    for layer in range(DEPTH):
        i = layer // 2
        if layer % 2 == 0:
            fq, fk, fvt, mq, mk, mvt = _even_proj(
                xf, norm_mix_pre[layer], even_w_in[i], even_b_forget[i], even_q_norm[i],
                even_w_uq[i], even_kv_norm[i], even_w_ukv[i], rope, bsz, seq)
            shp = (bsz, seq, HEAD_COLS)
            fo = _flash_attention(fq.reshape(shp), fk.reshape(shp), fvt, tq)
            mo = _flash_attention(mq.reshape(shp), mk.reshape(shp), mvt, tq)
            w_out = even_w_out[i].astype(BF16)
            xf = _out_proj([fo.reshape(t, -1), mo.reshape(t, -1)],
                           [w_out[:FOX_WIDTH], w_out[FOX_WIDTH:]], norm_mix_post[layer], xf)
        else:
            lambda_init = 0.8 - 0.6 * math.exp(-0.3 * layer)
            q, k, vt = _odd_proj(xf, norm_mix_pre[layer], odd_w_in[i], bsz, seq)
            o = _diff_attention(q.reshape(bsz, seq, -1), k.reshape(bsz, seq, -1), vt,
                                odd_lambda_q1[i], odd_lambda_k1[i],
                                odd_lambda_q2[i], odd_lambda_k2[i], odd_subln[i], lambda_init, tq)
            xf = _out_proj([o.reshape(t, -1)], [odd_w_out[i].astype(BF16)],
                           norm_mix_post[layer], xf)
        xf = _ffn(xf, norm_ffn_pre[layer], ffn_w_up[layer], ffn_conv_w[layer], ffn_conv_b[layer],
                  ffn_w_down[layer], norm_ffn_post[layer], seq)
    return xf.reshape(bsz, seq, d)
```

```python
import functools
import math

import numpy as np
import jax
import jax.numpy as jnp
from jax import lax
from jax.experimental import pallas as pl
from jax.experimental.pallas import tpu as pltpu

D_MODEL = 1024
DEPTH = 4
RMS_EPS = 1e-6
NEG_INF = -1e30
HEADS = 8
FOX_HEAD_DIM = 64
FOX_WIDTH = HEADS * FOX_HEAD_DIM
MLA_NOPE_DIM = 64
MLA_ROPE_DIM = 32
MLA_V_DIM = 64
MLA_Q_RANK = 384
MLA_KV_RANK = 256
ROPE_THETA = 10000.0
DIFF_HEAD_DIM = 64
DIFF_V_DIM = 2 * DIFF_HEAD_DIM
D_FF = 2816
CONV_WIDTH = 3

LANES = 128
SUBLANES = 8
BF16_ROWS = 16
LOG2E = 1.4426950408889634
HEAD_COLS = HEADS * LANES
VMEM_LIMIT_BYTES = 56 * 1024 * 1024

PROJ_ROWS = 512
OUT_ROWS = 1024
FFN_ROWS = 512
FFN_CHUNK = 256
ATTN_TILE = 512
ATTN_KEYS = 512
FLASH_HEADS_PER_STEP = 8
DIFF_HEADS_PER_STEP = 4
F32 = jnp.float32
BF16 = jnp.bfloat16


def _params(*sem):
    return pltpu.CompilerParams(dimension_semantics=sem, vmem_limit_bytes=VMEM_LIMIT_BYTES)


def _const_spec(shape):
    nd = len(shape)
    return pl.BlockSpec(shape, lambda *_: (0,) * nd)


def _rms(x, g):
    return x * lax.rsqrt(jnp.mean(x * x, axis=-1, keepdims=True) + RMS_EPS) * g


def _split3(x):
    hi = x.astype(BF16)
    r1 = x - hi.astype(F32)
    mid = r1.astype(BF16)
    lo = (r1 - mid.astype(F32)).astype(BF16)
    return hi, mid, lo


def _rope(blk, ra, rb, rc):
    return blk * ra + pltpu.roll(blk, 16, 1) * rb + pltpu.roll(blk, LANES - 16, 1) * rc


def _vt_rows(dv):
    return dv + BF16_ROWS


def _store_vt(vt_ref, v, dv):
    tm = v.shape[0]
    per = LANES // dv
    for blk in range(v.shape[1] // LANES):
        vt = v[:, blk * LANES:(blk + 1) * LANES].T.astype(BF16)
        for s in range(per):
            vt_ref[0, blk * per + s, 0, 0:dv, :] = vt[s * dv:(s + 1) * dv, :]
    row = lax.broadcasted_iota(jnp.int32, (BF16_ROWS, tm), 0)
    ones_row = jnp.where(row == 0, 1.0, 0.0).astype(BF16)
    for hd in range(HEADS):
        vt_ref[0, hd, 0, dv:dv + BF16_ROWS, :] = ones_row


def _vt_spec(dv, tm, tk, tiles_per_seq):
    per_chunk = tk // tm
    return pl.BlockSpec(
        (1, HEADS, 1, _vt_rows(dv), tm),
        lambda i: (i // tiles_per_seq, 0, (i % tiles_per_seq) // per_chunk, 0,
                   (i % tiles_per_seq) % per_chunk))


_EV_Q, _EV_K, _EV_V = 0, FOX_WIDTH, 2 * FOX_WIDTH
_EV_G = _EV_V + FOX_WIDTH
_EV_CQ = _EV_G + LANES
_EV_CKV = _EV_CQ + MLA_Q_RANK
_EV_KR = _EV_CKV + MLA_KV_RANK
_EV_WIDTH = _EV_KR + LANES


def _even_proj_kernel(x_ref, g_ref, w_ref, bf_ref, qn_ref, wuq_ref, kvn_ref, wuk_ref, wuv_ref,
                      ec_ref, ra_ref, rb_ref, rc_ref, cst_ref,
                      fq_ref, fk_ref, fv_ref, mq_ref, mk_ref, mv_ref, carry_ref,
                      *, tiles_per_seq):
    tm = x_ref.shape[0]
    i = pl.program_id(0)

    @pl.when(i % tiles_per_seq == 0)
    def _():
        carry_ref[...] = jnp.zeros_like(carry_ref)

    h = _rms(x_ref[...], g_ref[...]).astype(BF16)
    q_aug = cst_ref[0:1, :]

    def proj(lo, hi):
        return jnp.dot(h, w_ref[:, lo:hi], preferred_element_type=F32)

    lane = lax.broadcasted_iota(jnp.int32, (tm, LANES), 1)

    def spread_heads(o_ref, packed, aug):
        for pair in range(HEADS // 2):
            blk = packed[:, pair * LANES:(pair + 1) * LANES]
            for odd, src in enumerate((blk, pltpu.roll(blk, FOX_HEAD_DIM, 1))):
                sl = slice((2 * pair + odd) * LANES, (2 * pair + odd + 1) * LANES)
                o_ref[:, sl] = jnp.where(lane < FOX_HEAD_DIM, src, aug[:, sl]).astype(BF16)

    fox_scale = FOX_HEAD_DIM ** -0.5 * LOG2E
    spread_heads(fq_ref, proj(_EV_Q, _EV_K) * fox_scale, q_aug)
    _store_vt(fv_ref, proj(_EV_V, _EV_G), FOX_HEAD_DIM)

    z = proj(_EV_G, _EV_CQ) + bf_ref[...]
    logf = (jnp.minimum(z, 0.0) - jnp.log1p(jnp.exp(-jnp.abs(z)))) * LOG2E
    row = lax.broadcasted_iota(jnp.int32, (tm, tm), 0)
    col = lax.broadcasted_iota(jnp.int32, (tm, tm), 1)
    tril = jnp.where(row >= col, 1.0, 0.0).astype(BF16)
    parts = jnp.dot(tril, jnp.concatenate(_split3(logf), axis=1), preferred_element_type=F32)
    c = (parts[:, :LANES] + parts[:, LANES:2 * LANES] + parts[:, 2 * LANES:]
         + carry_ref[SUBLANES - 1:SUBLANES, :])
    carry_ref[...] = c[tm - SUBLANES:, :]
    c_aug = jnp.dot(jnp.concatenate(_split3(c), axis=1), ec_ref[...], preferred_element_type=F32)
    spread_heads(fk_ref, proj(_EV_K, _EV_V), c_aug)

    ra, rb, rc = ra_ref[...], rb_ref[...], rc_ref[...]

    cq = _rms(proj(_EV_CQ, _EV_CKV), qn_ref[...]).astype(BF16)
    mla_scale = (MLA_NOPE_DIM + MLA_ROPE_DIM) ** -0.5 * LOG2E
    q = jnp.dot(cq, wuq_ref[...], preferred_element_type=F32) * mla_scale
    for hd in range(HEADS):
        sl = slice(hd * LANES, (hd + 1) * LANES)
        mq_ref[:, sl] = _rope(q[:, sl], ra, rb, rc).astype(BF16)

    ckv = _rms(proj(_EV_CKV, _EV_KR), kvn_ref[...]).astype(BF16)
    kr = _rope(proj(_EV_KR, _EV_WIDTH), ra, rb, rc)
    kn = jnp.dot(ckv, wuk_ref[...], preferred_element_type=F32)
    for hd in range(HEADS):
        sl = slice(hd * LANES, (hd + 1) * LANES)
        mk_ref[:, sl] = (kn[:, sl] + kr).astype(BF16)
    _store_vt(mv_ref, jnp.dot(ckv, wuv_ref[...], preferred_element_type=F32), MLA_V_DIM)


def _pad_heads(w, dh):
    k = w.shape[0]
    w = w.reshape(k, HEADS, dh)
    return jnp.pad(w, ((0, 0), (0, 0), (0, LANES - dh))).reshape(k, HEAD_COLS)


def _even_consts():
    cst = np.zeros((SUBLANES, HEAD_COLS), np.float32)
    ec = np.zeros((3 * LANES, HEAD_COLS), np.float32)
    for hd in range(HEADS):
        cst[0, hd * LANES + 64: hd * LANES + 67] = 1.0
        for p in range(3):
            ec[p * LANES + hd, hd * LANES + 64 + p] = -1.0
    return jnp.asarray(cst), jnp.asarray(ec, dtype=BF16)


def _rope_tables(seq):
    half = MLA_ROPE_DIM // 2
    inv = ROPE_THETA ** (-jnp.arange(0, MLA_ROPE_DIM, 2, dtype=F32) / MLA_ROPE_DIM)
    ang = jnp.arange(seq, dtype=F32)[:, None] * inv[None, :]
    cos, sin = jnp.cos(ang), jnp.sin(ang)
    zeros = jnp.zeros((seq, half), F32)
    ra = jnp.concatenate([jnp.ones((seq, 64), F32), cos, cos, jnp.zeros((seq, 32), F32)], axis=1)
    rb = jnp.concatenate([jnp.zeros((seq, 64), F32), zeros, sin, jnp.zeros((seq, 32), F32)], axis=1)
    rc = jnp.concatenate([jnp.zeros((seq, 64), F32), -sin, zeros, jnp.zeros((seq, 32), F32)], axis=1)
    return ra, rb, rc


def _even_proj(x, g, w_in, b_forget, q_norm, w_uq, kv_norm, w_ukv, rope, bsz, seq):
    t = x.shape[0]
    tm, tk = PROJ_ROWS, ATTN_KEYS
    cuts = np.cumsum([FOX_WIDTH, FOX_WIDTH, FOX_WIDTH, HEADS, MLA_Q_RANK, MLA_KV_RANK]).tolist()
    wfq, wfk, wfv, wg, wcq, wckv, wkr = jnp.split(w_in, cuts, axis=1)
    w1 = jnp.concatenate([
        wfq, wfk, wfv,
        jnp.pad(wg, ((0, 0), (0, LANES - HEADS))), wcq, wckv,
        jnp.pad(wkr, ((0, 0), (64, LANES - 64 - MLA_ROPE_DIM)))], axis=1).astype(BF16)
    assert w1.shape[1] == _EV_WIDTH
    bf = jnp.pad(b_forget, (0, LANES - HEADS)).reshape(1, LANES)
    wuq = _pad_heads(w_uq, MLA_NOPE_DIM + MLA_ROPE_DIM).astype(BF16)
    wukv = w_ukv.reshape(MLA_KV_RANK, HEADS, MLA_NOPE_DIM + MLA_V_DIM)
    wuk = _pad_heads(wukv[:, :, :MLA_NOPE_DIM].reshape(MLA_KV_RANK, -1), MLA_NOPE_DIM).astype(BF16)
    wuv = wukv[:, :, MLA_NOPE_DIM:].reshape(MLA_KV_RANK, -1).astype(BF16)
    cst, ec = _even_consts()
    ra, rb, rc = rope
    tiles_per_seq = seq // tm
    rope_spec = pl.BlockSpec((tm, LANES), lambda i: (i % tiles_per_seq, 0))
    row_spec = pl.BlockSpec((tm, HEAD_COLS), lambda i: (i, 0))
    out_sds = jax.ShapeDtypeStruct((t, HEAD_COLS), BF16)
    vt_spec = _vt_spec(FOX_HEAD_DIM, tm, tk, tiles_per_seq)
    vt_sds = jax.ShapeDtypeStruct((bsz, HEADS, seq // tk, _vt_rows(FOX_HEAD_DIM), tk), BF16)
    return pl.pallas_call(
        functools.partial(_even_proj_kernel, tiles_per_seq=tiles_per_seq),
        grid=(t // tm,),
        in_specs=[pl.BlockSpec((tm, D_MODEL), lambda i: (i, 0)),
                  _const_spec((1, D_MODEL)), _const_spec(w1.shape), _const_spec((1, LANES)),
                  _const_spec((1, MLA_Q_RANK)), _const_spec(wuq.shape),
                  _const_spec((1, MLA_KV_RANK)), _const_spec(wuk.shape), _const_spec(wuv.shape),
                  _const_spec(ec.shape), rope_spec, rope_spec, rope_spec, _const_spec(cst.shape)],
        out_specs=[row_spec, row_spec, vt_spec, row_spec, row_spec, vt_spec],
        out_shape=[out_sds, out_sds, vt_sds, out_sds, out_sds, vt_sds],
        scratch_shapes=[pltpu.VMEM((SUBLANES, LANES), F32)],
        compiler_params=_params("arbitrary"),
        name="even_proj",
    )(x, g.reshape(1, -1), w1, bf, q_norm.reshape(1, -1), wuq, kv_norm.reshape(1, -1), wuk, wuv,
      ec, ra, rb, rc, cst)


def _alibi_slope(hd):
    return 2.0 ** (-8.0 * (hd + 1) / HEADS)


def _odd_proj_kernel(x_ref, g_ref, w_ref, q_ref, k_ref, v_ref, *, tiles_per_seq):
    tm = x_ref.shape[0]
    i = pl.program_id(0)
    h = _rms(x_ref[...], g_ref[...]).astype(BF16)
    scale = DIFF_HEAD_DIM ** -0.5 * LOG2E
    q_ref[...] = (jnp.dot(h, w_ref[:, :HEAD_COLS], preferred_element_type=F32) * scale).astype(BF16)
    k = jnp.dot(h, w_ref[:, HEAD_COLS:2 * HEAD_COLS], preferred_element_type=F32)
    v = jnp.dot(h, w_ref[:, 2 * HEAD_COLS:], preferred_element_type=F32)
    pos = ((i % tiles_per_seq) * tm + lax.broadcasted_iota(jnp.int32, (tm, 1), 0)).astype(F32)
    lane = lax.broadcasted_iota(jnp.int32, (tm, LANES), 1)
    for hd in range(HEADS):
        sl = slice(hd * LANES, (hd + 1) * LANES)
        b_hi, b_mid, b_lo = (p.astype(F32) for p in _split3(pos * (_alibi_slope(hd) * LOG2E)))
        aug = jnp.where(lane == 0, b_hi, jnp.where(lane == 1, b_mid, jnp.where(lane == 2, b_lo, 0.0)))
        k_ref[:, 2 * hd * LANES:(2 * hd + 1) * LANES] = k[:, sl].astype(BF16)
        k_ref[:, (2 * hd + 1) * LANES:(2 * hd + 2) * LANES] = aug.astype(BF16)
    _store_vt(v_ref, v, DIFF_V_DIM)


def _odd_proj(x, g, w_in, bsz, seq):
    t = x.shape[0]
    tm, tk = PROJ_ROWS, ATTN_KEYS
    w = w_in.astype(BF16)
    tiles_per_seq = seq // tm
    return pl.pallas_call(
        functools.partial(_odd_proj_kernel, tiles_per_seq=tiles_per_seq),
        grid=(t // tm,),
        in_specs=[pl.BlockSpec((tm, D_MODEL), lambda i: (i, 0)),
                  _const_spec((1, D_MODEL)), _const_spec(w.shape)],
        out_specs=[pl.BlockSpec((tm, HEAD_COLS), lambda i: (i, 0)),
                   pl.BlockSpec((tm, 2 * HEAD_COLS), lambda i: (i, 0)),
                   _vt_spec(DIFF_V_DIM, tm, tk, tiles_per_seq)],
        out_shape=[jax.ShapeDtypeStruct((t, HEAD_COLS), BF16),
                   jax.ShapeDtypeStruct((t, 2 * HEAD_COLS), BF16),
                   jax.ShapeDtypeStruct((bsz, HEADS, seq // tk, _vt_rows(DIFF_V_DIM), tk), BF16)],
        compiler_params=_params("arbitrary"),
        name="odd_proj",
    )(x, g.reshape(1, -1), w)


_NT = (((1,), (1,)), ((), ()))


def _attend(s, vt, m, acc, key0):
    if key0 is not None:
        key = lax.broadcasted_iota(jnp.int32, s.shape, 0) + key0
        qry = lax.broadcasted_iota(jnp.int32, s.shape, 1)
        s = jnp.where(key <= qry, s, NEG_INF)
    m_new = jnp.maximum(m, jnp.max(s, axis=0, keepdims=True))
    p = jnp.exp2(s - m_new).astype(BF16)
    acc = jnp.exp2(m - m_new) * acc + jnp.dot(vt, p, preferred_element_type=F32)
    return m_new, acc


def _stream_attention(qk, vts, s_ref, qi, nq, rows, n):
    tk = s_ref.shape[0]
    r = nq // tk
    s_ref[...] = qk(0, 0)

    def sweep(j, carry, key0, last):
        carry = list(carry)
        s_cur = s_ref[...]
        for k in range(n):
            if k + 1 < n:
                s_next = qk(k + 1, j)
            elif not last:
                s_next = qk(0, j + 1)
            m, acc = _attend(s_cur, vts(k, j), carry[2 * k], carry[2 * k + 1], key0)
            carry[2 * k], carry[2 * k + 1] = m, acc
            if k + 1 == n and not last:
                s_ref[...] = s_next
            s_cur = s_next
        return tuple(carry)

    def body(jj, carry):
        for u in range(r):
            carry = sweep(jj * r + u, carry, None, False)
        return carry

    init = (jnp.full((1, nq), NEG_INF, F32), jnp.zeros((rows, nq), F32)) * n
    carry = lax.fori_loop(0, qi, body, init)
    for u in range(r):
        carry = sweep(qi * r + u, carry, u * tk, u == r - 1)
    return carry[1::2]


def _flash_kernel(q_ref, k_ref, vt_ref, o_ref, s_ref):
    tq = q_ref.shape[1]
    n, rows, tk = vt_ref.shape[1], vt_ref.shape[3], vt_ref.shape[4]
    dv = rows - BF16_ROWS
    qs = [q_ref[0, :, hd * LANES:(hd + 1) * LANES] for hd in range(n)]

    def qk(hd, j):
        start = pl.multiple_of(j * tk, tk)
        kc = k_ref[0, pl.ds(start, tk), hd * LANES:(hd + 1) * LANES]
        return lax.dot_general(kc, qs[hd], _NT, preferred_element_type=F32)

    accs = _stream_attention(qk, lambda hd, j: vt_ref[0, hd, j], s_ref, pl.program_id(2),
                             tq, rows, n)
    o_t = jnp.concatenate([a[:dv] / a[dv:dv + 1] for a in accs], axis=0)
    o_ref[0] = o_t.T.astype(BF16)


def _flash_attention(q, k, vt, tq):
    b, s, _ = q.shape
    _, _, chunks, rows, tk = vt.shape
    assert tq % tk == 0
    n = FLASH_HEADS_PER_STEP
    dv = rows - BF16_ROWS
    return pl.pallas_call(
        _flash_kernel,
        grid=(b, HEADS // n, s // tq),
        in_specs=[pl.BlockSpec((1, tq, n * LANES), lambda bi, hp, i: (bi, i, hp)),
                  pl.BlockSpec((1, s, n * LANES), lambda bi, hp, i: (bi, 0, hp)),
                  pl.BlockSpec((1, n, chunks, rows, tk), lambda bi, hp, i: (bi, hp, 0, 0, 0))],
        out_specs=pl.BlockSpec((1, tq, n * dv), lambda bi, hp, i: (bi, i, hp)),
        out_shape=jax.ShapeDtypeStruct((b, s, HEADS * (rows - BF16_ROWS)), BF16),
        scratch_shapes=[pltpu.VMEM((tk, tq), F32)],
        compiler_params=_params("arbitrary", "arbitrary", "arbitrary"),
        name="flash_attention",
    )(q, k, vt)


def _diff_kernel(q_ref, k_ref, vt_ref, lq1_ref, lk1_ref, lq2_ref, lk2_ref, sub_ref, o_ref, s_ref,
                 *, lambda_init):
    tq = q_ref.shape[1]
    n, rows, tk = vt_ref.shape[1], vt_ref.shape[3], vt_ref.shape[4]
    dv = rows - BF16_ROWS
    lane = lax.broadcasted_iota(jnp.int32, (tq, LANES), 1)
    aug = jnp.where(lane < 3, 1.0, 0.0).astype(BF16)
    qs = []
    for hd in range(n):
        q = q_ref[0, :, hd * LANES:(hd + 1) * LANES]
        zero = jnp.zeros_like(q)
        qs.append(jnp.concatenate([jnp.where(lane < DIFF_HEAD_DIM, q, zero), aug], axis=1))
        qs.append(jnp.concatenate([jnp.where(lane >= DIFF_HEAD_DIM, q, zero), aug], axis=1))

    def qk(st, j):
        hd = st // 2
        start = pl.multiple_of(j * tk, tk)
        kc = k_ref[0, pl.ds(start, tk), 2 * hd * LANES:2 * (hd + 1) * LANES]
        return lax.dot_general(kc, qs[st], _NT, preferred_element_type=F32)

    accs = _stream_attention(qk, lambda st, j: vt_ref[0, st // 2, j], s_ref, pl.program_id(2),
                             tq, rows, 2 * n)
    lam = (jnp.exp(jnp.sum(lq1_ref[...] * lk1_ref[...], axis=1, keepdims=True))
           - jnp.exp(jnp.sum(lq2_ref[...] * lk2_ref[...], axis=1, keepdims=True)) + lambda_init)
    for hd in range(n):
        a1, a2 = accs[2 * hd], accs[2 * hd + 1]
        o = (a1[:dv] / a1[dv:dv + 1] - lam * (a2[:dv] / a2[dv:dv + 1])).T
        o_ref[0, :, hd * dv:(hd + 1) * dv] = (
            _rms(o, sub_ref[...]) * (1.0 - lambda_init)).astype(BF16)


def _diff_attention(q, k, vt, lq1, lk1, lq2, lk2, subln, lambda_init, tq):
    b, s, _ = q.shape
    _, _, chunks, rows, tk = vt.shape
    assert tq % tk == 0
    vec = lambda a: a.reshape(1, -1)
    n = DIFF_HEADS_PER_STEP
    return pl.pallas_call(
        functools.partial(_diff_kernel, lambda_init=lambda_init),
        grid=(b, HEADS // n, s // tq),
        in_specs=[pl.BlockSpec((1, tq, n * LANES), lambda bi, hd, i: (bi, i, hd)),
                  pl.BlockSpec((1, s, 2 * n * LANES), lambda bi, hd, i: (bi, 0, hd)),
                  pl.BlockSpec((1, n, chunks, rows, tk), lambda bi, hd, i: (bi, hd, 0, 0, 0)),
                  _const_spec((1, DIFF_HEAD_DIM)), _const_spec((1, DIFF_HEAD_DIM)),
                  _const_spec((1, DIFF_HEAD_DIM)), _const_spec((1, DIFF_HEAD_DIM)),
                  _const_spec((1, DIFF_V_DIM))],
        out_specs=pl.BlockSpec((1, tq, n * LANES), lambda bi, hd, i: (bi, i, hd)),
        out_shape=jax.ShapeDtypeStruct(q.shape, BF16),
        scratch_shapes=[pltpu.VMEM((tk, tq), F32)],
        compiler_params=_params("arbitrary", "arbitrary", "arbitrary"),
        name="diff_attention",
    )(q, k, vt, vec(lq1), vec(lk1), vec(lq2), vec(lk2), vec(subln))


def _out_proj_kernel(*refs, n_in):
    a_refs, w_refs = refs[:n_in], refs[n_in:2 * n_in]
    g_ref, x_ref, o_ref = refs[2 * n_in:]
    m = jnp.dot(a_refs[0][...], w_refs[0][...], preferred_element_type=F32)
    for a_ref, w_ref in zip(a_refs[1:], w_refs[1:]):
        m = m + jnp.dot(a_ref[...], w_ref[...], preferred_element_type=F32)
    o_ref[...] = x_ref[...] + _rms(m, g_ref[...])


def _out_proj(acts, weights, g, x):
    t = x.shape[0]
    tm = OUT_ROWS
    n_in = len(acts)
    x_spec = pl.BlockSpec((tm, D_MODEL), lambda i: (i, 0))
    return pl.pallas_call(
        functools.partial(_out_proj_kernel, n_in=n_in),
        grid=(t // tm,),
        in_specs=([pl.BlockSpec((tm, a.shape[1]), lambda i: (i, 0)) for a in acts]
                  + [_const_spec(w.shape) for w in weights]
                  + [_const_spec((1, D_MODEL)), x_spec]),
        out_specs=x_spec,
        out_shape=jax.ShapeDtypeStruct(x.shape, F32),
        compiler_params=_params("arbitrary"),
        name="out_proj",
    )(*acts, *weights, g.reshape(1, -1), x)


_N_FFN_CHUNKS = D_FF // FFN_CHUNK
_GELU_C = math.sqrt(2.0 / math.pi)


def _ffn_kernel(x_ref, gpre_ref, wup_ref, cw_ref, wdn_ref, gpost_ref, o_ref,
                h_ref, hist_ref, gbuf_ref, gva_ref, gvb_ref, mid_ref, *, tiles_per_seq):
    tm = x_ref.shape[0]
    nc = _N_FFN_CHUNKS
    i = pl.program_id(0)

    @pl.when(i % tiles_per_seq == 0)
    def _():
        hist_ref[...] = jnp.zeros_like(hist_ref)

    h_ref[...] = _rms(x_ref[...], gpre_ref[...]).astype(BF16)

    def up(c):
        return jnp.dot(h_ref[...], wup_ref[c], preferred_element_type=F32)

    def gated(c, gv_ref):
        gate, val = gv_ref[:, :FFN_CHUNK], gv_ref[:, FFN_CHUNK:]
        gbuf_ref[0:SUBLANES, :] = hist_ref[c]
        gbuf_ref[SUBLANES:, :] = gate
        hist_ref[c] = gate[tm - SUBLANES:, :]
        cw = cw_ref[c]
        pre = (gbuf_ref[SUBLANES - 2:SUBLANES - 2 + tm, :] * cw[0:1, :]
               + gbuf_ref[SUBLANES - 1:SUBLANES - 1 + tm, :] * cw[1:2, :]
               + gate * cw[2:3, :] + cw[3:4, :])
        act = 0.5 * pre * (1.0 + jnp.tanh(_GELU_C * (pre + 0.044715 * (pre * pre * pre))))
        mid_ref[c] = (act * val).astype(BF16)

    assert nc % 2 == 1
    gva_ref[...] = up(0)

    def body(k, _):
        gated(2 * k, gva_ref)
        gvb_ref[...] = up(2 * k + 1)
        gated(2 * k + 1, gvb_ref)
        gva_ref[...] = up(2 * k + 2)
        return 0

    lax.fori_loop(0, nc // 2, body, 0)
    gated(nc - 1, gva_ref)
    f = jnp.dot(mid_ref[0], wdn_ref[0], preferred_element_type=F32)
    for c in range(1, nc):
        f = f + jnp.dot(mid_ref[c], wdn_ref[c], preferred_element_type=F32)
    o_ref[...] = x_ref[...] + _rms(f, gpost_ref[...])


def _ffn(x, g_pre, w_up, conv_w, conv_b, w_down, g_post, seq):
    t = x.shape[0]
    tm = FFN_ROWS
    nc = _N_FFN_CHUNKS
    wg = w_up[:, :D_FF].reshape(D_MODEL, nc, FFN_CHUNK)
    wv = w_up[:, D_FF:].reshape(D_MODEL, nc, FFN_CHUNK)
    wup = jnp.concatenate([wg, wv], axis=2).transpose(1, 0, 2).astype(BF16)
    wdn = w_down.reshape(nc, FFN_CHUNK, D_MODEL).astype(BF16)
    cw = jnp.concatenate([conv_w, conv_b[None, :],
                          jnp.zeros((SUBLANES - CONV_WIDTH - 1, D_FF), F32)], axis=0)
    cw = cw.reshape(SUBLANES, nc, FFN_CHUNK).transpose(1, 0, 2)
    x_spec = pl.BlockSpec((tm, D_MODEL), lambda i: (i, 0))
    return pl.pallas_call(
        functools.partial(_ffn_kernel, tiles_per_seq=seq // tm),
        grid=(t // tm,),
        in_specs=[x_spec, _const_spec((1, D_MODEL)), _const_spec(wup.shape), _const_spec(cw.shape),
                  _const_spec(wdn.shape), _const_spec((1, D_MODEL))],
        out_specs=x_spec,
        out_shape=jax.ShapeDtypeStruct(x.shape, F32),
        scratch_shapes=[pltpu.VMEM((tm, D_MODEL), BF16),
                        pltpu.VMEM((nc, SUBLANES, FFN_CHUNK), F32),
                        pltpu.VMEM((tm + SUBLANES, FFN_CHUNK), F32),
                        pltpu.VMEM((tm, 2 * FFN_CHUNK), F32),
                        pltpu.VMEM((tm, 2 * FFN_CHUNK), F32),
                        pltpu.VMEM((nc, tm, FFN_CHUNK), BF16)],
        compiler_params=_params("arbitrary"),
        name="conv_ffn",
    )(x, g_pre.reshape(1, -1), wup, cw, wdn, g_post.reshape(1, -1))


def kernel(x, norm_mix_pre, norm_mix_post, norm_ffn_pre, norm_ffn_post, even_w_in, even_b_forget, even_q_norm, even_w_uq, even_kv_norm, even_w_ukv, even_w_out, odd_w_in, odd_lambda_q1, odd_lambda_k1, odd_lambda_q2, odd_lambda_k2, odd_subln, odd_w_out, ffn_w_up, ffn_conv_w, ffn_conv_b, ffn_w_down):
    bsz, seq, d = x.shape
    assert d == D_MODEL and seq % ATTN_TILE == 0 and seq % FFN_ROWS == 0
    t = bsz * seq
    tq = ATTN_TILE
    rope = _rope_tables(seq)
    xf = x.reshape(t, d)
    for layer in range(DEPTH):
        i = layer // 2
        if layer % 2 == 0:
            fq, fk, fvt, mq, mk, mvt = _even_proj(
                xf, norm_mix_pre[layer], even_w_in[i], even_b_forget[i], even_q_norm[i],
                even_w_uq[i], even_kv_norm[i], even_w_ukv[i], rope, bsz, seq)
            shp = (bsz, seq, HEAD_COLS)
            fo = _flash_attention(fq.reshape(shp), fk.reshape(shp), fvt, tq)
            mo = _flash_attention(mq.reshape(shp), mk.reshape(shp), mvt, tq)
            w_out = even_w_out[i].astype(BF16)
            xf = _out_proj([fo.reshape(t, -1), mo.reshape(t, -1)],
                           [w_out[:FOX_WIDTH], w_out[FOX_WIDTH:]], norm_mix_post[layer], xf)
        else:
            lambda_init = 0.8 - 0.6 * math.exp(-0.3 * layer)
            q, k, vt = _odd_proj(xf, norm_mix_pre[layer], odd_w_in[i], bsz, seq)
            o = _diff_attention(q.reshape(bsz, seq, -1), k.reshape(bsz, seq, -1), vt,
                                odd_lambda_q1[i], odd_lambda_k1[i],
                                odd_lambda_q2[i], odd_lambda_k2[i], odd_subln[i], lambda_init, tq)
            xf = _out_proj([o.reshape(t, -1)], [odd_w_out[i].astype(BF16)],
                           norm_mix_post[layer], xf)
        xf = _ffn(xf, norm_ffn_pre[layer], ffn_w_up[layer], ffn_conv_w[layer], ffn_conv_b[layer],
                  ffn_w_down[layer], norm_ffn_post[layer], seq)
    return xf.reshape(bsz, seq, d)
```

```python
import functools
import math

import numpy as np
import jax
import jax.numpy as jnp
from jax import lax
from jax.experimental import pallas as pl
from jax.experimental.pallas import tpu as pltpu

D_MODEL = 1024
DEPTH = 4
RMS_EPS = 1e-6
NEG_INF = -1e30
HEADS = 8
FOX_HEAD_DIM = 64
FOX_WIDTH = HEADS * FOX_HEAD_DIM
MLA_NOPE_DIM = 64
MLA_ROPE_DIM = 32
MLA_V_DIM = 64
MLA_Q_RANK = 384
MLA_KV_RANK = 256
ROPE_THETA = 10000.0
DIFF_HEAD_DIM = 64
DIFF_V_DIM = 2 * DIFF_HEAD_DIM
D_FF = 2816
CONV_WIDTH = 3

LANES = 128
SUBLANES = 8
BF16_ROWS = 16
LOG2E = 1.4426950408889634
HEAD_COLS = HEADS * LANES
VMEM_LIMIT_BYTES = 56 * 1024 * 1024

PROJ_ROWS = 512
FFN_ROWS = 512
FFN_CHUNK = 256
ATTN_TILE = 512
ATTN_KEYS = 512
FLASH_HEADS_PER_STEP = 8
DIFF_HEADS_PER_STEP = 4
F32 = jnp.float32
BF16 = jnp.bfloat16


def _params(*sem, flags=None):
    return pltpu.CompilerParams(dimension_semantics=sem, vmem_limit_bytes=VMEM_LIMIT_BYTES,
                                flags=flags)


def _const_spec(shape):
    nd = len(shape)
    return pl.BlockSpec(shape, lambda *_: (0,) * nd, pipeline_mode=pl.Buffered(1))


def _rms(x, g):
    return x * lax.rsqrt(jnp.mean(x * x, axis=-1, keepdims=True) + RMS_EPS) * g


def _split3(x):
    hi = x.astype(BF16)
    r1 = x - hi.astype(F32)
    mid = r1.astype(BF16)
    lo = (r1 - mid.astype(F32)).astype(BF16)
    return hi, mid, lo


def _rope(blk, ra, rb, rc):
    return blk * ra + pltpu.roll(blk, 16, 1) * rb + pltpu.roll(blk, LANES - 16, 1) * rc


def _vt_rows(dv):
    return dv + BF16_ROWS


def _store_vt(vt_ref, v, dv):
    tm = v.shape[0]
    per = LANES // dv
    for blk in range(v.shape[1] // LANES):
        vt = v[:, blk * LANES:(blk + 1) * LANES].T.astype(BF16)
        for s in range(per):
            vt_ref[0, blk * per + s, 0, 0:dv, :] = vt[s * dv:(s + 1) * dv, :]
    row = lax.broadcasted_iota(jnp.int32, (BF16_ROWS, tm), 0)
    ones_row = jnp.where(row == 0, 1.0, 0.0).astype(BF16)
    for hd in range(HEADS):
        vt_ref[0, hd, 0, dv:dv + BF16_ROWS, :] = ones_row


def _vt_spec(dv, tm, tk, tiles_per_seq):
    per_chunk = tk // tm
    return pl.BlockSpec(
        (1, HEADS, 1, _vt_rows(dv), tm),
        lambda i: (i // tiles_per_seq, 0, (i % tiles_per_seq) // per_chunk, 0,
                   (i % tiles_per_seq) % per_chunk))


_EV_Q, _EV_K, _EV_V = 0, FOX_WIDTH, 2 * FOX_WIDTH
_EV_G = _EV_V + FOX_WIDTH
_EV_CQ = _EV_G + LANES
_EV_CKV = _EV_CQ + MLA_Q_RANK
_EV_KR = _EV_CKV + MLA_KV_RANK
_EV_WIDTH = _EV_KR + LANES


def _even_proj_kernel(x_ref, g_ref, w_ref, bf_ref, qn_ref, wuq_ref, kvn_ref, wuk_ref, wuv_ref,
                      ec_ref, ra_ref, rb_ref, rc_ref, cst_ref,
                      fq_ref, fk_ref, fv_ref, mq_ref, mk_ref, mv_ref, carry_ref,
                      *, tiles_per_seq):
    tm = x_ref.shape[0]
    i = pl.program_id(0)

    @pl.when(i % tiles_per_seq == 0)
    def _():
        carry_ref[...] = jnp.zeros_like(carry_ref)

    h = _rms(x_ref[...], g_ref[...]).astype(BF16)
    q_aug = cst_ref[0:1, :]

    def proj(lo, hi):
        return jnp.dot(h, w_ref[:, lo:hi], preferred_element_type=F32)

    lane = lax.broadcasted_iota(jnp.int32, (tm, LANES), 1)

    def spread_heads(o_ref, packed, aug):
        for pair in range(HEADS // 2):
            blk = packed[:, pair * LANES:(pair + 1) * LANES]
            for odd, src in enumerate((blk, pltpu.roll(blk, FOX_HEAD_DIM, 1))):
                sl = slice((2 * pair + odd) * LANES, (2 * pair + odd + 1) * LANES)
                o_ref[:, sl] = jnp.where(lane < FOX_HEAD_DIM, src, aug[:, sl]).astype(BF16)

    fox_scale = FOX_HEAD_DIM ** -0.5 * LOG2E
    spread_heads(fq_ref, proj(_EV_Q, _EV_K) * fox_scale, q_aug)
    _store_vt(fv_ref, proj(_EV_V, _EV_G), FOX_HEAD_DIM)

    z = proj(_EV_G, _EV_CQ) + bf_ref[...]
    logf = (jnp.minimum(z, 0.0) - jnp.log1p(jnp.exp(-jnp.abs(z)))) * LOG2E
    row = lax.broadcasted_iota(jnp.int32, (tm, tm), 0)
    col = lax.broadcasted_iota(jnp.int32, (tm, tm), 1)
    tril = jnp.where(row >= col, 1.0, 0.0).astype(BF16)
    parts = jnp.dot(tril, jnp.concatenate(_split3(logf), axis=1), preferred_element_type=F32)
    c = (parts[:, :LANES] + parts[:, LANES:2 * LANES] + parts[:, 2 * LANES:]
         + carry_ref[SUBLANES - 1:SUBLANES, :])
    carry_ref[...] = c[tm - SUBLANES:, :]
    c_aug = jnp.dot(jnp.concatenate(_split3(c), axis=1), ec_ref[...], preferred_element_type=F32)
    spread_heads(fk_ref, proj(_EV_K, _EV_V), c_aug)

    ra, rb, rc = ra_ref[...], rb_ref[...], rc_ref[...]

    cq = _rms(proj(_EV_CQ, _EV_CKV), qn_ref[...]).astype(BF16)
    mla_scale = (MLA_NOPE_DIM + MLA_ROPE_DIM) ** -0.5 * LOG2E
    q = jnp.dot(cq, wuq_ref[...], preferred_element_type=F32) * mla_scale
    for hd in range(HEADS):
        sl = slice(hd * LANES, (hd + 1) * LANES)
        mq_ref[:, sl] = _rope(q[:, sl], ra, rb, rc).astype(BF16)

    ckv = _rms(proj(_EV_CKV, _EV_KR), kvn_ref[...]).astype(BF16)
    kr = _rope(proj(_EV_KR, _EV_WIDTH), ra, rb, rc)
    kn = jnp.dot(ckv, wuk_ref[...], preferred_element_type=F32)
    for hd in range(HEADS):
        sl = slice(hd * LANES, (hd + 1) * LANES)
        mk_ref[:, sl] = (kn[:, sl] + kr).astype(BF16)
    _store_vt(mv_ref, jnp.dot(ckv, wuv_ref[...], preferred_element_type=F32), MLA_V_DIM)


def _pad_heads(w, dh):
    k = w.shape[0]
    w = w.reshape(k, HEADS, dh)
    return jnp.pad(w, ((0, 0), (0, 0), (0, LANES - dh))).reshape(k, HEAD_COLS)


def _even_consts():
    cst = np.zeros((SUBLANES, HEAD_COLS), np.float32)
    ec = np.zeros((3 * LANES, HEAD_COLS), np.float32)
    for hd in range(HEADS):
        cst[0, hd * LANES + 64: hd * LANES + 67] = 1.0
        for p in range(3):
            ec[p * LANES + hd, hd * LANES + 64 + p] = -1.0
    return jnp.asarray(cst), jnp.asarray(ec, dtype=BF16)


def _rope_tables(seq):
    half = MLA_ROPE_DIM // 2
    inv = ROPE_THETA ** (-jnp.arange(0, MLA_ROPE_DIM, 2, dtype=F32) / MLA_ROPE_DIM)
    ang = jnp.arange(seq, dtype=F32)[:, None] * inv[None, :]
    cos, sin = jnp.cos(ang), jnp.sin(ang)
    zeros = jnp.zeros((seq, half), F32)
    ra = jnp.concatenate([jnp.ones((seq, 64), F32), cos, cos, jnp.zeros((seq, 32), F32)], axis=1)
    rb = jnp.concatenate([jnp.zeros((seq, 64), F32), zeros, sin, jnp.zeros((seq, 32), F32)], axis=1)
    rc = jnp.concatenate([jnp.zeros((seq, 64), F32), -sin, zeros, jnp.zeros((seq, 32), F32)], axis=1)
    return ra, rb, rc


def _even_proj(x, g, w_in, b_forget, q_norm, w_uq, kv_norm, w_ukv, rope, bsz, seq):
    t = x.shape[0]
    tm, tk = PROJ_ROWS, ATTN_KEYS
    cuts = np.cumsum([FOX_WIDTH, FOX_WIDTH, FOX_WIDTH, HEADS, MLA_Q_RANK, MLA_KV_RANK]).tolist()
    wfq, wfk, wfv, wg, wcq, wckv, wkr = jnp.split(w_in, cuts, axis=1)
    w1 = jnp.concatenate([
        wfq, wfk, wfv,
        jnp.pad(wg, ((0, 0), (0, LANES - HEADS))), wcq, wckv,
        jnp.pad(wkr, ((0, 0), (64, LANES - 64 - MLA_ROPE_DIM)))], axis=1).astype(BF16)
    assert w1.shape[1] == _EV_WIDTH
    bf = jnp.pad(b_forget, (0, LANES - HEADS)).reshape(1, LANES)
    wuq = _pad_heads(w_uq, MLA_NOPE_DIM + MLA_ROPE_DIM).astype(BF16)
    wukv = w_ukv.reshape(MLA_KV_RANK, HEADS, MLA_NOPE_DIM + MLA_V_DIM)
    wuk = _pad_heads(wukv[:, :, :MLA_NOPE_DIM].reshape(MLA_KV_RANK, -1), MLA_NOPE_DIM).astype(BF16)
    wuv = wukv[:, :, MLA_NOPE_DIM:].reshape(MLA_KV_RANK, -1).astype(BF16)
    cst, ec = _even_consts()
    ra, rb, rc = rope
    tiles_per_seq = seq // tm
    rope_spec = pl.BlockSpec((tm, LANES), lambda i: (i % tiles_per_seq, 0))
    row_spec = pl.BlockSpec((tm, HEAD_COLS), lambda i: (i, 0))
    out_sds = jax.ShapeDtypeStruct((t, HEAD_COLS), BF16)
    vt_spec = _vt_spec(FOX_HEAD_DIM, tm, tk, tiles_per_seq)
    vt_sds = jax.ShapeDtypeStruct((bsz, HEADS, seq // tk, _vt_rows(FOX_HEAD_DIM), tk), BF16)
    return pl.pallas_call(
        functools.partial(_even_proj_kernel, tiles_per_seq=tiles_per_seq),
        grid=(t // tm,),
        in_specs=[pl.BlockSpec((tm, D_MODEL), lambda i: (i, 0)),
                  _const_spec((1, D_MODEL)), _const_spec(w1.shape), _const_spec((1, LANES)),
                  _const_spec((1, MLA_Q_RANK)), _const_spec(wuq.shape),
                  _const_spec((1, MLA_KV_RANK)), _const_spec(wuk.shape), _const_spec(wuv.shape),
                  _const_spec(ec.shape), rope_spec, rope_spec, rope_spec, _const_spec(cst.shape)],
        out_specs=[row_spec, row_spec, vt_spec, row_spec, row_spec, vt_spec],
        out_shape=[out_sds, out_sds, vt_sds, out_sds, out_sds, vt_sds],
        scratch_shapes=[pltpu.VMEM((SUBLANES, LANES), F32)],
        compiler_params=_params("arbitrary"),
        name="even_proj",
    )(x, g.reshape(1, -1), w1, bf, q_norm.reshape(1, -1), wuq, kv_norm.reshape(1, -1), wuk, wuv,
      ec, ra, rb, rc, cst)


def _alibi_slope(hd):
    return 2.0 ** (-8.0 * (hd + 1) / HEADS)


def _odd_proj_kernel(x_ref, g_ref, w_ref, q_ref, k_ref, v_ref, *, tiles_per_seq):
    tm = x_ref.shape[0]
    i = pl.program_id(0)
    h = _rms(x_ref[...], g_ref[...]).astype(BF16)
    scale = DIFF_HEAD_DIM ** -0.5 * LOG2E
    q_ref[...] = (jnp.dot(h, w_ref[:, :HEAD_COLS], preferred_element_type=F32) * scale).astype(BF16)
    k = jnp.dot(h, w_ref[:, HEAD_COLS:2 * HEAD_COLS], preferred_element_type=F32)
    v = jnp.dot(h, w_ref[:, 2 * HEAD_COLS:], preferred_element_type=F32)
    pos = ((i % tiles_per_seq) * tm + lax.broadcasted_iota(jnp.int32, (tm, 1), 0)).astype(F32)
    lane = lax.broadcasted_iota(jnp.int32, (tm, LANES), 1)
    for hd in range(HEADS):
        sl = slice(hd * LANES, (hd + 1) * LANES)
        b_hi, b_mid, b_lo = (p.astype(F32) for p in _split3(pos * (_alibi_slope(hd) * LOG2E)))
        aug = jnp.where(lane == 0, b_hi, jnp.where(lane == 1, b_mid, jnp.where(lane == 2, b_lo, 0.0)))
        k_ref[:, 2 * hd * LANES:(2 * hd + 1) * LANES] = k[:, sl].astype(BF16)
        k_ref[:, (2 * hd + 1) * LANES:(2 * hd + 2) * LANES] = aug.astype(BF16)
    _store_vt(v_ref, v, DIFF_V_DIM)


def _odd_proj(x, g, w_in, bsz, seq):
    t = x.shape[0]
    tm, tk = PROJ_ROWS, ATTN_KEYS
    w = w_in.astype(BF16)
    tiles_per_seq = seq // tm
    return pl.pallas_call(
        functools.partial(_odd_proj_kernel, tiles_per_seq=tiles_per_seq),
        grid=(t // tm,),
        in_specs=[pl.BlockSpec((tm, D_MODEL), lambda i: (i, 0)),
                  _const_spec((1, D_MODEL)), _const_spec(w.shape)],
        out_specs=[pl.BlockSpec((tm, HEAD_COLS), lambda i: (i, 0)),
                   pl.BlockSpec((tm, 2 * HEAD_COLS), lambda i: (i, 0)),
                   _vt_spec(DIFF_V_DIM, tm, tk, tiles_per_seq)],
        out_shape=[jax.ShapeDtypeStruct((t, HEAD_COLS), BF16),
                   jax.ShapeDtypeStruct((t, 2 * HEAD_COLS), BF16),
                   jax.ShapeDtypeStruct((bsz, HEADS, seq // tk, _vt_rows(DIFF_V_DIM), tk), BF16)],
        compiler_params=_params("arbitrary"),
        name="odd_proj",
    )(x, g.reshape(1, -1), w)


_NT = (((1,), (1,)), ((), ()))


def _attend(s, vt, m, acc, key0):
    if key0 is not None:
        key = lax.broadcasted_iota(jnp.int32, s.shape, 0) + key0
        qry = lax.broadcasted_iota(jnp.int32, s.shape, 1)
        s = jnp.where(key <= qry, s, NEG_INF)
    m_new = jnp.maximum(m, jnp.max(s, axis=0, keepdims=True))
    p = jnp.exp2(s - m_new).astype(BF16)
    acc = jnp.exp2(m - m_new) * acc + jnp.dot(vt, p, preferred_element_type=F32)
    return m_new, acc


def _stream_attention(qk, vts, s_ref, qi, nq, rows, n):
    tk = s_ref.shape[0]
    r = nq // tk
    s_ref[...] = qk(0, 0)

    def sweep(j, carry, key0, last):
        carry = list(carry)
        s_cur = s_ref[...]
        for k in range(n):
            if k + 1 < n:
                s_next = qk(k + 1, j)
            elif not last:
                s_next = qk(0, j + 1)
            m, acc = _attend(s_cur, vts(k, j), carry[2 * k], carry[2 * k + 1], key0)
            carry[2 * k], carry[2 * k + 1] = m, acc
            if k + 1 == n and not last:
                s_ref[...] = s_next
            s_cur = s_next
        return tuple(carry)

    def body(jj, carry):
        for u in range(r):
            carry = sweep(jj * r + u, carry, None, False)
        return carry

    init = (jnp.full((1, nq), NEG_INF, F32), jnp.zeros((rows, nq), F32)) * n
    carry = lax.fori_loop(0, qi, body, init)
    for u in range(r):
        carry = sweep(qi * r + u, carry, u * tk, u == r - 1)
    return carry[1::2]


def _flash_kernel(q_ref, k_ref, vt_ref, o_ref, s_ref):
    tq = q_ref.shape[1]
    n, rows, tk = vt_ref.shape[1], vt_ref.shape[3], vt_ref.shape[4]
    dv = rows - BF16_ROWS
    qs = [q_ref[0, :, hd * LANES:(hd + 1) * LANES] for hd in range(n)]

    def qk(hd, j):
        start = pl.multiple_of(j * tk, tk)
        kc = k_ref[0, pl.ds(start, tk), hd * LANES:(hd + 1) * LANES]
        return lax.dot_general(kc, qs[hd], _NT, preferred_element_type=F32)

    accs = _stream_attention(qk, lambda hd, j: vt_ref[0, hd, j], s_ref, pl.program_id(2),
                             tq, rows, n)
    o_t = jnp.concatenate([a[:dv] / a[dv:dv + 1] for a in accs], axis=0)
    o_ref[0] = o_t.T.astype(BF16)


def _flash_attention(q, k, vt, tq):
    b, s, _ = q.shape
    _, _, chunks, rows, tk = vt.shape
    assert tq % tk == 0
    n = FLASH_HEADS_PER_STEP
    dv = rows - BF16_ROWS
    return pl.pallas_call(
        _flash_kernel,
        grid=(b, HEADS // n, s // tq),
        in_specs=[pl.BlockSpec((1, tq, n * LANES), lambda bi, hp, i: (bi, i, hp)),
                  pl.BlockSpec((1, s, n * LANES), lambda bi, hp, i: (bi, 0, hp)),
                  pl.BlockSpec((1, n, chunks, rows, tk), lambda bi, hp, i: (bi, hp, 0, 0, 0))],
        out_specs=pl.BlockSpec((1, tq, n * dv), lambda bi, hp, i: (bi, i, hp)),
        out_shape=jax.ShapeDtypeStruct((b, s, HEADS * (rows - BF16_ROWS)), BF16),
        scratch_shapes=[pltpu.VMEM((tk, tq), F32)],
        compiler_params=_params("arbitrary", "arbitrary", "arbitrary"),
        name="flash_attention",
    )(q, k, vt)


def _diff_kernel(q_ref, k_ref, vt_ref, lq1_ref, lk1_ref, lq2_ref, lk2_ref, sub_ref, o_ref, s_ref,
                 *, lambda_init):
    tq = q_ref.shape[1]
    n, rows, tk = vt_ref.shape[1], vt_ref.shape[3], vt_ref.shape[4]
    dv = rows - BF16_ROWS
    lane = lax.broadcasted_iota(jnp.int32, (tq, LANES), 1)
    aug = jnp.where(lane < 3, 1.0, 0.0).astype(BF16)
    qs = []
    for hd in range(n):
        q = q_ref[0, :, hd * LANES:(hd + 1) * LANES]
        zero = jnp.zeros_like(q)
        qs.append(jnp.concatenate([jnp.where(lane < DIFF_HEAD_DIM, q, zero), aug], axis=1))
        qs.append(jnp.concatenate([jnp.where(lane >= DIFF_HEAD_DIM, q, zero), aug], axis=1))

    def qk(st, j):
        hd = st // 2
        start = pl.multiple_of(j * tk, tk)
        kc = k_ref[0, pl.ds(start, tk), 2 * hd * LANES:2 * (hd + 1) * LANES]
        return lax.dot_general(kc, qs[st], _NT, preferred_element_type=F32)

    accs = _stream_attention(qk, lambda st, j: vt_ref[0, st // 2, j], s_ref, pl.program_id(2),
                             tq, rows, 2 * n)
    lam = (jnp.exp(jnp.sum(lq1_ref[...] * lk1_ref[...], axis=1, keepdims=True))
           - jnp.exp(jnp.sum(lq2_ref[...] * lk2_ref[...], axis=1, keepdims=True)) + lambda_init)
    for hd in range(n):
        a1, a2 = accs[2 * hd], accs[2 * hd + 1]
        o = (a1[:dv] / a1[dv:dv + 1] - lam * (a2[:dv] / a2[dv:dv + 1])).T
        o_ref[0, :, hd * dv:(hd + 1) * dv] = (
            _rms(o, sub_ref[...]) * (1.0 - lambda_init)).astype(BF16)


def _diff_attention(q, k, vt, lq1, lk1, lq2, lk2, subln, lambda_init, tq):
    b, s, _ = q.shape
    _, _, chunks, rows, tk = vt.shape
    assert tq % tk == 0
    vec = lambda a: a.reshape(1, -1)
    n = DIFF_HEADS_PER_STEP
    return pl.pallas_call(
        functools.partial(_diff_kernel, lambda_init=lambda_init),
        grid=(b, HEADS // n, s // tq),
        in_specs=[pl.BlockSpec((1, tq, n * LANES), lambda bi, hd, i: (bi, i, hd)),
                  pl.BlockSpec((1, s, 2 * n * LANES), lambda bi, hd, i: (bi, 0, hd)),
                  pl.BlockSpec((1, n, chunks, rows, tk), lambda bi, hd, i: (bi, hd, 0, 0, 0)),
                  _const_spec((1, DIFF_HEAD_DIM)), _const_spec((1, DIFF_HEAD_DIM)),
                  _const_spec((1, DIFF_HEAD_DIM)), _const_spec((1, DIFF_HEAD_DIM)),
                  _const_spec((1, DIFF_V_DIM))],
        out_specs=pl.BlockSpec((1, tq, n * LANES), lambda bi, hd, i: (bi, i, hd)),
        out_shape=jax.ShapeDtypeStruct(q.shape, BF16),
        scratch_shapes=[pltpu.VMEM((tk, tq), F32)],
        compiler_params=_params("arbitrary", "arbitrary", "arbitrary"),
        name="diff_attention",
    )(q, k, vt, vec(lq1), vec(lk1), vec(lq2), vec(lk2), vec(subln))


_N_FFN_CHUNKS = D_FF // FFN_CHUNK
_GELU_C = math.sqrt(2.0 / math.pi)


def _mix_ffn_kernel(*refs, n_in, tiles_per_seq):
    a_refs, wo_refs = refs[:n_in], refs[n_in:2 * n_in]
    (gmix_ref, x_ref, gpre_ref, wup_ref, cw_ref, wdn_ref, gpost_ref, o_ref,
     x1_ref, h_ref, hist_ref, gbuf_ref, gva_ref, gvb_ref, mid_ref) = refs[2 * n_in:]
    tm = x_ref.shape[0]
    nc = _N_FFN_CHUNKS
    i = pl.program_id(0)

    @pl.when(i % tiles_per_seq == 0)
    def _():
        hist_ref[...] = jnp.zeros_like(hist_ref)

    mix = jnp.dot(a_refs[0][...], wo_refs[0][...], preferred_element_type=F32)
    for a_ref, wo_ref in zip(a_refs[1:], wo_refs[1:]):
        mix = mix + jnp.dot(a_ref[...], wo_ref[...], preferred_element_type=F32)
    x1_ref[...] = x_ref[...] + _rms(mix, gmix_ref[...])
    h_ref[...] = _rms(x1_ref[...], gpre_ref[...]).astype(BF16)

    def up(c):
        return jnp.dot(h_ref[...], wup_ref[c], preferred_element_type=F32)

    def gated(c, gv_ref):
        gate, val = gv_ref[:, :FFN_CHUNK], gv_ref[:, FFN_CHUNK:]
        gbuf_ref[0:SUBLANES, :] = hist_ref[c]
        gbuf_ref[SUBLANES:, :] = gate
        hist_ref[c] = gate[tm - SUBLANES:, :]
        cw = cw_ref[c]
        pre = (gbuf_ref[SUBLANES - 2:SUBLANES - 2 + tm, :] * cw[0:1, :]
               + gbuf_ref[SUBLANES - 1:SUBLANES - 1 + tm, :] * cw[1:2, :]
               + gate * cw[2:3, :] + cw[3:4, :])
        act = 0.5 * pre * (1.0 + jnp.tanh(_GELU_C * (pre + 0.044715 * (pre * pre * pre))))
        mid_ref[c] = (act * val).astype(BF16)

    assert nc % 2 == 1
    gva_ref[...] = up(0)

    def body(k, _):
        gated(2 * k, gva_ref)
        gvb_ref[...] = up(2 * k + 1)
        gated(2 * k + 1, gvb_ref)
        gva_ref[...] = up(2 * k + 2)
        return 0

    lax.fori_loop(0, nc // 2, body, 0)
    gated(nc - 1, gva_ref)
    f = jnp.dot(mid_ref[0], wdn_ref[0], preferred_element_type=F32)
    for c in range(1, nc):
        f = f + jnp.dot(mid_ref[c], wdn_ref[c], preferred_element_type=F32)
    o_ref[...] = x1_ref[...] + _rms(f, gpost_ref[...])


def _mix_ffn(acts, w_outs, g_mix, x, g_pre, w_up, conv_w, conv_b, w_down, g_post, seq):
    t = x.shape[0]
    tm = FFN_ROWS
    nc = _N_FFN_CHUNKS
    n_in = len(acts)
    wg = w_up[:, :D_FF].reshape(D_MODEL, nc, FFN_CHUNK)
    wv = w_up[:, D_FF:].reshape(D_MODEL, nc, FFN_CHUNK)
    wup = jnp.concatenate([wg, wv], axis=2).transpose(1, 0, 2).astype(BF16)
    wdn = w_down.reshape(nc, FFN_CHUNK, D_MODEL).astype(BF16)
    cw = jnp.concatenate([conv_w, conv_b[None, :],
                          jnp.zeros((SUBLANES - CONV_WIDTH - 1, D_FF), F32)], axis=0)
    cw = cw.reshape(SUBLANES, nc, FFN_CHUNK).transpose(1, 0, 2)
    x_spec = pl.BlockSpec((tm, D_MODEL), lambda i: (i, 0))
    vec_spec = _const_spec((1, D_MODEL))
    return pl.pallas_call(
        functools.partial(_mix_ffn_kernel, n_in=n_in, tiles_per_seq=seq // tm),
        grid=(t // tm,),
        in_specs=([pl.BlockSpec((tm, a.shape[1]), lambda i: (i, 0)) for a in acts]
                  + [_const_spec(w.shape) for w in w_outs]
                  + [vec_spec, x_spec, vec_spec, _const_spec(wup.shape), _const_spec(cw.shape),
                     _const_spec(wdn.shape), vec_spec]),
        out_specs=x_spec,
        out_shape=jax.ShapeDtypeStruct(x.shape, F32),
        scratch_shapes=[pltpu.VMEM((tm, D_MODEL), F32),
                        pltpu.VMEM((tm, D_MODEL), BF16),
                        pltpu.VMEM((nc, SUBLANES, FFN_CHUNK), F32),
                        pltpu.VMEM((tm + SUBLANES, FFN_CHUNK), F32),
                        pltpu.VMEM((tm, 2 * FFN_CHUNK), F32),
                        pltpu.VMEM((tm, 2 * FFN_CHUNK), F32),
                        pltpu.VMEM((nc, tm, FFN_CHUNK), BF16)],
        compiler_params=_params("arbitrary"),
        name="mix_ffn",
    )(*acts, *w_outs, g_mix.reshape(1, -1), x, g_pre.reshape(1, -1), wup, cw, wdn,
      g_post.reshape(1, -1))


def kernel(x, norm_mix_pre, norm_mix_post, norm_ffn_pre, norm_ffn_post, even_w_in, even_b_forget, even_q_norm, even_w_uq, even_kv_norm, even_w_ukv, even_w_out, odd_w_in, odd_lambda_q1, odd_lambda_k1, odd_lambda_q2, odd_lambda_k2, odd_subln, odd_w_out, ffn_w_up, ffn_conv_w, ffn_conv_b, ffn_w_down):
    bsz, seq, d = x.shape
    assert d == D_MODEL and seq % ATTN_TILE == 0 and seq % FFN_ROWS == 0
    t = bsz * seq
    tq = ATTN_TILE
    rope = _rope_tables(seq)
    xf = x.reshape(t, d)
    for layer in range(DEPTH):
        i = layer // 2
        if layer % 2 == 0:
            fq, fk, fvt, mq, mk, mvt = _even_proj(
                xf, norm_mix_pre[layer], even_w_in[i], even_b_forget[i], even_q_norm[i],
                even_w_uq[i], even_kv_norm[i], even_w_ukv[i], rope, bsz, seq)
            shp = (bsz, seq, HEAD_COLS)
            fo = _flash_attention(fq.reshape(shp), fk.reshape(shp), fvt, tq)
            mo = _flash_attention(mq.reshape(shp), mk.reshape(shp), mvt, tq)
            w_out = even_w_out[i].astype(BF16)
            acts = [fo.reshape(t, -1), mo.reshape(t, -1)]
            w_outs = [w_out[:FOX_WIDTH], w_out[FOX_WIDTH:]]
        else:
            lambda_init = 0.8 - 0.6 * math.exp(-0.3 * layer)
            q, k, vt = _odd_proj(xf, norm_mix_pre[layer], odd_w_in[i], bsz, seq)
            o = _diff_attention(q.reshape(bsz, seq, -1), k.reshape(bsz, seq, -1), vt,
                                odd_lambda_q1[i], odd_lambda_k1[i],
                                odd_lambda_q2[i], odd_lambda_k2[i], odd_subln[i], lambda_init, tq)
            acts, w_outs = [o.reshape(t, -1)], [odd_w_out[i].astype(BF16)]
        xf = _mix_ffn(acts, w_outs, norm_mix_post[layer], xf, norm_ffn_pre[layer], ffn_w_up[layer],
                      ffn_conv_w[layer], ffn_conv_b[layer], ffn_w_down[layer],
                      norm_ffn_post[layer], seq)
    return xf.reshape(bsz, seq, d)
```

```python
import functools
import math

import numpy as np
import jax
import jax.numpy as jnp
from jax import lax
from jax.experimental import pallas as pl
from jax.experimental.pallas import tpu as pltpu

D_MODEL = 1024
DEPTH = 4
RMS_EPS = 1e-6
NEG_INF = -1e30
HEADS = 8
FOX_HEAD_DIM = 64
FOX_WIDTH = HEADS * FOX_HEAD_DIM
MLA_NOPE_DIM = 64
MLA_ROPE_DIM = 32
MLA_V_DIM = 64
MLA_Q_RANK = 384
MLA_KV_RANK = 256
ROPE_THETA = 10000.0
DIFF_HEAD_DIM = 64
DIFF_V_DIM = 2 * DIFF_HEAD_DIM
D_FF = 2816
CONV_WIDTH = 3

LANES = 128
SUBLANES = 8
BF16_ROWS = 16
LOG2E = 1.4426950408889634
HEAD_COLS = HEADS * LANES
VMEM_LIMIT_BYTES = 56 * 1024 * 1024

PROJ_ROWS = 512
FFN_ROWS = 512
FFN_CHUNK = 256
ATTN_TILE = 512
FLASH_HEADS_PER_STEP = 8
DIFF_HEADS_PER_STEP = 4
F32 = jnp.float32
BF16 = jnp.bfloat16


def _params(*sem, flags=None):
    return pltpu.CompilerParams(dimension_semantics=sem, vmem_limit_bytes=VMEM_LIMIT_BYTES,
                                flags=flags)


def _const_spec(shape):
    nd = len(shape)
    return pl.BlockSpec(shape, lambda *_: (0,) * nd, pipeline_mode=pl.Buffered(1))


def _rms(x, g):
    return x * lax.rsqrt(jnp.mean(x * x, axis=-1, keepdims=True) + RMS_EPS) * g


def _split3(x):
    hi = x.astype(BF16)
    r1 = x - hi.astype(F32)
    mid = r1.astype(BF16)
    lo = (r1 - mid.astype(F32)).astype(BF16)
    return hi, mid, lo


def _rope(blk, ra, rb, rc):
    return blk * ra + pltpu.roll(blk, 16, 1) * rb + pltpu.roll(blk, LANES - 16, 1) * rc


def _vt_rows(dv):
    return dv + BF16_ROWS


def _store_vt(vt_ref, v, dv):
    tm = v.shape[0]
    per = LANES // dv
    for blk in range(v.shape[1] // LANES):
        vt = v[:, blk * LANES:(blk + 1) * LANES].T.astype(BF16)
        for s in range(per):
            vt_ref[0, blk * per + s, 0, 0:dv, :] = vt[s * dv:(s + 1) * dv, :]
    row = lax.broadcasted_iota(jnp.int32, (BF16_ROWS, tm), 0)
    ones_row = jnp.where(row == 0, 1.0, 0.0).astype(BF16)
    for hd in range(HEADS):
        vt_ref[0, hd, 0, dv:dv + BF16_ROWS, :] = ones_row


def _vt_spec(dv, tm, tk, tiles_per_seq):
    per_chunk = tk // tm
    return pl.BlockSpec(
        (1, HEADS, 1, _vt_rows(dv), tm),
        lambda i: (i // tiles_per_seq, 0, (i % tiles_per_seq) // per_chunk, 0,
                   (i % tiles_per_seq) % per_chunk))


_EV_Q, _EV_K, _EV_V = 0, FOX_WIDTH, 2 * FOX_WIDTH
_EV_G = _EV_V + FOX_WIDTH
_EV_CQ = _EV_G + LANES
_EV_CKV = _EV_CQ + MLA_Q_RANK
_EV_KR = _EV_CKV + MLA_KV_RANK
_EV_WIDTH = _EV_KR + LANES


def _even_proj_kernel(x_ref, g_ref, w_ref, bf_ref, qn_ref, wuq_ref, kvn_ref, wuk_ref, wuv_ref,
                      ec_ref, ra_ref, rb_ref, rc_ref, cst_ref,
                      fq_ref, fk_ref, fv_ref, mq_ref, mk_ref, mv_ref, carry_ref,
                      *, tiles_per_seq):
    tm = x_ref.shape[0]
    i = pl.program_id(0)

    @pl.when(i % tiles_per_seq == 0)
    def _():
        carry_ref[...] = jnp.zeros_like(carry_ref)

    h = _rms(x_ref[...], g_ref[...]).astype(BF16)
    q_aug = cst_ref[0:1, :]

    def proj(lo, hi):
        return jnp.dot(h, w_ref[:, lo:hi], preferred_element_type=F32)

    lane = lax.broadcasted_iota(jnp.int32, (tm, LANES), 1)

    def spread_heads(o_ref, packed, aug):
        for pair in range(HEADS // 2):
            blk = packed[:, pair * LANES:(pair + 1) * LANES]
            for odd, src in enumerate((blk, pltpu.roll(blk, FOX_HEAD_DIM, 1))):
                sl = slice((2 * pair + odd) * LANES, (2 * pair + odd + 1) * LANES)
                o_ref[:, sl] = jnp.where(lane < FOX_HEAD_DIM, src, aug[:, sl]).astype(BF16)

    fox_scale = FOX_HEAD_DIM ** -0.5 * LOG2E
    spread_heads(fq_ref, proj(_EV_Q, _EV_K) * fox_scale, q_aug)
    _store_vt(fv_ref, proj(_EV_V, _EV_G), FOX_HEAD_DIM)

    z = proj(_EV_G, _EV_CQ) + bf_ref[...]
    logf = (jnp.minimum(z, 0.0) - jnp.log1p(jnp.exp(-jnp.abs(z)))) * LOG2E
    row = lax.broadcasted_iota(jnp.int32, (tm, tm), 0)
    col = lax.broadcasted_iota(jnp.int32, (tm, tm), 1)
    tril = jnp.where(row >= col, 1.0, 0.0).astype(BF16)
    parts = jnp.dot(tril, jnp.concatenate(_split3(logf), axis=1), preferred_element_type=F32)
    c = (parts[:, :LANES] + parts[:, LANES:2 * LANES] + parts[:, 2 * LANES:]
         + carry_ref[SUBLANES - 1:SUBLANES, :])
    carry_ref[...] = c[tm - SUBLANES:, :]
    c_aug = jnp.dot(jnp.concatenate(_split3(c), axis=1), ec_ref[...], preferred_element_type=F32)
    spread_heads(fk_ref, proj(_EV_K, _EV_V), c_aug)

    ra, rb, rc = ra_ref[...], rb_ref[...], rc_ref[...]

    cq = _rms(proj(_EV_CQ, _EV_CKV), qn_ref[...]).astype(BF16)
    mla_scale = (MLA_NOPE_DIM + MLA_ROPE_DIM) ** -0.5 * LOG2E
    q = jnp.dot(cq, wuq_ref[...], preferred_element_type=F32) * mla_scale
    for hd in range(HEADS):
        sl = slice(hd * LANES, (hd + 1) * LANES)
        mq_ref[:, sl] = _rope(q[:, sl], ra, rb, rc).astype(BF16)

    ckv = _rms(proj(_EV_CKV, _EV_KR), kvn_ref[...]).astype(BF16)
    kr = _rope(proj(_EV_KR, _EV_WIDTH), ra, rb, rc)
    kn = jnp.dot(ckv, wuk_ref[...], preferred_element_type=F32)
    for hd in range(HEADS):
        sl = slice(hd * LANES, (hd + 1) * LANES)
        mk_ref[:, sl] = (kn[:, sl] + kr).astype(BF16)
    _store_vt(mv_ref, jnp.dot(ckv, wuv_ref[...], preferred_element_type=F32), MLA_V_DIM)


def _pad_heads(w, dh):
    k = w.shape[0]
    w = w.reshape(k, HEADS, dh)
    return jnp.pad(w, ((0, 0), (0, 0), (0, LANES - dh))).reshape(k, HEAD_COLS)


def _even_consts():
    cst = np.zeros((SUBLANES, HEAD_COLS), np.float32)
    ec = np.zeros((3 * LANES, HEAD_COLS), np.float32)
    for hd in range(HEADS):
        cst[0, hd * LANES + 64: hd * LANES + 67] = 1.0
        for p in range(3):
            ec[p * LANES + hd, hd * LANES + 64 + p] = -1.0
    return jnp.asarray(cst), jnp.asarray(ec, dtype=BF16)


def _rope_tables(seq):
    half = MLA_ROPE_DIM // 2
    inv = ROPE_THETA ** (-jnp.arange(0, MLA_ROPE_DIM, 2, dtype=F32) / MLA_ROPE_DIM)
    ang = jnp.arange(seq, dtype=F32)[:, None] * inv[None, :]
    cos, sin = jnp.cos(ang), jnp.sin(ang)
    zeros = jnp.zeros((seq, half), F32)
    ra = jnp.concatenate([jnp.ones((seq, 64), F32), cos, cos, jnp.zeros((seq, 32), F32)], axis=1)
    rb = jnp.concatenate([jnp.zeros((seq, 64), F32), zeros, sin, jnp.zeros((seq, 32), F32)], axis=1)
    rc = jnp.concatenate([jnp.zeros((seq, 64), F32), -sin, zeros, jnp.zeros((seq, 32), F32)], axis=1)
    return ra, rb, rc


def _even_proj(x, g, w_in, b_forget, q_norm, w_uq, kv_norm, w_ukv, rope, bsz, seq):
    t = x.shape[0]
    tm, tk = PROJ_ROWS, ATTN_TILE
    cuts = np.cumsum([FOX_WIDTH, FOX_WIDTH, FOX_WIDTH, HEADS, MLA_Q_RANK, MLA_KV_RANK]).tolist()
    wfq, wfk, wfv, wg, wcq, wckv, wkr = jnp.split(w_in, cuts, axis=1)
    w1 = jnp.concatenate([
        wfq, wfk, wfv,
        jnp.pad(wg, ((0, 0), (0, LANES - HEADS))), wcq, wckv,
        jnp.pad(wkr, ((0, 0), (64, LANES - 64 - MLA_ROPE_DIM)))], axis=1).astype(BF16)
    assert w1.shape[1] == _EV_WIDTH
    bf = jnp.pad(b_forget, (0, LANES - HEADS)).reshape(1, LANES)
    wuq = _pad_heads(w_uq, MLA_NOPE_DIM + MLA_ROPE_DIM).astype(BF16)
    wukv = w_ukv.reshape(MLA_KV_RANK, HEADS, MLA_NOPE_DIM + MLA_V_DIM)
    wuk = _pad_heads(wukv[:, :, :MLA_NOPE_DIM].reshape(MLA_KV_RANK, -1), MLA_NOPE_DIM).astype(BF16)
    wuv = wukv[:, :, MLA_NOPE_DIM:].reshape(MLA_KV_RANK, -1).astype(BF16)
    cst, ec = _even_consts()
    ra, rb, rc = rope
    tiles_per_seq = seq // tm
    rope_spec = pl.BlockSpec((tm, LANES), lambda i: (i % tiles_per_seq, 0))
    row_spec = pl.BlockSpec((tm, HEAD_COLS), lambda i: (i, 0))
    out_sds = jax.ShapeDtypeStruct((t, HEAD_COLS), BF16)
    vt_spec = _vt_spec(FOX_HEAD_DIM, tm, tk, tiles_per_seq)
    vt_sds = jax.ShapeDtypeStruct((bsz, HEADS, seq // tk, _vt_rows(FOX_HEAD_DIM), tk), BF16)
    return pl.pallas_call(
        functools.partial(_even_proj_kernel, tiles_per_seq=tiles_per_seq),
        grid=(t // tm,),
        in_specs=[pl.BlockSpec((tm, D_MODEL), lambda i: (i, 0)),
                  _const_spec((1, D_MODEL)), _const_spec(w1.shape), _const_spec((1, LANES)),
                  _const_spec((1, MLA_Q_RANK)), _const_spec(wuq.shape),
                  _const_spec((1, MLA_KV_RANK)), _const_spec(wuk.shape), _const_spec(wuv.shape),
                  _const_spec(ec.shape), rope_spec, rope_spec, rope_spec, _const_spec(cst.shape)],
        out_specs=[row_spec, row_spec, vt_spec, row_spec, row_spec, vt_spec],
        out_shape=[out_sds, out_sds, vt_sds, out_sds, out_sds, vt_sds],
        scratch_shapes=[pltpu.VMEM((SUBLANES, LANES), F32)],
        compiler_params=_params("arbitrary"),
        name="even_proj",
    )(x, g.reshape(1, -1), w1, bf, q_norm.reshape(1, -1), wuq, kv_norm.reshape(1, -1), wuk, wuv,
      ec, ra, rb, rc, cst)


def _alibi_slope(hd):
    return 2.0 ** (-8.0 * (hd + 1) / HEADS)


def _odd_proj_kernel(x_ref, g_ref, w_ref, q_ref, k_ref, v_ref, *, tiles_per_seq):
    tm = x_ref.shape[0]
    i = pl.program_id(0)
    h = _rms(x_ref[...], g_ref[...]).astype(BF16)
    scale = DIFF_HEAD_DIM ** -0.5 * LOG2E
    q_ref[...] = (jnp.dot(h, w_ref[:, :HEAD_COLS], preferred_element_type=F32) * scale).astype(BF16)
    k = jnp.dot(h, w_ref[:, HEAD_COLS:2 * HEAD_COLS], preferred_element_type=F32)
    v = jnp.dot(h, w_ref[:, 2 * HEAD_COLS:], preferred_element_type=F32)
    pos = ((i % tiles_per_seq) * tm + lax.broadcasted_iota(jnp.int32, (tm, 1), 0)).astype(F32)
    lane = lax.broadcasted_iota(jnp.int32, (tm, LANES), 1)
    for hd in range(HEADS):
        sl = slice(hd * LANES, (hd + 1) * LANES)
        b_hi, b_mid, b_lo = (p.astype(F32) for p in _split3(pos * (_alibi_slope(hd) * LOG2E)))
        aug = jnp.where(lane == 0, b_hi, jnp.where(lane == 1, b_mid, jnp.where(lane == 2, b_lo, 0.0)))
        k_ref[:, 2 * hd * LANES:(2 * hd + 1) * LANES] = k[:, sl].astype(BF16)
        k_ref[:, (2 * hd + 1) * LANES:(2 * hd + 2) * LANES] = aug.astype(BF16)
    _store_vt(v_ref, v, DIFF_V_DIM)


def _odd_proj(x, g, w_in, bsz, seq):
    t = x.shape[0]
    tm, tk = PROJ_ROWS, ATTN_TILE
    w = w_in.astype(BF16)
    tiles_per_seq = seq // tm
    return pl.pallas_call(
        functools.partial(_odd_proj_kernel, tiles_per_seq=tiles_per_seq),
        grid=(t // tm,),
        in_specs=[pl.BlockSpec((tm, D_MODEL), lambda i: (i, 0)),
                  _const_spec((1, D_MODEL)), _const_spec(w.shape)],
        out_specs=[pl.BlockSpec((tm, HEAD_COLS), lambda i: (i, 0)),
                   pl.BlockSpec((tm, 2 * HEAD_COLS), lambda i: (i, 0)),
                   _vt_spec(DIFF_V_DIM, tm, tk, tiles_per_seq)],
        out_shape=[jax.ShapeDtypeStruct((t, HEAD_COLS), BF16),
                   jax.ShapeDtypeStruct((t, 2 * HEAD_COLS), BF16),
                   jax.ShapeDtypeStruct((bsz, HEADS, seq // tk, _vt_rows(DIFF_V_DIM), tk), BF16)],
        compiler_params=_params("arbitrary"),
        name="odd_proj",
    )(x, g.reshape(1, -1), w)


_NT = (((1,), (1,)), ((), ()))


def _attend(s, vt, m, acc, causal):
    if causal:
        key = lax.broadcasted_iota(jnp.int32, s.shape, 0)
        qry = lax.broadcasted_iota(jnp.int32, s.shape, 1)
        s = jnp.where(key <= qry, s, NEG_INF)
    m_new = jnp.maximum(m, jnp.max(s, axis=0, keepdims=True))
    p = jnp.exp2(s - m_new).astype(BF16)
    acc = jnp.exp2(m - m_new) * acc + jnp.dot(vt, p, preferred_element_type=F32)
    return m_new, acc


def _stream_attention(qk, vts, s_ref, qi, rows, n):
    t = s_ref.shape[0]
    whole = (0, t)
    s_ref[...] = qk(0, 0, whole, whole)

    def sweep(j, carry):
        carry = list(carry)
        s_cur = s_ref[...]
        for k in range(n):
            s_next = qk(k + 1, j, whole, whole) if k + 1 < n else qk(0, j + 1, whole, whole)
            carry[2 * k], carry[2 * k + 1] = _attend(s_cur, vts(k, j, whole), carry[2 * k],
                                                     carry[2 * k + 1], False)
            s_cur = s_next
        s_ref[...] = s_cur
        return tuple(carry)

    init = (jnp.full((1, t), NEG_INF, F32), jnp.zeros((rows, t), F32)) * n
    carry = list(lax.fori_loop(0, qi, sweep, init))

    s_cur = s_ref[...]
    for k in range(n):
        if k + 1 < n:
            s_next = qk(k + 1, qi, whole, whole)
        carry[2 * k], carry[2 * k + 1] = _attend(s_cur, vts(k, qi, whole), carry[2 * k],
                                                 carry[2 * k + 1], True)
        s_cur = s_next
    return carry[1::2]


def _flash_kernel(q_ref, k_ref, vt_ref, o_ref, s_ref):
    tq = q_ref.shape[1]
    n, rows, tk = vt_ref.shape[1], vt_ref.shape[3], vt_ref.shape[4]
    dv = rows - BF16_ROWS
    qs = [q_ref[0, :, hd * LANES:(hd + 1) * LANES] for hd in range(n)]

    def qk(hd, j, keys, qrys):
        start = pl.multiple_of(j * tk, tk) + keys[0]
        kc = k_ref[0, pl.ds(start, keys[1]), hd * LANES:(hd + 1) * LANES]
        return lax.dot_general(kc, qs[hd][qrys[0]:qrys[0] + qrys[1]], _NT,
                               preferred_element_type=F32)

    def vts(hd, j, keys):
        return vt_ref[0, hd, j, :, keys[0]:keys[0] + keys[1]]

    accs = _stream_attention(qk, vts, s_ref, pl.program_id(2), rows, n)
    o_t = jnp.concatenate([a[:dv] / a[dv:dv + 1] for a in accs], axis=0)
    o_ref[0] = o_t.T.astype(BF16)


def _flash_attention(q, k, vt, tq):
    b, s, _ = q.shape
    _, _, chunks, rows, tk = vt.shape
    assert tq == tk
    n = FLASH_HEADS_PER_STEP
    dv = rows - BF16_ROWS
    return pl.pallas_call(
        _flash_kernel,
        grid=(b, HEADS // n, s // tq),
        in_specs=[pl.BlockSpec((1, tq, n * LANES), lambda bi, hp, i: (bi, i, hp)),
                  pl.BlockSpec((1, s, n * LANES), lambda bi, hp, i: (bi, 0, hp)),
                  pl.BlockSpec((1, n, chunks, rows, tk), lambda bi, hp, i: (bi, hp, 0, 0, 0))],
        out_specs=pl.BlockSpec((1, tq, n * dv), lambda bi, hp, i: (bi, i, hp)),
        out_shape=jax.ShapeDtypeStruct((b, s, HEADS * (rows - BF16_ROWS)), BF16),
        scratch_shapes=[pltpu.VMEM((tk, tq), F32)],
        compiler_params=_params("arbitrary", "arbitrary", "arbitrary"),
        name="flash_attention",
    )(q, k, vt)


def _diff_kernel(q_ref, k_ref, vt_ref, lq1_ref, lk1_ref, lq2_ref, lk2_ref, sub_ref, o_ref, s_ref,
                 *, lambda_init):
    tq = q_ref.shape[1]
    n, rows, tk = vt_ref.shape[1], vt_ref.shape[3], vt_ref.shape[4]
    dv = rows - BF16_ROWS
    lane = lax.broadcasted_iota(jnp.int32, (tq, LANES), 1)
    aug = jnp.where(lane < 3, 1.0, 0.0).astype(BF16)
    qs = []
    for hd in range(n):
        q = q_ref[0, :, hd * LANES:(hd + 1) * LANES]
        zero = jnp.zeros_like(q)
        qs.append(jnp.concatenate([jnp.where(lane < DIFF_HEAD_DIM, q, zero), aug], axis=1))
        qs.append(jnp.concatenate([jnp.where(lane >= DIFF_HEAD_DIM, q, zero), aug], axis=1))

    def qk(st, j, keys, qrys):
        hd = st // 2
        start = pl.multiple_of(j * tk, tk) + keys[0]
        kc = k_ref[0, pl.ds(start, keys[1]), 2 * hd * LANES:2 * (hd + 1) * LANES]
        return lax.dot_general(kc, qs[st][qrys[0]:qrys[0] + qrys[1]], _NT,
                               preferred_element_type=F32)

    def vts(st, j, keys):
        return vt_ref[0, st // 2, j, :, keys[0]:keys[0] + keys[1]]

    accs = _stream_attention(qk, vts, s_ref, pl.program_id(2), rows, 2 * n)
    lam = (jnp.exp(jnp.sum(lq1_ref[...] * lk1_ref[...], axis=1, keepdims=True))
           - jnp.exp(jnp.sum(lq2_ref[...] * lk2_ref[...], axis=1, keepdims=True)) + lambda_init)
    for hd in range(n):
        a1, a2 = accs[2 * hd], accs[2 * hd + 1]
        o = (a1[:dv] / a1[dv:dv + 1] - lam * (a2[:dv] / a2[dv:dv + 1])).T
        o_ref[0, :, hd * dv:(hd + 1) * dv] = (
            _rms(o, sub_ref[...]) * (1.0 - lambda_init)).astype(BF16)


def _diff_attention(q, k, vt, lq1, lk1, lq2, lk2, subln, lambda_init, tq):
    b, s, _ = q.shape
    _, _, chunks, rows, tk = vt.shape
    assert tq == tk
    vec = lambda a: a.reshape(1, -1)
    n = DIFF_HEADS_PER_STEP
    return pl.pallas_call(
        functools.partial(_diff_kernel, lambda_init=lambda_init),
        grid=(b, HEADS // n, s // tq),
        in_specs=[pl.BlockSpec((1, tq, n * LANES), lambda bi, hd, i: (bi, i, hd)),
                  pl.BlockSpec((1, s, 2 * n * LANES), lambda bi, hd, i: (bi, 0, hd)),
                  pl.BlockSpec((1, n, chunks, rows, tk), lambda bi, hd, i: (bi, hd, 0, 0, 0)),
                  _const_spec((1, DIFF_HEAD_DIM)), _const_spec((1, DIFF_HEAD_DIM)),
                  _const_spec((1, DIFF_HEAD_DIM)), _const_spec((1, DIFF_HEAD_DIM)),
                  _const_spec((1, DIFF_V_DIM))],
        out_specs=pl.BlockSpec((1, tq, n * LANES), lambda bi, hd, i: (bi, i, hd)),
        out_shape=jax.ShapeDtypeStruct(q.shape, BF16),
        scratch_shapes=[pltpu.VMEM((tk, tq), F32)],
        compiler_params=_params("arbitrary", "arbitrary", "arbitrary"),
        name="diff_attention",
    )(q, k, vt, vec(lq1), vec(lk1), vec(lq2), vec(lk2), vec(subln))


_N_FFN_CHUNKS = D_FF // FFN_CHUNK
_GELU_C = math.sqrt(2.0 / math.pi)


def _mix_ffn_kernel(*refs, n_in, tiles_per_seq):
    a_refs, wo_refs = refs[:n_in], refs[n_in:2 * n_in]
    (gmix_ref, x_ref, gpre_ref, wup_ref, cw_ref, wdn_ref, gpost_ref, o_ref,
     x1_ref, h_ref, hist_ref, gbuf_ref, gva_ref, gvb_ref, mid_ref) = refs[2 * n_in:]
    tm = x_ref.shape[0]
    nc = _N_FFN_CHUNKS
    i = pl.program_id(0)

    @pl.when(i % tiles_per_seq == 0)
    def _():
        hist_ref[...] = jnp.zeros_like(hist_ref)

    mix = jnp.dot(a_refs[0][...], wo_refs[0][...], preferred_element_type=F32)
    for a_ref, wo_ref in zip(a_refs[1:], wo_refs[1:]):
        mix = mix + jnp.dot(a_ref[...], wo_ref[...], preferred_element_type=F32)
    x1_ref[...] = x_ref[...] + _rms(mix, gmix_ref[...])
    h_ref[...] = _rms(x1_ref[...], gpre_ref[...]).astype(BF16)

    def up(c):
        return jnp.dot(h_ref[...], wup_ref[c], preferred_element_type=F32)

    def gated(c, gv_ref):
        gate, val = gv_ref[:, :FFN_CHUNK], gv_ref[:, FFN_CHUNK:]
        gbuf_ref[0:SUBLANES, :] = hist_ref[c]
        gbuf_ref[SUBLANES:, :] = gate
        hist_ref[c] = gate[tm - SUBLANES:, :]
        cw = cw_ref[c]
        pre = (gbuf_ref[SUBLANES - 2:SUBLANES - 2 + tm, :] * cw[0:1, :]
               + gbuf_ref[SUBLANES - 1:SUBLANES - 1 + tm, :] * cw[1:2, :]
               + gate * cw[2:3, :] + cw[3:4, :])
        pre = pre.astype(BF16)
        act = 0.5 * pre * (1.0 + jnp.tanh(_GELU_C * (pre + 0.044715 * (pre * pre * pre))))
        mid_ref[c] = act * val.astype(BF16)

    assert nc % 2 == 1
    gva_ref[...] = up(0)

    def body(k, _):
        gated(2 * k, gva_ref)
        gvb_ref[...] = up(2 * k + 1)
        gated(2 * k + 1, gvb_ref)
        gva_ref[...] = up(2 * k + 2)
        return 0

    lax.fori_loop(0, nc // 2, body, 0)
    gated(nc - 1, gva_ref)
    f = jnp.dot(mid_ref[0], wdn_ref[0], preferred_element_type=F32)
    for c in range(1, nc):
        f = f + jnp.dot(mid_ref[c], wdn_ref[c], preferred_element_type=F32)
    o_ref[...] = x1_ref[...] + _rms(f, gpost_ref[...])


def _mix_ffn(acts, w_outs, g_mix, x, g_pre, w_up, conv_w, conv_b, w_down, g_post, seq):
    t = x.shape[0]
    tm = FFN_ROWS
    nc = _N_FFN_CHUNKS
    n_in = len(acts)
    wg = w_up[:, :D_FF].reshape(D_MODEL, nc, FFN_CHUNK)
    wv = w_up[:, D_FF:].reshape(D_MODEL, nc, FFN_CHUNK)
    wup = jnp.concatenate([wg, wv], axis=2).transpose(1, 0, 2).astype(BF16)
    wdn = w_down.reshape(nc, FFN_CHUNK, D_MODEL).astype(BF16)
    cw = jnp.concatenate([conv_w, conv_b[None, :],
                          jnp.zeros((SUBLANES - CONV_WIDTH - 1, D_FF), F32)], axis=0)
    cw = cw.reshape(SUBLANES, nc, FFN_CHUNK).transpose(1, 0, 2)
    x_spec = pl.BlockSpec((tm, D_MODEL), lambda i: (i, 0))
    vec_spec = _const_spec((1, D_MODEL))
    return pl.pallas_call(
        functools.partial(_mix_ffn_kernel, n_in=n_in, tiles_per_seq=seq // tm),
        grid=(t // tm,),
        in_specs=([pl.BlockSpec((tm, a.shape[1]), lambda i: (i, 0)) for a in acts]
                  + [_const_spec(w.shape) for w in w_outs]
                  + [vec_spec, x_spec, vec_spec, _const_spec(wup.shape), _const_spec(cw.shape),
                     _const_spec(wdn.shape), vec_spec]),
        out_specs=x_spec,
        out_shape=jax.ShapeDtypeStruct(x.shape, F32),
        scratch_shapes=[pltpu.VMEM((tm, D_MODEL), F32),
                        pltpu.VMEM((tm, D_MODEL), BF16),
                        pltpu.VMEM((nc, SUBLANES, FFN_CHUNK), F32),
                        pltpu.VMEM((tm + SUBLANES, FFN_CHUNK), F32),
                        pltpu.VMEM((tm, 2 * FFN_CHUNK), F32),
                        pltpu.VMEM((tm, 2 * FFN_CHUNK), F32),
                        pltpu.VMEM((nc, tm, FFN_CHUNK), BF16)],
        compiler_params=_params("arbitrary"),
        name="mix_ffn",
    )(*acts, *w_outs, g_mix.reshape(1, -1), x, g_pre.reshape(1, -1), wup, cw, wdn,
      g_post.reshape(1, -1))


def kernel(x, norm_mix_pre, norm_mix_post, norm_ffn_pre, norm_ffn_post, even_w_in, even_b_forget, even_q_norm, even_w_uq, even_kv_norm, even_w_ukv, even_w_out, odd_w_in, odd_lambda_q1, odd_lambda_k1, odd_lambda_q2, odd_lambda_k2, odd_subln, odd_w_out, ffn_w_up, ffn_conv_w, ffn_conv_b, ffn_w_down):
    bsz, seq, d = x.shape
    assert d == D_MODEL and seq % ATTN_TILE == 0 and seq % FFN_ROWS == 0
    t = bsz * seq
    tq = ATTN_TILE
    rope = _rope_tables(seq)
    xf = x.reshape(t, d)
    for layer in range(DEPTH):
        i = layer // 2
        if layer % 2 == 0:
            fq, fk, fvt, mq, mk, mvt = _even_proj(
                xf, norm_mix_pre[layer], even_w_in[i], even_b_forget[i], even_q_norm[i],
                even_w_uq[i], even_kv_norm[i], even_w_ukv[i], rope, bsz, seq)
            shp = (bsz, seq, HEAD_COLS)
            fo = _flash_attention(fq.reshape(shp), fk.reshape(shp), fvt, tq)
            mo = _flash_attention(mq.reshape(shp), mk.reshape(shp), mvt, tq)
            w_out = even_w_out[i].astype(BF16)
            acts = [fo.reshape(t, -1), mo.reshape(t, -1)]
            w_outs = [w_out[:FOX_WIDTH], w_out[FOX_WIDTH:]]
        else:
            lambda_init = 0.8 - 0.6 * math.exp(-0.3 * layer)
            q, k, vt = _odd_proj(xf, norm_mix_pre[layer], odd_w_in[i], bsz, seq)
            o = _diff_attention(q.reshape(bsz, seq, -1), k.reshape(bsz, seq, -1), vt,
                                odd_lambda_q1[i], odd_lambda_k1[i],
                                odd_lambda_q2[i], odd_lambda_k2[i], odd_subln[i], lambda_init, tq)
            acts, w_outs = [o.reshape(t, -1)], [odd_w_out[i].astype(BF16)]
        xf = _mix_ffn(acts, w_outs, norm_mix_post[layer], xf, norm_ffn_pre[layer], ffn_w_up[layer],
                      ffn_conv_w[layer], ffn_conv_b[layer], ffn_w_down[layer],
                      norm_ffn_post[layer], seq)
    return xf.reshape(bsz, seq, d)
```

```python
import functools
import math

import numpy as np
import jax
import jax.numpy as jnp
from jax import lax
from jax.experimental import pallas as pl
from jax.experimental.pallas import tpu as pltpu

D_MODEL = 1024
DEPTH = 4
RMS_EPS = 1e-6
NEG_INF = -1e30
HEADS = 8
FOX_HEAD_DIM = 64
FOX_WIDTH = HEADS * FOX_HEAD_DIM
MLA_NOPE_DIM = 64
MLA_ROPE_DIM = 32
MLA_V_DIM = 64
MLA_Q_RANK = 384
MLA_KV_RANK = 256
ROPE_THETA = 10000.0
DIFF_HEAD_DIM = 64
DIFF_V_DIM = 2 * DIFF_HEAD_DIM
D_FF = 2816
CONV_WIDTH = 3

LANES = 128
SUBLANES = 8
BF16_ROWS = 16
LOG2E = 1.4426950408889634
HEAD_COLS = HEADS * LANES
VMEM_LIMIT_BYTES = 56 * 1024 * 1024

PROJ_ROWS = 512
FFN_ROWS = 512
FFN_CHUNK = 256
ATTN_TILE = 512
ATTN_LAG = 2
FLASH_HEADS_PER_STEP = 8
DIFF_HEADS_PER_STEP = 4
F32 = jnp.float32
BF16 = jnp.bfloat16


def _params(*sem, flags=None):
    return pltpu.CompilerParams(dimension_semantics=sem, vmem_limit_bytes=VMEM_LIMIT_BYTES,
                                flags=flags)


def _const_spec(shape):
    nd = len(shape)
    return pl.BlockSpec(shape, lambda *_: (0,) * nd, pipeline_mode=pl.Buffered(1))


def _rms(x, g):
    return x * lax.rsqrt(jnp.mean(x * x, axis=-1, keepdims=True) + RMS_EPS) * g


def _split3(x):
    hi = x.astype(BF16)
    r1 = x - hi.astype(F32)
    mid = r1.astype(BF16)
    lo = (r1 - mid.astype(F32)).astype(BF16)
    return hi, mid, lo


def _rope(blk, ra, rb, rc):
    return blk * ra + pltpu.roll(blk, 16, 1) * rb + pltpu.roll(blk, LANES - 16, 1) * rc


def _vt_rows(dv):
    return dv + BF16_ROWS


def _store_vt(vt_ref, v, dv):
    tm = v.shape[0]
    per = LANES // dv
    for blk in range(v.shape[1] // LANES):
        vt = v[:, blk * LANES:(blk + 1) * LANES].T.astype(BF16)
        for s in range(per):
            vt_ref[0, blk * per + s, 0, 0:dv, :] = vt[s * dv:(s + 1) * dv, :]
    row = lax.broadcasted_iota(jnp.int32, (BF16_ROWS, tm), 0)
    ones_row = jnp.where(row == 0, 1.0, 0.0).astype(BF16)
    for hd in range(HEADS):
        vt_ref[0, hd, 0, dv:dv + BF16_ROWS, :] = ones_row


def _vt_spec(dv, tm, tk, tiles_per_seq):
    per_chunk = tk // tm
    return pl.BlockSpec(
        (1, HEADS, 1, _vt_rows(dv), tm),
        lambda i: (i // tiles_per_seq, 0, (i % tiles_per_seq) // per_chunk, 0,
                   (i % tiles_per_seq) % per_chunk))


_EV_Q, _EV_K, _EV_V = 0, FOX_WIDTH, 2 * FOX_WIDTH
_EV_G = _EV_V + FOX_WIDTH
_EV_CQ = _EV_G + LANES
_EV_CKV = _EV_CQ + MLA_Q_RANK
_EV_KR = _EV_CKV + MLA_KV_RANK
_EV_WIDTH = _EV_KR + LANES


def _even_proj_kernel(x_ref, g_ref, w_ref, bf_ref, qn_ref, wuq_ref, kvn_ref, wuk_ref, wuv_ref,
                      ec_ref, ra_ref, rb_ref, rc_ref, cst_ref,
                      fq_ref, fk_ref, fv_ref, mq_ref, mk_ref, mv_ref, carry_ref,
                      *, tiles_per_seq):
    tm = x_ref.shape[0]
    i = pl.program_id(0)

    @pl.when(i % tiles_per_seq == 0)
    def _():
        carry_ref[...] = jnp.zeros_like(carry_ref)

    h = _rms(x_ref[...], g_ref[...]).astype(BF16)
    q_aug = cst_ref[0:1, :]

    def proj(lo, hi):
        return jnp.dot(h, w_ref[:, lo:hi], preferred_element_type=F32)

    lane = lax.broadcasted_iota(jnp.int32, (tm, LANES), 1)

    def spread_heads(o_ref, packed, aug):
        for pair in range(HEADS // 2):
            blk = packed[:, pair * LANES:(pair + 1) * LANES]
            for odd, src in enumerate((blk, pltpu.roll(blk, FOX_HEAD_DIM, 1))):
                sl = slice((2 * pair + odd) * LANES, (2 * pair + odd + 1) * LANES)
                o_ref[:, sl] = jnp.where(lane < FOX_HEAD_DIM, src, aug[:, sl]).astype(BF16)

    fox_scale = FOX_HEAD_DIM ** -0.5 * LOG2E
    spread_heads(fq_ref, proj(_EV_Q, _EV_K) * fox_scale, q_aug)
    _store_vt(fv_ref, proj(_EV_V, _EV_G), FOX_HEAD_DIM)

    z = proj(_EV_G, _EV_CQ) + bf_ref[...]
    logf = (jnp.minimum(z, 0.0) - jnp.log1p(jnp.exp(-jnp.abs(z)))) * LOG2E
    row = lax.broadcasted_iota(jnp.int32, (tm, tm), 0)
    col = lax.broadcasted_iota(jnp.int32, (tm, tm), 1)
    tril = jnp.where(row >= col, 1.0, 0.0).astype(BF16)
    parts = jnp.dot(tril, jnp.concatenate(_split3(logf), axis=1), preferred_element_type=F32)
    c = (parts[:, :LANES] + parts[:, LANES:2 * LANES] + parts[:, 2 * LANES:]
         + carry_ref[SUBLANES - 1:SUBLANES, :])
    carry_ref[...] = c[tm - SUBLANES:, :]
    c_aug = jnp.dot(jnp.concatenate(_split3(c), axis=1), ec_ref[...], preferred_element_type=F32)
    spread_heads(fk_ref, proj(_EV_K, _EV_V), c_aug)

    ra, rb, rc = ra_ref[...], rb_ref[...], rc_ref[...]

    cq = _rms(proj(_EV_CQ, _EV_CKV), qn_ref[...]).astype(BF16)
    mla_scale = (MLA_NOPE_DIM + MLA_ROPE_DIM) ** -0.5 * LOG2E
    q = jnp.dot(cq, wuq_ref[...], preferred_element_type=F32) * mla_scale
    for hd in range(HEADS):
        sl = slice(hd * LANES, (hd + 1) * LANES)
        mq_ref[:, sl] = _rope(q[:, sl], ra, rb, rc).astype(BF16)

    ckv = _rms(proj(_EV_CKV, _EV_KR), kvn_ref[...]).astype(BF16)
    kr = _rope(proj(_EV_KR, _EV_WIDTH), ra, rb, rc)
    kn = jnp.dot(ckv, wuk_ref[...], preferred_element_type=F32)
    for hd in range(HEADS):
        sl = slice(hd * LANES, (hd + 1) * LANES)
        mk_ref[:, sl] = (kn[:, sl] + kr).astype(BF16)
    _store_vt(mv_ref, jnp.dot(ckv, wuv_ref[...], preferred_element_type=F32), MLA_V_DIM)


def _pad_heads(w, dh):
    k = w.shape[0]
    w = w.reshape(k, HEADS, dh)
    return jnp.pad(w, ((0, 0), (0, 0), (0, LANES - dh))).reshape(k, HEAD_COLS)


def _even_consts():
    cst = np.zeros((SUBLANES, HEAD_COLS), np.float32)
    ec = np.zeros((3 * LANES, HEAD_COLS), np.float32)
    for hd in range(HEADS):
        cst[0, hd * LANES + 64: hd * LANES + 67] = 1.0
        for p in range(3):
            ec[p * LANES + hd, hd * LANES + 64 + p] = -1.0
    return jnp.asarray(cst), jnp.asarray(ec, dtype=BF16)


def _rope_tables(seq):
    half = MLA_ROPE_DIM // 2
    inv = ROPE_THETA ** (-jnp.arange(0, MLA_ROPE_DIM, 2, dtype=F32) / MLA_ROPE_DIM)
    ang = jnp.arange(seq, dtype=F32)[:, None] * inv[None, :]
    cos, sin = jnp.cos(ang), jnp.sin(ang)
    zeros = jnp.zeros((seq, half), F32)
    ra = jnp.concatenate([jnp.ones((seq, 64), F32), cos, cos, jnp.zeros((seq, 32), F32)], axis=1)
    rb = jnp.concatenate([jnp.zeros((seq, 64), F32), zeros, sin, jnp.zeros((seq, 32), F32)], axis=1)
    rc = jnp.concatenate([jnp.zeros((seq, 64), F32), -sin, zeros, jnp.zeros((seq, 32), F32)], axis=1)
    return ra, rb, rc


def _even_proj(x, g, w_in, b_forget, q_norm, w_uq, kv_norm, w_ukv, rope, bsz, seq):
    t = x.shape[0]
    tm, tk = PROJ_ROWS, ATTN_TILE
    cuts = np.cumsum([FOX_WIDTH, FOX_WIDTH, FOX_WIDTH, HEADS, MLA_Q_RANK, MLA_KV_RANK]).tolist()
    wfq, wfk, wfv, wg, wcq, wckv, wkr = jnp.split(w_in, cuts, axis=1)
    w1 = jnp.concatenate([
        wfq, wfk, wfv,
        jnp.pad(wg, ((0, 0), (0, LANES - HEADS))), wcq, wckv,
        jnp.pad(wkr, ((0, 0), (64, LANES - 64 - MLA_ROPE_DIM)))], axis=1).astype(BF16)
    assert w1.shape[1] == _EV_WIDTH
    bf = jnp.pad(b_forget, (0, LANES - HEADS)).reshape(1, LANES)
    wuq = _pad_heads(w_uq, MLA_NOPE_DIM + MLA_ROPE_DIM).astype(BF16)
    wukv = w_ukv.reshape(MLA_KV_RANK, HEADS, MLA_NOPE_DIM + MLA_V_DIM)
    wuk = _pad_heads(wukv[:, :, :MLA_NOPE_DIM].reshape(MLA_KV_RANK, -1), MLA_NOPE_DIM).astype(BF16)
    wuv = wukv[:, :, MLA_NOPE_DIM:].reshape(MLA_KV_RANK, -1).astype(BF16)
    cst, ec = _even_consts()
    ra, rb, rc = rope
    tiles_per_seq = seq // tm
    rope_spec = pl.BlockSpec((tm, LANES), lambda i: (i % tiles_per_seq, 0))
    row_spec = pl.BlockSpec((tm, HEAD_COLS), lambda i: (i, 0))
    out_sds = jax.ShapeDtypeStruct((t, HEAD_COLS), BF16)
    vt_spec = _vt_spec(FOX_HEAD_DIM, tm, tk, tiles_per_seq)
    vt_sds = jax.ShapeDtypeStruct((bsz, HEADS, seq // tk, _vt_rows(FOX_HEAD_DIM), tk), BF16)
    return pl.pallas_call(
        functools.partial(_even_proj_kernel, tiles_per_seq=tiles_per_seq),
        grid=(t // tm,),
        in_specs=[pl.BlockSpec((tm, D_MODEL), lambda i: (i, 0)),
                  _const_spec((1, D_MODEL)), _const_spec(w1.shape), _const_spec((1, LANES)),
                  _const_spec((1, MLA_Q_RANK)), _const_spec(wuq.shape),
                  _const_spec((1, MLA_KV_RANK)), _const_spec(wuk.shape), _const_spec(wuv.shape),
                  _const_spec(ec.shape), rope_spec, rope_spec, rope_spec, _const_spec(cst.shape)],
        out_specs=[row_spec, row_spec, vt_spec, row_spec, row_spec, vt_spec],
        out_shape=[out_sds, out_sds, vt_sds, out_sds, out_sds, vt_sds],
        scratch_shapes=[pltpu.VMEM((SUBLANES, LANES), F32)],
        compiler_params=_params("arbitrary"),
        name="even_proj",
    )(x, g.reshape(1, -1), w1, bf, q_norm.reshape(1, -1), wuq, kv_norm.reshape(1, -1), wuk, wuv,
      ec, ra, rb, rc, cst)


def _alibi_slope(hd):
    return 2.0 ** (-8.0 * (hd + 1) / HEADS)


def _odd_proj_kernel(x_ref, g_ref, w_ref, q_ref, k_ref, v_ref, *, tiles_per_seq):
    tm = x_ref.shape[0]
    i = pl.program_id(0)
    h = _rms(x_ref[...], g_ref[...]).astype(BF16)
    scale = DIFF_HEAD_DIM ** -0.5 * LOG2E
    q_ref[...] = (jnp.dot(h, w_ref[:, :HEAD_COLS], preferred_element_type=F32) * scale).astype(BF16)
    k = jnp.dot(h, w_ref[:, HEAD_COLS:2 * HEAD_COLS], preferred_element_type=F32)
    v = jnp.dot(h, w_ref[:, 2 * HEAD_COLS:], preferred_element_type=F32)
    pos = ((i % tiles_per_seq) * tm + lax.broadcasted_iota(jnp.int32, (tm, 1), 0)).astype(F32)
    lane = lax.broadcasted_iota(jnp.int32, (tm, LANES), 1)
    for hd in range(HEADS):
        sl = slice(hd * LANES, (hd + 1) * LANES)
        b_hi, b_mid, b_lo = (p.astype(F32) for p in _split3(pos * (_alibi_slope(hd) * LOG2E)))
        aug = jnp.where(lane == 0, b_hi, jnp.where(lane == 1, b_mid, jnp.where(lane == 2, b_lo, 0.0)))
        k_ref[:, 2 * hd * LANES:(2 * hd + 1) * LANES] = k[:, sl].astype(BF16)
        k_ref[:, (2 * hd + 1) * LANES:(2 * hd + 2) * LANES] = aug.astype(BF16)
    _store_vt(v_ref, v, DIFF_V_DIM)


def _odd_proj(x, g, w_in, bsz, seq):
    t = x.shape[0]
    tm, tk = PROJ_ROWS, ATTN_TILE
    w = w_in.astype(BF16)
    tiles_per_seq = seq // tm
    return pl.pallas_call(
        functools.partial(_odd_proj_kernel, tiles_per_seq=tiles_per_seq),
        grid=(t // tm,),
        in_specs=[pl.BlockSpec((tm, D_MODEL), lambda i: (i, 0)),
                  _const_spec((1, D_MODEL)), _const_spec(w.shape)],
        out_specs=[pl.BlockSpec((tm, HEAD_COLS), lambda i: (i, 0)),
                   pl.BlockSpec((tm, 2 * HEAD_COLS), lambda i: (i, 0)),
                   _vt_spec(DIFF_V_DIM, tm, tk, tiles_per_seq)],
        out_shape=[jax.ShapeDtypeStruct((t, HEAD_COLS), BF16),
                   jax.ShapeDtypeStruct((t, 2 * HEAD_COLS), BF16),
                   jax.ShapeDtypeStruct((bsz, HEADS, seq // tk, _vt_rows(DIFF_V_DIM), tk), BF16)],
        compiler_params=_params("arbitrary"),
        name="odd_proj",
    )(x, g.reshape(1, -1), w)


_NT = (((1,), (1,)), ((), ()))


def _attend(s, vt, m, acc, causal):
    if causal:
        key = lax.broadcasted_iota(jnp.int32, s.shape, 0)
        qry = lax.broadcasted_iota(jnp.int32, s.shape, 1)
        s = jnp.where(key <= qry, s, NEG_INF)
    m_new = jnp.maximum(m, jnp.max(s, axis=0, keepdims=True))
    p = jnp.exp2(s - m_new).astype(BF16)
    acc = jnp.exp2(m - m_new) * acc + jnp.dot(vt, p, preferred_element_type=F32)
    return m_new, acc


def _stream_attention(qk, vts, s_ref, qi, rows, n):
    lag, t = s_ref.shape[0], s_ref.shape[1]
    whole = (0, t)
    assert n >= lag
    for l in range(lag):
        s_ref[l] = qk(l, 0, whole, whole)

    def sweep(j, carry):
        carry = list(carry)
        pending = [s_ref[l] for l in range(lag)]
        for k in range(n):
            ahead = k + lag
            pending.append(qk(ahead, j, whole, whole) if ahead < n
                           else qk(ahead - n, j + 1, whole, whole))
            carry[2 * k], carry[2 * k + 1] = _attend(pending.pop(0), vts(k, j, whole),
                                                     carry[2 * k], carry[2 * k + 1], False)
        for l in range(lag):
            s_ref[l] = pending[l]
        return tuple(carry)

    init = (jnp.full((1, t), NEG_INF, F32), jnp.zeros((rows, t), F32)) * n
    carry = list(lax.fori_loop(0, qi, sweep, init))

    pending = [s_ref[l] for l in range(lag)]
    for k in range(n):
        if k + lag < n:
            pending.append(qk(k + lag, qi, whole, whole))
        carry[2 * k], carry[2 * k + 1] = _attend(pending.pop(0), vts(k, qi, whole),
                                                 carry[2 * k], carry[2 * k + 1], True)
    return carry[1::2]


def _flash_kernel(q_ref, k_ref, vt_ref, o_ref, s_ref):
    tq = q_ref.shape[1]
    n, rows, tk = vt_ref.shape[1], vt_ref.shape[3], vt_ref.shape[4]
    dv = rows - BF16_ROWS
    qs = [q_ref[0, :, hd * LANES:(hd + 1) * LANES] for hd in range(n)]

    def qk(hd, j, keys, qrys):
        start = pl.multiple_of(j * tk, tk) + keys[0]
        kc = k_ref[0, pl.ds(start, keys[1]), hd * LANES:(hd + 1) * LANES]
        return lax.dot_general(kc, qs[hd][qrys[0]:qrys[0] + qrys[1]], _NT,
                               preferred_element_type=F32)

    def vts(hd, j, keys):
        return vt_ref[0, hd, j, :, keys[0]:keys[0] + keys[1]]

    accs = _stream_attention(qk, vts, s_ref, pl.program_id(2), rows, n)
    o_t = jnp.concatenate([a[:dv] / a[dv:dv + 1] for a in accs], axis=0)
    o_ref[0] = o_t.T.astype(BF16)


def _flash_attention(q, k, vt, tq):
    b, s, _ = q.shape
    _, _, chunks, rows, tk = vt.shape
    assert tq == tk
    n = FLASH_HEADS_PER_STEP
    dv = rows - BF16_ROWS
    return pl.pallas_call(
        _flash_kernel,
        grid=(b, HEADS // n, s // tq),
        in_specs=[pl.BlockSpec((1, tq, n * LANES), lambda bi, hp, i: (bi, i, hp)),
                  pl.BlockSpec((1, s, n * LANES), lambda bi, hp, i: (bi, 0, hp)),
                  pl.BlockSpec((1, n, chunks, rows, tk), lambda bi, hp, i: (bi, hp, 0, 0, 0))],
        out_specs=pl.BlockSpec((1, tq, n * dv), lambda bi, hp, i: (bi, i, hp)),
        out_shape=jax.ShapeDtypeStruct((b, s, HEADS * (rows - BF16_ROWS)), BF16),
        scratch_shapes=[pltpu.VMEM((ATTN_LAG, tk, tq), F32)],
        compiler_params=_params("arbitrary", "arbitrary", "arbitrary"),
        name="flash_attention",
    )(q, k, vt)


def _diff_kernel(q_ref, k_ref, vt_ref, lq1_ref, lk1_ref, lq2_ref, lk2_ref, sub_ref, o_ref, s_ref,
                 *, lambda_init):
    tq = q_ref.shape[1]
    n, rows, tk = vt_ref.shape[1], vt_ref.shape[3], vt_ref.shape[4]
    dv = rows - BF16_ROWS
    lane = lax.broadcasted_iota(jnp.int32, (tq, LANES), 1)
    aug = jnp.where(lane < 3, 1.0, 0.0).astype(BF16)
    qs = []
    for hd in range(n):
        q = q_ref[0, :, hd * LANES:(hd + 1) * LANES]
        zero = jnp.zeros_like(q)
        qs.append(jnp.concatenate([jnp.where(lane < DIFF_HEAD_DIM, q, zero), aug], axis=1))
        qs.append(jnp.concatenate([jnp.where(lane >= DIFF_HEAD_DIM, q, zero), aug], axis=1))

    def qk(st, j, keys, qrys):
        hd = st // 2
        start = pl.multiple_of(j * tk, tk) + keys[0]
        kc = k_ref[0, pl.ds(start, keys[1]), 2 * hd * LANES:2 * (hd + 1) * LANES]
        return lax.dot_general(kc, qs[st][qrys[0]:qrys[0] + qrys[1]], _NT,
                               preferred_element_type=F32)

    def vts(st, j, keys):
        return vt_ref[0, st // 2, j, :, keys[0]:keys[0] + keys[1]]

    accs = _stream_attention(qk, vts, s_ref, pl.program_id(2), rows, 2 * n)
    lam = (jnp.exp(jnp.sum(lq1_ref[...] * lk1_ref[...], axis=1, keepdims=True))
           - jnp.exp(jnp.sum(lq2_ref[...] * lk2_ref[...], axis=1, keepdims=True)) + lambda_init)
    for hd in range(n):
        a1, a2 = accs[2 * hd], accs[2 * hd + 1]
        o = (a1[:dv] / a1[dv:dv + 1] - lam * (a2[:dv] / a2[dv:dv + 1])).T
        o_ref[0, :, hd * dv:(hd + 1) * dv] = (
            _rms(o, sub_ref[...]) * (1.0 - lambda_init)).astype(BF16)


def _diff_attention(q, k, vt, lq1, lk1, lq2, lk2, subln, lambda_init, tq):
    b, s, _ = q.shape
    _, _, chunks, rows, tk = vt.shape
    assert tq == tk
    vec = lambda a: a.reshape(1, -1)
    n = DIFF_HEADS_PER_STEP
    return pl.pallas_call(
        functools.partial(_diff_kernel, lambda_init=lambda_init),
        grid=(b, HEADS // n, s // tq),
        in_specs=[pl.BlockSpec((1, tq, n * LANES), lambda bi, hd, i: (bi, i, hd)),
                  pl.BlockSpec((1, s, 2 * n * LANES), lambda bi, hd, i: (bi, 0, hd)),
                  pl.BlockSpec((1, n, chunks, rows, tk), lambda bi, hd, i: (bi, hd, 0, 0, 0)),
                  _const_spec((1, DIFF_HEAD_DIM)), _const_spec((1, DIFF_HEAD_DIM)),
                  _const_spec((1, DIFF_HEAD_DIM)), _const_spec((1, DIFF_HEAD_DIM)),
                  _const_spec((1, DIFF_V_DIM))],
        out_specs=pl.BlockSpec((1, tq, n * LANES), lambda bi, hd, i: (bi, i, hd)),
        out_shape=jax.ShapeDtypeStruct(q.shape, BF16),
        scratch_shapes=[pltpu.VMEM((ATTN_LAG, tk, tq), F32)],
        compiler_params=_params("arbitrary", "arbitrary", "arbitrary"),
        name="diff_attention",
    )(q, k, vt, vec(lq1), vec(lk1), vec(lq2), vec(lk2), vec(subln))


_N_FFN_CHUNKS = D_FF // FFN_CHUNK
_GELU_C = math.sqrt(2.0 / math.pi)


def _mix_ffn_kernel(*refs, n_in, tiles_per_seq):
    a_refs, wo_refs = refs[:n_in], refs[n_in:2 * n_in]
    (gmix_ref, x_ref, gpre_ref, wup_ref, cw_ref, wdn_ref, gpost_ref, o_ref,
     x1_ref, h_ref, hist_ref, gbuf_ref, gva_ref, gvb_ref, mid_ref) = refs[2 * n_in:]
    tm = x_ref.shape[0]
    nc = _N_FFN_CHUNKS
    i = pl.program_id(0)

    @pl.when(i % tiles_per_seq == 0)
    def _():
        hist_ref[...] = jnp.zeros_like(hist_ref)

    mix = jnp.dot(a_refs[0][...], wo_refs[0][...], preferred_element_type=F32)
    for a_ref, wo_ref in zip(a_refs[1:], wo_refs[1:]):
        mix = mix + jnp.dot(a_ref[...], wo_ref[...], preferred_element_type=F32)
    x1_ref[...] = x_ref[...] + _rms(mix, gmix_ref[...])
    h_ref[...] = _rms(x1_ref[...], gpre_ref[...]).astype(BF16)

    def up(c):
        return jnp.dot(h_ref[...], wup_ref[c], preferred_element_type=F32)

    def gated(c, gv_ref):
        gate, val = gv_ref[:, :FFN_CHUNK], gv_ref[:, FFN_CHUNK:]
        gbuf_ref[0:SUBLANES, :] = hist_ref[c]
        gbuf_ref[SUBLANES:, :] = gate
        hist_ref[c] = gate[tm - SUBLANES:, :]
        cw = cw_ref[c]
        pre = (gbuf_ref[SUBLANES - 2:SUBLANES - 2 + tm, :] * cw[0:1, :]
               + gbuf_ref[SUBLANES - 1:SUBLANES - 1 + tm, :] * cw[1:2, :]
               + gate * cw[2:3, :] + cw[3:4, :])
        pre = pre.astype(BF16)
        act = 0.5 * pre * (1.0 + jnp.tanh(_GELU_C * (pre + 0.044715 * (pre * pre * pre))))
        mid_ref[c] = act * val.astype(BF16)

    assert nc % 2 == 1
    gva_ref[...] = up(0)

    def body(k, _):
        gated(2 * k, gva_ref)
        gvb_ref[...] = up(2 * k + 1)
        gated(2 * k + 1, gvb_ref)
        gva_ref[...] = up(2 * k + 2)
        return 0

    lax.fori_loop(0, nc // 2, body, 0)
    gated(nc - 1, gva_ref)
    f = jnp.dot(mid_ref[0], wdn_ref[0], preferred_element_type=F32)
    for c in range(1, nc):
        f = f + jnp.dot(mid_ref[c], wdn_ref[c], preferred_element_type=F32)
    o_ref[...] = x1_ref[...] + _rms(f, gpost_ref[...])


def _mix_ffn(acts, w_outs, g_mix, x, g_pre, w_up, conv_w, conv_b, w_down, g_post, seq):
    t = x.shape[0]
    tm = FFN_ROWS
    nc = _N_FFN_CHUNKS
    n_in = len(acts)
    wg = w_up[:, :D_FF].reshape(D_MODEL, nc, FFN_CHUNK)
    wv = w_up[:, D_FF:].reshape(D_MODEL, nc, FFN_CHUNK)
    wup = jnp.concatenate([wg, wv], axis=2).transpose(1, 0, 2).astype(BF16)
    wdn = w_down.reshape(nc, FFN_CHUNK, D_MODEL).astype(BF16)
    cw = jnp.concatenate([conv_w, conv_b[None, :],
                          jnp.zeros((SUBLANES - CONV_WIDTH - 1, D_FF), F32)], axis=0)
    cw = cw.reshape(SUBLANES, nc, FFN_CHUNK).transpose(1, 0, 2)
    x_spec = pl.BlockSpec((tm, D_MODEL), lambda i: (i, 0))
    vec_spec = _const_spec((1, D_MODEL))
    return pl.pallas_call(
        functools.partial(_mix_ffn_kernel, n_in=n_in, tiles_per_seq=seq // tm),
        grid=(t // tm,),
        in_specs=([pl.BlockSpec((tm, a.shape[1]), lambda i: (i, 0)) for a in acts]
                  + [_const_spec(w.shape) for w in w_outs]
                  + [vec_spec, x_spec, vec_spec, _const_spec(wup.shape), _const_spec(cw.shape),
                     _const_spec(wdn.shape), vec_spec]),
        out_specs=x_spec,
        out_shape=jax.ShapeDtypeStruct(x.shape, F32),
        scratch_shapes=[pltpu.VMEM((tm, D_MODEL), F32),
                        pltpu.VMEM((tm, D_MODEL), BF16),
                        pltpu.VMEM((nc, SUBLANES, FFN_CHUNK), F32),
                        pltpu.VMEM((tm + SUBLANES, FFN_CHUNK), F32),
                        pltpu.VMEM((tm, 2 * FFN_CHUNK), F32),
                        pltpu.VMEM((tm, 2 * FFN_CHUNK), F32),
                        pltpu.VMEM((nc, tm, FFN_CHUNK), BF16)],
        compiler_params=_params("arbitrary"),
        name="mix_ffn",
    )(*acts, *w_outs, g_mix.reshape(1, -1), x, g_pre.reshape(1, -1), wup, cw, wdn,
      g_post.reshape(1, -1))


def kernel(x, norm_mix_pre, norm_mix_post, norm_ffn_pre, norm_ffn_post, even_w_in, even_b_forget, even_q_norm, even_w_uq, even_kv_norm, even_w_ukv, even_w_out, odd_w_in, odd_lambda_q1, odd_lambda_k1, odd_lambda_q2, odd_lambda_k2, odd_subln, odd_w_out, ffn_w_up, ffn_conv_w, ffn_conv_b, ffn_w_down):
    bsz, seq, d = x.shape
    assert d == D_MODEL and seq % ATTN_TILE == 0 and seq % FFN_ROWS == 0
    t = bsz * seq
    tq = ATTN_TILE
    rope = _rope_tables(seq)
    xf = x.reshape(t, d)
    for layer in range(DEPTH):
        i = layer // 2
        if layer % 2 == 0:
            fq, fk, fvt, mq, mk, mvt = _even_proj(
                xf, norm_mix_pre[layer], even_w_in[i], even_b_forget[i], even_q_norm[i],
                even_w_uq[i], even_kv_norm[i], even_w_ukv[i], rope, bsz, seq)
            shp = (bsz, seq, HEAD_COLS)
            fo = _flash_attention(fq.reshape(shp), fk.reshape(shp), fvt, tq)
            mo = _flash_attention(mq.reshape(shp), mk.reshape(shp), mvt, tq)
            w_out = even_w_out[i].astype(BF16)
            acts = [fo.reshape(t, -1), mo.reshape(t, -1)]
            w_outs = [w_out[:FOX_WIDTH], w_out[FOX_WIDTH:]]
        else:
            lambda_init = 0.8 - 0.6 * math.exp(-0.3 * layer)
            q, k, vt = _odd_proj(xf, norm_mix_pre[layer], odd_w_in[i], bsz, seq)
            o = _diff_attention(q.reshape(bsz, seq, -1), k.reshape(bsz, seq, -1), vt,
                                odd_lambda_q1[i], odd_lambda_k1[i],
                                odd_lambda_q2[i], odd_lambda_k2[i], odd_subln[i], lambda_init, tq)
            acts, w_outs = [o.reshape(t, -1)], [odd_w_out[i].astype(BF16)]
        xf = _mix_ffn(acts, w_outs, norm_mix_post[layer], xf, norm_ffn_pre[layer], ffn_w_up[layer],
                      ffn_conv_w[layer], ffn_conv_b[layer], ffn_w_down[layer],
                      norm_ffn_post[layer], seq)
    return xf.reshape(bsz, seq, d)
```

```python
import functools
import math

import numpy as np
import jax
import jax.numpy as jnp
from jax import lax
from jax.experimental import pallas as pl
from jax.experimental.pallas import tpu as pltpu

D_MODEL = 1024
DEPTH = 4
RMS_EPS = 1e-6
NEG_INF = -1e30
HEADS = 8
FOX_HEAD_DIM = 64
FOX_WIDTH = HEADS * FOX_HEAD_DIM
MLA_NOPE_DIM = 64
MLA_ROPE_DIM = 32
MLA_V_DIM = 64
MLA_Q_RANK = 384
MLA_KV_RANK = 256
ROPE_THETA = 10000.0
DIFF_HEAD_DIM = 64
DIFF_V_DIM = 2 * DIFF_HEAD_DIM
D_FF = 2816
CONV_WIDTH = 3

LANES = 128
SUBLANES = 8
BF16_ROWS = 16
LOG2E = 1.4426950408889634
HEAD_COLS = HEADS * LANES
VMEM_LIMIT_BYTES = 56 * 1024 * 1024

PROJ_ROWS = 512
FFN_ROWS = 512
FFN_CHUNK = 256
ATTN_TILE = 512
ATTN_LAG = 2
FLASH_HEADS_PER_STEP = 8
DIFF_HEADS_PER_STEP = 4
F32 = jnp.float32
BF16 = jnp.bfloat16


def _params(*sem, flags=None):
    return pltpu.CompilerParams(dimension_semantics=sem, vmem_limit_bytes=VMEM_LIMIT_BYTES,
                                flags=flags)


def _const_spec(shape):
    nd = len(shape)
    return pl.BlockSpec(shape, lambda *_: (0,) * nd, pipeline_mode=pl.Buffered(1))


def _rms(x, g):
    return x * lax.rsqrt(jnp.mean(x * x, axis=-1, keepdims=True) + RMS_EPS) * g


def _split3(x):
    hi = x.astype(BF16)
    r1 = x - hi.astype(F32)
    mid = r1.astype(BF16)
    lo = (r1 - mid.astype(F32)).astype(BF16)
    return hi, mid, lo


def _rope(blk, ra, rb, rc):
    return blk * ra + pltpu.roll(blk, 16, 1) * rb + pltpu.roll(blk, LANES - 16, 1) * rc


def _vt_rows(dv):
    return dv + BF16_ROWS


def _store_vt(vt_ref, v, dv):
    tm = v.shape[0]
    per = LANES // dv
    for blk in range(v.shape[1] // LANES):
        vt = v[:, blk * LANES:(blk + 1) * LANES].T.astype(BF16)
        for s in range(per):
            vt_ref[0, blk * per + s, 0, 0:dv, :] = vt[s * dv:(s + 1) * dv, :]
    row = lax.broadcasted_iota(jnp.int32, (BF16_ROWS, tm), 0)
    ones_row = jnp.where(row == 0, 1.0, 0.0).astype(BF16)
    for hd in range(HEADS):
        vt_ref[0, hd, 0, dv:dv + BF16_ROWS, :] = ones_row


def _vt_spec(dv, tm, tk, tiles_per_seq):
    per_chunk = tk // tm
    return pl.BlockSpec(
        (1, HEADS, 1, _vt_rows(dv), tm),
        lambda i: (i // tiles_per_seq, 0, (i % tiles_per_seq) // per_chunk, 0,
                   (i % tiles_per_seq) % per_chunk))


_EV_Q, _EV_K, _EV_V = 0, FOX_WIDTH, 2 * FOX_WIDTH
_EV_G = _EV_V + FOX_WIDTH
_EV_CQ = _EV_G + LANES
_EV_CKV = _EV_CQ + MLA_Q_RANK
_EV_KR = _EV_CKV + MLA_KV_RANK
_EV_WIDTH = _EV_KR + LANES


def _even_proj_kernel(x_ref, g_ref, w_ref, bf_ref, qn_ref, wuq_ref, kvn_ref, wuk_ref, wuv_ref,
                      ec_ref, ra_ref, rb_ref, rc_ref, cst_ref,
                      fq_ref, fk_ref, fv_ref, mq_ref, mk_ref, mv_ref, carry_ref,
                      *, tiles_per_seq):
    tm = x_ref.shape[0]
    i = pl.program_id(0)

    @pl.when(i % tiles_per_seq == 0)
    def _():
        carry_ref[...] = jnp.zeros_like(carry_ref)

    h = _rms(x_ref[...], g_ref[...]).astype(BF16)
    q_aug = cst_ref[0:1, :]

    def proj(lo, hi):
        return jnp.dot(h, w_ref[:, lo:hi], preferred_element_type=F32)

    lane = lax.broadcasted_iota(jnp.int32, (tm, LANES), 1)

    def spread_heads(o_ref, packed, aug):
        for pair in range(HEADS // 2):
            blk = packed[:, pair * LANES:(pair + 1) * LANES]
            for odd, src in enumerate((blk, pltpu.roll(blk, FOX_HEAD_DIM, 1))):
                sl = slice((2 * pair + odd) * LANES, (2 * pair + odd + 1) * LANES)
                o_ref[:, sl] = jnp.where(lane < FOX_HEAD_DIM, src, aug[:, sl]).astype(BF16)

    ra, rb, rc = ra_ref[...], rb_ref[...], rc_ref[...]
    fox_scale = FOX_HEAD_DIM ** -0.5 * LOG2E
    mla_scale = (MLA_NOPE_DIM + MLA_ROPE_DIM) ** -0.5 * LOG2E

    z = proj(_EV_G, _EV_CQ) + bf_ref[...]
    cq_raw = proj(_EV_CQ, _EV_CKV)
    ckv_raw = proj(_EV_CKV, _EV_KR)
    kr_raw = proj(_EV_KR, _EV_WIDTH)

    logf = (jnp.minimum(z, 0.0) - jnp.log1p(jnp.exp(-jnp.abs(z)))) * LOG2E
    spread_heads(fq_ref, proj(_EV_Q, _EV_K) * fox_scale, q_aug)

    row = lax.broadcasted_iota(jnp.int32, (tm, tm), 0)
    col = lax.broadcasted_iota(jnp.int32, (tm, tm), 1)
    tril = jnp.where(row >= col, 1.0, 0.0).astype(BF16)
    parts = jnp.dot(tril, jnp.concatenate(_split3(logf), axis=1), preferred_element_type=F32)
    _store_vt(fv_ref, proj(_EV_V, _EV_G), FOX_HEAD_DIM)
    c = (parts[:, :LANES] + parts[:, LANES:2 * LANES] + parts[:, 2 * LANES:]
         + carry_ref[SUBLANES - 1:SUBLANES, :])
    carry_ref[...] = c[tm - SUBLANES:, :]

    cq = _rms(cq_raw, qn_ref[...]).astype(BF16)
    q = jnp.dot(cq, wuq_ref[...], preferred_element_type=F32) * mla_scale
    for hd in range(HEADS):
        sl = slice(hd * LANES, (hd + 1) * LANES)
        mq_ref[:, sl] = _rope(q[:, sl], ra, rb, rc).astype(BF16)

    ckv = _rms(ckv_raw, kvn_ref[...]).astype(BF16)
    kr = _rope(kr_raw, ra, rb, rc)
    kn = jnp.dot(ckv, wuk_ref[...], preferred_element_type=F32)
    for hd in range(HEADS):
        sl = slice(hd * LANES, (hd + 1) * LANES)
        mk_ref[:, sl] = (kn[:, sl] + kr).astype(BF16)
    _store_vt(mv_ref, jnp.dot(ckv, wuv_ref[...], preferred_element_type=F32), MLA_V_DIM)

    c_aug = jnp.dot(jnp.concatenate(_split3(c), axis=1), ec_ref[...], preferred_element_type=F32)
    spread_heads(fk_ref, proj(_EV_K, _EV_V), c_aug)


def _pad_heads(w, dh):
    k = w.shape[0]
    w = w.reshape(k, HEADS, dh)
    return jnp.pad(w, ((0, 0), (0, 0), (0, LANES - dh))).reshape(k, HEAD_COLS)


def _even_consts():
    cst = np.zeros((SUBLANES, HEAD_COLS), np.float32)
    ec = np.zeros((3 * LANES, HEAD_COLS), np.float32)
    for hd in range(HEADS):
        cst[0, hd * LANES + 64: hd * LANES + 67] = 1.0
        for p in range(3):
            ec[p * LANES + hd, hd * LANES + 64 + p] = -1.0
    return jnp.asarray(cst), jnp.asarray(ec, dtype=BF16)


def _rope_tables(seq):
    half = MLA_ROPE_DIM // 2
    inv = ROPE_THETA ** (-jnp.arange(0, MLA_ROPE_DIM, 2, dtype=F32) / MLA_ROPE_DIM)
    ang = jnp.arange(seq, dtype=F32)[:, None] * inv[None, :]
    cos, sin = jnp.cos(ang), jnp.sin(ang)
    zeros = jnp.zeros((seq, half), F32)
    ra = jnp.concatenate([jnp.ones((seq, 64), F32), cos, cos, jnp.zeros((seq, 32), F32)], axis=1)
    rb = jnp.concatenate([jnp.zeros((seq, 64), F32), zeros, sin, jnp.zeros((seq, 32), F32)], axis=1)
    rc = jnp.concatenate([jnp.zeros((seq, 64), F32), -sin, zeros, jnp.zeros((seq, 32), F32)], axis=1)
    return ra, rb, rc


def _even_proj(x, g, w_in, b_forget, q_norm, w_uq, kv_norm, w_ukv, rope, bsz, seq):
    t = x.shape[0]
    tm, tk = PROJ_ROWS, ATTN_TILE
    cuts = np.cumsum([FOX_WIDTH, FOX_WIDTH, FOX_WIDTH, HEADS, MLA_Q_RANK, MLA_KV_RANK]).tolist()
    wfq, wfk, wfv, wg, wcq, wckv, wkr = jnp.split(w_in, cuts, axis=1)
    w1 = jnp.concatenate([
        wfq, wfk, wfv,
        jnp.pad(wg, ((0, 0), (0, LANES - HEADS))), wcq, wckv,
        jnp.pad(wkr, ((0, 0), (64, LANES - 64 - MLA_ROPE_DIM)))], axis=1).astype(BF16)
    assert w1.shape[1] == _EV_WIDTH
    bf = jnp.pad(b_forget, (0, LANES - HEADS)).reshape(1, LANES)
    wuq = _pad_heads(w_uq, MLA_NOPE_DIM + MLA_ROPE_DIM).astype(BF16)
    wukv = w_ukv.reshape(MLA_KV_RANK, HEADS, MLA_NOPE_DIM + MLA_V_DIM)
    wuk = _pad_heads(wukv[:, :, :MLA_NOPE_DIM].reshape(MLA_KV_RANK, -1), MLA_NOPE_DIM).astype(BF16)
    wuv = wukv[:, :, MLA_NOPE_DIM:].reshape(MLA_KV_RANK, -1).astype(BF16)
    cst, ec = _even_consts()
    ra, rb, rc = rope
    tiles_per_seq = seq // tm
    rope_spec = pl.BlockSpec((tm, LANES), lambda i: (i % tiles_per_seq, 0))
    row_spec = pl.BlockSpec((tm, HEAD_COLS), lambda i: (i, 0))
    out_sds = jax.ShapeDtypeStruct((t, HEAD_COLS), BF16)
    vt_spec = _vt_spec(FOX_HEAD_DIM, tm, tk, tiles_per_seq)
    vt_sds = jax.ShapeDtypeStruct((bsz, HEADS, seq // tk, _vt_rows(FOX_HEAD_DIM), tk), BF16)
    return pl.pallas_call(
        functools.partial(_even_proj_kernel, tiles_per_seq=tiles_per_seq),
        grid=(t // tm,),
        in_specs=[pl.BlockSpec((tm, D_MODEL), lambda i: (i, 0)),
                  _const_spec((1, D_MODEL)), _const_spec(w1.shape), _const_spec((1, LANES)),
                  _const_spec((1, MLA_Q_RANK)), _const_spec(wuq.shape),
                  _const_spec((1, MLA_KV_RANK)), _const_spec(wuk.shape), _const_spec(wuv.shape),
                  _const_spec(ec.shape), rope_spec, rope_spec, rope_spec, _const_spec(cst.shape)],
        out_specs=[row_spec, row_spec, vt_spec, row_spec, row_spec, vt_spec],
        out_shape=[out_sds, out_sds, vt_sds, out_sds, out_sds, vt_sds],
        scratch_shapes=[pltpu.VMEM((SUBLANES, LANES), F32)],
        compiler_params=_params("arbitrary"),
        name="even_proj",
    )(x, g.reshape(1, -1), w1, bf, q_norm.reshape(1, -1), wuq, kv_norm.reshape(1, -1), wuk, wuv,
      ec, ra, rb, rc, cst)


def _alibi_slope(hd):
    return 2.0 ** (-8.0 * (hd + 1) / HEADS)


def _odd_proj_kernel(x_ref, g_ref, w_ref, q_ref, k_ref, v_ref, *, tiles_per_seq):
    tm = x_ref.shape[0]
    i = pl.program_id(0)
    h = _rms(x_ref[...], g_ref[...]).astype(BF16)
    scale = DIFF_HEAD_DIM ** -0.5 * LOG2E
    v = jnp.dot(h, w_ref[:, 2 * HEAD_COLS:], preferred_element_type=F32)
    k = jnp.dot(h, w_ref[:, HEAD_COLS:2 * HEAD_COLS], preferred_element_type=F32)
    _store_vt(v_ref, v, DIFF_V_DIM)
    q = jnp.dot(h, w_ref[:, :HEAD_COLS], preferred_element_type=F32)
    pos = ((i % tiles_per_seq) * tm + lax.broadcasted_iota(jnp.int32, (tm, 1), 0)).astype(F32)
    lane = lax.broadcasted_iota(jnp.int32, (tm, LANES), 1)
    for hd in range(HEADS):
        sl = slice(hd * LANES, (hd + 1) * LANES)
        b_hi, b_mid, b_lo = (p.astype(F32) for p in _split3(pos * (_alibi_slope(hd) * LOG2E)))
        aug = jnp.where(lane == 0, b_hi, jnp.where(lane == 1, b_mid, jnp.where(lane == 2, b_lo, 0.0)))
        k_ref[:, 2 * hd * LANES:(2 * hd + 1) * LANES] = k[:, sl].astype(BF16)
        k_ref[:, (2 * hd + 1) * LANES:(2 * hd + 2) * LANES] = aug.astype(BF16)
    q_ref[...] = (q * scale).astype(BF16)


def _odd_proj(x, g, w_in, bsz, seq):
    t = x.shape[0]
    tm, tk = PROJ_ROWS, ATTN_TILE
    w = w_in.astype(BF16)
    tiles_per_seq = seq // tm
    return pl.pallas_call(
        functools.partial(_odd_proj_kernel, tiles_per_seq=tiles_per_seq),
        grid=(t // tm,),
        in_specs=[pl.BlockSpec((tm, D_MODEL), lambda i: (i, 0)),
                  _const_spec((1, D_MODEL)), _const_spec(w.shape)],
        out_specs=[pl.BlockSpec((tm, HEAD_COLS), lambda i: (i, 0)),
                   pl.BlockSpec((tm, 2 * HEAD_COLS), lambda i: (i, 0)),
                   _vt_spec(DIFF_V_DIM, tm, tk, tiles_per_seq)],
        out_shape=[jax.ShapeDtypeStruct((t, HEAD_COLS), BF16),
                   jax.ShapeDtypeStruct((t, 2 * HEAD_COLS), BF16),
                   jax.ShapeDtypeStruct((bsz, HEADS, seq // tk, _vt_rows(DIFF_V_DIM), tk), BF16)],
        compiler_params=_params("arbitrary"),
        name="odd_proj",
    )(x, g.reshape(1, -1), w)


_NT = (((1,), (1,)), ((), ()))


def _attend(s, vt, m, acc, causal=False):
    if causal:
        key = lax.broadcasted_iota(jnp.int32, s.shape, 0)
        qry = lax.broadcasted_iota(jnp.int32, s.shape, 1)
        s = jnp.where(key <= qry, s, NEG_INF)
    m_new = jnp.maximum(m, jnp.max(s, axis=0, keepdims=True))
    p = jnp.exp2(s - m_new).astype(BF16)
    acc = jnp.exp2(m - m_new) * acc + jnp.dot(vt, p, preferred_element_type=F32)
    return m_new, acc


def _stream_attention(qk, vts, s_ref, rows, n):
    lag, t = s_ref.shape[0], s_ref.shape[1]
    qi = pl.program_id(2)
    assert n >= lag
    for l in range(lag):
        s_ref[l] = qk(l, 0)

    def sweep(j, carry):
        carry = list(carry)
        pending = [s_ref[l] for l in range(lag)]
        for k in range(n):
            ahead = k + lag
            pending.append(qk(ahead, j) if ahead < n else qk(ahead - n, j + 1))
            carry[2 * k], carry[2 * k + 1] = _attend(pending.pop(0), vts(k, j),
                                                     carry[2 * k], carry[2 * k + 1])
        for l in range(lag):
            s_ref[l] = pending[l]
        return tuple(carry)

    init = (jnp.full((1, t), NEG_INF, F32), jnp.zeros((rows, t), F32)) * n
    carry = list(lax.fori_loop(0, qi, sweep, init))

    pending = [s_ref[l] for l in range(lag)]
    for k in range(n):
        if k + lag < n:
            pending.append(qk(k + lag, qi))
        carry[2 * k], carry[2 * k + 1] = _attend(pending.pop(0), vts(k, qi),
                                                 carry[2 * k], carry[2 * k + 1], True)
    return carry[1::2]


def _flash_kernel(q_ref, k_ref, vt_ref, o_ref, s_ref):
    n, rows, tk = vt_ref.shape[1], vt_ref.shape[3], vt_ref.shape[4]
    dv = rows - BF16_ROWS
    qs = [q_ref[0, :, hd * LANES:(hd + 1) * LANES] for hd in range(n)]

    def qk(hd, j):
        start = pl.multiple_of(j * tk, tk)
        kc = k_ref[0, pl.ds(start, tk), hd * LANES:(hd + 1) * LANES]
        return lax.dot_general(kc, qs[hd], _NT, preferred_element_type=F32)

    accs = _stream_attention(qk, lambda hd, j: vt_ref[0, hd, j], s_ref, rows, n)
    o_t = jnp.concatenate([a[:dv] / a[dv:dv + 1] for a in accs], axis=0)
    o_ref[0] = o_t.T.astype(BF16)


def _flash_attention(q, k, vt, tq):
    b, s, _ = q.shape
    _, _, chunks, rows, tk = vt.shape
    assert tq == tk
    n = FLASH_HEADS_PER_STEP
    dv = rows - BF16_ROWS
    return pl.pallas_call(
        _flash_kernel,
        grid=(b, HEADS // n, s // tq),
        in_specs=[pl.BlockSpec((1, tq, n * LANES), lambda bi, hp, i: (bi, i, hp)),
                  pl.BlockSpec((1, s, n * LANES), lambda bi, hp, i: (bi, 0, hp)),
                  pl.BlockSpec((1, n, chunks, rows, tk), lambda bi, hp, i: (bi, hp, 0, 0, 0))],
        out_specs=pl.BlockSpec((1, tq, n * dv), lambda bi, hp, i: (bi, i, hp)),
        out_shape=jax.ShapeDtypeStruct((b, s, HEADS * (rows - BF16_ROWS)), BF16),
        scratch_shapes=[pltpu.VMEM((ATTN_LAG, tk, tq), F32)],
        compiler_params=_params("arbitrary", "arbitrary", "arbitrary"),
        name="flash_attention",
    )(q, k, vt)


def _diff_kernel(q_ref, k_ref, vt_ref, lq1_ref, lk1_ref, lq2_ref, lk2_ref, sub_ref, o_ref, s_ref,
                 *, lambda_init):
    tq = q_ref.shape[1]
    n, rows, tk = vt_ref.shape[1], vt_ref.shape[3], vt_ref.shape[4]
    dv = rows - BF16_ROWS
    lane = lax.broadcasted_iota(jnp.int32, (tq, LANES), 1)
    aug = jnp.where(lane < 3, 1.0, 0.0).astype(BF16)
    qs = []
    for hd in range(n):
        q = q_ref[0, :, hd * LANES:(hd + 1) * LANES]
        zero = jnp.zeros_like(q)
        qs.append(jnp.concatenate([jnp.where(lane < DIFF_HEAD_DIM, q, zero), aug], axis=1))
        qs.append(jnp.concatenate([jnp.where(lane >= DIFF_HEAD_DIM, q, zero), aug], axis=1))

    def qk(st, j):
        hd = st // 2
        start = pl.multiple_of(j * tk, tk)
        kc = k_ref[0, pl.ds(start, tk), 2 * hd * LANES:2 * (hd + 1) * LANES]
        return lax.dot_general(kc, qs[st], _NT, preferred_element_type=F32)

    accs = _stream_attention(qk, lambda st, j: vt_ref[0, st // 2, j], s_ref, rows, 2 * n)
    lam = (jnp.exp(jnp.sum(lq1_ref[...] * lk1_ref[...], axis=1, keepdims=True))
           - jnp.exp(jnp.sum(lq2_ref[...] * lk2_ref[...], axis=1, keepdims=True)) + lambda_init)
    for hd in range(n):
        a1, a2 = accs[2 * hd], accs[2 * hd + 1]
        o = (a1[:dv] / a1[dv:dv + 1] - lam * (a2[:dv] / a2[dv:dv + 1])).T
        o_ref[0, :, hd * dv:(hd + 1) * dv] = (
            _rms(o, sub_ref[...]) * (1.0 - lambda_init)).astype(BF16)


def _diff_attention(q, k, vt, lq1, lk1, lq2, lk2, subln, lambda_init, tq):
    b, s, _ = q.shape
    _, _, chunks, rows, tk = vt.shape
    assert tq == tk
    vec = lambda a: a.reshape(1, -1)
    n = DIFF_HEADS_PER_STEP
    return pl.pallas_call(
        functools.partial(_diff_kernel, lambda_init=lambda_init),
        grid=(b, HEADS // n, s // tq),
        in_specs=[pl.BlockSpec((1, tq, n * LANES), lambda bi, hd, i: (bi, i, hd)),
                  pl.BlockSpec((1, s, 2 * n * LANES), lambda bi, hd, i: (bi, 0, hd)),
                  pl.BlockSpec((1, n, chunks, rows, tk), lambda bi, hd, i: (bi, hd, 0, 0, 0)),
                  _const_spec((1, DIFF_HEAD_DIM)), _const_spec((1, DIFF_HEAD_DIM)),
                  _const_spec((1, DIFF_HEAD_DIM)), _const_spec((1, DIFF_HEAD_DIM)),
                  _const_spec((1, DIFF_V_DIM))],
        out_specs=pl.BlockSpec((1, tq, n * LANES), lambda bi, hd, i: (bi, i, hd)),
        out_shape=jax.ShapeDtypeStruct(q.shape, BF16),
        scratch_shapes=[pltpu.VMEM((ATTN_LAG, tk, tq), F32)],
        compiler_params=_params("arbitrary", "arbitrary", "arbitrary"),
        name="diff_attention",
    )(q, k, vt, vec(lq1), vec(lk1), vec(lq2), vec(lk2), vec(subln))


_N_FFN_CHUNKS = D_FF // FFN_CHUNK
_GELU_C = math.sqrt(2.0 / math.pi)


def _mix_ffn_kernel(*refs, n_in, tiles_per_seq):
    a_refs, wo_refs = refs[:n_in], refs[n_in:2 * n_in]
    (gmix_ref, x_ref, gpre_ref, wup_ref, cw_ref, wdn_ref, gpost_ref, o_ref,
     x1_ref, h_ref, hist_ref, gva_ref, gvb_ref, mid_ref) = refs[2 * n_in:]
    tm = x_ref.shape[0]
    nc = _N_FFN_CHUNKS
    i = pl.program_id(0)

    @pl.when(i % tiles_per_seq == 0)
    def _():
        hist_ref[...] = jnp.zeros_like(hist_ref)

    mix = jnp.dot(a_refs[0][...], wo_refs[0][...], preferred_element_type=F32)
    for a_ref, wo_ref in zip(a_refs[1:], wo_refs[1:]):
        mix = mix + jnp.dot(a_ref[...], wo_ref[...], preferred_element_type=F32)
    x1_ref[...] = x_ref[...] + _rms(mix, gmix_ref[...])
    h_ref[...] = _rms(x1_ref[...], gpre_ref[...]).astype(BF16)

    def up(c):
        return jnp.dot(h_ref[...], wup_ref[c], preferred_element_type=F32)

    def gated(c, gv_ref):
        gv_ref[0:SUBLANES, :FFN_CHUNK] = hist_ref[c]
        gate = gv_ref[SUBLANES:, :FFN_CHUNK]
        val = gv_ref[SUBLANES:, FFN_CHUNK:]
        hist_ref[c] = gv_ref[tm:, :FFN_CHUNK]
        cw = cw_ref[c]
        pre = (gv_ref[SUBLANES - 2:SUBLANES - 2 + tm, :FFN_CHUNK] * cw[0:1, :]
               + gv_ref[SUBLANES - 1:SUBLANES - 1 + tm, :FFN_CHUNK] * cw[1:2, :]
               + gate * cw[2:3, :] + cw[3:4, :])
        pre = pre.astype(BF16)
        act = 0.5 * pre * (1.0 + jnp.tanh(_GELU_C * (pre + 0.044715 * (pre * pre * pre))))
        mid_ref[c] = act * val.astype(BF16)

    assert nc % 2 == 1
    gva_ref[SUBLANES:, :] = up(0)

    def body(k, _):
        gated(2 * k, gva_ref)
        gvb_ref[SUBLANES:, :] = up(2 * k + 1)
        gated(2 * k + 1, gvb_ref)
        gva_ref[SUBLANES:, :] = up(2 * k + 2)
        return 0

    lax.fori_loop(0, nc // 2, body, 0)
    gated(nc - 1, gva_ref)
    f = jnp.dot(mid_ref[0], wdn_ref[0], preferred_element_type=F32)
    for c in range(1, nc):
        f = f + jnp.dot(mid_ref[c], wdn_ref[c], preferred_element_type=F32)
    o_ref[...] = x1_ref[...] + _rms(f, gpost_ref[...])


def _mix_ffn(acts, w_outs, g_mix, x, g_pre, w_up, conv_w, conv_b, w_down, g_post, seq):
    t = x.shape[0]
    tm = FFN_ROWS
    nc = _N_FFN_CHUNKS
    n_in = len(acts)
    wg = w_up[:, :D_FF].reshape(D_MODEL, nc, FFN_CHUNK)
    wv = w_up[:, D_FF:].reshape(D_MODEL, nc, FFN_CHUNK)
    wup = jnp.concatenate([wg, wv], axis=2).transpose(1, 0, 2).astype(BF16)
    wdn = w_down.reshape(nc, FFN_CHUNK, D_MODEL).astype(BF16)
    cw = jnp.concatenate([conv_w, conv_b[None, :],
                          jnp.zeros((SUBLANES - CONV_WIDTH - 1, D_FF), F32)], axis=0)
    cw = cw.reshape(SUBLANES, nc, FFN_CHUNK).transpose(1, 0, 2)
    x_spec = pl.BlockSpec((tm, D_MODEL), lambda i: (i, 0))
    vec_spec = _const_spec((1, D_MODEL))
    return pl.pallas_call(
        functools.partial(_mix_ffn_kernel, n_in=n_in, tiles_per_seq=seq // tm),
        grid=(t // tm,),
        in_specs=([pl.BlockSpec((tm, a.shape[1]), lambda i: (i, 0)) for a in acts]
                  + [_const_spec(w.shape) for w in w_outs]
                  + [vec_spec, x_spec, vec_spec, _const_spec(wup.shape), _const_spec(cw.shape),
                     _const_spec(wdn.shape), vec_spec]),
        out_specs=x_spec,
        out_shape=jax.ShapeDtypeStruct(x.shape, F32),
        scratch_shapes=[pltpu.VMEM((tm, D_MODEL), F32),
                        pltpu.VMEM((tm, D_MODEL), BF16),
                        pltpu.VMEM((nc, SUBLANES, FFN_CHUNK), F32),
                        pltpu.VMEM((tm + SUBLANES, 2 * FFN_CHUNK), F32),
                        pltpu.VMEM((tm + SUBLANES, 2 * FFN_CHUNK), F32),
                        pltpu.VMEM((nc, tm, FFN_CHUNK), BF16)],
        compiler_params=_params("arbitrary"),
        name="mix_ffn",
    )(*acts, *w_outs, g_mix.reshape(1, -1), x, g_pre.reshape(1, -1), wup, cw, wdn,
      g_post.reshape(1, -1))


def kernel(x, norm_mix_pre, norm_mix_post, norm_ffn_pre, norm_ffn_post, even_w_in, even_b_forget, even_q_norm, even_w_uq, even_kv_norm, even_w_ukv, even_w_out, odd_w_in, odd_lambda_q1, odd_lambda_k1, odd_lambda_q2, odd_lambda_k2, odd_subln, odd_w_out, ffn_w_up, ffn_conv_w, ffn_conv_b, ffn_w_down):
    bsz, seq, d = x.shape
    assert d == D_MODEL and seq % ATTN_TILE == 0 and seq % FFN_ROWS == 0
    t = bsz * seq
    tq = ATTN_TILE
    rope = _rope_tables(seq)
    xf = x.reshape(t, d)
    for layer in range(DEPTH):
        i = layer // 2
        if layer % 2 == 0:
            fq, fk, fvt, mq, mk, mvt = _even_proj(
                xf, norm_mix_pre[layer], even_w_in[i], even_b_forget[i], even_q_norm[i],
                even_w_uq[i], even_kv_norm[i], even_w_ukv[i], rope, bsz, seq)
            shp = (bsz, seq, HEAD_COLS)
            fo = _flash_attention(fq.reshape(shp), fk.reshape(shp), fvt, tq)
            mo = _flash_attention(mq.reshape(shp), mk.reshape(shp), mvt, tq)
            w_out = even_w_out[i].astype(BF16)
            acts = [fo.reshape(t, -1), mo.reshape(t, -1)]
            w_outs = [w_out[:FOX_WIDTH], w_out[FOX_WIDTH:]]
        else:
            lambda_init = 0.8 - 0.6 * math.exp(-0.3 * layer)
            q, k, vt = _odd_proj(xf, norm_mix_pre[layer], odd_w_in[i], bsz, seq)
            o = _diff_attention(q.reshape(bsz, seq, -1), k.reshape(bsz, seq, -1), vt,
                                odd_lambda_q1[i], odd_lambda_k1[i],
                                odd_lambda_q2[i], odd_lambda_k2[i], odd_subln[i], lambda_init, tq)
            acts, w_outs = [o.reshape(t, -1)], [odd_w_out[i].astype(BF16)]
        xf = _mix_ffn(acts, w_outs, norm_mix_post[layer], xf, norm_ffn_pre[layer], ffn_w_up[layer],
                      ffn_conv_w[layer], ffn_conv_b[layer], ffn_w_down[layer],
                      norm_ffn_post[layer], seq)
    return xf.reshape(bsz, seq, d)
```

```python
import functools
import math

import numpy as np
import jax
import jax.numpy as jnp
from jax import lax
from jax.experimental import pallas as pl
from jax.experimental.pallas import tpu as pltpu

D_MODEL = 1024
DEPTH = 4
RMS_EPS = 1e-6
NEG_INF = -1e30
HEADS = 8
FOX_HEAD_DIM = 64
FOX_WIDTH = HEADS * FOX_HEAD_DIM
MLA_NOPE_DIM = 64
MLA_ROPE_DIM = 32
MLA_V_DIM = 64
MLA_Q_RANK = 384
MLA_KV_RANK = 256
ROPE_THETA = 10000.0
DIFF_HEAD_DIM = 64
DIFF_V_DIM = 2 * DIFF_HEAD_DIM
D_FF = 2816
CONV_WIDTH = 3

LANES = 128
SUBLANES = 8
BF16_ROWS = 16
LOG2E = 1.4426950408889634
HEAD_COLS = HEADS * LANES
VMEM_LIMIT_BYTES = 56 * 1024 * 1024

PROJ_ROWS = 512
FFN_ROWS = 512
FFN_CHUNK = 256
ATTN_TILE = 512
FLASH_HEADS_PER_STEP, FLASH_KEYS, FLASH_LAG = 8, 256, 3
DIFF_HEADS_PER_STEP, DIFF_KEYS, DIFF_LAG = 4, 512, 2
F32 = jnp.float32
BF16 = jnp.bfloat16


def _params(*sem, flags=None):
    return pltpu.CompilerParams(dimension_semantics=sem, vmem_limit_bytes=VMEM_LIMIT_BYTES,
                                flags=flags)


def _const_spec(shape):
    nd = len(shape)
    return pl.BlockSpec(shape, lambda *_: (0,) * nd, pipeline_mode=pl.Buffered(1))


def _rms(x, g):
    return x * lax.rsqrt(jnp.mean(x * x, axis=-1, keepdims=True) + RMS_EPS) * g


def _split3(x):
    hi = x.astype(BF16)
    r1 = x - hi.astype(F32)
    mid = r1.astype(BF16)
    lo = (r1 - mid.astype(F32)).astype(BF16)
    return hi, mid, lo


def _rope(blk, ra, rb, rc):
    return blk * ra + pltpu.roll(blk, 16, 1) * rb + pltpu.roll(blk, LANES - 16, 1) * rc


def _vt_rows(dv):
    return dv + BF16_ROWS


def _store_vt(vt_ref, v, dv):
    tm = v.shape[0]
    per = LANES // dv
    for blk in range(v.shape[1] // LANES):
        vt = v[:, blk * LANES:(blk + 1) * LANES].T.astype(BF16)
        for s in range(per):
            vt_ref[0, blk * per + s, 0, 0:dv, :] = vt[s * dv:(s + 1) * dv, :]
    row = lax.broadcasted_iota(jnp.int32, (BF16_ROWS, tm), 0)
    ones_row = jnp.where(row == 0, 1.0, 0.0).astype(BF16)
    for hd in range(HEADS):
        vt_ref[0, hd, 0, dv:dv + BF16_ROWS, :] = ones_row


def _vt_spec(dv, tm, tk, tiles_per_seq):
    per_chunk = tk // tm
    return pl.BlockSpec(
        (1, HEADS, 1, _vt_rows(dv), tm),
        lambda i: (i // tiles_per_seq, 0, (i % tiles_per_seq) // per_chunk, 0,
                   (i % tiles_per_seq) % per_chunk))


_EV_Q, _EV_K, _EV_V = 0, FOX_WIDTH, 2 * FOX_WIDTH
_EV_G = _EV_V + FOX_WIDTH
_EV_CQ = _EV_G + LANES
_EV_CKV = _EV_CQ + MLA_Q_RANK
_EV_KR = _EV_CKV + MLA_KV_RANK
_EV_WIDTH = _EV_KR + LANES


def _even_proj_kernel(x_ref, g_ref, w_ref, bf_ref, qn_ref, wuq_ref, kvn_ref, wuk_ref, wuv_ref,
                      ec_ref, ra_ref, rb_ref, rc_ref, cst_ref,
                      fq_ref, fk_ref, fv_ref, mq_ref, mk_ref, mv_ref, carry_ref,
                      *, tiles_per_seq):
    tm = x_ref.shape[0]
    i = pl.program_id(0)

    @pl.when(i % tiles_per_seq == 0)
    def _():
        carry_ref[...] = jnp.zeros_like(carry_ref)

    h = _rms(x_ref[...], g_ref[...]).astype(BF16)
    q_aug = cst_ref[0:1, :]

    def proj(lo, hi):
        return jnp.dot(h, w_ref[:, lo:hi], preferred_element_type=F32)

    lane = lax.broadcasted_iota(jnp.int32, (tm, LANES), 1)

    def spread_heads(o_ref, packed, aug):
        for pair in range(HEADS // 2):
            blk = packed[:, pair * LANES:(pair + 1) * LANES]
            for odd, src in enumerate((blk, pltpu.roll(blk, FOX_HEAD_DIM, 1))):
                sl = slice((2 * pair + odd) * LANES, (2 * pair + odd + 1) * LANES)
                o_ref[:, sl] = jnp.where(lane < FOX_HEAD_DIM, src, aug[:, sl]).astype(BF16)

    ra, rb, rc = ra_ref[...], rb_ref[...], rc_ref[...]
    fox_scale = FOX_HEAD_DIM ** -0.5 * LOG2E
    mla_scale = (MLA_NOPE_DIM + MLA_ROPE_DIM) ** -0.5 * LOG2E

    z = proj(_EV_G, _EV_CQ) + bf_ref[...]
    cq_raw = proj(_EV_CQ, _EV_CKV)
    ckv_raw = proj(_EV_CKV, _EV_KR)
    kr_raw = proj(_EV_KR, _EV_WIDTH)

    logf = (jnp.minimum(z, 0.0) - jnp.log1p(jnp.exp(-jnp.abs(z)))) * LOG2E
    spread_heads(fq_ref, proj(_EV_Q, _EV_K) * fox_scale, q_aug)

    row = lax.broadcasted_iota(jnp.int32, (tm, tm), 0)
    col = lax.broadcasted_iota(jnp.int32, (tm, tm), 1)
    tril = jnp.where(row >= col, 1.0, 0.0).astype(BF16)
    parts = jnp.dot(tril, jnp.concatenate(_split3(logf), axis=1), preferred_element_type=F32)
    _store_vt(fv_ref, proj(_EV_V, _EV_G), FOX_HEAD_DIM)
    c = (parts[:, :LANES] + parts[:, LANES:2 * LANES] + parts[:, 2 * LANES:]
         + carry_ref[SUBLANES - 1:SUBLANES, :])
    carry_ref[...] = c[tm - SUBLANES:, :]

    cq = _rms(cq_raw, qn_ref[...]).astype(BF16)
    q = jnp.dot(cq, wuq_ref[...], preferred_element_type=F32) * mla_scale
    for hd in range(HEADS):
        sl = slice(hd * LANES, (hd + 1) * LANES)
        mq_ref[:, sl] = _rope(q[:, sl], ra, rb, rc).astype(BF16)

    ckv = _rms(ckv_raw, kvn_ref[...]).astype(BF16)
    kr = _rope(kr_raw, ra, rb, rc)
    kn = jnp.dot(ckv, wuk_ref[...], preferred_element_type=F32)
    for hd in range(HEADS):
        sl = slice(hd * LANES, (hd + 1) * LANES)
        mk_ref[:, sl] = (kn[:, sl] + kr).astype(BF16)
    _store_vt(mv_ref, jnp.dot(ckv, wuv_ref[...], preferred_element_type=F32), MLA_V_DIM)

    c_aug = jnp.dot(jnp.concatenate(_split3(c), axis=1), ec_ref[...], preferred_element_type=F32)
    spread_heads(fk_ref, proj(_EV_K, _EV_V), c_aug)


def _pad_heads(w, dh):
    k = w.shape[0]
    w = w.reshape(k, HEADS, dh)
    return jnp.pad(w, ((0, 0), (0, 0), (0, LANES - dh))).reshape(k, HEAD_COLS)


def _even_consts():
    cst = np.zeros((SUBLANES, HEAD_COLS), np.float32)
    ec = np.zeros((3 * LANES, HEAD_COLS), np.float32)
    for hd in range(HEADS):
        cst[0, hd * LANES + 64: hd * LANES + 67] = 1.0
        for p in range(3):
            ec[p * LANES + hd, hd * LANES + 64 + p] = -1.0
    return jnp.asarray(cst), jnp.asarray(ec, dtype=BF16)


def _rope_tables(seq):
    half = MLA_ROPE_DIM // 2
    inv = ROPE_THETA ** (-jnp.arange(0, MLA_ROPE_DIM, 2, dtype=F32) / MLA_ROPE_DIM)
    ang = jnp.arange(seq, dtype=F32)[:, None] * inv[None, :]
    cos, sin = jnp.cos(ang), jnp.sin(ang)
    zeros = jnp.zeros((seq, half), F32)
    ra = jnp.concatenate([jnp.ones((seq, 64), F32), cos, cos, jnp.zeros((seq, 32), F32)], axis=1)
    rb = jnp.concatenate([jnp.zeros((seq, 64), F32), zeros, sin, jnp.zeros((seq, 32), F32)], axis=1)
    rc = jnp.concatenate([jnp.zeros((seq, 64), F32), -sin, zeros, jnp.zeros((seq, 32), F32)], axis=1)
    return ra, rb, rc


def _even_proj(x, g, w_in, b_forget, q_norm, w_uq, kv_norm, w_ukv, rope, bsz, seq):
    t = x.shape[0]
    tm, tk = PROJ_ROWS, ATTN_TILE
    cuts = np.cumsum([FOX_WIDTH, FOX_WIDTH, FOX_WIDTH, HEADS, MLA_Q_RANK, MLA_KV_RANK]).tolist()
    wfq, wfk, wfv, wg, wcq, wckv, wkr = jnp.split(w_in, cuts, axis=1)
    w1 = jnp.concatenate([
        wfq, wfk, wfv,
        jnp.pad(wg, ((0, 0), (0, LANES - HEADS))), wcq, wckv,
        jnp.pad(wkr, ((0, 0), (64, LANES - 64 - MLA_ROPE_DIM)))], axis=1).astype(BF16)
    assert w1.shape[1] == _EV_WIDTH
    bf = jnp.pad(b_forget, (0, LANES - HEADS)).reshape(1, LANES)
    wuq = _pad_heads(w_uq, MLA_NOPE_DIM + MLA_ROPE_DIM).astype(BF16)
    wukv = w_ukv.reshape(MLA_KV_RANK, HEADS, MLA_NOPE_DIM + MLA_V_DIM)
    wuk = _pad_heads(wukv[:, :, :MLA_NOPE_DIM].reshape(MLA_KV_RANK, -1), MLA_NOPE_DIM).astype(BF16)
    wuv = wukv[:, :, MLA_NOPE_DIM:].reshape(MLA_KV_RANK, -1).astype(BF16)
    cst, ec = _even_consts()
    ra, rb, rc = rope
    tiles_per_seq = seq // tm
    rope_spec = pl.BlockSpec((tm, LANES), lambda i: (i % tiles_per_seq, 0))
    row_spec = pl.BlockSpec((tm, HEAD_COLS), lambda i: (i, 0))
    out_sds = jax.ShapeDtypeStruct((t, HEAD_COLS), BF16)
    vt_spec = _vt_spec(FOX_HEAD_DIM, tm, tk, tiles_per_seq)
    vt_sds = jax.ShapeDtypeStruct((bsz, HEADS, seq // tk, _vt_rows(FOX_HEAD_DIM), tk), BF16)
    return pl.pallas_call(
        functools.partial(_even_proj_kernel, tiles_per_seq=tiles_per_seq),
        grid=(t // tm,),
        in_specs=[pl.BlockSpec((tm, D_MODEL), lambda i: (i, 0)),
                  _const_spec((1, D_MODEL)), _const_spec(w1.shape), _const_spec((1, LANES)),
                  _const_spec((1, MLA_Q_RANK)), _const_spec(wuq.shape),
                  _const_spec((1, MLA_KV_RANK)), _const_spec(wuk.shape), _const_spec(wuv.shape),
                  _const_spec(ec.shape), rope_spec, rope_spec, rope_spec, _const_spec(cst.shape)],
        out_specs=[row_spec, row_spec, vt_spec, row_spec, row_spec, vt_spec],
        out_shape=[out_sds, out_sds, vt_sds, out_sds, out_sds, vt_sds],
        scratch_shapes=[pltpu.VMEM((SUBLANES, LANES), F32)],
        compiler_params=_params("arbitrary"),
        name="even_proj",
    )(x, g.reshape(1, -1), w1, bf, q_norm.reshape(1, -1), wuq, kv_norm.reshape(1, -1), wuk, wuv,
      ec, ra, rb, rc, cst)


def _alibi_slope(hd):
    return 2.0 ** (-8.0 * (hd + 1) / HEADS)


def _odd_proj_kernel(x_ref, g_ref, w_ref, q_ref, k_ref, v_ref, *, tiles_per_seq):
    tm = x_ref.shape[0]
    i = pl.program_id(0)
    h = _rms(x_ref[...], g_ref[...]).astype(BF16)
    scale = DIFF_HEAD_DIM ** -0.5 * LOG2E
    v = jnp.dot(h, w_ref[:, 2 * HEAD_COLS:], preferred_element_type=F32)
    k = jnp.dot(h, w_ref[:, HEAD_COLS:2 * HEAD_COLS], preferred_element_type=F32)
    _store_vt(v_ref, v, DIFF_V_DIM)
    q = jnp.dot(h, w_ref[:, :HEAD_COLS], preferred_element_type=F32)
    pos = ((i % tiles_per_seq) * tm + lax.broadcasted_iota(jnp.int32, (tm, 1), 0)).astype(F32)
    lane = lax.broadcasted_iota(jnp.int32, (tm, LANES), 1)
    for hd in range(HEADS):
        sl = slice(hd * LANES, (hd + 1) * LANES)
        b_hi, b_mid, b_lo = (p.astype(F32) for p in _split3(pos * (_alibi_slope(hd) * LOG2E)))
        aug = jnp.where(lane == 0, b_hi, jnp.where(lane == 1, b_mid, jnp.where(lane == 2, b_lo, 0.0)))
        k_ref[:, 2 * hd * LANES:(2 * hd + 1) * LANES] = k[:, sl].astype(BF16)
        k_ref[:, (2 * hd + 1) * LANES:(2 * hd + 2) * LANES] = aug.astype(BF16)
    q_ref[...] = (q * scale).astype(BF16)


def _odd_proj(x, g, w_in, bsz, seq):
    t = x.shape[0]
    tm, tk = PROJ_ROWS, ATTN_TILE
    w = w_in.astype(BF16)
    tiles_per_seq = seq // tm
    return pl.pallas_call(
        functools.partial(_odd_proj_kernel, tiles_per_seq=tiles_per_seq),
        grid=(t // tm,),
        in_specs=[pl.BlockSpec((tm, D_MODEL), lambda i: (i, 0)),
                  _const_spec((1, D_MODEL)), _const_spec(w.shape)],
        out_specs=[pl.BlockSpec((tm, HEAD_COLS), lambda i: (i, 0)),
                   pl.BlockSpec((tm, 2 * HEAD_COLS), lambda i: (i, 0)),
                   _vt_spec(DIFF_V_DIM, tm, tk, tiles_per_seq)],
        out_shape=[jax.ShapeDtypeStruct((t, HEAD_COLS), BF16),
                   jax.ShapeDtypeStruct((t, 2 * HEAD_COLS), BF16),
                   jax.ShapeDtypeStruct((bsz, HEADS, seq // tk, _vt_rows(DIFF_V_DIM), tk), BF16)],
        compiler_params=_params("arbitrary"),
        name="odd_proj",
    )(x, g.reshape(1, -1), w)


_NT = (((1,), (1,)), ((), ()))


def _attend(s, vt, m, acc, key0=None):
    if key0 is not None:
        key = lax.broadcasted_iota(jnp.int32, s.shape, 0) + key0
        qry = lax.broadcasted_iota(jnp.int32, s.shape, 1)
        s = jnp.where(key <= qry, s, NEG_INF)
    m_new = jnp.maximum(m, jnp.max(s, axis=0, keepdims=True))
    p = jnp.exp2(s - m_new).astype(BF16)
    acc = jnp.exp2(m - m_new) * acc + jnp.dot(vt, p, preferred_element_type=F32)
    return m_new, acc


def _stream_attention(qk, vts, s_ref, rows, n):
    lag, tk, t = s_ref.shape
    r = t // tk
    qi = pl.program_id(2)
    assert n >= lag
    for l in range(lag):
        s_ref[l] = qk(l, 0, 0)

    def sweep(j, u, carry, diagonal):
        carry = list(carry)
        nxt = (j, u + 1) if u + 1 < r else (j + 1, 0)
        last = diagonal and u + 1 == r
        pending = [s_ref[l] for l in range(lag)]
        for k in range(n):
            ahead = k + lag
            if ahead < n:
                pending.append(qk(ahead, j, u))
            elif not last:
                pending.append(qk(ahead - n, *nxt))
            carry[2 * k], carry[2 * k + 1] = _attend(pending.pop(0), vts(k, j, u), carry[2 * k],
                                                     carry[2 * k + 1], u * tk if diagonal else None)
        if not last:
            for l in range(lag):
                s_ref[l] = pending[l]
        return tuple(carry)

    def body(j, carry):
        for u in range(r):
            carry = sweep(j, u, carry, False)
        return carry

    init = (jnp.full((1, t), NEG_INF, F32), jnp.zeros((rows, t), F32)) * n
    carry = lax.fori_loop(0, qi, body, init)
    for u in range(r):
        carry = sweep(qi, u, carry, True)
    return carry[1::2]


def _flash_kernel(q_ref, k_ref, vt_ref, o_ref, s_ref):
    n, rows, t = vt_ref.shape[1], vt_ref.shape[3], vt_ref.shape[4]
    tk = s_ref.shape[1]
    dv = rows - BF16_ROWS
    qs = [q_ref[0, :, hd * LANES:(hd + 1) * LANES] for hd in range(n)]

    def qk(hd, j, u):
        start = pl.multiple_of(j * t, t) + u * tk
        kc = k_ref[0, pl.ds(start, tk), hd * LANES:(hd + 1) * LANES]
        return lax.dot_general(kc, qs[hd], _NT, preferred_element_type=F32)

    def vts(hd, j, u):
        return vt_ref[0, hd, j, :, u * tk:(u + 1) * tk]

    accs = _stream_attention(qk, vts, s_ref, rows, n)
    o_t = jnp.concatenate([a[:dv] / a[dv:dv + 1] for a in accs], axis=0)
    o_ref[0] = o_t.T.astype(BF16)


def _flash_attention(q, k, vt, tq):
    b, s, _ = q.shape
    _, _, chunks, rows, tk = vt.shape
    assert tq == tk
    n = FLASH_HEADS_PER_STEP
    dv = rows - BF16_ROWS
    return pl.pallas_call(
        _flash_kernel,
        grid=(b, HEADS // n, s // tq),
        in_specs=[pl.BlockSpec((1, tq, n * LANES), lambda bi, hp, i: (bi, i, hp)),
                  pl.BlockSpec((1, s, n * LANES), lambda bi, hp, i: (bi, 0, hp)),
                  pl.BlockSpec((1, n, chunks, rows, tk), lambda bi, hp, i: (bi, hp, 0, 0, 0))],
        out_specs=pl.BlockSpec((1, tq, n * dv), lambda bi, hp, i: (bi, i, hp)),
        out_shape=jax.ShapeDtypeStruct((b, s, HEADS * (rows - BF16_ROWS)), BF16),
        scratch_shapes=[pltpu.VMEM((FLASH_LAG, FLASH_KEYS, tq), F32)],
        compiler_params=_params("arbitrary", "arbitrary", "arbitrary"),
        name="flash_attention",
    )(q, k, vt)


def _diff_kernel(q_ref, k_ref, vt_ref, lq1_ref, lk1_ref, lq2_ref, lk2_ref, sub_ref, o_ref, s_ref,
                 *, lambda_init):
    tq = q_ref.shape[1]
    n, rows, t = vt_ref.shape[1], vt_ref.shape[3], vt_ref.shape[4]
    tk = s_ref.shape[1]
    dv = rows - BF16_ROWS
    lane = lax.broadcasted_iota(jnp.int32, (tq, LANES), 1)
    aug = jnp.where(lane < 3, 1.0, 0.0).astype(BF16)
    qs = []
    for hd in range(n):
        q = q_ref[0, :, hd * LANES:(hd + 1) * LANES]
        zero = jnp.zeros_like(q)
        qs.append(jnp.concatenate([jnp.where(lane < DIFF_HEAD_DIM, q, zero), aug], axis=1))
        qs.append(jnp.concatenate([jnp.where(lane >= DIFF_HEAD_DIM, q, zero), aug], axis=1))

    def qk(st, j, u):
        hd = st // 2
        start = pl.multiple_of(j * t, t) + u * tk
        kc = k_ref[0, pl.ds(start, tk), 2 * hd * LANES:2 * (hd + 1) * LANES]
        return lax.dot_general(kc, qs[st], _NT, preferred_element_type=F32)

    def vts(st, j, u):
        return vt_ref[0, st // 2, j, :, u * tk:(u + 1) * tk]

    accs = _stream_attention(qk, vts, s_ref, rows, 2 * n)
    lam = (jnp.exp(jnp.sum(lq1_ref[...] * lk1_ref[...], axis=1, keepdims=True))
           - jnp.exp(jnp.sum(lq2_ref[...] * lk2_ref[...], axis=1, keepdims=True)) + lambda_init)
    for hd in range(n):
        a1, a2 = accs[2 * hd], accs[2 * hd + 1]
        o = (a1[:dv] / a1[dv:dv + 1] - lam * (a2[:dv] / a2[dv:dv + 1])).T
        o_ref[0, :, hd * dv:(hd + 1) * dv] = (
            _rms(o, sub_ref[...]) * (1.0 - lambda_init)).astype(BF16)


def _diff_attention(q, k, vt, lq1, lk1, lq2, lk2, subln, lambda_init, tq):
    b, s, _ = q.shape
    _, _, chunks, rows, tk = vt.shape
    assert tq == tk
    vec = lambda a: a.reshape(1, -1)
    n = DIFF_HEADS_PER_STEP
    return pl.pallas_call(
        functools.partial(_diff_kernel, lambda_init=lambda_init),
        grid=(b, HEADS // n, s // tq),
        in_specs=[pl.BlockSpec((1, tq, n * LANES), lambda bi, hd, i: (bi, i, hd)),
                  pl.BlockSpec((1, s, 2 * n * LANES), lambda bi, hd, i: (bi, 0, hd)),
                  pl.BlockSpec((1, n, chunks, rows, tk), lambda bi, hd, i: (bi, hd, 0, 0, 0)),
                  _const_spec((1, DIFF_HEAD_DIM)), _const_spec((1, DIFF_HEAD_DIM)),
                  _const_spec((1, DIFF_HEAD_DIM)), _const_spec((1, DIFF_HEAD_DIM)),
                  _const_spec((1, DIFF_V_DIM))],
        out_specs=pl.BlockSpec((1, tq, n * LANES), lambda bi, hd, i: (bi, i, hd)),
        out_shape=jax.ShapeDtypeStruct(q.shape, BF16),
        scratch_shapes=[pltpu.VMEM((DIFF_LAG, DIFF_KEYS, tq), F32)],
        compiler_params=_params("arbitrary", "arbitrary", "arbitrary"),
        name="diff_attention",
    )(q, k, vt, vec(lq1), vec(lk1), vec(lq2), vec(lk2), vec(subln))


_N_FFN_CHUNKS = D_FF // FFN_CHUNK
_GELU_C = math.sqrt(2.0 / math.pi)


def _mix_ffn_kernel(*refs, n_in, tiles_per_seq):
    a_refs, wo_refs = refs[:n_in], refs[n_in:2 * n_in]
    (gmix_ref, x_ref, gpre_ref, wup_ref, cw_ref, wdn_ref, gpost_ref, o_ref,
     x1_ref, h_ref, hist_ref, gva_ref, gvb_ref, mid_ref) = refs[2 * n_in:]
    tm = x_ref.shape[0]
    nc = _N_FFN_CHUNKS
    i = pl.program_id(0)

    @pl.when(i % tiles_per_seq == 0)
    def _():
        hist_ref[...] = jnp.zeros_like(hist_ref)

    mix = jnp.dot(a_refs[0][...], wo_refs[0][...], preferred_element_type=F32)
    for a_ref, wo_ref in zip(a_refs[1:], wo_refs[1:]):
        mix = mix + jnp.dot(a_ref[...], wo_ref[...], preferred_element_type=F32)
    x1_ref[...] = x_ref[...] + _rms(mix, gmix_ref[...])
    h_ref[...] = _rms(x1_ref[...], gpre_ref[...]).astype(BF16)

    def up(c):
        return jnp.dot(h_ref[...], wup_ref[c], preferred_element_type=F32)

    def gated(c, gv_ref):
        gv_ref[0:SUBLANES, :FFN_CHUNK] = hist_ref[c]
        gate = gv_ref[SUBLANES:, :FFN_CHUNK]
        val = gv_ref[SUBLANES:, FFN_CHUNK:]
        hist_ref[c] = gv_ref[tm:, :FFN_CHUNK]
        cw = cw_ref[c]
        pre = (gv_ref[SUBLANES - 2:SUBLANES - 2 + tm, :FFN_CHUNK] * cw[0:1, :]
               + gv_ref[SUBLANES - 1:SUBLANES - 1 + tm, :FFN_CHUNK] * cw[1:2, :]
               + gate * cw[2:3, :] + cw[3:4, :])
        pre = pre.astype(BF16)
        act = 0.5 * pre * (1.0 + jnp.tanh(_GELU_C * (pre + 0.044715 * (pre * pre * pre))))
        mid_ref[c] = act * val.astype(BF16)

    assert nc % 2 == 1
    gva_ref[SUBLANES:, :] = up(0)

    def body(k, _):
        gated(2 * k, gva_ref)
        gvb_ref[SUBLANES:, :] = up(2 * k + 1)
        gated(2 * k + 1, gvb_ref)
        gva_ref[SUBLANES:, :] = up(2 * k + 2)
        return 0

    lax.fori_loop(0, nc // 2, body, 0)
    gated(nc - 1, gva_ref)
    f = jnp.dot(mid_ref[0], wdn_ref[0], preferred_element_type=F32)
    for c in range(1, nc):
        f = f + jnp.dot(mid_ref[c], wdn_ref[c], preferred_element_type=F32)
    o_ref[...] = x1_ref[...] + _rms(f, gpost_ref[...])


def _mix_ffn(acts, w_outs, g_mix, x, g_pre, w_up, conv_w, conv_b, w_down, g_post, seq):
    t = x.shape[0]
    tm = FFN_ROWS
    nc = _N_FFN_CHUNKS
    n_in = len(acts)
    wg = w_up[:, :D_FF].reshape(D_MODEL, nc, FFN_CHUNK)
    wv = w_up[:, D_FF:].reshape(D_MODEL, nc, FFN_CHUNK)
    wup = jnp.concatenate([wg, wv], axis=2).transpose(1, 0, 2).astype(BF16)
    wdn = w_down.reshape(nc, FFN_CHUNK, D_MODEL).astype(BF16)
    cw = jnp.concatenate([conv_w, conv_b[None, :],
                          jnp.zeros((SUBLANES - CONV_WIDTH - 1, D_FF), F32)], axis=0)
    cw = cw.reshape(SUBLANES, nc, FFN_CHUNK).transpose(1, 0, 2)
    x_spec = pl.BlockSpec((tm, D_MODEL), lambda i: (i, 0))
    vec_spec = _const_spec((1, D_MODEL))
    return pl.pallas_call(
        functools.partial(_mix_ffn_kernel, n_in=n_in, tiles_per_seq=seq // tm),
        grid=(t // tm,),
        in_specs=([pl.BlockSpec((tm, a.shape[1]), lambda i: (i, 0)) for a in acts]
                  + [_const_spec(w.shape) for w in w_outs]
                  + [vec_spec, x_spec, vec_spec, _const_spec(wup.shape), _const_spec(cw.shape),
                     _const_spec(wdn.shape), vec_spec]),
        out_specs=x_spec,
        out_shape=jax.ShapeDtypeStruct(x.shape, F32),
        scratch_shapes=[pltpu.VMEM((tm, D_MODEL), F32),
                        pltpu.VMEM((tm, D_MODEL), BF16),
                        pltpu.VMEM((nc, SUBLANES, FFN_CHUNK), F32),
                        pltpu.VMEM((tm + SUBLANES, 2 * FFN_CHUNK), F32),
                        pltpu.VMEM((tm + SUBLANES, 2 * FFN_CHUNK), F32),
                        pltpu.VMEM((nc, tm, FFN_CHUNK), BF16)],
        compiler_params=_params("arbitrary"),
        name="mix_ffn",
    )(*acts, *w_outs, g_mix.reshape(1, -1), x, g_pre.reshape(1, -1), wup, cw, wdn,
      g_post.reshape(1, -1))


def kernel(x, norm_mix_pre, norm_mix_post, norm_ffn_pre, norm_ffn_post, even_w_in, even_b_forget, even_q_norm, even_w_uq, even_kv_norm, even_w_ukv, even_w_out, odd_w_in, odd_lambda_q1, odd_lambda_k1, odd_lambda_q2, odd_lambda_k2, odd_subln, odd_w_out, ffn_w_up, ffn_conv_w, ffn_conv_b, ffn_w_down):
    bsz, seq, d = x.shape
    assert d == D_MODEL and seq % ATTN_TILE == 0 and seq % FFN_ROWS == 0
    t = bsz * seq
    tq = ATTN_TILE
    rope = _rope_tables(seq)
    xf = x.reshape(t, d)
    for layer in range(DEPTH):
        i = layer // 2
        if layer % 2 == 0:
            fq, fk, fvt, mq, mk, mvt = _even_proj(
                xf, norm_mix_pre[layer], even_w_in[i], even_b_forget[i], even_q_norm[i],
                even_w_uq[i], even_kv_norm[i], even_w_ukv[i], rope, bsz, seq)
            shp = (bsz, seq, HEAD_COLS)
            fo = _flash_attention(fq.reshape(shp), fk.reshape(shp), fvt, tq)
            mo = _flash_attention(mq.reshape(shp), mk.reshape(shp), mvt, tq)
            w_out = even_w_out[i].astype(BF16)
            acts = [fo.reshape(t, -1), mo.reshape(t, -1)]
            w_outs = [w_out[:FOX_WIDTH], w_out[FOX_WIDTH:]]
        else:
            lambda_init = 0.8 - 0.6 * math.exp(-0.3 * layer)
            q, k, vt = _odd_proj(xf, norm_mix_pre[layer], odd_w_in[i], bsz, seq)
            o = _diff_attention(q.reshape(bsz, seq, -1), k.reshape(bsz, seq, -1), vt,
                                odd_lambda_q1[i], odd_lambda_k1[i],
                                odd_lambda_q2[i], odd_lambda_k2[i], odd_subln[i], lambda_init, tq)
            acts, w_outs = [o.reshape(t, -1)], [odd_w_out[i].astype(BF16)]
        xf = _mix_ffn(acts, w_outs, norm_mix_post[layer], xf, norm_ffn_pre[layer], ffn_w_up[layer],
                      ffn_conv_w[layer], ffn_conv_b[layer], ffn_w_down[layer],
                      norm_ffn_post[layer], seq)
    return xf.reshape(bsz, seq, d)
```

```python
import functools
import math

import numpy as np
import jax
import jax.numpy as jnp
from jax import lax
from jax.experimental import pallas as pl
from jax.experimental.pallas import tpu as pltpu

D_MODEL = 1024
DEPTH = 4
RMS_EPS = 1e-6
NEG_INF = -1e30
HEADS = 8
FOX_HEAD_DIM = 64
FOX_WIDTH = HEADS * FOX_HEAD_DIM
MLA_NOPE_DIM = 64
MLA_ROPE_DIM = 32
MLA_V_DIM = 64
MLA_Q_RANK = 384
MLA_KV_RANK = 256
ROPE_THETA = 10000.0
DIFF_HEAD_DIM = 64
DIFF_V_DIM = 2 * DIFF_HEAD_DIM
D_FF = 2816
CONV_WIDTH = 3

LANES = 128
SUBLANES = 8
BF16_ROWS = 16
LOG2E = 1.4426950408889634
HEAD_COLS = HEADS * LANES
VMEM_LIMIT_BYTES = 56 * 1024 * 1024

PROJ_ROWS = 512
FFN_ROWS = 512
FFN_CHUNK = 256
ATTN_TILE = 512
FLASH_HEADS_PER_STEP, FLASH_KEYS, FLASH_LAG = 8, 256, 3
DIFF_HEADS_PER_STEP, DIFF_KEYS, DIFF_LAG = 4, 512, 2
F32 = jnp.float32
BF16 = jnp.bfloat16


def _params(*sem, flags=None):
    return pltpu.CompilerParams(dimension_semantics=sem, vmem_limit_bytes=VMEM_LIMIT_BYTES,
                                flags=flags)


def _const_spec(shape):
    nd = len(shape)
    return pl.BlockSpec(shape, lambda *_: (0,) * nd, pipeline_mode=pl.Buffered(1))


def _rms(x, g):
    return x * lax.rsqrt(jnp.mean(x * x, axis=-1, keepdims=True) + RMS_EPS) * g


def _split3(x):
    hi = x.astype(BF16)
    r1 = x - hi.astype(F32)
    mid = r1.astype(BF16)
    lo = (r1 - mid.astype(F32)).astype(BF16)
    return hi, mid, lo


def _rope(blk, ra, rb, rc):
    return blk * ra + pltpu.roll(blk, 16, 1) * rb + pltpu.roll(blk, LANES - 16, 1) * rc


def _vt_rows(dv):
    return dv + BF16_ROWS


def _store_vt(vt_ref, v, dv):
    tm = v.shape[0]
    per = LANES // dv
    for blk in range(v.shape[1] // LANES):
        vt = v[:, blk * LANES:(blk + 1) * LANES].T.astype(BF16)
        for s in range(per):
            vt_ref[0, blk * per + s, 0, 0:dv, :] = vt[s * dv:(s + 1) * dv, :]
    row = lax.broadcasted_iota(jnp.int32, (BF16_ROWS, tm), 0)
    ones_row = jnp.where(row == 0, 1.0, 0.0).astype(BF16)
    for hd in range(HEADS):
        vt_ref[0, hd, 0, dv:dv + BF16_ROWS, :] = ones_row


def _vt_spec(dv, tm, tk, tiles_per_seq):
    per_chunk = tk // tm
    return pl.BlockSpec(
        (1, HEADS, 1, _vt_rows(dv), tm),
        lambda i: (i // tiles_per_seq, 0, (i % tiles_per_seq) // per_chunk, 0,
                   (i % tiles_per_seq) % per_chunk))


_EV_Q, _EV_K, _EV_V = 0, FOX_WIDTH, 2 * FOX_WIDTH
_EV_G = _EV_V + FOX_WIDTH
_EV_CQ = _EV_G + LANES
_EV_CKV = _EV_CQ + MLA_Q_RANK
_EV_KR = _EV_CKV + MLA_KV_RANK
_EV_WIDTH = _EV_KR + LANES


def _even_proj_kernel(x_ref, g_ref, w_ref, bf_ref, qn_ref, wuq_ref, kvn_ref, wuk_ref, wuv_ref,
                      ec_ref, ra_ref, rb_ref, rc_ref, cst_ref,
                      fq_ref, fk_ref, fv_ref, mq_ref, mk_ref, mv_ref, carry_ref,
                      *, tiles_per_seq):
    tm = x_ref.shape[0]
    i = pl.program_id(0)

    @pl.when(i % tiles_per_seq == 0)
    def _():
        carry_ref[...] = jnp.zeros_like(carry_ref)

    h = _rms(x_ref[...], g_ref[...]).astype(BF16)
    q_aug = cst_ref[0:1, :]

    def proj(lo, hi):
        return jnp.dot(h, w_ref[:, lo:hi], preferred_element_type=F32)

    lane = lax.broadcasted_iota(jnp.int32, (tm, LANES), 1)

    def spread_heads(o_ref, packed, aug):
        for pair in range(HEADS // 2):
            blk = packed[:, pair * LANES:(pair + 1) * LANES]
            for odd, src in enumerate((blk, pltpu.roll(blk, FOX_HEAD_DIM, 1))):
                sl = slice((2 * pair + odd) * LANES, (2 * pair + odd + 1) * LANES)
                o_ref[:, sl] = jnp.where(lane < FOX_HEAD_DIM, src, aug[:, sl]).astype(BF16)

    ra, rb, rc = ra_ref[...], rb_ref[...], rc_ref[...]
    fox_scale = FOX_HEAD_DIM ** -0.5 * LOG2E
    mla_scale = (MLA_NOPE_DIM + MLA_ROPE_DIM) ** -0.5 * LOG2E

    z = proj(_EV_G, _EV_CQ) + bf_ref[...]
    cq_raw = proj(_EV_CQ, _EV_CKV)
    ckv_raw = proj(_EV_CKV, _EV_KR)
    kr_raw = proj(_EV_KR, _EV_WIDTH)

    logf = (jnp.minimum(z, 0.0) - jnp.log1p(jnp.exp(-jnp.abs(z)))) * LOG2E
    spread_heads(fq_ref, proj(_EV_Q, _EV_K) * fox_scale, q_aug)

    row = lax.broadcasted_iota(jnp.int32, (tm, tm), 0)
    col = lax.broadcasted_iota(jnp.int32, (tm, tm), 1)
    tril = jnp.where(row >= col, 1.0, 0.0).astype(BF16)
    parts = jnp.dot(tril, jnp.concatenate(_split3(logf), axis=1), preferred_element_type=F32)
    _store_vt(fv_ref, proj(_EV_V, _EV_G), FOX_HEAD_DIM)
    c = (parts[:, :LANES] + parts[:, LANES:2 * LANES] + parts[:, 2 * LANES:]
         + carry_ref[SUBLANES - 1:SUBLANES, :])
    carry_ref[...] = c[tm - SUBLANES:, :]

    cq = _rms(cq_raw, qn_ref[...]).astype(BF16)
    q = jnp.dot(cq, wuq_ref[...], preferred_element_type=F32) * mla_scale
    for hd in range(HEADS):
        sl = slice(hd * LANES, (hd + 1) * LANES)
        mq_ref[:, sl] = _rope(q[:, sl], ra, rb, rc).astype(BF16)

    ckv = _rms(ckv_raw, kvn_ref[...]).astype(BF16)
    kr = _rope(kr_raw, ra, rb, rc)
    kn = jnp.dot(ckv, wuk_ref[...], preferred_element_type=F32)
    for hd in range(HEADS):
        sl = slice(hd * LANES, (hd + 1) * LANES)
        mk_ref[:, sl] = (kn[:, sl] + kr).astype(BF16)
    _store_vt(mv_ref, jnp.dot(ckv, wuv_ref[...], preferred_element_type=F32), MLA_V_DIM)

    c_aug = jnp.dot(jnp.concatenate(_split3(c), axis=1), ec_ref[...], preferred_element_type=F32)
    spread_heads(fk_ref, proj(_EV_K, _EV_V), c_aug)


def _pad_heads(w, dh):
    k = w.shape[0]
    w = w.reshape(k, HEADS, dh)
    return jnp.pad(w, ((0, 0), (0, 0), (0, LANES - dh))).reshape(k, HEAD_COLS)


def _even_consts():
    cst = np.zeros((SUBLANES, HEAD_COLS), np.float32)
    ec = np.zeros((3 * LANES, HEAD_COLS), np.float32)
    for hd in range(HEADS):
        cst[0, hd * LANES + 64: hd * LANES + 67] = 1.0
        for p in range(3):
            ec[p * LANES + hd, hd * LANES + 64 + p] = -1.0
    return jnp.asarray(cst), jnp.asarray(ec, dtype=BF16)


def _rope_tables(seq):
    half = MLA_ROPE_DIM // 2
    inv = ROPE_THETA ** (-jnp.arange(0, MLA_ROPE_DIM, 2, dtype=F32) / MLA_ROPE_DIM)
    ang = jnp.arange(seq, dtype=F32)[:, None] * inv[None, :]
    cos, sin = jnp.cos(ang), jnp.sin(ang)
    zeros = jnp.zeros((seq, half), F32)
    ra = jnp.concatenate([jnp.ones((seq, 64), F32), cos, cos, jnp.zeros((seq, 32), F32)], axis=1)
    rb = jnp.concatenate([jnp.zeros((seq, 64), F32), zeros, sin, jnp.zeros((seq, 32), F32)], axis=1)
    rc = jnp.concatenate([jnp.zeros((seq, 64), F32), -sin, zeros, jnp.zeros((seq, 32), F32)], axis=1)
    return ra, rb, rc


def _even_proj(x, g, w_in, b_forget, q_norm, w_uq, kv_norm, w_ukv, rope, bsz, seq):
    t = x.shape[0]
    tm, tk = PROJ_ROWS, ATTN_TILE
    cuts = np.cumsum([FOX_WIDTH, FOX_WIDTH, FOX_WIDTH, HEADS, MLA_Q_RANK, MLA_KV_RANK]).tolist()
    wfq, wfk, wfv, wg, wcq, wckv, wkr = jnp.split(w_in, cuts, axis=1)
    w1 = jnp.concatenate([
        wfq, wfk, wfv,
        jnp.pad(wg, ((0, 0), (0, LANES - HEADS))), wcq, wckv,
        jnp.pad(wkr, ((0, 0), (64, LANES - 64 - MLA_ROPE_DIM)))], axis=1).astype(BF16)
    assert w1.shape[1] == _EV_WIDTH
    bf = jnp.pad(b_forget, (0, LANES - HEADS)).reshape(1, LANES)
    wuq = _pad_heads(w_uq, MLA_NOPE_DIM + MLA_ROPE_DIM).astype(BF16)
    wukv = w_ukv.reshape(MLA_KV_RANK, HEADS, MLA_NOPE_DIM + MLA_V_DIM)
    wuk = _pad_heads(wukv[:, :, :MLA_NOPE_DIM].reshape(MLA_KV_RANK, -1), MLA_NOPE_DIM).astype(BF16)
    wuv = wukv[:, :, MLA_NOPE_DIM:].reshape(MLA_KV_RANK, -1).astype(BF16)
    cst, ec = _even_consts()
    ra, rb, rc = rope
    tiles_per_seq = seq // tm
    rope_spec = pl.BlockSpec((tm, LANES), lambda i: (i % tiles_per_seq, 0))
    row_spec = pl.BlockSpec((tm, HEAD_COLS), lambda i: (i, 0))
    out_sds = jax.ShapeDtypeStruct((t, HEAD_COLS), BF16)
    vt_spec = _vt_spec(FOX_HEAD_DIM, tm, tk, tiles_per_seq)
    vt_sds = jax.ShapeDtypeStruct((bsz, HEADS, seq // tk, _vt_rows(FOX_HEAD_DIM), tk), BF16)
    return pl.pallas_call(
        functools.partial(_even_proj_kernel, tiles_per_seq=tiles_per_seq),
        grid=(t // tm,),
        in_specs=[pl.BlockSpec((tm, D_MODEL), lambda i: (i, 0)),
                  _const_spec((1, D_MODEL)), _const_spec(w1.shape), _const_spec((1, LANES)),
                  _const_spec((1, MLA_Q_RANK)), _const_spec(wuq.shape),
                  _const_spec((1, MLA_KV_RANK)), _const_spec(wuk.shape), _const_spec(wuv.shape),
                  _const_spec(ec.shape), rope_spec, rope_spec, rope_spec, _const_spec(cst.shape)],
        out_specs=[row_spec, row_spec, vt_spec, row_spec, row_spec, vt_spec],
        out_shape=[out_sds, out_sds, vt_sds, out_sds, out_sds, vt_sds],
        scratch_shapes=[pltpu.VMEM((SUBLANES, LANES), F32)],
        compiler_params=_params("arbitrary"),
        name="even_proj",
    )(x, g.reshape(1, -1), w1, bf, q_norm.reshape(1, -1), wuq, kv_norm.reshape(1, -1), wuk, wuv,
      ec, ra, rb, rc, cst)


def _alibi_slope(hd):
    return 2.0 ** (-8.0 * (hd + 1) / HEADS)


def _odd_proj_kernel(x_ref, g_ref, w_ref, q_ref, k_ref, v_ref, *, tiles_per_seq):
    tm = x_ref.shape[0]
    i = pl.program_id(0)
    h = _rms(x_ref[...], g_ref[...]).astype(BF16)
    scale = DIFF_HEAD_DIM ** -0.5 * LOG2E
    v = jnp.dot(h, w_ref[:, 2 * HEAD_COLS:], preferred_element_type=F32)
    k = jnp.dot(h, w_ref[:, HEAD_COLS:2 * HEAD_COLS], preferred_element_type=F32)
    _store_vt(v_ref, v, DIFF_V_DIM)
    q = jnp.dot(h, w_ref[:, :HEAD_COLS], preferred_element_type=F32)
    pos = ((i % tiles_per_seq) * tm + lax.broadcasted_iota(jnp.int32, (tm, 1), 0)).astype(F32)
    lane = lax.broadcasted_iota(jnp.int32, (tm, LANES), 1)
    d = DIFF_HEAD_DIM
    q_aug = jnp.where((lane >= d) & (lane < d + 3), 1.0, 0.0)
    for hd in range(HEADS):
        sl = slice(hd * LANES, (hd + 1) * LANES)
        b_hi, b_mid, b_lo = (p.astype(F32) for p in _split3(pos * (_alibi_slope(hd) * LOG2E)))
        k_aug = jnp.where(lane == d, b_hi,
                          jnp.where(lane == d + 1, b_mid, jnp.where(lane == d + 2, b_lo, 0.0)))
        qb, kb = q[:, sl] * scale, k[:, sl]
        for br, (qs, ks) in enumerate(((qb, kb), (pltpu.roll(qb, d, 1), pltpu.roll(kb, d, 1)))):
            so = slice((2 * hd + br) * LANES, (2 * hd + br + 1) * LANES)
            q_ref[:, so] = jnp.where(lane < d, qs, q_aug).astype(BF16)
            k_ref[:, so] = jnp.where(lane < d, ks, k_aug).astype(BF16)


def _odd_proj(x, g, w_in, bsz, seq):
    t = x.shape[0]
    tm, tk = PROJ_ROWS, ATTN_TILE
    w = w_in.astype(BF16)
    tiles_per_seq = seq // tm
    return pl.pallas_call(
        functools.partial(_odd_proj_kernel, tiles_per_seq=tiles_per_seq),
        grid=(t // tm,),
        in_specs=[pl.BlockSpec((tm, D_MODEL), lambda i: (i, 0)),
                  _const_spec((1, D_MODEL)), _const_spec(w.shape)],
        out_specs=[pl.BlockSpec((tm, 2 * HEAD_COLS), lambda i: (i, 0)),
                   pl.BlockSpec((tm, 2 * HEAD_COLS), lambda i: (i, 0)),
                   _vt_spec(DIFF_V_DIM, tm, tk, tiles_per_seq)],
        out_shape=[jax.ShapeDtypeStruct((t, 2 * HEAD_COLS), BF16),
                   jax.ShapeDtypeStruct((t, 2 * HEAD_COLS), BF16),
                   jax.ShapeDtypeStruct((bsz, HEADS, seq // tk, _vt_rows(DIFF_V_DIM), tk), BF16)],
        compiler_params=_params("arbitrary"),
        name="odd_proj",
    )(x, g.reshape(1, -1), w)


_NT = (((1,), (1,)), ((), ()))


def _attend(s, vt, m, acc, key0=None):
    if key0 is not None:
        key = lax.broadcasted_iota(jnp.int32, s.shape, 0) + key0
        qry = lax.broadcasted_iota(jnp.int32, s.shape, 1)
        s = jnp.where(key <= qry, s, NEG_INF)
    m_new = jnp.maximum(m, jnp.max(s, axis=0, keepdims=True))
    p = jnp.exp2(s - m_new).astype(BF16)
    acc = jnp.exp2(m - m_new) * acc + jnp.dot(vt, p, preferred_element_type=F32)
    return m_new, acc


def _stream_attention(qk, vts, s_ref, rows, n):
    lag, tk, t = s_ref.shape
    r = t // tk
    qi = pl.program_id(2)
    assert n >= lag
    for l in range(lag):
        s_ref[l] = qk(l, 0, 0)

    def sweep(j, u, carry, diagonal):
        carry = list(carry)
        nxt = (j, u + 1) if u + 1 < r else (j + 1, 0)
        last = diagonal and u + 1 == r
        pending = [s_ref[l] for l in range(lag)]
        for k in range(n):
            ahead = k + lag
            if ahead < n:
                pending.append(qk(ahead, j, u))
            elif not last:
                pending.append(qk(ahead - n, *nxt))
            carry[2 * k], carry[2 * k + 1] = _attend(pending.pop(0), vts(k, j, u), carry[2 * k],
                                                     carry[2 * k + 1], u * tk if diagonal else None)
        if not last:
            for l in range(lag):
                s_ref[l] = pending[l]
        return tuple(carry)

    def body(j, carry):
        for u in range(r):
            carry = sweep(j, u, carry, False)
        return carry

    init = (jnp.full((1, t), NEG_INF, F32), jnp.zeros((rows, t), F32)) * n
    carry = lax.fori_loop(0, qi, body, init)
    for u in range(r):
        carry = sweep(qi, u, carry, True)
    return carry[1::2]


def _flash_kernel(q_ref, k_ref, vt_ref, o_ref, s_ref):
    n, rows, t = vt_ref.shape[1], vt_ref.shape[3], vt_ref.shape[4]
    tk = s_ref.shape[1]
    dv = rows - BF16_ROWS
    qs = [q_ref[0, :, hd * LANES:(hd + 1) * LANES] for hd in range(n)]

    def qk(hd, j, u):
        start = pl.multiple_of(j * t, t) + u * tk
        kc = k_ref[0, pl.ds(start, tk), hd * LANES:(hd + 1) * LANES]
        return lax.dot_general(kc, qs[hd], _NT, preferred_element_type=F32)

    def vts(hd, j, u):
        return vt_ref[0, hd, j, :, u * tk:(u + 1) * tk]

    accs = _stream_attention(qk, vts, s_ref, rows, n)
    o_t = jnp.concatenate([a[:dv] / a[dv:dv + 1] for a in accs], axis=0)
    o_ref[0] = o_t.T.astype(BF16)


def _flash_attention(q, k, vt, tq):
    b, s, _ = q.shape
    _, _, chunks, rows, tk = vt.shape
    assert tq == tk
    n = FLASH_HEADS_PER_STEP
    dv = rows - BF16_ROWS
    return pl.pallas_call(
        _flash_kernel,
        grid=(b, HEADS // n, s // tq),
        in_specs=[pl.BlockSpec((1, tq, n * LANES), lambda bi, hp, i: (bi, i, hp)),
                  pl.BlockSpec((1, s, n * LANES), lambda bi, hp, i: (bi, 0, hp)),
                  pl.BlockSpec((1, n, chunks, rows, tk), lambda bi, hp, i: (bi, hp, 0, 0, 0))],
        out_specs=pl.BlockSpec((1, tq, n * dv), lambda bi, hp, i: (bi, i, hp)),
        out_shape=jax.ShapeDtypeStruct((b, s, HEADS * (rows - BF16_ROWS)), BF16),
        scratch_shapes=[pltpu.VMEM((FLASH_LAG, FLASH_KEYS, tq), F32)],
        compiler_params=_params("arbitrary", "arbitrary", "arbitrary"),
        name="flash_attention",
    )(q, k, vt)


def _diff_kernel(q_ref, k_ref, vt_ref, lq1_ref, lk1_ref, lq2_ref, lk2_ref, sub_ref, o_ref, s_ref,
                 *, lambda_init):
    n, rows, t = vt_ref.shape[1], vt_ref.shape[3], vt_ref.shape[4]
    tk = s_ref.shape[1]
    dv = rows - BF16_ROWS
    qs = [q_ref[0, :, st * LANES:(st + 1) * LANES] for st in range(2 * n)]

    def qk(st, j, u):
        start = pl.multiple_of(j * t, t) + u * tk
        kc = k_ref[0, pl.ds(start, tk), st * LANES:(st + 1) * LANES]
        return lax.dot_general(kc, qs[st], _NT, preferred_element_type=F32)

    def vts(st, j, u):
        return vt_ref[0, st // 2, j, :, u * tk:(u + 1) * tk]

    accs = _stream_attention(qk, vts, s_ref, rows, 2 * n)
    lam = (jnp.exp(jnp.sum(lq1_ref[...] * lk1_ref[...], axis=1, keepdims=True))
           - jnp.exp(jnp.sum(lq2_ref[...] * lk2_ref[...], axis=1, keepdims=True)) + lambda_init)
    for hd in range(n):
        a1, a2 = accs[2 * hd], accs[2 * hd + 1]
        o = (a1[:dv] / a1[dv:dv + 1] - lam * (a2[:dv] / a2[dv:dv + 1])).T
        o_ref[0, :, hd * dv:(hd + 1) * dv] = (
            _rms(o, sub_ref[...]) * (1.0 - lambda_init)).astype(BF16)


def _diff_attention(q, k, vt, lq1, lk1, lq2, lk2, subln, lambda_init, tq):
    b, s, _ = q.shape
    _, _, chunks, rows, tk = vt.shape
    assert tq == tk
    vec = lambda a: a.reshape(1, -1)
    n = DIFF_HEADS_PER_STEP
    return pl.pallas_call(
        functools.partial(_diff_kernel, lambda_init=lambda_init),
        grid=(b, HEADS // n, s // tq),
        in_specs=[pl.BlockSpec((1, tq, 2 * n * LANES), lambda bi, hd, i: (bi, i, hd)),
                  pl.BlockSpec((1, s, 2 * n * LANES), lambda bi, hd, i: (bi, 0, hd)),
                  pl.BlockSpec((1, n, chunks, rows, tk), lambda bi, hd, i: (bi, hd, 0, 0, 0)),
                  _const_spec((1, DIFF_HEAD_DIM)), _const_spec((1, DIFF_HEAD_DIM)),
                  _const_spec((1, DIFF_HEAD_DIM)), _const_spec((1, DIFF_HEAD_DIM)),
                  _const_spec((1, DIFF_V_DIM))],
        out_specs=pl.BlockSpec((1, tq, n * LANES), lambda bi, hd, i: (bi, i, hd)),
        out_shape=jax.ShapeDtypeStruct((b, s, HEADS * (rows - BF16_ROWS)), BF16),
        scratch_shapes=[pltpu.VMEM((DIFF_LAG, DIFF_KEYS, tq), F32)],
        compiler_params=_params("arbitrary", "arbitrary", "arbitrary"),
        name="diff_attention",
    )(q, k, vt, vec(lq1), vec(lk1), vec(lq2), vec(lk2), vec(subln))


_N_FFN_CHUNKS = D_FF // FFN_CHUNK
_GELU_C = math.sqrt(2.0 / math.pi)


def _mix_ffn_kernel(*refs, n_in, tiles_per_seq):
    a_refs, wo_refs = refs[:n_in], refs[n_in:2 * n_in]
    (gmix_ref, x_ref, gpre_ref, wup_ref, cw_ref, wdn_ref, gpost_ref, o_ref,
     x1_ref, h_ref, hist_ref, gva_ref, gvb_ref, mid_ref) = refs[2 * n_in:]
    tm = x_ref.shape[0]
    nc = _N_FFN_CHUNKS
    i = pl.program_id(0)

    @pl.when(i % tiles_per_seq == 0)
    def _():
        hist_ref[...] = jnp.zeros_like(hist_ref)

    mix = jnp.dot(a_refs[0][...], wo_refs[0][...], preferred_element_type=F32)
    for a_ref, wo_ref in zip(a_refs[1:], wo_refs[1:]):
        mix = mix + jnp.dot(a_ref[...], wo_ref[...], preferred_element_type=F32)
    x1_ref[...] = x_ref[...] + _rms(mix, gmix_ref[...])
    h_ref[...] = _rms(x1_ref[...], gpre_ref[...]).astype(BF16)

    def up(c):
        return jnp.dot(h_ref[...], wup_ref[c], preferred_element_type=F32)

    def gated(c, gv_ref):
        gv_ref[0:SUBLANES, :FFN_CHUNK] = hist_ref[c]
        gate = gv_ref[SUBLANES:, :FFN_CHUNK]
        val = gv_ref[SUBLANES:, FFN_CHUNK:]
        hist_ref[c] = gv_ref[tm:, :FFN_CHUNK]
        cw = cw_ref[c]
        pre = (gv_ref[SUBLANES - 2:SUBLANES - 2 + tm, :FFN_CHUNK] * cw[0:1, :]
               + gv_ref[SUBLANES - 1:SUBLANES - 1 + tm, :FFN_CHUNK] * cw[1:2, :]
               + gate * cw[2:3, :] + cw[3:4, :])
        pre = pre.astype(BF16)
        act = 0.5 * pre * (1.0 + jnp.tanh(_GELU_C * (pre + 0.044715 * (pre * pre * pre))))
        mid_ref[c] = act * val.astype(BF16)

    assert nc % 2 == 1
    gva_ref[SUBLANES:, :] = up(0)

    def body(k, _):
        gated(2 * k, gva_ref)
        gvb_ref[SUBLANES:, :] = up(2 * k + 1)
        gated(2 * k + 1, gvb_ref)
        gva_ref[SUBLANES:, :] = up(2 * k + 2)
        return 0

    lax.fori_loop(0, nc // 2, body, 0)
    gated(nc - 1, gva_ref)
    f = jnp.dot(mid_ref[0], wdn_ref[0], preferred_element_type=F32)
    for c in range(1, nc):
        f = f + jnp.dot(mid_ref[c], wdn_ref[c], preferred_element_type=F32)
    o_ref[...] = x1_ref[...] + _rms(f, gpost_ref[...])


def _mix_ffn(acts, w_outs, g_mix, x, g_pre, w_up, conv_w, conv_b, w_down, g_post, seq):
    t = x.shape[0]
    tm = FFN_ROWS
    nc = _N_FFN_CHUNKS
    n_in = len(acts)
    wg = w_up[:, :D_FF].reshape(D_MODEL, nc, FFN_CHUNK)
    wv = w_up[:, D_FF:].reshape(D_MODEL, nc, FFN_CHUNK)
    wup = jnp.concatenate([wg, wv], axis=2).transpose(1, 0, 2).astype(BF16)
    wdn = w_down.reshape(nc, FFN_CHUNK, D_MODEL).astype(BF16)
    cw = jnp.concatenate([conv_w, conv_b[None, :],
                          jnp.zeros((SUBLANES - CONV_WIDTH - 1, D_FF), F32)], axis=0)
    cw = cw.reshape(SUBLANES, nc, FFN_CHUNK).transpose(1, 0, 2)
    x_spec = pl.BlockSpec((tm, D_MODEL), lambda i: (i, 0))
    vec_spec = _const_spec((1, D_MODEL))
    return pl.pallas_call(
        functools.partial(_mix_ffn_kernel, n_in=n_in, tiles_per_seq=seq // tm),
        grid=(t // tm,),
        in_specs=([pl.BlockSpec((tm, a.shape[1]), lambda i: (i, 0)) for a in acts]
                  + [_const_spec(w.shape) for w in w_outs]
                  + [vec_spec, x_spec, vec_spec, _const_spec(wup.shape), _const_spec(cw.shape),
                     _const_spec(wdn.shape), vec_spec]),
        out_specs=x_spec,
        out_shape=jax.ShapeDtypeStruct(x.shape, F32),
        scratch_shapes=[pltpu.VMEM((tm, D_MODEL), F32),
                        pltpu.VMEM((tm, D_MODEL), BF16),
                        pltpu.VMEM((nc, SUBLANES, FFN_CHUNK), F32),
                        pltpu.VMEM((tm + SUBLANES, 2 * FFN_CHUNK), F32),
                        pltpu.VMEM((tm + SUBLANES, 2 * FFN_CHUNK), F32),
                        pltpu.VMEM((nc, tm, FFN_CHUNK), BF16)],
        compiler_params=_params("arbitrary"),
        name="mix_ffn",
    )(*acts, *w_outs, g_mix.reshape(1, -1), x, g_pre.reshape(1, -1), wup, cw, wdn,
      g_post.reshape(1, -1))


def kernel(x, norm_mix_pre, norm_mix_post, norm_ffn_pre, norm_ffn_post, even_w_in, even_b_forget, even_q_norm, even_w_uq, even_kv_norm, even_w_ukv, even_w_out, odd_w_in, odd_lambda_q1, odd_lambda_k1, odd_lambda_q2, odd_lambda_k2, odd_subln, odd_w_out, ffn_w_up, ffn_conv_w, ffn_conv_b, ffn_w_down):
    bsz, seq, d = x.shape
    assert d == D_MODEL and seq % ATTN_TILE == 0 and seq % FFN_ROWS == 0
    t = bsz * seq
    tq = ATTN_TILE
    rope = _rope_tables(seq)
    xf = x.reshape(t, d)
    for layer in range(DEPTH):
        i = layer // 2
        if layer % 2 == 0:
            fq, fk, fvt, mq, mk, mvt = _even_proj(
                xf, norm_mix_pre[layer], even_w_in[i], even_b_forget[i], even_q_norm[i],
                even_w_uq[i], even_kv_norm[i], even_w_ukv[i], rope, bsz, seq)
            shp = (bsz, seq, HEAD_COLS)
            fo = _flash_attention(fq.reshape(shp), fk.reshape(shp), fvt, tq)
            mo = _flash_attention(mq.reshape(shp), mk.reshape(shp), mvt, tq)
            w_out = even_w_out[i].astype(BF16)
            acts = [fo.reshape(t, -1), mo.reshape(t, -1)]
            w_outs = [w_out[:FOX_WIDTH], w_out[FOX_WIDTH:]]
        else:
            lambda_init = 0.8 - 0.6 * math.exp(-0.3 * layer)
            q, k, vt = _odd_proj(xf, norm_mix_pre[layer], odd_w_in[i], bsz, seq)
            o = _diff_attention(q.reshape(bsz, seq, -1), k.reshape(bsz, seq, -1), vt,
                                odd_lambda_q1[i], odd_lambda_k1[i],
                                odd_lambda_q2[i], odd_lambda_k2[i], odd_subln[i], lambda_init, tq)
            acts, w_outs = [o.reshape(t, -1)], [odd_w_out[i].astype(BF16)]
        xf = _mix_ffn(acts, w_outs, norm_mix_post[layer], xf, norm_ffn_pre[layer], ffn_w_up[layer],
                      ffn_conv_w[layer], ffn_conv_b[layer], ffn_w_down[layer],
                      norm_ffn_post[layer], seq)
    return xf.reshape(bsz, seq, d)
```

```python
import functools
import math

import numpy as np
import jax
import jax.numpy as jnp
from jax import lax
from jax.experimental import pallas as pl
from jax.experimental.pallas import tpu as pltpu

D_MODEL = 1024
DEPTH = 4
RMS_EPS = 1e-6
NEG_INF = -1e30
HEADS = 8
FOX_HEAD_DIM = 64
FOX_WIDTH = HEADS * FOX_HEAD_DIM
MLA_NOPE_DIM = 64
MLA_ROPE_DIM = 32
MLA_V_DIM = 64
MLA_Q_RANK = 384
MLA_KV_RANK = 256
ROPE_THETA = 10000.0
DIFF_HEAD_DIM = 64
DIFF_V_DIM = 2 * DIFF_HEAD_DIM
D_FF = 2816
CONV_WIDTH = 3

LANES = 128
SUBLANES = 8
BF16_ROWS = 16
LOG2E = 1.4426950408889634
HEAD_COLS = HEADS * LANES
VMEM_LIMIT_BYTES = 56 * 1024 * 1024

PROJ_ROWS = 512
FFN_ROWS = 512
FFN_TILES_PER_STEP = 2
FFN_CHUNK = 256
ATTN_TILE = 512
FLASH_HEADS_PER_STEP, FLASH_KEYS, FLASH_LAG = 8, 256, 3
DIFF_HEADS_PER_STEP, DIFF_KEYS, DIFF_LAG = 4, 512, 2
F32 = jnp.float32
BF16 = jnp.bfloat16


def _params(*sem, flags=None):
    return pltpu.CompilerParams(dimension_semantics=sem, vmem_limit_bytes=VMEM_LIMIT_BYTES,
                                flags=flags)


def _const_spec(shape):
    nd = len(shape)
    return pl.BlockSpec(shape, lambda *_: (0,) * nd, pipeline_mode=pl.Buffered(1))


def _rms(x, g):
    return x * lax.rsqrt(jnp.mean(x * x, axis=-1, keepdims=True) + RMS_EPS) * g


def _split3(x):
    hi = x.astype(BF16)
    r1 = x - hi.astype(F32)
    mid = r1.astype(BF16)
    lo = (r1 - mid.astype(F32)).astype(BF16)
    return hi, mid, lo


def _rope(blk, ra, rb, rc):
    return blk * ra + pltpu.roll(blk, 16, 1) * rb + pltpu.roll(blk, LANES - 16, 1) * rc


def _vt_rows(dv):
    return dv + BF16_ROWS


def _store_vt(vt_ref, v, dv):
    tm = v.shape[0]
    per = LANES // dv
    for blk in range(v.shape[1] // LANES):
        vt = v[:, blk * LANES:(blk + 1) * LANES].T.astype(BF16)
        for s in range(per):
            vt_ref[0, blk * per + s, 0, 0:dv, :] = vt[s * dv:(s + 1) * dv, :]
    row = lax.broadcasted_iota(jnp.int32, (BF16_ROWS, tm), 0)
    ones_row = jnp.where(row == 0, 1.0, 0.0).astype(BF16)
    for hd in range(HEADS):
        vt_ref[0, hd, 0, dv:dv + BF16_ROWS, :] = ones_row


def _vt_spec(dv, tm, tk, tiles_per_seq):
    per_chunk = tk // tm
    return pl.BlockSpec(
        (1, HEADS, 1, _vt_rows(dv), tm),
        lambda i: (i // tiles_per_seq, 0, (i % tiles_per_seq) // per_chunk, 0,
                   (i % tiles_per_seq) % per_chunk))


_EV_Q, _EV_K, _EV_V = 0, FOX_WIDTH, 2 * FOX_WIDTH
_EV_G = _EV_V + FOX_WIDTH
_EV_CQ = _EV_G + LANES
_EV_CKV = _EV_CQ + MLA_Q_RANK
_EV_KR = _EV_CKV + MLA_KV_RANK
_EV_WIDTH = _EV_KR + LANES


def _even_proj_kernel(x_ref, g_ref, w_ref, bf_ref, qn_ref, wuq_ref, kvn_ref, wuk_ref, wuv_ref,
                      ec_ref, ra_ref, rb_ref, rc_ref, cst_ref,
                      fq_ref, fk_ref, fv_ref, mq_ref, mk_ref, mv_ref, carry_ref,
                      *, tiles_per_seq):
    tm = x_ref.shape[0]
    i = pl.program_id(0)

    @pl.when(i % tiles_per_seq == 0)
    def _():
        carry_ref[...] = jnp.zeros_like(carry_ref)

    h = _rms(x_ref[...], g_ref[...]).astype(BF16)
    q_aug = cst_ref[0:1, :]

    def proj(lo, hi):
        return jnp.dot(h, w_ref[:, lo:hi], preferred_element_type=F32)

    lane = lax.broadcasted_iota(jnp.int32, (tm, LANES), 1)

    def spread_heads(o_ref, packed, aug):
        for pair in range(HEADS // 2):
            blk = packed[:, pair * LANES:(pair + 1) * LANES]
            for odd, src in enumerate((blk, pltpu.roll(blk, FOX_HEAD_DIM, 1))):
                sl = slice((2 * pair + odd) * LANES, (2 * pair + odd + 1) * LANES)
                o_ref[:, sl] = jnp.where(lane < FOX_HEAD_DIM, src, aug[:, sl]).astype(BF16)

    ra, rb, rc = ra_ref[...], rb_ref[...], rc_ref[...]
    fox_scale = FOX_HEAD_DIM ** -0.5 * LOG2E
    mla_scale = (MLA_NOPE_DIM + MLA_ROPE_DIM) ** -0.5 * LOG2E

    z = proj(_EV_G, _EV_CQ) + bf_ref[...]
    cq_raw = proj(_EV_CQ, _EV_CKV)
    ckv_raw = proj(_EV_CKV, _EV_KR)
    kr_raw = proj(_EV_KR, _EV_WIDTH)

    logf = (jnp.minimum(z, 0.0) - jnp.log1p(jnp.exp(-jnp.abs(z)))) * LOG2E
    spread_heads(fq_ref, proj(_EV_Q, _EV_K) * fox_scale, q_aug)

    row = lax.broadcasted_iota(jnp.int32, (tm, tm), 0)
    col = lax.broadcasted_iota(jnp.int32, (tm, tm), 1)
    tril = jnp.where(row >= col, 1.0, 0.0).astype(BF16)
    parts = jnp.dot(tril, jnp.concatenate(_split3(logf), axis=1), preferred_element_type=F32)
    _store_vt(fv_ref, proj(_EV_V, _EV_G), FOX_HEAD_DIM)
    c = (parts[:, :LANES] + parts[:, LANES:2 * LANES] + parts[:, 2 * LANES:]
         + carry_ref[SUBLANES - 1:SUBLANES, :])
    carry_ref[...] = c[tm - SUBLANES:, :]

    cq = _rms(cq_raw, qn_ref[...]).astype(BF16)
    q = jnp.dot(cq, wuq_ref[...], preferred_element_type=F32) * mla_scale
    for hd in range(HEADS):
        sl = slice(hd * LANES, (hd + 1) * LANES)
        mq_ref[:, sl] = _rope(q[:, sl], ra, rb, rc).astype(BF16)

    ckv = _rms(ckv_raw, kvn_ref[...]).astype(BF16)
    kr = _rope(kr_raw, ra, rb, rc)
    kn = jnp.dot(ckv, wuk_ref[...], preferred_element_type=F32)
    for hd in range(HEADS):
        sl = slice(hd * LANES, (hd + 1) * LANES)
        mk_ref[:, sl] = (kn[:, sl] + kr).astype(BF16)
    _store_vt(mv_ref, jnp.dot(ckv, wuv_ref[...], preferred_element_type=F32), MLA_V_DIM)

    c_aug = jnp.dot(jnp.concatenate(_split3(c), axis=1), ec_ref[...], preferred_element_type=F32)
    spread_heads(fk_ref, proj(_EV_K, _EV_V), c_aug)


def _pad_heads(w, dh):
    k = w.shape[0]
    w = w.reshape(k, HEADS, dh)
    return jnp.pad(w, ((0, 0), (0, 0), (0, LANES - dh))).reshape(k, HEAD_COLS)


def _even_consts():
    cst = np.zeros((SUBLANES, HEAD_COLS), np.float32)
    ec = np.zeros((3 * LANES, HEAD_COLS), np.float32)
    for hd in range(HEADS):
        cst[0, hd * LANES + 64: hd * LANES + 67] = 1.0
        for p in range(3):
            ec[p * LANES + hd, hd * LANES + 64 + p] = -1.0
    return jnp.asarray(cst), jnp.asarray(ec, dtype=BF16)


def _rope_tables(seq):
    half = MLA_ROPE_DIM // 2
    inv = ROPE_THETA ** (-jnp.arange(0, MLA_ROPE_DIM, 2, dtype=F32) / MLA_ROPE_DIM)
    ang = jnp.arange(seq, dtype=F32)[:, None] * inv[None, :]
    cos, sin = jnp.cos(ang), jnp.sin(ang)
    zeros = jnp.zeros((seq, half), F32)
    ra = jnp.concatenate([jnp.ones((seq, 64), F32), cos, cos, jnp.zeros((seq, 32), F32)], axis=1)
    rb = jnp.concatenate([jnp.zeros((seq, 64), F32), zeros, sin, jnp.zeros((seq, 32), F32)], axis=1)
    rc = jnp.concatenate([jnp.zeros((seq, 64), F32), -sin, zeros, jnp.zeros((seq, 32), F32)], axis=1)
    return ra, rb, rc


def _even_proj(x, g, w_in, b_forget, q_norm, w_uq, kv_norm, w_ukv, rope, bsz, seq):
    t = x.shape[0]
    tm, tk = PROJ_ROWS, ATTN_TILE
    cuts = np.cumsum([FOX_WIDTH, FOX_WIDTH, FOX_WIDTH, HEADS, MLA_Q_RANK, MLA_KV_RANK]).tolist()
    wfq, wfk, wfv, wg, wcq, wckv, wkr = jnp.split(w_in, cuts, axis=1)
    w1 = jnp.concatenate([
        wfq, wfk, wfv,
        jnp.pad(wg, ((0, 0), (0, LANES - HEADS))), wcq, wckv,
        jnp.pad(wkr, ((0, 0), (64, LANES - 64 - MLA_ROPE_DIM)))], axis=1).astype(BF16)
    assert w1.shape[1] == _EV_WIDTH
    bf = jnp.pad(b_forget, (0, LANES - HEADS)).reshape(1, LANES)
    wuq = _pad_heads(w_uq, MLA_NOPE_DIM + MLA_ROPE_DIM).astype(BF16)
    wukv = w_ukv.reshape(MLA_KV_RANK, HEADS, MLA_NOPE_DIM + MLA_V_DIM)
    wuk = _pad_heads(wukv[:, :, :MLA_NOPE_DIM].reshape(MLA_KV_RANK, -1), MLA_NOPE_DIM).astype(BF16)
    wuv = wukv[:, :, MLA_NOPE_DIM:].reshape(MLA_KV_RANK, -1).astype(BF16)
    cst, ec = _even_consts()
    ra, rb, rc = rope
    tiles_per_seq = seq // tm
    rope_spec = pl.BlockSpec((tm, LANES), lambda i: (i % tiles_per_seq, 0))
    row_spec = pl.BlockSpec((tm, HEAD_COLS), lambda i: (i, 0))
    out_sds = jax.ShapeDtypeStruct((t, HEAD_COLS), BF16)
    vt_spec = _vt_spec(FOX_HEAD_DIM, tm, tk, tiles_per_seq)
    vt_sds = jax.ShapeDtypeStruct((bsz, HEADS, seq // tk, _vt_rows(FOX_HEAD_DIM), tk), BF16)
    return pl.pallas_call(
        functools.partial(_even_proj_kernel, tiles_per_seq=tiles_per_seq),
        grid=(t // tm,),
        in_specs=[pl.BlockSpec((tm, D_MODEL), lambda i: (i, 0)),
                  _const_spec((1, D_MODEL)), _const_spec(w1.shape), _const_spec((1, LANES)),
                  _const_spec((1, MLA_Q_RANK)), _const_spec(wuq.shape),
                  _const_spec((1, MLA_KV_RANK)), _const_spec(wuk.shape), _const_spec(wuv.shape),
                  _const_spec(ec.shape), rope_spec, rope_spec, rope_spec, _const_spec(cst.shape)],
        out_specs=[row_spec, row_spec, vt_spec, row_spec, row_spec, vt_spec],
        out_shape=[out_sds, out_sds, vt_sds, out_sds, out_sds, vt_sds],
        scratch_shapes=[pltpu.VMEM((SUBLANES, LANES), F32)],
        compiler_params=_params("arbitrary"),
        name="even_proj",
    )(x, g.reshape(1, -1), w1, bf, q_norm.reshape(1, -1), wuq, kv_norm.reshape(1, -1), wuk, wuv,
      ec, ra, rb, rc, cst)


def _alibi_slope(hd):
    return 2.0 ** (-8.0 * (hd + 1) / HEADS)


def _odd_proj_kernel(x_ref, g_ref, w_ref, q_ref, k_ref, v_ref, *, tiles_per_seq):
    tm = x_ref.shape[0]
    i = pl.program_id(0)
    h = _rms(x_ref[...], g_ref[...]).astype(BF16)
    scale = DIFF_HEAD_DIM ** -0.5 * LOG2E
    v = jnp.dot(h, w_ref[:, 2 * HEAD_COLS:], preferred_element_type=F32)
    k = jnp.dot(h, w_ref[:, HEAD_COLS:2 * HEAD_COLS], preferred_element_type=F32)
    _store_vt(v_ref, v, DIFF_V_DIM)
    q = jnp.dot(h, w_ref[:, :HEAD_COLS], preferred_element_type=F32)
    pos = ((i % tiles_per_seq) * tm + lax.broadcasted_iota(jnp.int32, (tm, 1), 0)).astype(F32)
    lane = lax.broadcasted_iota(jnp.int32, (tm, LANES), 1)
    d = DIFF_HEAD_DIM
    q_aug = jnp.where((lane >= d) & (lane < d + 3), 1.0, 0.0)
    for hd in range(HEADS):
        sl = slice(hd * LANES, (hd + 1) * LANES)
        b_hi, b_mid, b_lo = (p.astype(F32) for p in _split3(pos * (_alibi_slope(hd) * LOG2E)))
        k_aug = jnp.where(lane == d, b_hi,
                          jnp.where(lane == d + 1, b_mid, jnp.where(lane == d + 2, b_lo, 0.0)))
        qb, kb = q[:, sl] * scale, k[:, sl]
        for br, (qs, ks) in enumerate(((qb, kb), (pltpu.roll(qb, d, 1), pltpu.roll(kb, d, 1)))):
            so = slice((2 * hd + br) * LANES, (2 * hd + br + 1) * LANES)
            q_ref[:, so] = jnp.where(lane < d, qs, q_aug).astype(BF16)
            k_ref[:, so] = jnp.where(lane < d, ks, k_aug).astype(BF16)


def _odd_proj(x, g, w_in, bsz, seq):
    t = x.shape[0]
    tm, tk = PROJ_ROWS, ATTN_TILE
    w = w_in.astype(BF16)
    tiles_per_seq = seq // tm
    return pl.pallas_call(
        functools.partial(_odd_proj_kernel, tiles_per_seq=tiles_per_seq),
        grid=(t // tm,),
        in_specs=[pl.BlockSpec((tm, D_MODEL), lambda i: (i, 0)),
                  _const_spec((1, D_MODEL)), _const_spec(w.shape)],
        out_specs=[pl.BlockSpec((tm, 2 * HEAD_COLS), lambda i: (i, 0)),
                   pl.BlockSpec((tm, 2 * HEAD_COLS), lambda i: (i, 0)),
                   _vt_spec(DIFF_V_DIM, tm, tk, tiles_per_seq)],
        out_shape=[jax.ShapeDtypeStruct((t, 2 * HEAD_COLS), BF16),
                   jax.ShapeDtypeStruct((t, 2 * HEAD_COLS), BF16),
                   jax.ShapeDtypeStruct((bsz, HEADS, seq // tk, _vt_rows(DIFF_V_DIM), tk), BF16)],
        compiler_params=_params("arbitrary"),
        name="odd_proj",
    )(x, g.reshape(1, -1), w)


_NT = (((1,), (1,)), ((), ()))


def _attend(s, vt, m, acc, key0=None):
    if key0 is None:
        m_new = jnp.maximum(m, jnp.max(s, axis=0, keepdims=True))
        p = jnp.exp2(s - m_new).astype(BF16)
    else:
        nr, nc = s.shape[0] // LANES, s.shape[1] // LANES
        key = lax.broadcasted_iota(jnp.int32, (LANES, LANES), 0)
        qry = lax.broadcasted_iota(jnp.int32, (LANES, LANES), 1)
        sq = [[s[a * LANES:(a + 1) * LANES, b * LANES:(b + 1) * LANES] for b in range(nc)]
              for a in range(nr)]
        k0 = key0 // LANES
        for a in range(nr):
            if k0 + a < nc:
                sq[a][k0 + a] = jnp.where(key <= qry, sq[a][k0 + a], NEG_INF)
        col_max = []
        for b in range(nc):
            vis = [sq[a][b] for a in range(nr) if k0 + a <= b]
            col_max.append(functools.reduce(jnp.maximum, [jnp.max(x, axis=0, keepdims=True)
                                                          for x in vis])
                           if vis else jnp.full((1, LANES), NEG_INF, F32))
        m_new = jnp.maximum(m, jnp.concatenate(col_max, axis=1))
        p = jnp.concatenate(
            [jnp.concatenate(
                [jnp.exp2(sq[a][b] - m_new[:, b * LANES:(b + 1) * LANES]).astype(BF16)
                 if k0 + a <= b else jnp.zeros((LANES, LANES), BF16) for b in range(nc)], axis=1)
             for a in range(nr)], axis=0)
    acc = jnp.exp2(m - m_new) * acc + jnp.dot(vt, p, preferred_element_type=F32)
    return m_new, acc


def _stream_attention(qk, vts, s_ref, rows, n):
    lag, tk, t = s_ref.shape
    r = t // tk
    qi = pl.program_id(2)
    assert n >= lag
    for l in range(lag):
        s_ref[l] = qk(l, 0, 0)

    def sweep(j, u, carry, diagonal):
        carry = list(carry)
        nxt = (j, u + 1) if u + 1 < r else (j + 1, 0)
        last = diagonal and u + 1 == r
        pending = [s_ref[l] for l in range(lag)]
        for k in range(n):
            ahead = k + lag
            if ahead < n:
                pending.append(qk(ahead, j, u))
            elif not last:
                pending.append(qk(ahead - n, *nxt))
            carry[2 * k], carry[2 * k + 1] = _attend(pending.pop(0), vts(k, j, u), carry[2 * k],
                                                     carry[2 * k + 1], u * tk if diagonal else None)
        if not last:
            for l in range(lag):
                s_ref[l] = pending[l]
        return tuple(carry)

    def body(j, carry):
        for u in range(r):
            carry = sweep(j, u, carry, False)
        return carry

    init = (jnp.full((1, t), NEG_INF, F32), jnp.zeros((rows, t), F32)) * n
    carry = lax.fori_loop(0, qi, body, init)
    for u in range(r):
        carry = sweep(qi, u, carry, True)
    return carry[1::2]


def _flash_kernel(q_ref, k_ref, vt_ref, o_ref, s_ref):
    n, rows, t = vt_ref.shape[1], vt_ref.shape[3], vt_ref.shape[4]
    tk = s_ref.shape[1]
    dv = rows - BF16_ROWS
    qs = [q_ref[0, :, hd * LANES:(hd + 1) * LANES] for hd in range(n)]

    def qk(hd, j, u):
        start = pl.multiple_of(j * t, t) + u * tk
        kc = k_ref[0, pl.ds(start, tk), hd * LANES:(hd + 1) * LANES]
        return lax.dot_general(kc, qs[hd], _NT, preferred_element_type=F32)

    def vts(hd, j, u):
        return vt_ref[0, hd, j, :, u * tk:(u + 1) * tk]

    accs = _stream_attention(qk, vts, s_ref, rows, n)
    o_t = jnp.concatenate([a[:dv] / a[dv:dv + 1] for a in accs], axis=0)
    o_ref[0] = o_t.T.astype(BF16)


def _flash_attention(q, k, vt, tq):
    b, s, _ = q.shape
    _, _, chunks, rows, tk = vt.shape
    assert tq == tk
    n = FLASH_HEADS_PER_STEP
    dv = rows - BF16_ROWS
    return pl.pallas_call(
        _flash_kernel,
        grid=(b, HEADS // n, s // tq),
        in_specs=[pl.BlockSpec((1, tq, n * LANES), lambda bi, hp, i: (bi, i, hp)),
                  pl.BlockSpec((1, s, n * LANES), lambda bi, hp, i: (bi, 0, hp)),
                  pl.BlockSpec((1, n, chunks, rows, tk), lambda bi, hp, i: (bi, hp, 0, 0, 0))],
        out_specs=pl.BlockSpec((1, tq, n * dv), lambda bi, hp, i: (bi, i, hp)),
        out_shape=jax.ShapeDtypeStruct((b, s, HEADS * (rows - BF16_ROWS)), BF16),
        scratch_shapes=[pltpu.VMEM((FLASH_LAG, FLASH_KEYS, tq), F32)],
        compiler_params=_params("arbitrary", "arbitrary", "arbitrary"),
        name="flash_attention",
    )(q, k, vt)


def _diff_kernel(q_ref, k_ref, vt_ref, lq1_ref, lk1_ref, lq2_ref, lk2_ref, sub_ref, o_ref, s_ref,
                 *, lambda_init):
    n, rows, t = vt_ref.shape[1], vt_ref.shape[3], vt_ref.shape[4]
    tk = s_ref.shape[1]
    dv = rows - BF16_ROWS
    qs = [q_ref[0, :, st * LANES:(st + 1) * LANES] for st in range(2 * n)]

    def qk(st, j, u):
        start = pl.multiple_of(j * t, t) + u * tk
        kc = k_ref[0, pl.ds(start, tk), st * LANES:(st + 1) * LANES]
        return lax.dot_general(kc, qs[st], _NT, preferred_element_type=F32)

    def vts(st, j, u):
        return vt_ref[0, st // 2, j, :, u * tk:(u + 1) * tk]

    accs = _stream_attention(qk, vts, s_ref, rows, 2 * n)
    lam = (jnp.exp(jnp.sum(lq1_ref[...] * lk1_ref[...], axis=1, keepdims=True))
           - jnp.exp(jnp.sum(lq2_ref[...] * lk2_ref[...], axis=1, keepdims=True)) + lambda_init)
    for hd in range(n):
        a1, a2 = accs[2 * hd], accs[2 * hd + 1]
        o = (a1[:dv] / a1[dv:dv + 1] - lam * (a2[:dv] / a2[dv:dv + 1])).T
        o_ref[0, :, hd * dv:(hd + 1) * dv] = (
            _rms(o, sub_ref[...]) * (1.0 - lambda_init)).astype(BF16)


def _diff_attention(q, k, vt, lq1, lk1, lq2, lk2, subln, lambda_init, tq):
    b, s, _ = q.shape
    _, _, chunks, rows, tk = vt.shape
    assert tq == tk
    vec = lambda a: a.reshape(1, -1)
    n = DIFF_HEADS_PER_STEP
    return pl.pallas_call(
        functools.partial(_diff_kernel, lambda_init=lambda_init),
        grid=(b, HEADS // n, s // tq),
        in_specs=[pl.BlockSpec((1, tq, 2 * n * LANES), lambda bi, hd, i: (bi, i, hd)),
                  pl.BlockSpec((1, s, 2 * n * LANES), lambda bi, hd, i: (bi, 0, hd)),
                  pl.BlockSpec((1, n, chunks, rows, tk), lambda bi, hd, i: (bi, hd, 0, 0, 0)),
                  _const_spec((1, DIFF_HEAD_DIM)), _const_spec((1, DIFF_HEAD_DIM)),
                  _const_spec((1, DIFF_HEAD_DIM)), _const_spec((1, DIFF_HEAD_DIM)),
                  _const_spec((1, DIFF_V_DIM))],
        out_specs=pl.BlockSpec((1, tq, n * LANES), lambda bi, hd, i: (bi, i, hd)),
        out_shape=jax.ShapeDtypeStruct((b, s, HEADS * (rows - BF16_ROWS)), BF16),
        scratch_shapes=[pltpu.VMEM((DIFF_LAG, DIFF_KEYS, tq), F32)],
        compiler_params=_params("arbitrary", "arbitrary", "arbitrary"),
        name="diff_attention",
    )(q, k, vt, vec(lq1), vec(lk1), vec(lq2), vec(lk2), vec(subln))


_N_FFN_CHUNKS = D_FF // FFN_CHUNK
_GELU_C = math.sqrt(2.0 / math.pi)


def _mix_ffn_kernel(*refs, n_in, steps_per_seq):
    a_refs, wo_refs = refs[:n_in], refs[n_in:2 * n_in]
    (gmix_ref, x_ref, gpre_ref, wup_ref, cw_ref, wdn_ref, gpost_ref, o_ref,
     h_ref, hist_ref, gva_ref, gvb_ref, *mid_refs) = refs[2 * n_in:]
    tm = h_ref.shape[0]
    nc = _N_FFN_CHUNKS
    i = pl.program_id(0)

    @pl.when(i % steps_per_seq == 0)
    def _():
        hist_ref[...] = jnp.zeros_like(hist_ref)

    def out_proj(rows):
        mix = jnp.dot(a_refs[0][rows, :], wo_refs[0][...], preferred_element_type=F32)
        for a_ref, wo_ref in zip(a_refs[1:], wo_refs[1:]):
            mix = mix + jnp.dot(a_ref[rows, :], wo_ref[...], preferred_element_type=F32)
        return mix

    def pre_norms(rows, mix):
        x1 = x_ref[rows, :] + _rms(mix, gmix_ref[...])
        o_ref[rows, :] = x1
        h_ref[...] = _rms(x1, gpre_ref[...]).astype(BF16)

    def up(c):
        return jnp.dot(h_ref[...], wup_ref[c], preferred_element_type=F32)

    def down(mid_ref):
        f = jnp.dot(mid_ref[0], wdn_ref[0], preferred_element_type=F32)
        for c in range(1, nc):
            f = f + jnp.dot(mid_ref[c], wdn_ref[c], preferred_element_type=F32)
        return f

    def gated(c, gv_ref, mid_ref):
        gv_ref[0:SUBLANES, :FFN_CHUNK] = hist_ref[c]
        gate = gv_ref[SUBLANES:, :FFN_CHUNK]
        val = gv_ref[SUBLANES:, FFN_CHUNK:]
        hist_ref[c] = gv_ref[tm:, :FFN_CHUNK]
        cw = cw_ref[c]
        pre = (gv_ref[SUBLANES - 2:SUBLANES - 2 + tm, :FFN_CHUNK] * cw[0:1, :]
               + gv_ref[SUBLANES - 1:SUBLANES - 1 + tm, :FFN_CHUNK] * cw[1:2, :]
               + gate * cw[2:3, :] + cw[3:4, :])
        pre = pre.astype(BF16)
        act = 0.5 * pre * (1.0 + jnp.tanh(_GELU_C * (pre + 0.044715 * (pre * pre * pre))))
        mid_ref[c] = act * val.astype(BF16)

    def chunks_after_first(mid_ref):
        def body(k, _):
            gated(2 * k, gva_ref, mid_ref)
            gvb_ref[SUBLANES:, :] = up(2 * k + 1)
            gated(2 * k + 1, gvb_ref, mid_ref)
            gva_ref[SUBLANES:, :] = up(2 * k + 2)
            return 0

        lax.fori_loop(0, nc // 2, body, 0)
        gated(nc - 1, gva_ref, mid_ref)

    assert nc % 2 == 1
    n_tiles = len(mid_refs)
    rows = [slice(s * tm, (s + 1) * tm) for s in range(n_tiles)]
    pre_norms(rows[0], out_proj(rows[0]))
    gva_ref[SUBLANES:, :] = up(0)
    for s in range(n_tiles):
        chunks_after_first(mid_refs[s])
        if s + 1 < n_tiles:
            mix_next = out_proj(rows[s + 1])
        f = down(mid_refs[s])
        if s + 1 < n_tiles:
            pre_norms(rows[s + 1], mix_next)
            gva_ref[SUBLANES:, :] = up(0)
        o_ref[rows[s], :] = o_ref[rows[s], :] + _rms(f, gpost_ref[...])


def _mix_ffn(acts, w_outs, g_mix, x, g_pre, w_up, conv_w, conv_b, w_down, g_post, seq):
    t = x.shape[0]
    tm = FFN_ROWS
    nc = _N_FFN_CHUNKS
    n_in = len(acts)
    wg = w_up[:, :D_FF].reshape(D_MODEL, nc, FFN_CHUNK)
    wv = w_up[:, D_FF:].reshape(D_MODEL, nc, FFN_CHUNK)
    wup = jnp.concatenate([wg, wv], axis=2).transpose(1, 0, 2).astype(BF16)
    wdn = w_down.reshape(nc, FFN_CHUNK, D_MODEL).astype(BF16)
    cw = jnp.concatenate([conv_w, conv_b[None, :],
                          jnp.zeros((SUBLANES - CONV_WIDTH - 1, D_FF), F32)], axis=0)
    cw = cw.reshape(SUBLANES, nc, FFN_CHUNK).transpose(1, 0, 2)
    step_rows = tm * FFN_TILES_PER_STEP
    assert seq % step_rows == 0
    x_spec = pl.BlockSpec((step_rows, D_MODEL), lambda i: (i, 0))
    vec_spec = _const_spec((1, D_MODEL))
    return pl.pallas_call(
        functools.partial(_mix_ffn_kernel, n_in=n_in, steps_per_seq=seq // step_rows),
        grid=(t // step_rows,),
        in_specs=([pl.BlockSpec((step_rows, a.shape[1]), lambda i: (i, 0)) for a in acts]
                  + [_const_spec(w.shape) for w in w_outs]
                  + [vec_spec, x_spec, vec_spec, _const_spec(wup.shape), _const_spec(cw.shape),
                     _const_spec(wdn.shape), vec_spec]),
        out_specs=x_spec,
        out_shape=jax.ShapeDtypeStruct(x.shape, F32),
        scratch_shapes=([pltpu.VMEM((tm, D_MODEL), BF16),
                         pltpu.VMEM((nc, SUBLANES, FFN_CHUNK), F32),
                         pltpu.VMEM((tm + SUBLANES, 2 * FFN_CHUNK), F32),
                         pltpu.VMEM((tm + SUBLANES, 2 * FFN_CHUNK), F32)]
                        + [pltpu.VMEM((nc, tm, FFN_CHUNK), BF16)] * FFN_TILES_PER_STEP),
        compiler_params=_params("arbitrary"),
        name="mix_ffn",
    )(*acts, *w_outs, g_mix.reshape(1, -1), x, g_pre.reshape(1, -1), wup, cw, wdn,
      g_post.reshape(1, -1))


def kernel(x, norm_mix_pre, norm_mix_post, norm_ffn_pre, norm_ffn_post, even_w_in, even_b_forget, even_q_norm, even_w_uq, even_kv_norm, even_w_ukv, even_w_out, odd_w_in, odd_lambda_q1, odd_lambda_k1, odd_lambda_q2, odd_lambda_k2, odd_subln, odd_w_out, ffn_w_up, ffn_conv_w, ffn_conv_b, ffn_w_down):
    bsz, seq, d = x.shape
    assert d == D_MODEL and seq % ATTN_TILE == 0 and seq % FFN_ROWS == 0
    t = bsz * seq
    tq = ATTN_TILE
    rope = _rope_tables(seq)
    xf = x.reshape(t, d)
    for layer in range(DEPTH):
        i = layer // 2
        if layer % 2 == 0:
            fq, fk, fvt, mq, mk, mvt = _even_proj(
                xf, norm_mix_pre[layer], even_w_in[i], even_b_forget[i], even_q_norm[i],
                even_w_uq[i], even_kv_norm[i], even_w_ukv[i], rope, bsz, seq)
            shp = (bsz, seq, HEAD_COLS)
            fo = _flash_attention(fq.reshape(shp), fk.reshape(shp), fvt, tq)
            mo = _flash_attention(mq.reshape(shp), mk.reshape(shp), mvt, tq)
            w_out = even_w_out[i].astype(BF16)
            acts = [fo.reshape(t, -1), mo.reshape(t, -1)]
            w_outs = [w_out[:FOX_WIDTH], w_out[FOX_WIDTH:]]
        else:
            lambda_init = 0.8 - 0.6 * math.exp(-0.3 * layer)
            q, k, vt = _odd_proj(xf, norm_mix_pre[layer], odd_w_in[i], bsz, seq)
            o = _diff_attention(q.reshape(bsz, seq, -1), k.reshape(bsz, seq, -1), vt,
                                odd_lambda_q1[i], odd_lambda_k1[i],
                                odd_lambda_q2[i], odd_lambda_k2[i], odd_subln[i], lambda_init, tq)
            acts, w_outs = [o.reshape(t, -1)], [odd_w_out[i].astype(BF16)]
        xf = _mix_ffn(acts, w_outs, norm_mix_post[layer], xf, norm_ffn_pre[layer], ffn_w_up[layer],
                      ffn_conv_w[layer], ffn_conv_b[layer], ffn_w_down[layer],
                      norm_ffn_post[layer], seq)
    return xf.reshape(bsz, seq, d)
```

```python
import functools
import math

import numpy as np
import jax
import jax.numpy as jnp
from jax import lax
from jax.experimental import pallas as pl
from jax.experimental.pallas import tpu as pltpu

D_MODEL = 1024
DEPTH = 4
RMS_EPS = 1e-6
NEG_INF = -1e30
HEADS = 8
FOX_HEAD_DIM = 64
FOX_WIDTH = HEADS * FOX_HEAD_DIM
MLA_NOPE_DIM = 64
MLA_ROPE_DIM = 32
MLA_V_DIM = 64
MLA_Q_RANK = 384
MLA_KV_RANK = 256
ROPE_THETA = 10000.0
DIFF_HEAD_DIM = 64
DIFF_V_DIM = 2 * DIFF_HEAD_DIM
D_FF = 2816
CONV_WIDTH = 3

LANES = 128
SUBLANES = 8
BF16_ROWS = 16
LOG2E = 1.4426950408889634
HEAD_COLS = HEADS * LANES
VMEM_LIMIT_BYTES = 56 * 1024 * 1024

PROJ_TILES_PER_STEP = 2
FFN_ROWS = 512
FFN_TILES_PER_STEP = 2
FFN_CHUNK = 256
ATTN_TILE = 512
FLASH_HEADS_PER_STEP, FLASH_KEYS, FLASH_LAG = 8, 256, 3
DIFF_HEADS_PER_STEP, DIFF_KEYS, DIFF_LAG = 4, 512, 2
F32 = jnp.float32
BF16 = jnp.bfloat16


def _params(*sem):
    return pltpu.CompilerParams(dimension_semantics=sem, vmem_limit_bytes=VMEM_LIMIT_BYTES)


def _const_spec(shape):
    nd = len(shape)
    return pl.BlockSpec(shape, lambda *_: (0,) * nd, pipeline_mode=pl.Buffered(1))


def _rms(x, g):
    return x * lax.rsqrt(jnp.mean(x * x, axis=-1, keepdims=True) + RMS_EPS) * g


def _split3(x):
    hi = x.astype(BF16)
    r1 = x - hi.astype(F32)
    mid = r1.astype(BF16)
    lo = (r1 - mid.astype(F32)).astype(BF16)
    return hi, mid, lo


def _rope(blk, ra, rb, rc):
    return blk * ra + pltpu.roll(blk, 16, 1) * rb + pltpu.roll(blk, LANES - 16, 1) * rc


def _vt_rows(dv):
    return dv + BF16_ROWS


def _store_vt(vt_ref, v, dv, chunk):
    tk = v.shape[0]
    per = LANES // dv
    for blk in range(v.shape[1] // LANES):
        vt = v[:, blk * LANES:(blk + 1) * LANES].T.astype(BF16)
        for s in range(per):
            vt_ref[0, blk * per + s, chunk, 0:dv, :] = vt[s * dv:(s + 1) * dv, :]
    row = lax.broadcasted_iota(jnp.int32, (BF16_ROWS, tk), 0)
    ones_row = jnp.where(row == 0, 1.0, 0.0).astype(BF16)
    for hd in range(HEADS):
        vt_ref[0, hd, chunk, dv:dv + BF16_ROWS, :] = ones_row


def _vt_spec(dv, chunks_per_step, tk, steps_per_seq):
    return pl.BlockSpec((1, HEADS, chunks_per_step, _vt_rows(dv), tk),
                        lambda i: (i // steps_per_seq, 0, i % steps_per_seq, 0, 0))


_EV_Q, _EV_K, _EV_V = 0, FOX_WIDTH, 2 * FOX_WIDTH
_EV_G = _EV_V + FOX_WIDTH
_EV_CQ = _EV_G + LANES
_EV_CKV = _EV_CQ + MLA_Q_RANK
_EV_KR = _EV_CKV + MLA_KV_RANK
_EV_WIDTH = _EV_KR + LANES


def _even_proj_kernel(x_ref, g_ref, w_ref, bf_ref, qn_ref, wuq_ref, kvn_ref, wuk_ref, wuv_ref,
                      ec_ref, ra_ref, rb_ref, rc_ref, cst_ref,
                      fq_ref, fk_ref, fv_ref, mq_ref, mk_ref, mv_ref, carry_ref,
                      *, steps_per_seq):
    tm = fv_ref.shape[4]
    n_tiles = x_ref.shape[0] // tm
    i = pl.program_id(0)

    @pl.when(i % steps_per_seq == 0)
    def _():
        carry_ref[...] = jnp.zeros_like(carry_ref)

    hs = [_rms(x_ref[s * tm:(s + 1) * tm, :], g_ref[...]).astype(BF16) for s in range(n_tiles)]
    for s, h in enumerate(hs):
        _even_proj_tile(h, s, slice(s * tm, (s + 1) * tm), w_ref, bf_ref, qn_ref, wuq_ref, kvn_ref,
                        wuk_ref, wuv_ref, ec_ref, ra_ref, rb_ref, rc_ref, cst_ref,
                        fq_ref, fk_ref, fv_ref, mq_ref, mk_ref, mv_ref, carry_ref)


def _even_proj_tile(h, s, rows, w_ref, bf_ref, qn_ref, wuq_ref, kvn_ref, wuk_ref, wuv_ref,
                    ec_ref, ra_ref, rb_ref, rc_ref, cst_ref,
                    fq_ref, fk_ref, fv_ref, mq_ref, mk_ref, mv_ref, carry_ref):
    tm = h.shape[0]
    q_aug = cst_ref[0:1, :]

    def proj(lo, hi):
        return jnp.dot(h, w_ref[:, lo:hi], preferred_element_type=F32)

    lane = lax.broadcasted_iota(jnp.int32, (tm, LANES), 1)

    def spread_heads(o_ref, packed, aug):
        for pair in range(HEADS // 2):
            blk = packed[:, pair * LANES:(pair + 1) * LANES]
            for odd, src in enumerate((blk, pltpu.roll(blk, FOX_HEAD_DIM, 1))):
                sl = slice((2 * pair + odd) * LANES, (2 * pair + odd + 1) * LANES)
                o_ref[rows, sl] = jnp.where(lane < FOX_HEAD_DIM, src, aug[:, sl]).astype(BF16)

    ra, rb, rc = ra_ref[rows, :], rb_ref[rows, :], rc_ref[rows, :]
    fox_scale = FOX_HEAD_DIM ** -0.5 * LOG2E
    mla_scale = (MLA_NOPE_DIM + MLA_ROPE_DIM) ** -0.5 * LOG2E

    z = proj(_EV_G, _EV_CQ) + bf_ref[...]
    cq_raw = proj(_EV_CQ, _EV_CKV)
    ckv_raw = proj(_EV_CKV, _EV_KR)
    kr_raw = proj(_EV_KR, _EV_WIDTH)

    logf = (jnp.minimum(z, 0.0) - jnp.log1p(jnp.exp(-jnp.abs(z)))) * LOG2E
    spread_heads(fq_ref, proj(_EV_Q, _EV_K) * fox_scale, q_aug)

    row = lax.broadcasted_iota(jnp.int32, (tm, tm), 0)
    col = lax.broadcasted_iota(jnp.int32, (tm, tm), 1)
    tril = jnp.where(row >= col, 1.0, 0.0).astype(BF16)
    parts = jnp.dot(tril, jnp.concatenate(_split3(logf), axis=1), preferred_element_type=F32)
    _store_vt(fv_ref, proj(_EV_V, _EV_G), FOX_HEAD_DIM, s)
    c = (parts[:, :LANES] + parts[:, LANES:2 * LANES] + parts[:, 2 * LANES:]
         + carry_ref[SUBLANES - 1:SUBLANES, :])
    carry_ref[...] = c[tm - SUBLANES:, :]

    cq = _rms(cq_raw, qn_ref[...]).astype(BF16)
    q = jnp.dot(cq, wuq_ref[...], preferred_element_type=F32) * mla_scale
    for hd in range(HEADS):
        sl = slice(hd * LANES, (hd + 1) * LANES)
        mq_ref[rows, sl] = _rope(q[:, sl], ra, rb, rc).astype(BF16)

    ckv = _rms(ckv_raw, kvn_ref[...]).astype(BF16)
    kr = _rope(kr_raw, ra, rb, rc)
    kn = jnp.dot(ckv, wuk_ref[...], preferred_element_type=F32)
    for hd in range(HEADS):
        sl = slice(hd * LANES, (hd + 1) * LANES)
        mk_ref[rows, sl] = (kn[:, sl] + kr).astype(BF16)
    _store_vt(mv_ref, jnp.dot(ckv, wuv_ref[...], preferred_element_type=F32), MLA_V_DIM, s)

    c_aug = jnp.dot(jnp.concatenate(_split3(c), axis=1), ec_ref[...], preferred_element_type=F32)
    spread_heads(fk_ref, proj(_EV_K, _EV_V), c_aug)


def _pad_heads(w, dh):
    k = w.shape[0]
    w = w.reshape(k, HEADS, dh)
    return jnp.pad(w, ((0, 0), (0, 0), (0, LANES - dh))).reshape(k, HEAD_COLS)


def _even_consts():
    cst = np.zeros((SUBLANES, HEAD_COLS), np.float32)
    ec = np.zeros((3 * LANES, HEAD_COLS), np.float32)
    for hd in range(HEADS):
        cst[0, hd * LANES + 64: hd * LANES + 67] = 1.0
        for p in range(3):
            ec[p * LANES + hd, hd * LANES + 64 + p] = -1.0
    return jnp.asarray(cst), jnp.asarray(ec, dtype=BF16)


def _rope_tables(seq):
    half = MLA_ROPE_DIM // 2
    inv = ROPE_THETA ** (-jnp.arange(0, MLA_ROPE_DIM, 2, dtype=F32) / MLA_ROPE_DIM)
    ang = jnp.arange(seq, dtype=F32)[:, None] * inv[None, :]
    cos, sin = jnp.cos(ang), jnp.sin(ang)
    zeros = jnp.zeros((seq, half), F32)
    ra = jnp.concatenate([jnp.ones((seq, 64), F32), cos, cos, jnp.zeros((seq, 32), F32)], axis=1)
    rb = jnp.concatenate([jnp.zeros((seq, 64), F32), zeros, sin, jnp.zeros((seq, 32), F32)], axis=1)
    rc = jnp.concatenate([jnp.zeros((seq, 64), F32), -sin, zeros, jnp.zeros((seq, 32), F32)], axis=1)
    return ra, rb, rc


def _even_proj(x, g, w_in, b_forget, q_norm, w_uq, kv_norm, w_ukv, rope, bsz, seq):
    t = x.shape[0]
    tk = ATTN_TILE
    tm = tk * PROJ_TILES_PER_STEP
    cuts = np.cumsum([FOX_WIDTH, FOX_WIDTH, FOX_WIDTH, HEADS, MLA_Q_RANK, MLA_KV_RANK]).tolist()
    wfq, wfk, wfv, wg, wcq, wckv, wkr = jnp.split(w_in, cuts, axis=1)
    w1 = jnp.concatenate([
        wfq, wfk, wfv,
        jnp.pad(wg, ((0, 0), (0, LANES - HEADS))), wcq, wckv,
        jnp.pad(wkr, ((0, 0), (64, LANES - 64 - MLA_ROPE_DIM)))], axis=1).astype(BF16)
    assert w1.shape[1] == _EV_WIDTH
    bf = jnp.pad(b_forget, (0, LANES - HEADS)).reshape(1, LANES)
    wuq = _pad_heads(w_uq, MLA_NOPE_DIM + MLA_ROPE_DIM).astype(BF16)
    wukv = w_ukv.reshape(MLA_KV_RANK, HEADS, MLA_NOPE_DIM + MLA_V_DIM)
    wuk = _pad_heads(wukv[:, :, :MLA_NOPE_DIM].reshape(MLA_KV_RANK, -1), MLA_NOPE_DIM).astype(BF16)
    wuv = wukv[:, :, MLA_NOPE_DIM:].reshape(MLA_KV_RANK, -1).astype(BF16)
    cst, ec = _even_consts()
    ra, rb, rc = rope
    steps_per_seq = seq // tm
    rope_spec = pl.BlockSpec((tm, LANES), lambda i: (i % steps_per_seq, 0))
    row_spec = pl.BlockSpec((tm, HEAD_COLS), lambda i: (i, 0))
    out_sds = jax.ShapeDtypeStruct((t, HEAD_COLS), BF16)
    vt_spec = _vt_spec(FOX_HEAD_DIM, PROJ_TILES_PER_STEP, tk, steps_per_seq)
    vt_sds = jax.ShapeDtypeStruct((bsz, HEADS, seq // tk, _vt_rows(FOX_HEAD_DIM), tk), BF16)
    return pl.pallas_call(
        functools.partial(_even_proj_kernel, steps_per_seq=steps_per_seq),
        grid=(t // tm,),
        in_specs=[pl.BlockSpec((tm, D_MODEL), lambda i: (i, 0)),
                  _const_spec((1, D_MODEL)), _const_spec(w1.shape), _const_spec((1, LANES)),
                  _const_spec((1, MLA_Q_RANK)), _const_spec(wuq.shape),
                  _const_spec((1, MLA_KV_RANK)), _const_spec(wuk.shape), _const_spec(wuv.shape),
                  _const_spec(ec.shape), rope_spec, rope_spec, rope_spec, _const_spec(cst.shape)],
        out_specs=[row_spec, row_spec, vt_spec, row_spec, row_spec, vt_spec],
        out_shape=[out_sds, out_sds, vt_sds, out_sds, out_sds, vt_sds],
        scratch_shapes=[pltpu.VMEM((SUBLANES, LANES), F32)],
        compiler_params=_params("arbitrary"),
        name="even_proj",
    )(x, g.reshape(1, -1), w1, bf, q_norm.reshape(1, -1), wuq, kv_norm.reshape(1, -1), wuk, wuv,
      ec, ra, rb, rc, cst)


def _alibi_slope(hd):
    return 2.0 ** (-8.0 * (hd + 1) / HEADS)


def _odd_proj_kernel(x_ref, g_ref, w_ref, q_ref, k_ref, v_ref, *, steps_per_seq):
    tm = v_ref.shape[4]
    n_tiles = x_ref.shape[0] // tm
    i = pl.program_id(0)
    scale = DIFF_HEAD_DIM ** -0.5 * LOG2E
    d = DIFF_HEAD_DIM
    lane = lax.broadcasted_iota(jnp.int32, (tm, LANES), 1)
    q_aug = jnp.where((lane >= d) & (lane < d + 3), 1.0, 0.0)
    hs = [_rms(x_ref[s * tm:(s + 1) * tm, :], g_ref[...]).astype(BF16) for s in range(n_tiles)]
    for s, h in enumerate(hs):
        rows = slice(s * tm, (s + 1) * tm)
        v = jnp.dot(h, w_ref[:, 2 * HEAD_COLS:], preferred_element_type=F32)
        k = jnp.dot(h, w_ref[:, HEAD_COLS:2 * HEAD_COLS], preferred_element_type=F32)
        _store_vt(v_ref, v, DIFF_V_DIM, s)
        q = jnp.dot(h, w_ref[:, :HEAD_COLS], preferred_element_type=F32)
        pos = (((i % steps_per_seq) * n_tiles + s) * tm
               + lax.broadcasted_iota(jnp.int32, (tm, 1), 0)).astype(F32)
        for hd in range(HEADS):
            sl = slice(hd * LANES, (hd + 1) * LANES)
            b_hi, b_mid, b_lo = (p.astype(F32)
                                 for p in _split3(pos * (_alibi_slope(hd) * LOG2E)))
            k_aug = jnp.where(lane == d, b_hi,
                              jnp.where(lane == d + 1, b_mid, jnp.where(lane == d + 2, b_lo, 0.0)))
            qb, kb = q[:, sl] * scale, k[:, sl]
            for br, (qs, ks) in enumerate(((qb, kb),
                                           (pltpu.roll(qb, d, 1), pltpu.roll(kb, d, 1)))):
                so = slice((2 * hd + br) * LANES, (2 * hd + br + 1) * LANES)
                q_ref[rows, so] = jnp.where(lane < d, qs, q_aug).astype(BF16)
                k_ref[rows, so] = jnp.where(lane < d, ks, k_aug).astype(BF16)


def _odd_proj(x, g, w_in, bsz, seq):
    t = x.shape[0]
    tk = ATTN_TILE
    tm = tk * PROJ_TILES_PER_STEP
    w = w_in.astype(BF16)
    steps_per_seq = seq // tm
    return pl.pallas_call(
        functools.partial(_odd_proj_kernel, steps_per_seq=steps_per_seq),
        grid=(t // tm,),
        in_specs=[pl.BlockSpec((tm, D_MODEL), lambda i: (i, 0)),
                  _const_spec((1, D_MODEL)), _const_spec(w.shape)],
        out_specs=[pl.BlockSpec((tm, 2 * HEAD_COLS), lambda i: (i, 0)),
                   pl.BlockSpec((tm, 2 * HEAD_COLS), lambda i: (i, 0)),
                   _vt_spec(DIFF_V_DIM, PROJ_TILES_PER_STEP, tk, steps_per_seq)],
        out_shape=[jax.ShapeDtypeStruct((t, 2 * HEAD_COLS), BF16),
                   jax.ShapeDtypeStruct((t, 2 * HEAD_COLS), BF16),
                   jax.ShapeDtypeStruct((bsz, HEADS, seq // tk, _vt_rows(DIFF_V_DIM), tk), BF16)],
        compiler_params=_params("arbitrary"),
        name="odd_proj",
    )(x, g.reshape(1, -1), w)


_NT = (((1,), (1,)), ((), ()))


def _attend(s, vt, m, acc, key0=None):
    if key0 is None:
        m_new = jnp.maximum(m, jnp.max(s, axis=0, keepdims=True))
        p = jnp.exp2(s - m_new).astype(BF16)
    else:
        nr, nc = s.shape[0] // LANES, s.shape[1] // LANES
        key = lax.broadcasted_iota(jnp.int32, (LANES, LANES), 0)
        qry = lax.broadcasted_iota(jnp.int32, (LANES, LANES), 1)
        sq = [[s[a * LANES:(a + 1) * LANES, b * LANES:(b + 1) * LANES] for b in range(nc)]
              for a in range(nr)]
        k0 = key0 // LANES
        for a in range(nr):
            if k0 + a < nc:
                sq[a][k0 + a] = jnp.where(key <= qry, sq[a][k0 + a], NEG_INF)
        col_max = []
        for b in range(nc):
            vis = [sq[a][b] for a in range(nr) if k0 + a <= b]
            col_max.append(functools.reduce(jnp.maximum, [jnp.max(x, axis=0, keepdims=True)
                                                          for x in vis])
                           if vis else jnp.full((1, LANES), NEG_INF, F32))
        m_new = jnp.maximum(m, jnp.concatenate(col_max, axis=1))
        p = jnp.concatenate(
            [jnp.concatenate(
                [jnp.exp2(sq[a][b] - m_new[:, b * LANES:(b + 1) * LANES]).astype(BF16)
                 if k0 + a <= b else jnp.zeros((LANES, LANES), BF16) for b in range(nc)], axis=1)
             for a in range(nr)], axis=0)
    acc = jnp.exp2(m - m_new) * acc + jnp.dot(vt, p, preferred_element_type=F32)
    return m_new, acc


def _stream_attention(qk, vts, s_ref, rows, n):
    lag, tk, t = s_ref.shape
    r = t // tk
    qi = pl.program_id(2)
    assert n >= lag
    for l in range(lag):
        s_ref[l] = qk(l, 0, 0)

    def sweep(j, u, carry, diagonal):
        carry = list(carry)
        nxt = (j, u + 1) if u + 1 < r else (j + 1, 0)
        last = diagonal and u + 1 == r
        pending = [s_ref[l] for l in range(lag)]
        for k in range(n):
            ahead = k + lag
            if ahead < n:
                pending.append(qk(ahead, j, u))
            elif not last:
                pending.append(qk(ahead - n, *nxt))
            carry[2 * k], carry[2 * k + 1] = _attend(pending.pop(0), vts(k, j, u), carry[2 * k],
                                                     carry[2 * k + 1], u * tk if diagonal else None)
        if not last:
            for l in range(lag):
                s_ref[l] = pending[l]
        return tuple(carry)

    def body(j, carry):
        for u in range(r):
            carry = sweep(j, u, carry, False)
        return carry

    init = (jnp.full((1, t), NEG_INF, F32), jnp.zeros((rows, t), F32)) * n
    carry = lax.fori_loop(0, qi, body, init)
    for u in range(r):
        carry = sweep(qi, u, carry, True)
    return carry[1::2]


def _flash_kernel(q_ref, k_ref, vt_ref, o_ref, s_ref):
    n, rows, t = vt_ref.shape[1], vt_ref.shape[3], vt_ref.shape[4]
    tk = s_ref.shape[1]
    dv = rows - BF16_ROWS
    qs = [q_ref[0, :, hd * LANES:(hd + 1) * LANES] for hd in range(n)]

    def qk(hd, j, u):
        start = pl.multiple_of(j * t, t) + u * tk
        kc = k_ref[0, pl.ds(start, tk), hd * LANES:(hd + 1) * LANES]
        return lax.dot_general(kc, qs[hd], _NT, preferred_element_type=F32)

    def vts(hd, j, u):
        return vt_ref[0, hd, j, :, u * tk:(u + 1) * tk]

    accs = _stream_attention(qk, vts, s_ref, rows, n)
    o_t = jnp.concatenate([a[:dv] / a[dv:dv + 1] for a in accs], axis=0)
    o_ref[0] = o_t.T.astype(BF16)


def _flash_attention(q, k, vt, tq):
    b, s, _ = q.shape
    _, _, chunks, rows, tk = vt.shape
    assert tq == tk
    n = FLASH_HEADS_PER_STEP
    dv = rows - BF16_ROWS
    return pl.pallas_call(
        _flash_kernel,
        grid=(b, HEADS // n, s // tq),
        in_specs=[pl.BlockSpec((1, tq, n * LANES), lambda bi, hp, i: (bi, i, hp)),
                  pl.BlockSpec((1, s, n * LANES), lambda bi, hp, i: (bi, 0, hp)),
                  pl.BlockSpec((1, n, chunks, rows, tk), lambda bi, hp, i: (bi, hp, 0, 0, 0))],
        out_specs=pl.BlockSpec((1, tq, n * dv), lambda bi, hp, i: (bi, i, hp)),
        out_shape=jax.ShapeDtypeStruct((b, s, HEADS * (rows - BF16_ROWS)), BF16),
        scratch_shapes=[pltpu.VMEM((FLASH_LAG, FLASH_KEYS, tq), F32)],
        compiler_params=_params("arbitrary", "arbitrary", "arbitrary"),
        name="flash_attention",
    )(q, k, vt)


def _diff_kernel(q_ref, k_ref, vt_ref, lq1_ref, lk1_ref, lq2_ref, lk2_ref, sub_ref, o_ref, s_ref,
                 *, lambda_init):
    n, rows, t = vt_ref.shape[1], vt_ref.shape[3], vt_ref.shape[4]
    tk = s_ref.shape[1]
    dv = rows - BF16_ROWS
    qs = [q_ref[0, :, st * LANES:(st + 1) * LANES] for st in range(2 * n)]

    def qk(st, j, u):
        start = pl.multiple_of(j * t, t) + u * tk
        kc = k_ref[0, pl.ds(start, tk), st * LANES:(st + 1) * LANES]
        return lax.dot_general(kc, qs[st], _NT, preferred_element_type=F32)

    def vts(st, j, u):
        return vt_ref[0, st // 2, j, :, u * tk:(u + 1) * tk]

    accs = _stream_attention(qk, vts, s_ref, rows, 2 * n)
    lam = (jnp.exp(jnp.sum(lq1_ref[...] * lk1_ref[...], axis=1, keepdims=True))
           - jnp.exp(jnp.sum(lq2_ref[...] * lk2_ref[...], axis=1, keepdims=True)) + lambda_init)
    for hd in range(n):
        a1, a2 = accs[2 * hd], accs[2 * hd + 1]
        o = (a1[:dv] / a1[dv:dv + 1] - lam * (a2[:dv] / a2[dv:dv + 1])).T
        o_ref[0, :, hd * dv:(hd + 1) * dv] = (
            _rms(o, sub_ref[...]) * (1.0 - lambda_init)).astype(BF16)


def _diff_attention(q, k, vt, lq1, lk1, lq2, lk2, subln, lambda_init, tq):
    b, s, _ = q.shape
    _, _, chunks, rows, tk = vt.shape
    assert tq == tk
    vec = lambda a: a.reshape(1, -1)
    n = DIFF_HEADS_PER_STEP
    return pl.pallas_call(
        functools.partial(_diff_kernel, lambda_init=lambda_init),
        grid=(b, HEADS // n, s // tq),
        in_specs=[pl.BlockSpec((1, tq, 2 * n * LANES), lambda bi, hd, i: (bi, i, hd)),
                  pl.BlockSpec((1, s, 2 * n * LANES), lambda bi, hd, i: (bi, 0, hd)),
                  pl.BlockSpec((1, n, chunks, rows, tk), lambda bi, hd, i: (bi, hd, 0, 0, 0)),
                  _const_spec((1, DIFF_HEAD_DIM)), _const_spec((1, DIFF_HEAD_DIM)),
                  _const_spec((1, DIFF_HEAD_DIM)), _const_spec((1, DIFF_HEAD_DIM)),
                  _const_spec((1, DIFF_V_DIM))],
        out_specs=pl.BlockSpec((1, tq, n * LANES), lambda bi, hd, i: (bi, i, hd)),
        out_shape=jax.ShapeDtypeStruct((b, s, HEADS * (rows - BF16_ROWS)), BF16),
        scratch_shapes=[pltpu.VMEM((DIFF_LAG, DIFF_KEYS, tq), F32)],
        compiler_params=_params("arbitrary", "arbitrary", "arbitrary"),
        name="diff_attention",
    )(q, k, vt, vec(lq1), vec(lk1), vec(lq2), vec(lk2), vec(subln))


_N_FFN_CHUNKS = D_FF // FFN_CHUNK
_GELU_C = math.sqrt(2.0 / math.pi)


def _mix_ffn_kernel(*refs, n_in, steps_per_seq):
    a_refs, wo_refs = refs[:n_in], refs[n_in:2 * n_in]
    (gmix_ref, x_ref, gpre_ref, wup_ref, cw_ref, wdn_ref, gpost_ref, o_ref,
     h_ref, hist_ref, gva_ref, gvb_ref, *mid_refs) = refs[2 * n_in:]
    tm = h_ref.shape[0]
    nc = _N_FFN_CHUNKS
    i = pl.program_id(0)

    @pl.when(i % steps_per_seq == 0)
    def _():
        hist_ref[...] = jnp.zeros_like(hist_ref)

    def out_proj(rows):
        mix = jnp.dot(a_refs[0][rows, :], wo_refs[0][...], preferred_element_type=F32)
        for a_ref, wo_ref in zip(a_refs[1:], wo_refs[1:]):
            mix = mix + jnp.dot(a_ref[rows, :], wo_ref[...], preferred_element_type=F32)
        return mix

    def pre_norms(rows, mix):
        x1 = x_ref[rows, :] + _rms(mix, gmix_ref[...])
        o_ref[rows, :] = x1
        h_ref[...] = _rms(x1, gpre_ref[...]).astype(BF16)

    def up(c):
        return jnp.dot(h_ref[...], wup_ref[c], preferred_element_type=F32)

    def down(mid_ref):
        f = jnp.dot(mid_ref[0], wdn_ref[0], preferred_element_type=F32)
        for c in range(1, nc):
            f = f + jnp.dot(mid_ref[c], wdn_ref[c], preferred_element_type=F32)
        return f

    def gated(c, gv_ref, mid_ref):
        gv_ref[0:SUBLANES, :FFN_CHUNK] = hist_ref[c]
        gate = gv_ref[SUBLANES:, :FFN_CHUNK]
        val = gv_ref[SUBLANES:, FFN_CHUNK:]
        hist_ref[c] = gv_ref[tm:, :FFN_CHUNK]
        cw = cw_ref[c]
        pre = (gv_ref[SUBLANES - 2:SUBLANES - 2 + tm, :FFN_CHUNK] * cw[0:1, :]
               + gv_ref[SUBLANES - 1:SUBLANES - 1 + tm, :FFN_CHUNK] * cw[1:2, :]
               + gate * cw[2:3, :] + cw[3:4, :])
        pre = pre.astype(BF16)
        act = 0.5 * pre * (1.0 + jnp.tanh(_GELU_C * (pre + 0.044715 * (pre * pre * pre))))
        mid_ref[c] = act * val.astype(BF16)

    def chunks_after_first(mid_ref):
        def body(k, _):
            gated(2 * k, gva_ref, mid_ref)
            gvb_ref[SUBLANES:, :] = up(2 * k + 1)
            gated(2 * k + 1, gvb_ref, mid_ref)
            gva_ref[SUBLANES:, :] = up(2 * k + 2)
            return 0

        lax.fori_loop(0, nc // 2, body, 0)
        gated(nc - 1, gva_ref, mid_ref)

    assert nc % 2 == 1
    n_tiles = len(mid_refs)
    rows = [slice(s * tm, (s + 1) * tm) for s in range(n_tiles)]
    pre_norms(rows[0], out_proj(rows[0]))
    gva_ref[SUBLANES:, :] = up(0)
    for s in range(n_tiles):
        chunks_after_first(mid_refs[s])
        if s + 1 < n_tiles:
            mix_next = out_proj(rows[s + 1])
        f = down(mid_refs[s])
        if s + 1 < n_tiles:
            pre_norms(rows[s + 1], mix_next)
            gva_ref[SUBLANES:, :] = up(0)
        o_ref[rows[s], :] = o_ref[rows[s], :] + _rms(f, gpost_ref[...])


def _mix_ffn(acts, w_outs, g_mix, x, g_pre, w_up, conv_w, conv_b, w_down, g_post, seq):
    t = x.shape[0]
    tm = FFN_ROWS
    nc = _N_FFN_CHUNKS
    n_in = len(acts)
    wg = w_up[:, :D_FF].reshape(D_MODEL, nc, FFN_CHUNK)
    wv = w_up[:, D_FF:].reshape(D_MODEL, nc, FFN_CHUNK)
    wup = jnp.concatenate([wg, wv], axis=2).transpose(1, 0, 2).astype(BF16)
    wdn = w_down.reshape(nc, FFN_CHUNK, D_MODEL).astype(BF16)
    cw = jnp.concatenate([conv_w, conv_b[None, :],
                          jnp.zeros((SUBLANES - CONV_WIDTH - 1, D_FF), F32)], axis=0)
    cw = cw.reshape(SUBLANES, nc, FFN_CHUNK).transpose(1, 0, 2)
    step_rows = tm * FFN_TILES_PER_STEP
    assert seq % step_rows == 0
    x_spec = pl.BlockSpec((step_rows, D_MODEL), lambda i: (i, 0))
    vec_spec = _const_spec((1, D_MODEL))
    return pl.pallas_call(
        functools.partial(_mix_ffn_kernel, n_in=n_in, steps_per_seq=seq // step_rows),
        grid=(t // step_rows,),
        in_specs=([pl.BlockSpec((step_rows, a.shape[1]), lambda i: (i, 0)) for a in acts]
                  + [_const_spec(w.shape) for w in w_outs]
                  + [vec_spec, x_spec, vec_spec, _const_spec(wup.shape), _const_spec(cw.shape),
                     _const_spec(wdn.shape), vec_spec]),
        out_specs=x_spec,
        out_shape=jax.ShapeDtypeStruct(x.shape, F32),
        scratch_shapes=([pltpu.VMEM((tm, D_MODEL), BF16),
                         pltpu.VMEM((nc, SUBLANES, FFN_CHUNK), F32),
                         pltpu.VMEM((tm + SUBLANES, 2 * FFN_CHUNK), F32),
                         pltpu.VMEM((tm + SUBLANES, 2 * FFN_CHUNK), F32)]
                        + [pltpu.VMEM((nc, tm, FFN_CHUNK), BF16)] * FFN_TILES_PER_STEP),
        compiler_params=_params("arbitrary"),
        name="mix_ffn",
    )(*acts, *w_outs, g_mix.reshape(1, -1), x, g_pre.reshape(1, -1), wup, cw, wdn,
      g_post.reshape(1, -1))


def kernel(x, norm_mix_pre, norm_mix_post, norm_ffn_pre, norm_ffn_post, even_w_in, even_b_forget, even_q_norm, even_w_uq, even_kv_norm, even_w_ukv, even_w_out, odd_w_in, odd_lambda_q1, odd_lambda_k1, odd_lambda_q2, odd_lambda_k2, odd_subln, odd_w_out, ffn_w_up, ffn_conv_w, ffn_conv_b, ffn_w_down):
    bsz, seq, d = x.shape
    assert d == D_MODEL and seq % ATTN_TILE == 0 and seq % FFN_ROWS == 0
    t = bsz * seq
    tq = ATTN_TILE
    rope = _rope_tables(seq)
    xf = x.reshape(t, d)
    for layer in range(DEPTH):
        i = layer // 2
        if layer % 2 == 0:
            fq, fk, fvt, mq, mk, mvt = _even_proj(
                xf, norm_mix_pre[layer], even_w_in[i], even_b_forget[i], even_q_norm[i],
                even_w_uq[i], even_kv_norm[i], even_w_ukv[i], rope, bsz, seq)
            shp = (bsz, seq, HEAD_COLS)
            fo = _flash_attention(fq.reshape(shp), fk.reshape(shp), fvt, tq)
            mo = _flash_attention(mq.reshape(shp), mk.reshape(shp), mvt, tq)
            w_out = even_w_out[i].astype(BF16)
            acts = [fo.reshape(t, -1), mo.reshape(t, -1)]
            w_outs = [w_out[:FOX_WIDTH], w_out[FOX_WIDTH:]]
        else:
            lambda_init = 0.8 - 0.6 * math.exp(-0.3 * layer)
            q, k, vt = _odd_proj(xf, norm_mix_pre[layer], odd_w_in[i], bsz, seq)
            o = _diff_attention(q.reshape(bsz, seq, -1), k.reshape(bsz, seq, -1), vt,
                                odd_lambda_q1[i], odd_lambda_k1[i],
                                odd_lambda_q2[i], odd_lambda_k2[i], odd_subln[i], lambda_init, tq)
            acts, w_outs = [o.reshape(t, -1)], [odd_w_out[i].astype(BF16)]
        xf = _mix_ffn(acts, w_outs, norm_mix_post[layer], xf, norm_ffn_pre[layer], ffn_w_up[layer],
                      ffn_conv_w[layer], ffn_conv_b[layer], ffn_w_down[layer],
                      norm_ffn_post[layer], seq)
    return xf.reshape(bsz, seq, d)
```

```python
import functools
import math

import numpy as np
import jax
import jax.numpy as jnp
from jax import lax
from jax.experimental import pallas as pl
from jax.experimental.pallas import tpu as pltpu

D_MODEL = 1024
DEPTH = 4
RMS_EPS = 1e-6
NEG_INF = -1e30
HEADS = 8
FOX_HEAD_DIM = 64
FOX_WIDTH = HEADS * FOX_HEAD_DIM
MLA_NOPE_DIM = 64
MLA_ROPE_DIM = 32
MLA_V_DIM = 64
MLA_Q_RANK = 384
MLA_KV_RANK = 256
ROPE_THETA = 10000.0
DIFF_HEAD_DIM = 64
DIFF_V_DIM = 2 * DIFF_HEAD_DIM
D_FF = 2816
CONV_WIDTH = 3

LANES = 128
SUBLANES = 8
BF16_ROWS = 16
LOG2E = 1.4426950408889634
HEAD_COLS = HEADS * LANES
VMEM_LIMIT_BYTES = 56 * 1024 * 1024

PROJ_TILES_PER_STEP = 2
FFN_ROWS = 512
FFN_TILES_PER_STEP = 2
FFN_CHUNK = 256
ATTN_TILE = 512
FLASH_HEADS_PER_STEP, FLASH_KEYS, FLASH_LAG = 8, 256, 3
DIFF_HEADS_PER_STEP, DIFF_KEYS, DIFF_LAG = 4, 512, 2
F32 = jnp.float32
BF16 = jnp.bfloat16


def _params(*sem):
    return pltpu.CompilerParams(dimension_semantics=sem, vmem_limit_bytes=VMEM_LIMIT_BYTES)


def _const_spec(shape):
    nd = len(shape)
    return pl.BlockSpec(shape, lambda *_: (0,) * nd, pipeline_mode=pl.Buffered(1))


def _rms(x, g):
    return x * lax.rsqrt(jnp.mean(x * x, axis=-1, keepdims=True) + RMS_EPS) * g


def _split3(x):
    hi = x.astype(BF16)
    r1 = x - hi.astype(F32)
    mid = r1.astype(BF16)
    lo = (r1 - mid.astype(F32)).astype(BF16)
    return hi, mid, lo


def _rope(blk, ra, rb, rc):
    return blk * ra + pltpu.roll(blk, 16, 1) * rb + pltpu.roll(blk, LANES - 16, 1) * rc


def _vt_rows(dv):
    return dv + BF16_ROWS


def _store_vt(vt_ref, v, dv, chunk):
    tk = v.shape[0]
    per = LANES // dv
    for blk in range(v.shape[1] // LANES):
        vt = v[:, blk * LANES:(blk + 1) * LANES].T.astype(BF16)
        for s in range(per):
            vt_ref[0, blk * per + s, chunk, 0:dv, :] = vt[s * dv:(s + 1) * dv, :]
    row = lax.broadcasted_iota(jnp.int32, (BF16_ROWS, tk), 0)
    ones_row = jnp.where(row == 0, 1.0, 0.0).astype(BF16)
    for hd in range(HEADS):
        vt_ref[0, hd, chunk, dv:dv + BF16_ROWS, :] = ones_row


def _vt_spec(dv, chunks_per_step, tk, steps_per_seq):
    return pl.BlockSpec((1, HEADS, chunks_per_step, _vt_rows(dv), tk),
                        lambda i: (i // steps_per_seq, 0, i % steps_per_seq, 0, 0))


_EV_Q, _EV_K, _EV_V = 0, FOX_WIDTH, 2 * FOX_WIDTH
_EV_G = _EV_V + FOX_WIDTH
_EV_CQ = _EV_G + LANES
_EV_CKV = _EV_CQ + MLA_Q_RANK
_EV_KR = _EV_CKV + MLA_KV_RANK
_EV_WIDTH = _EV_KR + LANES


def _even_proj_kernel(x_ref, g_ref, w_ref, bf_ref, qn_ref, wuq_ref, kvn_ref, wuk_ref, wuv_ref,
                      ec_ref, ra_ref, rb_ref, rc_ref, cst_ref,
                      fq_ref, fk_ref, fv_ref, mq_ref, mk_ref, mv_ref, carry_ref,
                      *, steps_per_seq):
    tm = fv_ref.shape[4]
    n_tiles = x_ref.shape[0] // tm
    i = pl.program_id(0)

    @pl.when(i % steps_per_seq == 0)
    def _():
        carry_ref[...] = jnp.zeros_like(carry_ref)

    hs = [_rms(x_ref[s * tm:(s + 1) * tm, :], g_ref[...]).astype(BF16) for s in range(n_tiles)]
    for s, h in enumerate(hs):
        _even_proj_tile(h, s, slice(s * tm, (s + 1) * tm), w_ref, bf_ref, qn_ref, wuq_ref, kvn_ref,
                        wuk_ref, wuv_ref, ec_ref, ra_ref, rb_ref, rc_ref, cst_ref,
                        fq_ref, fk_ref, fv_ref, mq_ref, mk_ref, mv_ref, carry_ref)


def _even_proj_tile(h, s, rows, w_ref, bf_ref, qn_ref, wuq_ref, kvn_ref, wuk_ref, wuv_ref,
                    ec_ref, ra_ref, rb_ref, rc_ref, cst_ref,
                    fq_ref, fk_ref, fv_ref, mq_ref, mk_ref, mv_ref, carry_ref):
    tm = h.shape[0]
    q_aug = cst_ref[0:1, :]

    def proj(lo, hi):
        return jnp.dot(h, w_ref[:, lo:hi], preferred_element_type=F32)

    lane = lax.broadcasted_iota(jnp.int32, (tm, LANES), 1)

    def spread_heads(o_ref, packed, aug):
        for pair in range(HEADS // 2):
            blk = packed[:, pair * LANES:(pair + 1) * LANES]
            for odd, src in enumerate((blk, pltpu.roll(blk, FOX_HEAD_DIM, 1))):
                sl = slice((2 * pair + odd) * LANES, (2 * pair + odd + 1) * LANES)
                o_ref[rows, sl] = jnp.where(lane < FOX_HEAD_DIM, src, aug[:, sl]).astype(BF16)

    ra, rb, rc = ra_ref[rows, :], rb_ref[rows, :], rc_ref[rows, :]
    fox_scale = FOX_HEAD_DIM ** -0.5 * LOG2E
    mla_scale = (MLA_NOPE_DIM + MLA_ROPE_DIM) ** -0.5 * LOG2E

    z = proj(_EV_G, _EV_CQ) + bf_ref[...]
    cq_raw = proj(_EV_CQ, _EV_CKV)
    ckv_raw = proj(_EV_CKV, _EV_KR)
    kr_raw = proj(_EV_KR, _EV_WIDTH)

    logf = (jnp.minimum(z, 0.0) - jnp.log1p(jnp.exp(-jnp.abs(z)))) * LOG2E
    spread_heads(fq_ref, proj(_EV_Q, _EV_K) * fox_scale, q_aug)

    row = lax.broadcasted_iota(jnp.int32, (tm, tm), 0)
    col = lax.broadcasted_iota(jnp.int32, (tm, tm), 1)
    tril = jnp.where(row >= col, 1.0, 0.0).astype(BF16)
    parts = jnp.dot(tril, jnp.concatenate(_split3(logf), axis=1), preferred_element_type=F32)
    _store_vt(fv_ref, proj(_EV_V, _EV_G), FOX_HEAD_DIM, s)
    c = (parts[:, :LANES] + parts[:, LANES:2 * LANES] + parts[:, 2 * LANES:]
         + carry_ref[SUBLANES - 1:SUBLANES, :])
    carry_ref[...] = c[tm - SUBLANES:, :]

    cq = _rms(cq_raw, qn_ref[...]).astype(BF16)
    q = jnp.dot(cq, wuq_ref[...], preferred_element_type=F32) * mla_scale
    for hd in range(HEADS):
        sl = slice(hd * LANES, (hd + 1) * LANES)
        mq_ref[rows, sl] = _rope(q[:, sl], ra, rb, rc).astype(BF16)

    ckv = _rms(ckv_raw, kvn_ref[...]).astype(BF16)
    kr = _rope(kr_raw, ra, rb, rc)
    kn = jnp.dot(ckv, wuk_ref[...], preferred_element_type=F32)
    for hd in range(HEADS):
        sl = slice(hd * LANES, (hd + 1) * LANES)
        mk_ref[rows, sl] = (kn[:, sl] + kr).astype(BF16)
    _store_vt(mv_ref, jnp.dot(ckv, wuv_ref[...], preferred_element_type=F32), MLA_V_DIM, s)

    c_aug = jnp.dot(jnp.concatenate(_split3(c), axis=1), ec_ref[...], preferred_element_type=F32)
    spread_heads(fk_ref, proj(_EV_K, _EV_V), c_aug)


def _pad_heads(w, dh):
    k = w.shape[0]
    w = w.reshape(k, HEADS, dh)
    return jnp.pad(w, ((0, 0), (0, 0), (0, LANES - dh))).reshape(k, HEAD_COLS)


def _even_consts():
    cst = np.zeros((SUBLANES, HEAD_COLS), np.float32)
    ec = np.zeros((3 * LANES, HEAD_COLS), np.float32)
    for hd in range(HEADS):
        cst[0, hd * LANES + 64: hd * LANES + 67] = 1.0
        for p in range(3):
            ec[p * LANES + hd, hd * LANES + 64 + p] = -1.0
    return jnp.asarray(cst), jnp.asarray(ec, dtype=BF16)


def _rope_tables(seq):
    half = MLA_ROPE_DIM // 2
    inv = ROPE_THETA ** (-jnp.arange(0, MLA_ROPE_DIM, 2, dtype=F32) / MLA_ROPE_DIM)
    ang = jnp.arange(seq, dtype=F32)[:, None] * inv[None, :]
    cos, sin = jnp.cos(ang), jnp.sin(ang)
    zeros = jnp.zeros((seq, half), F32)
    ra = jnp.concatenate([jnp.ones((seq, 64), F32), cos, cos, jnp.zeros((seq, 32), F32)], axis=1)
    rb = jnp.concatenate([jnp.zeros((seq, 64), F32), zeros, sin, jnp.zeros((seq, 32), F32)], axis=1)
    rc = jnp.concatenate([jnp.zeros((seq, 64), F32), -sin, zeros, jnp.zeros((seq, 32), F32)], axis=1)
    return ra, rb, rc


def _even_proj(x, g, w_in, b_forget, q_norm, w_uq, kv_norm, w_ukv, rope, bsz, seq):
    t = x.shape[0]
    tk = ATTN_TILE
    tm = tk * PROJ_TILES_PER_STEP
    cuts = np.cumsum([FOX_WIDTH, FOX_WIDTH, FOX_WIDTH, HEADS, MLA_Q_RANK, MLA_KV_RANK]).tolist()
    wfq, wfk, wfv, wg, wcq, wckv, wkr = jnp.split(w_in, cuts, axis=1)
    w1 = jnp.concatenate([
        wfq, wfk, wfv,
        jnp.pad(wg, ((0, 0), (0, LANES - HEADS))), wcq, wckv,
        jnp.pad(wkr, ((0, 0), (64, LANES - 64 - MLA_ROPE_DIM)))], axis=1).astype(BF16)
    assert w1.shape[1] == _EV_WIDTH
    bf = jnp.pad(b_forget, (0, LANES - HEADS)).reshape(1, LANES)
    wuq = _pad_heads(w_uq, MLA_NOPE_DIM + MLA_ROPE_DIM).astype(BF16)
    wukv = w_ukv.reshape(MLA_KV_RANK, HEADS, MLA_NOPE_DIM + MLA_V_DIM)
    wuk = _pad_heads(wukv[:, :, :MLA_NOPE_DIM].reshape(MLA_KV_RANK, -1), MLA_NOPE_DIM).astype(BF16)
    wuv = wukv[:, :, MLA_NOPE_DIM:].reshape(MLA_KV_RANK, -1).astype(BF16)
    cst, ec = _even_consts()
    ra, rb, rc = rope
    steps_per_seq = seq // tm
    rope_spec = pl.BlockSpec((tm, LANES), lambda i: (i % steps_per_seq, 0))
    row_spec = pl.BlockSpec((tm, HEAD_COLS), lambda i: (i, 0))
    out_sds = jax.ShapeDtypeStruct((t, HEAD_COLS), BF16)
    vt_spec = _vt_spec(FOX_HEAD_DIM, PROJ_TILES_PER_STEP, tk, steps_per_seq)
    vt_sds = jax.ShapeDtypeStruct((bsz, HEADS, seq // tk, _vt_rows(FOX_HEAD_DIM), tk), BF16)
    return pl.pallas_call(
        functools.partial(_even_proj_kernel, steps_per_seq=steps_per_seq),
        grid=(t // tm,),
        in_specs=[pl.BlockSpec((tm, D_MODEL), lambda i: (i, 0)),
                  _const_spec((1, D_MODEL)), _const_spec(w1.shape), _const_spec((1, LANES)),
                  _const_spec((1, MLA_Q_RANK)), _const_spec(wuq.shape),
                  _const_spec((1, MLA_KV_RANK)), _const_spec(wuk.shape), _const_spec(wuv.shape),
                  _const_spec(ec.shape), rope_spec, rope_spec, rope_spec, _const_spec(cst.shape)],
        out_specs=[row_spec, row_spec, vt_spec, row_spec, row_spec, vt_spec],
        out_shape=[out_sds, out_sds, vt_sds, out_sds, out_sds, vt_sds],
        scratch_shapes=[pltpu.VMEM((SUBLANES, LANES), F32)],
        compiler_params=_params("arbitrary"),
        name="even_proj",
    )(x, g.reshape(1, -1), w1, bf, q_norm.reshape(1, -1), wuq, kv_norm.reshape(1, -1), wuk, wuv,
      ec, ra, rb, rc, cst)


def _alibi_slope(hd):
    return 2.0 ** (-8.0 * (hd + 1) / HEADS)


def _odd_proj_kernel(x_ref, g_ref, w_ref, q_ref, k_ref, v_ref, *, steps_per_seq):
    tm = v_ref.shape[4]
    n_tiles = x_ref.shape[0] // tm
    i = pl.program_id(0)
    scale = DIFF_HEAD_DIM ** -0.5 * LOG2E
    d = DIFF_HEAD_DIM
    lane = lax.broadcasted_iota(jnp.int32, (tm, LANES), 1)
    q_aug = jnp.where((lane >= d) & (lane < d + 3), 1.0, 0.0)
    hs = [_rms(x_ref[s * tm:(s + 1) * tm, :], g_ref[...]).astype(BF16) for s in range(n_tiles)]
    for s, h in enumerate(hs):
        rows = slice(s * tm, (s + 1) * tm)
        v = jnp.dot(h, w_ref[:, 2 * HEAD_COLS:], preferred_element_type=F32)
        k = jnp.dot(h, w_ref[:, HEAD_COLS:2 * HEAD_COLS], preferred_element_type=F32)
        _store_vt(v_ref, v, DIFF_V_DIM, s)
        q = jnp.dot(h, w_ref[:, :HEAD_COLS], preferred_element_type=F32)
        pos = (((i % steps_per_seq) * n_tiles + s) * tm
               + lax.broadcasted_iota(jnp.int32, (tm, 1), 0)).astype(F32)
        for hd in range(HEADS):
            sl = slice(hd * LANES, (hd + 1) * LANES)
            b_hi, b_mid, b_lo = (p.astype(F32)
                                 for p in _split3(pos * (_alibi_slope(hd) * LOG2E)))
            k_aug = jnp.where(lane == d, b_hi,
                              jnp.where(lane == d + 1, b_mid, jnp.where(lane == d + 2, b_lo, 0.0)))
            qb, kb = q[:, sl] * scale, k[:, sl]
            for br, (qs, ks) in enumerate(((qb, kb),
                                           (pltpu.roll(qb, d, 1), pltpu.roll(kb, d, 1)))):
                so = slice((2 * hd + br) * LANES, (2 * hd + br + 1) * LANES)
                q_ref[rows, so] = jnp.where(lane < d, qs, q_aug).astype(BF16)
                k_ref[rows, so] = jnp.where(lane < d, ks, k_aug).astype(BF16)


def _odd_proj(x, g, w_in, bsz, seq):
    t = x.shape[0]
    tk = ATTN_TILE
    tm = tk * PROJ_TILES_PER_STEP
    w = w_in.astype(BF16)
    steps_per_seq = seq // tm
    return pl.pallas_call(
        functools.partial(_odd_proj_kernel, steps_per_seq=steps_per_seq),
        grid=(t // tm,),
        in_specs=[pl.BlockSpec((tm, D_MODEL), lambda i: (i, 0)),
                  _const_spec((1, D_MODEL)), _const_spec(w.shape)],
        out_specs=[pl.BlockSpec((tm, 2 * HEAD_COLS), lambda i: (i, 0)),
                   pl.BlockSpec((tm, 2 * HEAD_COLS), lambda i: (i, 0)),
                   _vt_spec(DIFF_V_DIM, PROJ_TILES_PER_STEP, tk, steps_per_seq)],
        out_shape=[jax.ShapeDtypeStruct((t, 2 * HEAD_COLS), BF16),
                   jax.ShapeDtypeStruct((t, 2 * HEAD_COLS), BF16),
                   jax.ShapeDtypeStruct((bsz, HEADS, seq // tk, _vt_rows(DIFF_V_DIM), tk), BF16)],
        compiler_params=_params("arbitrary"),
        name="odd_proj",
    )(x, g.reshape(1, -1), w)


_NT = (((1,), (1,)), ((), ()))


def _attend(s, vt, m, acc, key0=None):
    if key0 is None:
        m_new = jnp.maximum(m, jnp.max(s, axis=0, keepdims=True))
        p = jnp.exp2(s - m_new).astype(BF16)
    else:
        nr, nc = s.shape[0] // LANES, s.shape[1] // LANES
        key = lax.broadcasted_iota(jnp.int32, (LANES, LANES), 0)
        qry = lax.broadcasted_iota(jnp.int32, (LANES, LANES), 1)
        sq = [[s[a * LANES:(a + 1) * LANES, b * LANES:(b + 1) * LANES] for b in range(nc)]
              for a in range(nr)]
        k0 = key0 // LANES
        for a in range(nr):
            if k0 + a < nc:
                sq[a][k0 + a] = jnp.where(key <= qry, sq[a][k0 + a], NEG_INF)
        col_max = []
        for b in range(nc):
            vis = [sq[a][b] for a in range(nr) if k0 + a <= b]
            col_max.append(functools.reduce(jnp.maximum, [jnp.max(x, axis=0, keepdims=True)
                                                          for x in vis])
                           if vis else jnp.full((1, LANES), NEG_INF, F32))
        m_new = jnp.maximum(m, jnp.concatenate(col_max, axis=1))
        p = jnp.concatenate(
            [jnp.concatenate(
                [jnp.exp2(sq[a][b] - m_new[:, b * LANES:(b + 1) * LANES]).astype(BF16)
                 if k0 + a <= b else jnp.zeros((LANES, LANES), BF16) for b in range(nc)], axis=1)
             for a in range(nr)], axis=0)
    acc = jnp.exp2(m - m_new) * acc + jnp.dot(vt, p, preferred_element_type=F32)
    return m_new, acc


def _stream_attention(qk, vts, s_ref, rows, n):
    lag, tk, t = s_ref.shape
    r = t // tk
    qi = pl.program_id(2)
    assert n >= lag
    for l in range(lag):
        s_ref[l] = qk(l, 0, 0)

    def sweep(j, u, carry, diagonal):
        carry = list(carry)
        nxt = (j, u + 1) if u + 1 < r else (j + 1, 0)
        last = diagonal and u + 1 == r
        pending = [s_ref[l] for l in range(lag)]
        for k in range(n):
            ahead = k + lag
            if ahead < n:
                pending.append(qk(ahead, j, u))
            elif not last:
                pending.append(qk(ahead - n, *nxt))
            carry[2 * k], carry[2 * k + 1] = _attend(pending.pop(0), vts(k, j, u), carry[2 * k],
                                                     carry[2 * k + 1], u * tk if diagonal else None)
        if not last:
            for l in range(lag):
                s_ref[l] = pending[l]
        return tuple(carry)

    def body(j, carry):
        for u in range(r):
            carry = sweep(j, u, carry, False)
        return carry

    init = (jnp.full((1, t), NEG_INF, F32), jnp.zeros((rows, t), F32)) * n
    carry = lax.fori_loop(0, qi, body, init)
    for u in range(r):
        carry = sweep(qi, u, carry, True)
    return carry[1::2]


def _flash_kernel(q_ref, k_ref, vt_ref, o_ref, s_ref):
    n, rows, t = vt_ref.shape[1], vt_ref.shape[3], vt_ref.shape[4]
    tk = s_ref.shape[1]
    dv = rows - BF16_ROWS
    qs = [q_ref[0, :, hd * LANES:(hd + 1) * LANES] for hd in range(n)]

    def qk(hd, j, u):
        start = pl.multiple_of(j * t, t) + u * tk
        kc = k_ref[0, pl.ds(start, tk), hd * LANES:(hd + 1) * LANES]
        return lax.dot_general(kc, qs[hd], _NT, preferred_element_type=F32)

    def vts(hd, j, u):
        return vt_ref[0, hd, j, :, u * tk:(u + 1) * tk]

    accs = _stream_attention(qk, vts, s_ref, rows, n)
    o_t = jnp.concatenate([a[:dv] / a[dv:dv + 1] for a in accs], axis=0)
    o_ref[0] = o_t.T.astype(BF16)


def _flash_attention(q, k, vt, tq):
    b, s, _ = q.shape
    _, _, chunks, rows, tk = vt.shape
    assert tq == tk
    n = FLASH_HEADS_PER_STEP
    dv = rows - BF16_ROWS
    return pl.pallas_call(
        _flash_kernel,
        grid=(b, HEADS // n, s // tq),
        in_specs=[pl.BlockSpec((1, tq, n * LANES), lambda bi, hp, i: (bi, i, hp)),
                  pl.BlockSpec((1, s, n * LANES), lambda bi, hp, i: (bi, 0, hp)),
                  pl.BlockSpec((1, n, chunks, rows, tk), lambda bi, hp, i: (bi, hp, 0, 0, 0))],
        out_specs=pl.BlockSpec((1, tq, n * dv), lambda bi, hp, i: (bi, i, hp)),
        out_shape=jax.ShapeDtypeStruct((b, s, HEADS * (rows - BF16_ROWS)), BF16),
        scratch_shapes=[pltpu.VMEM((FLASH_LAG, FLASH_KEYS, tq), F32)],
        compiler_params=_params("arbitrary", "arbitrary", "arbitrary"),
        name="flash_attention",
    )(q, k, vt)


def _diff_kernel(q_ref, k_ref, vt_ref, lq1_ref, lk1_ref, lq2_ref, lk2_ref, sub_ref, o_ref, s_ref,
                 *, lambda_init):
    n, rows, t = vt_ref.shape[1], vt_ref.shape[3], vt_ref.shape[4]
    tk = s_ref.shape[1]
    dv = rows - BF16_ROWS
    qs = [q_ref[0, :, st * LANES:(st + 1) * LANES] for st in range(2 * n)]

    def qk(st, j, u):
        start = pl.multiple_of(j * t, t) + u * tk
        kc = k_ref[0, pl.ds(start, tk), st * LANES:(st + 1) * LANES]
        return lax.dot_general(kc, qs[st], _NT, preferred_element_type=F32)

    def vts(st, j, u):
        return vt_ref[0, st // 2, j, :, u * tk:(u + 1) * tk]

    accs = _stream_attention(qk, vts, s_ref, rows, 2 * n)
    lam = (jnp.exp(jnp.sum(lq1_ref[...] * lk1_ref[...], axis=1, keepdims=True))
           - jnp.exp(jnp.sum(lq2_ref[...] * lk2_ref[...], axis=1, keepdims=True)) + lambda_init)
    for hd in range(n):
        a1, a2 = accs[2 * hd], accs[2 * hd + 1]
        o = (a1[:dv] / a1[dv:dv + 1] - lam * (a2[:dv] / a2[dv:dv + 1])).T
        o_ref[0, :, hd * dv:(hd + 1) * dv] = (
            _rms(o, sub_ref[...]) * (1.0 - lambda_init)).astype(BF16)


def _diff_attention(q, k, vt, lq1, lk1, lq2, lk2, subln, lambda_init, tq):
    b, s, _ = q.shape
    _, _, chunks, rows, tk = vt.shape
    assert tq == tk
    vec = lambda a: a.reshape(1, -1)
    n = DIFF_HEADS_PER_STEP
    return pl.pallas_call(
        functools.partial(_diff_kernel, lambda_init=lambda_init),
        grid=(b, HEADS // n, s // tq),
        in_specs=[pl.BlockSpec((1, tq, 2 * n * LANES), lambda bi, hd, i: (bi, i, hd)),
                  pl.BlockSpec((1, s, 2 * n * LANES), lambda bi, hd, i: (bi, 0, hd)),
                  pl.BlockSpec((1, n, chunks, rows, tk), lambda bi, hd, i: (bi, hd, 0, 0, 0)),
                  _const_spec((1, DIFF_HEAD_DIM)), _const_spec((1, DIFF_HEAD_DIM)),
                  _const_spec((1, DIFF_HEAD_DIM)), _const_spec((1, DIFF_HEAD_DIM)),
                  _const_spec((1, DIFF_V_DIM))],
        out_specs=pl.BlockSpec((1, tq, n * LANES), lambda bi, hd, i: (bi, i, hd)),
        out_shape=jax.ShapeDtypeStruct((b, s, HEADS * (rows - BF16_ROWS)), BF16),
        scratch_shapes=[pltpu.VMEM((DIFF_LAG, DIFF_KEYS, tq), F32)],
        compiler_params=_params("arbitrary", "arbitrary", "arbitrary"),
        name="diff_attention",
    )(q, k, vt, vec(lq1), vec(lk1), vec(lq2), vec(lk2), vec(subln))


_N_FFN_CHUNKS = D_FF // FFN_CHUNK
_GELU_C = math.sqrt(2.0 / math.pi)


def _mix_ffn_kernel(*refs, n_in, steps_per_seq):
    a_refs, wo_refs = refs[:n_in], refs[n_in:2 * n_in]
    (gmix_ref, x_ref, gpre_ref, wup_ref, cw_ref, wdn_ref, gpost_ref, o_ref,
     h_ref, hist_ref, gva_ref, gvb_ref, *mid_refs) = refs[2 * n_in:]
    tm = h_ref.shape[0]
    nc = _N_FFN_CHUNKS
    i = pl.program_id(0)

    @pl.when(i % steps_per_seq == 0)
    def _():
        hist_ref[...] = jnp.zeros_like(hist_ref)

    def out_proj(rows):
        mix = jnp.dot(a_refs[0][rows, :], wo_refs[0][...], preferred_element_type=F32)
        for a_ref, wo_ref in zip(a_refs[1:], wo_refs[1:]):
            mix = mix + jnp.dot(a_ref[rows, :], wo_ref[...], preferred_element_type=F32)
        return mix

    def pre_norms(rows, mix):
        x1 = x_ref[rows, :] + _rms(mix, gmix_ref[...])
        o_ref[rows, :] = x1
        h_ref[...] = _rms(x1, gpre_ref[...]).astype(BF16)

    def up(c):
        return jnp.dot(h_ref[...], wup_ref[c], preferred_element_type=F32)

    def down(mid_ref):
        f = jnp.dot(mid_ref[0], wdn_ref[0], preferred_element_type=F32)
        for c in range(1, nc):
            f = f + jnp.dot(mid_ref[c], wdn_ref[c], preferred_element_type=F32)
        return f

    def gated(c, gv_ref, mid_ref):
        gv_ref[0:SUBLANES, :FFN_CHUNK] = hist_ref[c]
        gate = gv_ref[SUBLANES:, :FFN_CHUNK]
        val = gv_ref[SUBLANES:, FFN_CHUNK:]
        hist_ref[c] = gv_ref[tm:, :FFN_CHUNK]
        cw = cw_ref[c]
        pre = (gv_ref[SUBLANES - 2:SUBLANES - 2 + tm, :FFN_CHUNK] * cw[0:1, :]
               + gv_ref[SUBLANES - 1:SUBLANES - 1 + tm, :FFN_CHUNK] * cw[1:2, :]
               + gate * cw[2:3, :] + cw[3:4, :])
        pre = pre.astype(BF16)
        act = 0.5 * pre * (1.0 + jnp.tanh(_GELU_C * (pre + 0.044715 * (pre * pre * pre))))
        mid_ref[c] = act * val.astype(BF16)

    def chunks_after_first(mid_ref):
        def body(k, _):
            gated(2 * k, gva_ref, mid_ref)
            gvb_ref[SUBLANES:, :] = up(2 * k + 1)
            gated(2 * k + 1, gvb_ref, mid_ref)
            gva_ref[SUBLANES:, :] = up(2 * k + 2)
            return 0

        for k in range(nc // 2):
            body(k, 0)
        gated(nc - 1, gva_ref, mid_ref)

    assert nc % 2 == 1
    n_tiles = len(mid_refs)
    rows = [slice(s * tm, (s + 1) * tm) for s in range(n_tiles)]
    pre_norms(rows[0], out_proj(rows[0]))
    gva_ref[SUBLANES:, :] = up(0)
    for s in range(n_tiles):
        chunks_after_first(mid_refs[s])
        if s + 1 < n_tiles:
            mix_next = out_proj(rows[s + 1])
        f = down(mid_refs[s])
        if s + 1 < n_tiles:
            pre_norms(rows[s + 1], mix_next)
            gva_ref[SUBLANES:, :] = up(0)
        o_ref[rows[s], :] = o_ref[rows[s], :] + _rms(f, gpost_ref[...])


def _mix_ffn(acts, w_outs, g_mix, x, g_pre, w_up, conv_w, conv_b, w_down, g_post, seq):
    t = x.shape[0]
    tm = FFN_ROWS
    nc = _N_FFN_CHUNKS
    n_in = len(acts)
    wg = w_up[:, :D_FF].reshape(D_MODEL, nc, FFN_CHUNK)
    wv = w_up[:, D_FF:].reshape(D_MODEL, nc, FFN_CHUNK)
    wup = jnp.concatenate([wg, wv], axis=2).transpose(1, 0, 2).astype(BF16)
    wdn = w_down.reshape(nc, FFN_CHUNK, D_MODEL).astype(BF16)
    cw = jnp.concatenate([conv_w, conv_b[None, :],
                          jnp.zeros((SUBLANES - CONV_WIDTH - 1, D_FF), F32)], axis=0)
    cw = cw.reshape(SUBLANES, nc, FFN_CHUNK).transpose(1, 0, 2)
    step_rows = tm * FFN_TILES_PER_STEP
    assert seq % step_rows == 0
    x_spec = pl.BlockSpec((step_rows, D_MODEL), lambda i: (i, 0))
    vec_spec = _const_spec((1, D_MODEL))
    return pl.pallas_call(
        functools.partial(_mix_ffn_kernel, n_in=n_in, steps_per_seq=seq // step_rows),
        grid=(t // step_rows,),
        in_specs=([pl.BlockSpec((step_rows, a.shape[1]), lambda i: (i, 0)) for a in acts]
                  + [_const_spec(w.shape) for w in w_outs]
                  + [vec_spec, x_spec, vec_spec, _const_spec(wup.shape), _const_spec(cw.shape),
                     _const_spec(wdn.shape), vec_spec]),
        out_specs=x_spec,
        out_shape=jax.ShapeDtypeStruct(x.shape, F32),
        scratch_shapes=([pltpu.VMEM((tm, D_MODEL), BF16),
                         pltpu.VMEM((nc, SUBLANES, FFN_CHUNK), F32),
                         pltpu.VMEM((tm + SUBLANES, 2 * FFN_CHUNK), F32),
                         pltpu.VMEM((tm + SUBLANES, 2 * FFN_CHUNK), F32)]
                        + [pltpu.VMEM((nc, tm, FFN_CHUNK), BF16)] * FFN_TILES_PER_STEP),
        compiler_params=_params("arbitrary"),
        name="mix_ffn",
    )(*acts, *w_outs, g_mix.reshape(1, -1), x, g_pre.reshape(1, -1), wup, cw, wdn,
      g_post.reshape(1, -1))


def kernel(x, norm_mix_pre, norm_mix_post, norm_ffn_pre, norm_ffn_post, even_w_in, even_b_forget, even_q_norm, even_w_uq, even_kv_norm, even_w_ukv, even_w_out, odd_w_in, odd_lambda_q1, odd_lambda_k1, odd_lambda_q2, odd_lambda_k2, odd_subln, odd_w_out, ffn_w_up, ffn_conv_w, ffn_conv_b, ffn_w_down):
    bsz, seq, d = x.shape
    assert d == D_MODEL and seq % ATTN_TILE == 0 and seq % FFN_ROWS == 0
    t = bsz * seq
    tq = ATTN_TILE
    rope = _rope_tables(seq)
    xf = x.reshape(t, d)
    for layer in range(DEPTH):
        i = layer // 2
        if layer % 2 == 0:
            fq, fk, fvt, mq, mk, mvt = _even_proj(
                xf, norm_mix_pre[layer], even_w_in[i], even_b_forget[i], even_q_norm[i],
                even_w_uq[i], even_kv_norm[i], even_w_ukv[i], rope, bsz, seq)
            shp = (bsz, seq, HEAD_COLS)
            fo = _flash_attention(fq.reshape(shp), fk.reshape(shp), fvt, tq)
            mo = _flash_attention(mq.reshape(shp), mk.reshape(shp), mvt, tq)
            w_out = even_w_out[i].astype(BF16)
            acts = [fo.reshape(t, -1), mo.reshape(t, -1)]
            w_outs = [w_out[:FOX_WIDTH], w_out[FOX_WIDTH:]]
        else:
            lambda_init = 0.8 - 0.6 * math.exp(-0.3 * layer)
            q, k, vt = _odd_proj(xf, norm_mix_pre[layer], odd_w_in[i], bsz, seq)
            o = _diff_attention(q.reshape(bsz, seq, -1), k.reshape(bsz, seq, -1), vt,
                                odd_lambda_q1[i], odd_lambda_k1[i],
                                odd_lambda_q2[i], odd_lambda_k2[i], odd_subln[i], lambda_init, tq)
            acts, w_outs = [o.reshape(t, -1)], [odd_w_out[i].astype(BF16)]
        xf = _mix_ffn(acts, w_outs, norm_mix_post[layer], xf, norm_ffn_pre[layer], ffn_w_up[layer],
                      ffn_conv_w[layer], ffn_conv_b[layer], ffn_w_down[layer],
                      norm_ffn_post[layer], seq)
    return xf.reshape(bsz, seq, d)
```

```python
import functools
import math

import numpy as np
import jax
import jax.numpy as jnp
from jax import lax
from jax.experimental import pallas as pl
from jax.experimental.pallas import tpu as pltpu

D_MODEL = 1024
DEPTH = 4
RMS_EPS = 1e-6
NEG_INF = -1e30
HEADS = 8
FOX_HEAD_DIM = 64
FOX_WIDTH = HEADS * FOX_HEAD_DIM
MLA_NOPE_DIM = 64
MLA_ROPE_DIM = 32
MLA_V_DIM = 64
MLA_Q_RANK = 384
MLA_KV_RANK = 256
ROPE_THETA = 10000.0
DIFF_HEAD_DIM = 64
DIFF_V_DIM = 2 * DIFF_HEAD_DIM
D_FF = 2816
CONV_WIDTH = 3

LANES = 128
SUBLANES = 8
BF16_ROWS = 16
LOG2E = 1.4426950408889634
HEAD_COLS = HEADS * LANES
VMEM_LIMIT_BYTES = 56 * 1024 * 1024

PROJ_TILES_PER_STEP = 2
FFN_ROWS = 512
FFN_TILES_PER_STEP = 2
FFN_CHUNK = 256
ATTN_TILE = 512
FLASH_HEADS_PER_STEP, FLASH_KEYS, FLASH_LAG = 2, 256, 3
DIFF_HEADS_PER_STEP, DIFF_KEYS, DIFF_LAG = 1, 512, 2
F32 = jnp.float32
BF16 = jnp.bfloat16


def _params(*sem):
    return pltpu.CompilerParams(dimension_semantics=sem, vmem_limit_bytes=VMEM_LIMIT_BYTES)


def _const_spec(shape):
    nd = len(shape)
    return pl.BlockSpec(shape, lambda *_: (0,) * nd, pipeline_mode=pl.Buffered(1))


def _rms(x, g):
    return x * lax.rsqrt(jnp.mean(x * x, axis=-1, keepdims=True) + RMS_EPS) * g


def _split3(x):
    hi = x.astype(BF16)
    r1 = x - hi.astype(F32)
    mid = r1.astype(BF16)
    lo = (r1 - mid.astype(F32)).astype(BF16)
    return hi, mid, lo


def _rope(blk, ra, rb, rc):
    return blk * ra + pltpu.roll(blk, 16, 1) * rb + pltpu.roll(blk, LANES - 16, 1) * rc


def _vt_rows(dv):
    return dv + BF16_ROWS


def _store_vt(vt_ref, v, dv, chunk):
    tk = v.shape[0]
    per = LANES // dv
    for blk in range(v.shape[1] // LANES):
        vt = v[:, blk * LANES:(blk + 1) * LANES].T.astype(BF16)
        for s in range(per):
            vt_ref[0, blk * per + s, chunk, 0:dv, :] = vt[s * dv:(s + 1) * dv, :]
    row = lax.broadcasted_iota(jnp.int32, (BF16_ROWS, tk), 0)
    ones_row = jnp.where(row == 0, 1.0, 0.0).astype(BF16)
    for hd in range(HEADS):
        vt_ref[0, hd, chunk, dv:dv + BF16_ROWS, :] = ones_row


def _vt_spec(dv, chunks_per_step, tk, steps_per_seq):
    return pl.BlockSpec((1, HEADS, chunks_per_step, _vt_rows(dv), tk),
                        lambda i: (i // steps_per_seq, 0, i % steps_per_seq, 0, 0))


_EV_Q, _EV_K, _EV_V = 0, FOX_WIDTH, 2 * FOX_WIDTH
_EV_G = _EV_V + FOX_WIDTH
_EV_CQ = _EV_G + LANES
_EV_CKV = _EV_CQ + MLA_Q_RANK
_EV_KR = _EV_CKV + MLA_KV_RANK
_EV_WIDTH = _EV_KR + LANES


def _even_proj_kernel(x_ref, g_ref, w_ref, bf_ref, qn_ref, wuq_ref, kvn_ref, wuk_ref, wuv_ref,
                      ec_ref, ra_ref, rb_ref, rc_ref, cst_ref,
                      fq_ref, fk_ref, fv_ref, mq_ref, mk_ref, mv_ref, carry_ref,
                      *, steps_per_seq):
    tm = fv_ref.shape[4]
    n_tiles = x_ref.shape[0] // tm
    i = pl.program_id(0)

    @pl.when(i % steps_per_seq == 0)
    def _():
        carry_ref[...] = jnp.zeros_like(carry_ref)

    hs = [_rms(x_ref[s * tm:(s + 1) * tm, :], g_ref[...]).astype(BF16) for s in range(n_tiles)]
    for s, h in enumerate(hs):
        _even_proj_tile(h, s, slice(s * tm, (s + 1) * tm), w_ref, bf_ref, qn_ref, wuq_ref, kvn_ref,
                        wuk_ref, wuv_ref, ec_ref, ra_ref, rb_ref, rc_ref, cst_ref,
                        fq_ref, fk_ref, fv_ref, mq_ref, mk_ref, mv_ref, carry_ref)


def _even_proj_tile(h, s, rows, w_ref, bf_ref, qn_ref, wuq_ref, kvn_ref, wuk_ref, wuv_ref,
                    ec_ref, ra_ref, rb_ref, rc_ref, cst_ref,
                    fq_ref, fk_ref, fv_ref, mq_ref, mk_ref, mv_ref, carry_ref):
    tm = h.shape[0]
    q_aug = cst_ref[0:1, :]

    def proj(lo, hi):
        return jnp.dot(h, w_ref[:, lo:hi], preferred_element_type=F32)

    lane = lax.broadcasted_iota(jnp.int32, (tm, LANES), 1)

    def spread_heads(o_ref, packed, aug):
        for pair in range(HEADS // 2):
            blk = packed[:, pair * LANES:(pair + 1) * LANES]
            for odd, src in enumerate((blk, pltpu.roll(blk, FOX_HEAD_DIM, 1))):
                sl = slice((2 * pair + odd) * LANES, (2 * pair + odd + 1) * LANES)
                o_ref[rows, sl] = jnp.where(lane < FOX_HEAD_DIM, src, aug[:, sl]).astype(BF16)

    ra, rb, rc = ra_ref[rows, :], rb_ref[rows, :], rc_ref[rows, :]
    fox_scale = FOX_HEAD_DIM ** -0.5 * LOG2E
    mla_scale = (MLA_NOPE_DIM + MLA_ROPE_DIM) ** -0.5 * LOG2E

    z = proj(_EV_G, _EV_CQ) + bf_ref[...]
    cq_raw = proj(_EV_CQ, _EV_CKV)
    ckv_raw = proj(_EV_CKV, _EV_KR)
    kr_raw = proj(_EV_KR, _EV_WIDTH)

    logf = (jnp.minimum(z, 0.0) - jnp.log1p(jnp.exp(-jnp.abs(z)))) * LOG2E
    spread_heads(fq_ref, proj(_EV_Q, _EV_K) * fox_scale, q_aug)

    row = lax.broadcasted_iota(jnp.int32, (tm, tm), 0)
    col = lax.broadcasted_iota(jnp.int32, (tm, tm), 1)
    tril = jnp.where(row >= col, 1.0, 0.0).astype(BF16)
    parts = jnp.dot(tril, jnp.concatenate(_split3(logf), axis=1), preferred_element_type=F32)
    _store_vt(fv_ref, proj(_EV_V, _EV_G), FOX_HEAD_DIM, s)
    c = (parts[:, :LANES] + parts[:, LANES:2 * LANES] + parts[:, 2 * LANES:]
         + carry_ref[SUBLANES - 1:SUBLANES, :])
    carry_ref[...] = c[tm - SUBLANES:, :]

    cq = _rms(cq_raw, qn_ref[...]).astype(BF16)
    q = jnp.dot(cq, wuq_ref[...], preferred_element_type=F32) * mla_scale
    for hd in range(HEADS):
        sl = slice(hd * LANES, (hd + 1) * LANES)
        mq_ref[rows, sl] = _rope(q[:, sl], ra, rb, rc).astype(BF16)

    ckv = _rms(ckv_raw, kvn_ref[...]).astype(BF16)
    kr = _rope(kr_raw, ra, rb, rc)
    kn = jnp.dot(ckv, wuk_ref[...], preferred_element_type=F32)
    for hd in range(HEADS):
        sl = slice(hd * LANES, (hd + 1) * LANES)
        mk_ref[rows, sl] = (kn[:, sl] + kr).astype(BF16)
    _store_vt(mv_ref, jnp.dot(ckv, wuv_ref[...], preferred_element_type=F32), MLA_V_DIM, s)

    c_aug = jnp.dot(jnp.concatenate(_split3(c), axis=1), ec_ref[...], preferred_element_type=F32)
    spread_heads(fk_ref, proj(_EV_K, _EV_V), c_aug)


def _pad_heads(w, dh):
    k = w.shape[0]
    w = w.reshape(k, HEADS, dh)
    return jnp.pad(w, ((0, 0), (0, 0), (0, LANES - dh))).reshape(k, HEAD_COLS)


def _even_consts():
    cst = np.zeros((SUBLANES, HEAD_COLS), np.float32)
    ec = np.zeros((3 * LANES, HEAD_COLS), np.float32)
    for hd in range(HEADS):
        cst[0, hd * LANES + 64: hd * LANES + 67] = 1.0
        for p in range(3):
            ec[p * LANES + hd, hd * LANES + 64 + p] = -1.0
    return jnp.asarray(cst), jnp.asarray(ec, dtype=BF16)


def _rope_tables(seq):
    half = MLA_ROPE_DIM // 2
    inv = ROPE_THETA ** (-jnp.arange(0, MLA_ROPE_DIM, 2, dtype=F32) / MLA_ROPE_DIM)
    ang = jnp.arange(seq, dtype=F32)[:, None] * inv[None, :]
    cos, sin = jnp.cos(ang), jnp.sin(ang)
    zeros = jnp.zeros((seq, half), F32)
    ra = jnp.concatenate([jnp.ones((seq, 64), F32), cos, cos, jnp.zeros((seq, 32), F32)], axis=1)
    rb = jnp.concatenate([jnp.zeros((seq, 64), F32), zeros, sin, jnp.zeros((seq, 32), F32)], axis=1)
    rc = jnp.concatenate([jnp.zeros((seq, 64), F32), -sin, zeros, jnp.zeros((seq, 32), F32)], axis=1)
    return ra, rb, rc


def _even_proj(x, g, w_in, b_forget, q_norm, w_uq, kv_norm, w_ukv, rope, bsz, seq):
    t = x.shape[0]
    tk = ATTN_TILE
    tm = tk * PROJ_TILES_PER_STEP
    cuts = np.cumsum([FOX_WIDTH, FOX_WIDTH, FOX_WIDTH, HEADS, MLA_Q_RANK, MLA_KV_RANK]).tolist()
    wfq, wfk, wfv, wg, wcq, wckv, wkr = jnp.split(w_in, cuts, axis=1)
    w1 = jnp.concatenate([
        wfq, wfk, wfv,
        jnp.pad(wg, ((0, 0), (0, LANES - HEADS))), wcq, wckv,
        jnp.pad(wkr, ((0, 0), (64, LANES - 64 - MLA_ROPE_DIM)))], axis=1).astype(BF16)
    assert w1.shape[1] == _EV_WIDTH
    bf = jnp.pad(b_forget, (0, LANES - HEADS)).reshape(1, LANES)
    wuq = _pad_heads(w_uq, MLA_NOPE_DIM + MLA_ROPE_DIM).astype(BF16)
    wukv = w_ukv.reshape(MLA_KV_RANK, HEADS, MLA_NOPE_DIM + MLA_V_DIM)
    wuk = _pad_heads(wukv[:, :, :MLA_NOPE_DIM].reshape(MLA_KV_RANK, -1), MLA_NOPE_DIM).astype(BF16)
    wuv = wukv[:, :, MLA_NOPE_DIM:].reshape(MLA_KV_RANK, -1).astype(BF16)
    cst, ec = _even_consts()
    ra, rb, rc = rope
    steps_per_seq = seq // tm
    rope_spec = pl.BlockSpec((tm, LANES), lambda i: (i % steps_per_seq, 0))
    row_spec = pl.BlockSpec((tm, HEAD_COLS), lambda i: (i, 0))
    out_sds = jax.ShapeDtypeStruct((t, HEAD_COLS), BF16)
    vt_spec = _vt_spec(FOX_HEAD_DIM, PROJ_TILES_PER_STEP, tk, steps_per_seq)
    vt_sds = jax.ShapeDtypeStruct((bsz, HEADS, seq // tk, _vt_rows(FOX_HEAD_DIM), tk), BF16)
    return pl.pallas_call(
        functools.partial(_even_proj_kernel, steps_per_seq=steps_per_seq),
        grid=(t // tm,),
        in_specs=[pl.BlockSpec((tm, D_MODEL), lambda i: (i, 0)),
                  _const_spec((1, D_MODEL)), _const_spec(w1.shape), _const_spec((1, LANES)),
                  _const_spec((1, MLA_Q_RANK)), _const_spec(wuq.shape),
                  _const_spec((1, MLA_KV_RANK)), _const_spec(wuk.shape), _const_spec(wuv.shape),
                  _const_spec(ec.shape), rope_spec, rope_spec, rope_spec, _const_spec(cst.shape)],
        out_specs=[row_spec, row_spec, vt_spec, row_spec, row_spec, vt_spec],
        out_shape=[out_sds, out_sds, vt_sds, out_sds, out_sds, vt_sds],
        scratch_shapes=[pltpu.VMEM((SUBLANES, LANES), F32)],
        compiler_params=_params("arbitrary"),
        name="even_proj",
    )(x, g.reshape(1, -1), w1, bf, q_norm.reshape(1, -1), wuq, kv_norm.reshape(1, -1), wuk, wuv,
      ec, ra, rb, rc, cst)


def _alibi_slope(hd):
    return 2.0 ** (-8.0 * (hd + 1) / HEADS)


def _odd_proj_kernel(x_ref, g_ref, w_ref, q_ref, k_ref, v_ref, *, steps_per_seq):
    tm = v_ref.shape[4]
    n_tiles = x_ref.shape[0] // tm
    i = pl.program_id(0)
    scale = DIFF_HEAD_DIM ** -0.5 * LOG2E
    d = DIFF_HEAD_DIM
    lane = lax.broadcasted_iota(jnp.int32, (tm, LANES), 1)
    q_aug = jnp.where((lane >= d) & (lane < d + 3), 1.0, 0.0)
    hs = [_rms(x_ref[s * tm:(s + 1) * tm, :], g_ref[...]).astype(BF16) for s in range(n_tiles)]
    for s, h in enumerate(hs):
        rows = slice(s * tm, (s + 1) * tm)
        v = jnp.dot(h, w_ref[:, 2 * HEAD_COLS:], preferred_element_type=F32)
        k = jnp.dot(h, w_ref[:, HEAD_COLS:2 * HEAD_COLS], preferred_element_type=F32)
        _store_vt(v_ref, v, DIFF_V_DIM, s)
        q = jnp.dot(h, w_ref[:, :HEAD_COLS], preferred_element_type=F32)
        pos = (((i % steps_per_seq) * n_tiles + s) * tm
               + lax.broadcasted_iota(jnp.int32, (tm, 1), 0)).astype(F32)
        for hd in range(HEADS):
            sl = slice(hd * LANES, (hd + 1) * LANES)
            b_hi, b_mid, b_lo = (p.astype(F32)
                                 for p in _split3(pos * (_alibi_slope(hd) * LOG2E)))
            k_aug = jnp.where(lane == d, b_hi,
                              jnp.where(lane == d + 1, b_mid, jnp.where(lane == d + 2, b_lo, 0.0)))
            qb, kb = q[:, sl] * scale, k[:, sl]
            for br, (qs, ks) in enumerate(((qb, kb),
                                           (pltpu.roll(qb, d, 1), pltpu.roll(kb, d, 1)))):
                so = slice((2 * hd + br) * LANES, (2 * hd + br + 1) * LANES)
                q_ref[rows, so] = jnp.where(lane < d, qs, q_aug).astype(BF16)
                k_ref[rows, so] = jnp.where(lane < d, ks, k_aug).astype(BF16)


def _odd_proj(x, g, w_in, bsz, seq):
    t = x.shape[0]
    tk = ATTN_TILE
    tm = tk * PROJ_TILES_PER_STEP
    w = w_in.astype(BF16)
    steps_per_seq = seq // tm
    return pl.pallas_call(
        functools.partial(_odd_proj_kernel, steps_per_seq=steps_per_seq),
        grid=(t // tm,),
        in_specs=[pl.BlockSpec((tm, D_MODEL), lambda i: (i, 0)),
                  _const_spec((1, D_MODEL)), _const_spec(w.shape)],
        out_specs=[pl.BlockSpec((tm, 2 * HEAD_COLS), lambda i: (i, 0)),
                   pl.BlockSpec((tm, 2 * HEAD_COLS), lambda i: (i, 0)),
                   _vt_spec(DIFF_V_DIM, PROJ_TILES_PER_STEP, tk, steps_per_seq)],
        out_shape=[jax.ShapeDtypeStruct((t, 2 * HEAD_COLS), BF16),
                   jax.ShapeDtypeStruct((t, 2 * HEAD_COLS), BF16),
                   jax.ShapeDtypeStruct((bsz, HEADS, seq // tk, _vt_rows(DIFF_V_DIM), tk), BF16)],
        compiler_params=_params("arbitrary"),
        name="odd_proj",
    )(x, g.reshape(1, -1), w)


_NT = (((1,), (1,)), ((), ()))


def _attend(s, vt, m, acc, key0=None):
    if key0 is None:
        m_new = jnp.maximum(m, jnp.max(s, axis=0, keepdims=True))
        p = jnp.exp2(s - m_new).astype(BF16)
    else:
        nr, nc = s.shape[0] // LANES, s.shape[1] // LANES
        key = lax.broadcasted_iota(jnp.int32, (LANES, LANES), 0)
        qry = lax.broadcasted_iota(jnp.int32, (LANES, LANES), 1)
        sq = [[s[a * LANES:(a + 1) * LANES, b * LANES:(b + 1) * LANES] for b in range(nc)]
              for a in range(nr)]
        k0 = key0 // LANES
        for a in range(nr):
            if k0 + a < nc:
                sq[a][k0 + a] = jnp.where(key <= qry, sq[a][k0 + a], NEG_INF)
        col_max = []
        for b in range(nc):
            vis = [sq[a][b] for a in range(nr) if k0 + a <= b]
            col_max.append(functools.reduce(jnp.maximum, [jnp.max(x, axis=0, keepdims=True)
                                                          for x in vis])
                           if vis else jnp.full((1, LANES), NEG_INF, F32))
        m_new = jnp.maximum(m, jnp.concatenate(col_max, axis=1))
        p = jnp.concatenate(
            [jnp.concatenate(
                [jnp.exp2(sq[a][b] - m_new[:, b * LANES:(b + 1) * LANES]).astype(BF16)
                 if k0 + a <= b else jnp.zeros((LANES, LANES), BF16) for b in range(nc)], axis=1)
             for a in range(nr)], axis=0)
    acc = jnp.exp2(m - m_new) * acc + jnp.dot(vt, p, preferred_element_type=F32)
    return m_new, acc


def _stream_attention(qk, vts, finish, q_tiles, t, tk, rows, n, lag):
    r = t // tk
    items = [(qi, j, u, k) for qi in range(q_tiles) for j in range(qi + 1) for u in range(r)
             for k in range(n)]
    pending, issued = [], 0
    state = None
    for i, (qi, j, u, k) in enumerate(items):
        while issued < min(i + lag + 1, len(items)):
            pending.append(qk(items[issued][3], *items[issued][:3]))
            issued += 1
        if j == 0 and u == 0 and k == 0:
            state = [(jnp.full((1, t), NEG_INF, F32), jnp.zeros((rows, t), F32))] * n
        state[k] = _attend(pending.pop(0), vts(k, j, u), *state[k], u * tk if j == qi else None)
        if j == qi and u == r - 1 and k == n - 1:
            finish(qi, [acc for _, acc in state])


def _flash_kernel(q_ref, k_ref, vt_ref, o_ref):
    n, chunks, rows, t = vt_ref.shape[1:]
    tk = FLASH_KEYS
    dv = rows - BF16_ROWS

    def qk(hd, qi, j, u):
        kc = k_ref[0, j * t + u * tk:j * t + (u + 1) * tk, hd * LANES:(hd + 1) * LANES]
        q = q_ref[0, qi * t:(qi + 1) * t, hd * LANES:(hd + 1) * LANES]
        return lax.dot_general(kc, q, _NT, preferred_element_type=F32)

    def vts(hd, j, u):
        return vt_ref[0, hd, j, :, u * tk:(u + 1) * tk]

    def finish(qi, accs):
        o_t = jnp.concatenate([a[:dv] / a[dv:dv + 1] for a in accs], axis=0)
        o_ref[0, qi * t:(qi + 1) * t, :] = o_t.T.astype(BF16)

    _stream_attention(qk, vts, finish, chunks, t, tk, rows, n, FLASH_LAG)


def _flash_attention(q, k, vt):
    b, s, _ = q.shape
    _, _, chunks, rows, t = vt.shape
    n = FLASH_HEADS_PER_STEP
    dv = rows - BF16_ROWS
    return pl.pallas_call(
        _flash_kernel,
        grid=(b, HEADS // n),
        in_specs=[pl.BlockSpec((1, s, n * LANES), lambda bi, hp: (bi, 0, hp)),
                  pl.BlockSpec((1, s, n * LANES), lambda bi, hp: (bi, 0, hp)),
                  pl.BlockSpec((1, n, chunks, rows, t), lambda bi, hp: (bi, hp, 0, 0, 0))],
        out_specs=pl.BlockSpec((1, s, n * dv), lambda bi, hp: (bi, 0, hp)),
        out_shape=jax.ShapeDtypeStruct((b, s, HEADS * dv), BF16),
        compiler_params=_params("arbitrary", "arbitrary"),
        name="flash_attention",
    )(q, k, vt)


def _diff_kernel(q_ref, k_ref, vt_ref, lq1_ref, lk1_ref, lq2_ref, lk2_ref, sub_ref, o_ref,
                 *, lambda_init):
    n, chunks, rows, t = vt_ref.shape[1:]
    tk = DIFF_KEYS
    dv = rows - BF16_ROWS

    def qk(st, qi, j, u):
        kc = k_ref[0, j * t + u * tk:j * t + (u + 1) * tk, st * LANES:(st + 1) * LANES]
        q = q_ref[0, qi * t:(qi + 1) * t, st * LANES:(st + 1) * LANES]
        return lax.dot_general(kc, q, _NT, preferred_element_type=F32)

    def vts(st, j, u):
        return vt_ref[0, st // 2, j, :, u * tk:(u + 1) * tk]

    lam = (jnp.exp(jnp.sum(lq1_ref[...] * lk1_ref[...], axis=1, keepdims=True))
           - jnp.exp(jnp.sum(lq2_ref[...] * lk2_ref[...], axis=1, keepdims=True)) + lambda_init)

    def finish(qi, accs):
        for hd in range(n):
            a1, a2 = accs[2 * hd], accs[2 * hd + 1]
            o = (a1[:dv] / a1[dv:dv + 1] - lam * (a2[:dv] / a2[dv:dv + 1])).T
            o_ref[0, qi * t:(qi + 1) * t, hd * dv:(hd + 1) * dv] = (
                _rms(o, sub_ref[...]) * (1.0 - lambda_init)).astype(BF16)

    _stream_attention(qk, vts, finish, chunks, t, tk, rows, 2 * n, DIFF_LAG)


def _diff_attention(q, k, vt, lq1, lk1, lq2, lk2, subln, lambda_init):
    b, s, _ = q.shape
    _, _, chunks, rows, t = vt.shape
    vec = lambda a: a.reshape(1, -1)
    n = DIFF_HEADS_PER_STEP
    return pl.pallas_call(
        functools.partial(_diff_kernel, lambda_init=lambda_init),
        grid=(b, HEADS // n),
        in_specs=[pl.BlockSpec((1, s, 2 * n * LANES), lambda bi, hd: (bi, 0, hd)),
                  pl.BlockSpec((1, s, 2 * n * LANES), lambda bi, hd: (bi, 0, hd)),
                  pl.BlockSpec((1, n, chunks, rows, t), lambda bi, hd: (bi, hd, 0, 0, 0)),
                  _const_spec((1, DIFF_HEAD_DIM)), _const_spec((1, DIFF_HEAD_DIM)),
                  _const_spec((1, DIFF_HEAD_DIM)), _const_spec((1, DIFF_HEAD_DIM)),
                  _const_spec((1, DIFF_V_DIM))],
        out_specs=pl.BlockSpec((1, s, n * LANES), lambda bi, hd: (bi, 0, hd)),
        out_shape=jax.ShapeDtypeStruct((b, s, HEADS * (rows - BF16_ROWS)), BF16),
        compiler_params=_params("arbitrary", "arbitrary"),
        name="diff_attention",
    )(q, k, vt, vec(lq1), vec(lk1), vec(lq2), vec(lk2), vec(subln))


_N_FFN_CHUNKS = D_FF // FFN_CHUNK
_GELU_C = math.sqrt(2.0 / math.pi)


def _mix_ffn_kernel(*refs, n_in, steps_per_seq):
    a_refs, wo_refs = refs[:n_in], refs[n_in:2 * n_in]
    (gmix_ref, x_ref, gpre_ref, wup_ref, cw_ref, wdn_ref, gpost_ref, o_ref,
     h_ref, hist_ref, gva_ref, gvb_ref, *mid_refs) = refs[2 * n_in:]
    tm = h_ref.shape[0]
    nc = _N_FFN_CHUNKS
    i = pl.program_id(0)

    @pl.when(i % steps_per_seq == 0)
    def _():
        hist_ref[...] = jnp.zeros_like(hist_ref)

    def out_proj(rows):
        mix = jnp.dot(a_refs[0][rows, :], wo_refs[0][...], preferred_element_type=F32)
        for a_ref, wo_ref in zip(a_refs[1:], wo_refs[1:]):
            mix = mix + jnp.dot(a_ref[rows, :], wo_ref[...], preferred_element_type=F32)
        return mix

    def pre_norms(rows, mix):
        x1 = x_ref[rows, :] + _rms(mix, gmix_ref[...])
        o_ref[rows, :] = x1
        h_ref[...] = _rms(x1, gpre_ref[...]).astype(BF16)

    def up(c):
        return jnp.dot(h_ref[...], wup_ref[c], preferred_element_type=F32)

    def down(mid_ref):
        f = jnp.dot(mid_ref[0], wdn_ref[0], preferred_element_type=F32)
        for c in range(1, nc):
            f = f + jnp.dot(mid_ref[c], wdn_ref[c], preferred_element_type=F32)
        return f

    def gated(c, gv_ref, mid_ref):
        gv_ref[0:SUBLANES, :FFN_CHUNK] = hist_ref[c]
        gate = gv_ref[SUBLANES:, :FFN_CHUNK]
        val = gv_ref[SUBLANES:, FFN_CHUNK:]
        hist_ref[c] = gv_ref[tm:, :FFN_CHUNK]
        cw = cw_ref[c]
        pre = (gv_ref[SUBLANES - 2:SUBLANES - 2 + tm, :FFN_CHUNK] * cw[0:1, :]
               + gv_ref[SUBLANES - 1:SUBLANES - 1 + tm, :FFN_CHUNK] * cw[1:2, :]
               + gate * cw[2:3, :] + cw[3:4, :])
        pre = pre.astype(BF16)
        act = 0.5 * pre * (1.0 + jnp.tanh(_GELU_C * (pre + 0.044715 * (pre * pre * pre))))
        mid_ref[c] = act * val.astype(BF16)

    def chunks_after_first(mid_ref):
        def body(k, _):
            gated(2 * k, gva_ref, mid_ref)
            gvb_ref[SUBLANES:, :] = up(2 * k + 1)
            gated(2 * k + 1, gvb_ref, mid_ref)
            gva_ref[SUBLANES:, :] = up(2 * k + 2)
            return 0

        for k in range(nc // 2):
            body(k, 0)
        gated(nc - 1, gva_ref, mid_ref)

    assert nc % 2 == 1
    n_tiles = len(mid_refs)
    rows = [slice(s * tm, (s + 1) * tm) for s in range(n_tiles)]
    pre_norms(rows[0], out_proj(rows[0]))
    gva_ref[SUBLANES:, :] = up(0)
    for s in range(n_tiles):
        chunks_after_first(mid_refs[s])
        if s + 1 < n_tiles:
            mix_next = out_proj(rows[s + 1])
        f = down(mid_refs[s])
        if s + 1 < n_tiles:
            pre_norms(rows[s + 1], mix_next)
            gva_ref[SUBLANES:, :] = up(0)
        o_ref[rows[s], :] = o_ref[rows[s], :] + _rms(f, gpost_ref[...])


def _mix_ffn(acts, w_outs, g_mix, x, g_pre, w_up, conv_w, conv_b, w_down, g_post, seq):
    t = x.shape[0]
    tm = FFN_ROWS
    nc = _N_FFN_CHUNKS
    n_in = len(acts)
    wg = w_up[:, :D_FF].reshape(D_MODEL, nc, FFN_CHUNK)
    wv = w_up[:, D_FF:].reshape(D_MODEL, nc, FFN_CHUNK)
    wup = jnp.concatenate([wg, wv], axis=2).transpose(1, 0, 2).astype(BF16)
    wdn = w_down.reshape(nc, FFN_CHUNK, D_MODEL).astype(BF16)
    cw = jnp.concatenate([conv_w, conv_b[None, :],
                          jnp.zeros((SUBLANES - CONV_WIDTH - 1, D_FF), F32)], axis=0)
    cw = cw.reshape(SUBLANES, nc, FFN_CHUNK).transpose(1, 0, 2)
    step_rows = tm * FFN_TILES_PER_STEP
    assert seq % step_rows == 0
    x_spec = pl.BlockSpec((step_rows, D_MODEL), lambda i: (i, 0))
    vec_spec = _const_spec((1, D_MODEL))
    return pl.pallas_call(
        functools.partial(_mix_ffn_kernel, n_in=n_in, steps_per_seq=seq // step_rows),
        grid=(t // step_rows,),
        in_specs=([pl.BlockSpec((step_rows, a.shape[1]), lambda i: (i, 0)) for a in acts]
                  + [_const_spec(w.shape) for w in w_outs]
                  + [vec_spec, x_spec, vec_spec, _const_spec(wup.shape), _const_spec(cw.shape),
                     _const_spec(wdn.shape), vec_spec]),
        out_specs=x_spec,
        out_shape=jax.ShapeDtypeStruct(x.shape, F32),
        scratch_shapes=([pltpu.VMEM((tm, D_MODEL), BF16),
                         pltpu.VMEM((nc, SUBLANES, FFN_CHUNK), F32),
                         pltpu.VMEM((tm + SUBLANES, 2 * FFN_CHUNK), F32),
                         pltpu.VMEM((tm + SUBLANES, 2 * FFN_CHUNK), F32)]
                        + [pltpu.VMEM((nc, tm, FFN_CHUNK), BF16)] * FFN_TILES_PER_STEP),
        compiler_params=_params("arbitrary"),
        name="mix_ffn",
    )(*acts, *w_outs, g_mix.reshape(1, -1), x, g_pre.reshape(1, -1), wup, cw, wdn,
      g_post.reshape(1, -1))


def kernel(x, norm_mix_pre, norm_mix_post, norm_ffn_pre, norm_ffn_post, even_w_in, even_b_forget, even_q_norm, even_w_uq, even_kv_norm, even_w_ukv, even_w_out, odd_w_in, odd_lambda_q1, odd_lambda_k1, odd_lambda_q2, odd_lambda_k2, odd_subln, odd_w_out, ffn_w_up, ffn_conv_w, ffn_conv_b, ffn_w_down):
    bsz, seq, d = x.shape
    assert d == D_MODEL and seq % ATTN_TILE == 0 and seq % FFN_ROWS == 0
    t = bsz * seq
    rope = _rope_tables(seq)
    xf = x.reshape(t, d)
    for layer in range(DEPTH):
        i = layer // 2
        if layer % 2 == 0:
            fq, fk, fvt, mq, mk, mvt = _even_proj(
                xf, norm_mix_pre[layer], even_w_in[i], even_b_forget[i], even_q_norm[i],
                even_w_uq[i], even_kv_norm[i], even_w_ukv[i], rope, bsz, seq)
            shp = (bsz, seq, HEAD_COLS)
            fo = _flash_attention(fq.reshape(shp), fk.reshape(shp), fvt)
            mo = _flash_attention(mq.reshape(shp), mk.reshape(shp), mvt)
            w_out = even_w_out[i].astype(BF16)
            acts = [fo.reshape(t, -1), mo.reshape(t, -1)]
            w_outs = [w_out[:FOX_WIDTH], w_out[FOX_WIDTH:]]
        else:
            lambda_init = 0.8 - 0.6 * math.exp(-0.3 * layer)
            q, k, vt = _odd_proj(xf, norm_mix_pre[layer], odd_w_in[i], bsz, seq)
            o = _diff_attention(q.reshape(bsz, seq, -1), k.reshape(bsz, seq, -1), vt,
                                odd_lambda_q1[i], odd_lambda_k1[i],
                                odd_lambda_q2[i], odd_lambda_k2[i], odd_subln[i], lambda_init)
            acts, w_outs = [o.reshape(t, -1)], [odd_w_out[i].astype(BF16)]
        xf = _mix_ffn(acts, w_outs, norm_mix_post[layer], xf, norm_ffn_pre[layer], ffn_w_up[layer],
                      ffn_conv_w[layer], ffn_conv_b[layer], ffn_w_down[layer],
                      norm_ffn_post[layer], seq)
    return xf.reshape(bsz, seq, d)
```

```python
import functools
import math

import numpy as np
import jax
import jax.numpy as jnp
from jax import lax
from jax.experimental import pallas as pl
from jax.experimental.pallas import tpu as pltpu

D_MODEL = 1024
DEPTH = 4
RMS_EPS = 1e-6
NEG_INF = -1e30
HEADS = 8
FOX_HEAD_DIM = 64
FOX_WIDTH = HEADS * FOX_HEAD_DIM
MLA_NOPE_DIM = 64
MLA_ROPE_DIM = 32
MLA_V_DIM = 64
MLA_Q_RANK = 384
MLA_KV_RANK = 256
ROPE_THETA = 10000.0
DIFF_HEAD_DIM = 64
DIFF_V_DIM = 2 * DIFF_HEAD_DIM
D_FF = 2816
CONV_WIDTH = 3

LANES = 128
SUBLANES = 8
BF16_ROWS = 16
LOG2E = 1.4426950408889634
HEAD_COLS = HEADS * LANES
VMEM_LIMIT_BYTES = 56 * 1024 * 1024

PROJ_TILES_PER_STEP = 2
FFN_ROWS = 512
FFN_TILES_PER_STEP = 2
FFN_CHUNK = 256
ATTN_TILE = 512
FLASH_HEADS_PER_STEP, FLASH_KEYS, FLASH_LAG = 2, 256, 3
DIFF_HEADS_PER_STEP, DIFF_KEYS, DIFF_LAG = 1, 256, 3
F32 = jnp.float32
BF16 = jnp.bfloat16


def _params(*sem):
    return pltpu.CompilerParams(dimension_semantics=sem, vmem_limit_bytes=VMEM_LIMIT_BYTES)


def _const_spec(shape):
    nd = len(shape)
    return pl.BlockSpec(shape, lambda *_: (0,) * nd, pipeline_mode=pl.Buffered(1))


def _rms(x, g):
    return x * lax.rsqrt(jnp.mean(x * x, axis=-1, keepdims=True) + RMS_EPS) * g


def _split3(x):
    hi = x.astype(BF16)
    r1 = x - hi.astype(F32)
    mid = r1.astype(BF16)
    lo = (r1 - mid.astype(F32)).astype(BF16)
    return hi, mid, lo


def _rope(blk, ra, rb, rc):
    return blk * ra + pltpu.roll(blk, 16, 1) * rb + pltpu.roll(blk, LANES - 16, 1) * rc


def _vt_rows(dv):
    return dv + BF16_ROWS


def _store_vt(vt_ref, v, dv, chunk):
    tk = v.shape[0]
    per = LANES // dv
    for blk in range(v.shape[1] // LANES):
        vt = v[:, blk * LANES:(blk + 1) * LANES].T.astype(BF16)
        for s in range(per):
            vt_ref[0, blk * per + s, chunk, 0:dv, :] = vt[s * dv:(s + 1) * dv, :]
    row = lax.broadcasted_iota(jnp.int32, (BF16_ROWS, tk), 0)
    ones_row = jnp.where(row == 0, 1.0, 0.0).astype(BF16)
    for hd in range(HEADS):
        vt_ref[0, hd, chunk, dv:dv + BF16_ROWS, :] = ones_row


def _vt_spec(dv, chunks_per_step, tk, steps_per_seq):
    return pl.BlockSpec((1, HEADS, chunks_per_step, _vt_rows(dv), tk),
                        lambda i: (i // steps_per_seq, 0, i % steps_per_seq, 0, 0))


_EV_Q, _EV_K, _EV_V = 0, FOX_WIDTH, 2 * FOX_WIDTH
_EV_G = _EV_V + FOX_WIDTH
_EV_CQ = _EV_G + LANES
_EV_CKV = _EV_CQ + MLA_Q_RANK
_EV_KR = _EV_CKV + MLA_KV_RANK
_EV_WIDTH = _EV_KR + LANES


def _even_proj_kernel(x_ref, g_ref, w_ref, bf_ref, qn_ref, wuq_ref, kvn_ref, wuk_ref, wuv_ref,
                      ec_ref, ra_ref, rb_ref, rc_ref, cst_ref,
                      fq_ref, fk_ref, fv_ref, mq_ref, mk_ref, mv_ref, carry_ref,
                      *, steps_per_seq):
    tm = fv_ref.shape[4]
    n_tiles = x_ref.shape[0] // tm
    i = pl.program_id(0)

    @pl.when(i % steps_per_seq == 0)
    def _():
        carry_ref[...] = jnp.zeros_like(carry_ref)

    hs = [_rms(x_ref[s * tm:(s + 1) * tm, :], g_ref[...]).astype(BF16) for s in range(n_tiles)]
    for s, h in enumerate(hs):
        _even_proj_tile(h, s, slice(s * tm, (s + 1) * tm), w_ref, bf_ref, qn_ref, wuq_ref, kvn_ref,
                        wuk_ref, wuv_ref, ec_ref, ra_ref, rb_ref, rc_ref, cst_ref,
                        fq_ref, fk_ref, fv_ref, mq_ref, mk_ref, mv_ref, carry_ref)


def _even_proj_tile(h, s, rows, w_ref, bf_ref, qn_ref, wuq_ref, kvn_ref, wuk_ref, wuv_ref,
                    ec_ref, ra_ref, rb_ref, rc_ref, cst_ref,
                    fq_ref, fk_ref, fv_ref, mq_ref, mk_ref, mv_ref, carry_ref):
    tm = h.shape[0]
    q_aug = cst_ref[0:1, :]

    def proj(lo, hi):
        return jnp.dot(h, w_ref[:, lo:hi], preferred_element_type=F32)

    lane = lax.broadcasted_iota(jnp.int32, (tm, LANES), 1)

    def spread_heads(o_ref, packed, aug):
        for pair in range(HEADS // 2):
            blk = packed[:, pair * LANES:(pair + 1) * LANES]
            for odd, src in enumerate((blk, pltpu.roll(blk, FOX_HEAD_DIM, 1))):
                sl = slice((2 * pair + odd) * LANES, (2 * pair + odd + 1) * LANES)
                o_ref[rows, sl] = jnp.where(lane < FOX_HEAD_DIM, src, aug[:, sl]).astype(BF16)

    ra, rb, rc = ra_ref[rows, :], rb_ref[rows, :], rc_ref[rows, :]
    fox_scale = FOX_HEAD_DIM ** -0.5 * LOG2E
    mla_scale = (MLA_NOPE_DIM + MLA_ROPE_DIM) ** -0.5 * LOG2E

    z = proj(_EV_G, _EV_CQ) + bf_ref[...]
    cq_raw = proj(_EV_CQ, _EV_CKV)
    ckv_raw = proj(_EV_CKV, _EV_KR)
    kr_raw = proj(_EV_KR, _EV_WIDTH)

    logf = (jnp.minimum(z, 0.0) - jnp.log1p(jnp.exp(-jnp.abs(z)))) * LOG2E
    spread_heads(fq_ref, proj(_EV_Q, _EV_K) * fox_scale, q_aug)

    row = lax.broadcasted_iota(jnp.int32, (tm, tm), 0)
    col = lax.broadcasted_iota(jnp.int32, (tm, tm), 1)
    tril = jnp.where(row >= col, 1.0, 0.0).astype(BF16)
    parts = jnp.dot(tril, jnp.concatenate(_split3(logf), axis=1), preferred_element_type=F32)
    _store_vt(fv_ref, proj(_EV_V, _EV_G), FOX_HEAD_DIM, s)
    c = (parts[:, :LANES] + parts[:, LANES:2 * LANES] + parts[:, 2 * LANES:]
         + carry_ref[SUBLANES - 1:SUBLANES, :])
    carry_ref[...] = c[tm - SUBLANES:, :]

    cq = _rms(cq_raw, qn_ref[...]).astype(BF16)
    q = jnp.dot(cq, wuq_ref[...], preferred_element_type=F32) * mla_scale
    for hd in range(HEADS):
        sl = slice(hd * LANES, (hd + 1) * LANES)
        mq_ref[rows, sl] = _rope(q[:, sl], ra, rb, rc).astype(BF16)

    ckv = _rms(ckv_raw, kvn_ref[...]).astype(BF16)
    kr = _rope(kr_raw, ra, rb, rc)
    kn = jnp.dot(ckv, wuk_ref[...], preferred_element_type=F32)
    for hd in range(HEADS):
        sl = slice(hd * LANES, (hd + 1) * LANES)
        mk_ref[rows, sl] = (kn[:, sl] + kr).astype(BF16)
    _store_vt(mv_ref, jnp.dot(ckv, wuv_ref[...], preferred_element_type=F32), MLA_V_DIM, s)

    c_aug = jnp.dot(jnp.concatenate(_split3(c), axis=1), ec_ref[...], preferred_element_type=F32)
    spread_heads(fk_ref, proj(_EV_K, _EV_V), c_aug)


def _pad_heads(w, dh):
    k = w.shape[0]
    w = w.reshape(k, HEADS, dh)
    return jnp.pad(w, ((0, 0), (0, 0), (0, LANES - dh))).reshape(k, HEAD_COLS)


def _even_consts():
    cst = np.zeros((SUBLANES, HEAD_COLS), np.float32)
    ec = np.zeros((3 * LANES, HEAD_COLS), np.float32)
    for hd in range(HEADS):
        cst[0, hd * LANES + 64: hd * LANES + 67] = 1.0
        for p in range(3):
            ec[p * LANES + hd, hd * LANES + 64 + p] = -1.0
    return jnp.asarray(cst), jnp.asarray(ec, dtype=BF16)


def _rope_tables(seq):
    half = MLA_ROPE_DIM // 2
    inv = ROPE_THETA ** (-jnp.arange(0, MLA_ROPE_DIM, 2, dtype=F32) / MLA_ROPE_DIM)
    ang = jnp.arange(seq, dtype=F32)[:, None] * inv[None, :]
    cos, sin = jnp.cos(ang), jnp.sin(ang)
    zeros = jnp.zeros((seq, half), F32)
    ra = jnp.concatenate([jnp.ones((seq, 64), F32), cos, cos, jnp.zeros((seq, 32), F32)], axis=1)
    rb = jnp.concatenate([jnp.zeros((seq, 64), F32), zeros, sin, jnp.zeros((seq, 32), F32)], axis=1)
    rc = jnp.concatenate([jnp.zeros((seq, 64), F32), -sin, zeros, jnp.zeros((seq, 32), F32)], axis=1)
    return ra, rb, rc


def _even_proj(x, g, w_in, b_forget, q_norm, w_uq, kv_norm, w_ukv, rope, bsz, seq):
    t = x.shape[0]
    tk = ATTN_TILE
    tm = tk * PROJ_TILES_PER_STEP
    cuts = np.cumsum([FOX_WIDTH, FOX_WIDTH, FOX_WIDTH, HEADS, MLA_Q_RANK, MLA_KV_RANK]).tolist()
    wfq, wfk, wfv, wg, wcq, wckv, wkr = jnp.split(w_in, cuts, axis=1)
    w1 = jnp.concatenate([
        wfq, wfk, wfv,
        jnp.pad(wg, ((0, 0), (0, LANES - HEADS))), wcq, wckv,
        jnp.pad(wkr, ((0, 0), (64, LANES - 64 - MLA_ROPE_DIM)))], axis=1).astype(BF16)
    assert w1.shape[1] == _EV_WIDTH
    bf = jnp.pad(b_forget, (0, LANES - HEADS)).reshape(1, LANES)
    wuq = _pad_heads(w_uq, MLA_NOPE_DIM + MLA_ROPE_DIM).astype(BF16)
    wukv = w_ukv.reshape(MLA_KV_RANK, HEADS, MLA_NOPE_DIM + MLA_V_DIM)
    wuk = _pad_heads(wukv[:, :, :MLA_NOPE_DIM].reshape(MLA_KV_RANK, -1), MLA_NOPE_DIM).astype(BF16)
    wuv = wukv[:, :, MLA_NOPE_DIM:].reshape(MLA_KV_RANK, -1).astype(BF16)
    cst, ec = _even_consts()
    ra, rb, rc = rope
    steps_per_seq = seq // tm
    rope_spec = pl.BlockSpec((tm, LANES), lambda i: (i % steps_per_seq, 0))
    row_spec = pl.BlockSpec((tm, HEAD_COLS), lambda i: (i, 0))
    out_sds = jax.ShapeDtypeStruct((t, HEAD_COLS), BF16)
    vt_spec = _vt_spec(FOX_HEAD_DIM, PROJ_TILES_PER_STEP, tk, steps_per_seq)
    vt_sds = jax.ShapeDtypeStruct((bsz, HEADS, seq // tk, _vt_rows(FOX_HEAD_DIM), tk), BF16)
    return pl.pallas_call(
        functools.partial(_even_proj_kernel, steps_per_seq=steps_per_seq),
        grid=(t // tm,),
        in_specs=[pl.BlockSpec((tm, D_MODEL), lambda i: (i, 0)),
                  _const_spec((1, D_MODEL)), _const_spec(w1.shape), _const_spec((1, LANES)),
                  _const_spec((1, MLA_Q_RANK)), _const_spec(wuq.shape),
                  _const_spec((1, MLA_KV_RANK)), _const_spec(wuk.shape), _const_spec(wuv.shape),
                  _const_spec(ec.shape), rope_spec, rope_spec, rope_spec, _const_spec(cst.shape)],
        out_specs=[row_spec, row_spec, vt_spec, row_spec, row_spec, vt_spec],
        out_shape=[out_sds, out_sds, vt_sds, out_sds, out_sds, vt_sds],
        scratch_shapes=[pltpu.VMEM((SUBLANES, LANES), F32)],
        compiler_params=_params("arbitrary"),
        name="even_proj",
    )(x, g.reshape(1, -1), w1, bf, q_norm.reshape(1, -1), wuq, kv_norm.reshape(1, -1), wuk, wuv,
      ec, ra, rb, rc, cst)


def _alibi_slope(hd):
    return 2.0 ** (-8.0 * (hd + 1) / HEADS)


def _odd_proj_kernel(x_ref, g_ref, w_ref, q_ref, k_ref, v_ref, *, steps_per_seq):
    tm = v_ref.shape[4]
    n_tiles = x_ref.shape[0] // tm
    i = pl.program_id(0)
    scale = DIFF_HEAD_DIM ** -0.5 * LOG2E
    d = DIFF_HEAD_DIM
    lane = lax.broadcasted_iota(jnp.int32, (tm, LANES), 1)
    q_aug = jnp.where((lane >= d) & (lane < d + 3), 1.0, 0.0)
    hs = [_rms(x_ref[s * tm:(s + 1) * tm, :], g_ref[...]).astype(BF16) for s in range(n_tiles)]
    for s, h in enumerate(hs):
        rows = slice(s * tm, (s + 1) * tm)
        v = jnp.dot(h, w_ref[:, 2 * HEAD_COLS:], preferred_element_type=F32)
        k = jnp.dot(h, w_ref[:, HEAD_COLS:2 * HEAD_COLS], preferred_element_type=F32)
        _store_vt(v_ref, v, DIFF_V_DIM, s)
        q = jnp.dot(h, w_ref[:, :HEAD_COLS], preferred_element_type=F32)
        pos = (((i % steps_per_seq) * n_tiles + s) * tm
               + lax.broadcasted_iota(jnp.int32, (tm, 1), 0)).astype(F32)
        for hd in range(HEADS):
            sl = slice(hd * LANES, (hd + 1) * LANES)
            b_hi, b_mid, b_lo = (p.astype(F32)
                                 for p in _split3(pos * (_alibi_slope(hd) * LOG2E)))
            k_aug = jnp.where(lane == d, b_hi,
                              jnp.where(lane == d + 1, b_mid, jnp.where(lane == d + 2, b_lo, 0.0)))
            qb, kb = q[:, sl] * scale, k[:, sl]
            for br, (qs, ks) in enumerate(((qb, kb),
                                           (pltpu.roll(qb, d, 1), pltpu.roll(kb, d, 1)))):
                so = slice((2 * hd + br) * LANES, (2 * hd + br + 1) * LANES)
                q_ref[rows, so] = jnp.where(lane < d, qs, q_aug).astype(BF16)
                k_ref[rows, so] = jnp.where(lane < d, ks, k_aug).astype(BF16)


def _odd_proj(x, g, w_in, bsz, seq):
    t = x.shape[0]
    tk = ATTN_TILE
    tm = tk * PROJ_TILES_PER_STEP
    w = w_in.astype(BF16)
    steps_per_seq = seq // tm
    return pl.pallas_call(
        functools.partial(_odd_proj_kernel, steps_per_seq=steps_per_seq),
        grid=(t // tm,),
        in_specs=[pl.BlockSpec((tm, D_MODEL), lambda i: (i, 0)),
                  _const_spec((1, D_MODEL)), _const_spec(w.shape)],
        out_specs=[pl.BlockSpec((tm, 2 * HEAD_COLS), lambda i: (i, 0)),
                   pl.BlockSpec((tm, 2 * HEAD_COLS), lambda i: (i, 0)),
                   _vt_spec(DIFF_V_DIM, PROJ_TILES_PER_STEP, tk, steps_per_seq)],
        out_shape=[jax.ShapeDtypeStruct((t, 2 * HEAD_COLS), BF16),
                   jax.ShapeDtypeStruct((t, 2 * HEAD_COLS), BF16),
                   jax.ShapeDtypeStruct((bsz, HEADS, seq // tk, _vt_rows(DIFF_V_DIM), tk), BF16)],
        compiler_params=_params("arbitrary"),
        name="odd_proj",
    )(x, g.reshape(1, -1), w)


_NT = (((1,), (1,)), ((), ()))


def _attend(s, vt, m, acc, key0=None):
    if key0 is None:
        m_new = jnp.maximum(m, jnp.max(s, axis=0, keepdims=True))
        p = jnp.exp2(s - m_new).astype(BF16)
    else:
        nr, nc = s.shape[0] // LANES, s.shape[1] // LANES
        key = lax.broadcasted_iota(jnp.int32, (LANES, LANES), 0)
        qry = lax.broadcasted_iota(jnp.int32, (LANES, LANES), 1)
        sq = [[s[a * LANES:(a + 1) * LANES, b * LANES:(b + 1) * LANES] for b in range(nc)]
              for a in range(nr)]
        k0 = key0 // LANES
        for a in range(nr):
            if k0 + a < nc:
                sq[a][k0 + a] = jnp.where(key <= qry, sq[a][k0 + a], NEG_INF)
        col_max = []
        for b in range(nc):
            vis = [sq[a][b] for a in range(nr) if k0 + a <= b]
            col_max.append(functools.reduce(jnp.maximum, [jnp.max(x, axis=0, keepdims=True)
                                                          for x in vis])
                           if vis else jnp.full((1, LANES), NEG_INF, F32))
        m_new = jnp.maximum(m, jnp.concatenate(col_max, axis=1))
        p = jnp.concatenate(
            [jnp.concatenate(
                [jnp.exp2(sq[a][b] - m_new[:, b * LANES:(b + 1) * LANES]).astype(BF16)
                 if k0 + a <= b else jnp.zeros((LANES, LANES), BF16) for b in range(nc)], axis=1)
             for a in range(nr)], axis=0)
    acc = jnp.exp2(m - m_new) * acc + jnp.dot(vt, p, preferred_element_type=F32)
    return m_new, acc


def _stream_attention(qk, vts, finish, q_tiles, t, tk, rows, n, lag):
    r = t // tk
    items = [(qi, j, u, k) for qi in range(q_tiles) for j in range(qi + 1) for u in range(r)
             for k in range(n)]
    pending, issued = [], 0
    state = None
    for i, (qi, j, u, k) in enumerate(items):
        while issued < min(i + lag + 1, len(items)):
            pending.append(qk(items[issued][3], *items[issued][:3]))
            issued += 1
        if j == 0 and u == 0 and k == 0:
            state = [(jnp.full((1, t), NEG_INF, F32), jnp.zeros((rows, t), F32))] * n
        state[k] = _attend(pending.pop(0), vts(k, j, u), *state[k], u * tk if j == qi else None)
        if j == qi and u == r - 1 and k == n - 1:
            finish(qi, [acc for _, acc in state])


def _flash_kernel(q_ref, k_ref, vt_ref, o_ref):
    n, chunks, rows, t = vt_ref.shape[1:]
    tk = FLASH_KEYS
    dv = rows - BF16_ROWS

    def qk(hd, qi, j, u):
        kc = k_ref[0, j * t + u * tk:j * t + (u + 1) * tk, hd * LANES:(hd + 1) * LANES]
        q = q_ref[0, qi * t:(qi + 1) * t, hd * LANES:(hd + 1) * LANES]
        return lax.dot_general(kc, q, _NT, preferred_element_type=F32)

    def vts(hd, j, u):
        return vt_ref[0, hd, j, :, u * tk:(u + 1) * tk]

    def finish(qi, accs):
        o_t = jnp.concatenate([a[:dv] / a[dv:dv + 1] for a in accs], axis=0)
        o_ref[0, qi * t:(qi + 1) * t, :] = o_t.T.astype(BF16)

    _stream_attention(qk, vts, finish, chunks, t, tk, rows, n, FLASH_LAG)


def _flash_attention(q, k, vt):
    b, s, _ = q.shape
    _, _, chunks, rows, t = vt.shape
    n = FLASH_HEADS_PER_STEP
    dv = rows - BF16_ROWS
    return pl.pallas_call(
        _flash_kernel,
        grid=(b, HEADS // n),
        in_specs=[pl.BlockSpec((1, s, n * LANES), lambda bi, hp: (bi, 0, hp)),
                  pl.BlockSpec((1, s, n * LANES), lambda bi, hp: (bi, 0, hp)),
                  pl.BlockSpec((1, n, chunks, rows, t), lambda bi, hp: (bi, hp, 0, 0, 0))],
        out_specs=pl.BlockSpec((1, s, n * dv), lambda bi, hp: (bi, 0, hp)),
        out_shape=jax.ShapeDtypeStruct((b, s, HEADS * dv), BF16),
        compiler_params=_params("arbitrary", "arbitrary"),
        name="flash_attention",
    )(q, k, vt)


def _diff_kernel(q_ref, k_ref, vt_ref, lq1_ref, lk1_ref, lq2_ref, lk2_ref, sub_ref, o_ref,
                 *, lambda_init):
    n, chunks, rows, t = vt_ref.shape[1:]
    tk = DIFF_KEYS
    dv = rows - BF16_ROWS

    def qk(st, qi, j, u):
        kc = k_ref[0, j * t + u * tk:j * t + (u + 1) * tk, st * LANES:(st + 1) * LANES]
        q = q_ref[0, qi * t:(qi + 1) * t, st * LANES:(st + 1) * LANES]
        return lax.dot_general(kc, q, _NT, preferred_element_type=F32)

    def vts(st, j, u):
        return vt_ref[0, st // 2, j, :, u * tk:(u + 1) * tk]

    lam = (jnp.exp(jnp.sum(lq1_ref[...] * lk1_ref[...], axis=1, keepdims=True))
           - jnp.exp(jnp.sum(lq2_ref[...] * lk2_ref[...], axis=1, keepdims=True)) + lambda_init)

    def finish(qi, accs):
        for hd in range(n):
            a1, a2 = accs[2 * hd], accs[2 * hd + 1]
            o = (a1[:dv] / a1[dv:dv + 1] - lam * (a2[:dv] / a2[dv:dv + 1])).T
            o_ref[0, qi * t:(qi + 1) * t, hd * dv:(hd + 1) * dv] = (
                _rms(o, sub_ref[...]) * (1.0 - lambda_init)).astype(BF16)

    _stream_attention(qk, vts, finish, chunks, t, tk, rows, 2 * n, DIFF_LAG)


def _diff_attention(q, k, vt, lq1, lk1, lq2, lk2, subln, lambda_init):
    b, s, _ = q.shape
    _, _, chunks, rows, t = vt.shape
    vec = lambda a: a.reshape(1, -1)
    n = DIFF_HEADS_PER_STEP
    return pl.pallas_call(
        functools.partial(_diff_kernel, lambda_init=lambda_init),
        grid=(b, HEADS // n),
        in_specs=[pl.BlockSpec((1, s, 2 * n * LANES), lambda bi, hd: (bi, 0, hd)),
                  pl.BlockSpec((1, s, 2 * n * LANES), lambda bi, hd: (bi, 0, hd)),
                  pl.BlockSpec((1, n, chunks, rows, t), lambda bi, hd: (bi, hd, 0, 0, 0)),
                  _const_spec((1, DIFF_HEAD_DIM)), _const_spec((1, DIFF_HEAD_DIM)),
                  _const_spec((1, DIFF_HEAD_DIM)), _const_spec((1, DIFF_HEAD_DIM)),
                  _const_spec((1, DIFF_V_DIM))],
        out_specs=pl.BlockSpec((1, s, n * LANES), lambda bi, hd: (bi, 0, hd)),
        out_shape=jax.ShapeDtypeStruct((b, s, HEADS * (rows - BF16_ROWS)), BF16),
        compiler_params=_params("arbitrary", "arbitrary"),
        name="diff_attention",
    )(q, k, vt, vec(lq1), vec(lk1), vec(lq2), vec(lk2), vec(subln))


_N_FFN_CHUNKS = D_FF // FFN_CHUNK
_GELU_C = math.sqrt(2.0 / math.pi)


def _mix_ffn_kernel(*refs, n_in, steps_per_seq):
    a_refs, wo_refs = refs[:n_in], refs[n_in:2 * n_in]
    (gmix_ref, x_ref, gpre_ref, wup_ref, cw_ref, wdn_ref, gpost_ref, o_ref,
     h_ref, hist_ref, gva_ref, gvb_ref, *mid_refs) = refs[2 * n_in:]
    tm = h_ref.shape[0]
    nc = _N_FFN_CHUNKS
    i = pl.program_id(0)

    @pl.when(i % steps_per_seq == 0)
    def _():
        hist_ref[...] = jnp.zeros_like(hist_ref)

    def out_proj(rows):
        mix = jnp.dot(a_refs[0][rows, :], wo_refs[0][...], preferred_element_type=F32)
        for a_ref, wo_ref in zip(a_refs[1:], wo_refs[1:]):
            mix = mix + jnp.dot(a_ref[rows, :], wo_ref[...], preferred_element_type=F32)
        return mix

    def pre_norms(rows, mix):
        x1 = x_ref[rows, :] + _rms(mix, gmix_ref[...])
        o_ref[rows, :] = x1
        h_ref[...] = _rms(x1, gpre_ref[...]).astype(BF16)

    def up(c):
        return jnp.dot(h_ref[...], wup_ref[c], preferred_element_type=F32)

    def down(mid_ref):
        f = jnp.dot(mid_ref[0], wdn_ref[0], preferred_element_type=F32)
        for c in range(1, nc):
            f = f + jnp.dot(mid_ref[c], wdn_ref[c], preferred_element_type=F32)
        return f

    def gated(c, gv_ref, mid_ref):
        gv_ref[0:SUBLANES, :FFN_CHUNK] = hist_ref[c]
        gate = gv_ref[SUBLANES:, :FFN_CHUNK]
        val = gv_ref[SUBLANES:, FFN_CHUNK:]
        hist_ref[c] = gv_ref[tm:, :FFN_CHUNK]
        cw = cw_ref[c]
        pre = (gv_ref[SUBLANES - 2:SUBLANES - 2 + tm, :FFN_CHUNK] * cw[0:1, :]
               + gv_ref[SUBLANES - 1:SUBLANES - 1 + tm, :FFN_CHUNK] * cw[1:2, :]
               + gate * cw[2:3, :] + cw[3:4, :])
        pre = pre.astype(BF16)
        act = 0.5 * pre * (1.0 + jnp.tanh(_GELU_C * (pre + 0.044715 * (pre * pre * pre))))
        mid_ref[c] = act * val.astype(BF16)

    def chunks_after_first(mid_ref):
        def body(k, _):
            gated(2 * k, gva_ref, mid_ref)
            gvb_ref[SUBLANES:, :] = up(2 * k + 1)
            gated(2 * k + 1, gvb_ref, mid_ref)
            gva_ref[SUBLANES:, :] = up(2 * k + 2)
            return 0

        for k in range(nc // 2):
            body(k, 0)
        gated(nc - 1, gva_ref, mid_ref)

    assert nc % 2 == 1
    n_tiles = len(mid_refs)
    rows = [slice(s * tm, (s + 1) * tm) for s in range(n_tiles)]
    pre_norms(rows[0], out_proj(rows[0]))
    gva_ref[SUBLANES:, :] = up(0)
    for s in range(n_tiles):
        chunks_after_first(mid_refs[s])
        if s + 1 < n_tiles:
            mix_next = out_proj(rows[s + 1])
        f = down(mid_refs[s])
        if s + 1 < n_tiles:
            pre_norms(rows[s + 1], mix_next)
            gva_ref[SUBLANES:, :] = up(0)
        o_ref[rows[s], :] = o_ref[rows[s], :] + _rms(f, gpost_ref[...])


def _mix_ffn(acts, w_outs, g_mix, x, g_pre, w_up, conv_w, conv_b, w_down, g_post, seq):
    t = x.shape[0]
    tm = FFN_ROWS
    nc = _N_FFN_CHUNKS
    n_in = len(acts)
    wg = w_up[:, :D_FF].reshape(D_MODEL, nc, FFN_CHUNK)
    wv = w_up[:, D_FF:].reshape(D_MODEL, nc, FFN_CHUNK)
    wup = jnp.concatenate([wg, wv], axis=2).transpose(1, 0, 2).astype(BF16)
    wdn = w_down.reshape(nc, FFN_CHUNK, D_MODEL).astype(BF16)
    cw = jnp.concatenate([conv_w, conv_b[None, :],
                          jnp.zeros((SUBLANES - CONV_WIDTH - 1, D_FF), F32)], axis=0)
    cw = cw.reshape(SUBLANES, nc, FFN_CHUNK).transpose(1, 0, 2)
    step_rows = tm * FFN_TILES_PER_STEP
    assert seq % step_rows == 0
    x_spec = pl.BlockSpec((step_rows, D_MODEL), lambda i: (i, 0))
    vec_spec = _const_spec((1, D_MODEL))
    return pl.pallas_call(
        functools.partial(_mix_ffn_kernel, n_in=n_in, steps_per_seq=seq // step_rows),
        grid=(t // step_rows,),
        in_specs=([pl.BlockSpec((step_rows, a.shape[1]), lambda i: (i, 0)) for a in acts]
                  + [_const_spec(w.shape) for w in w_outs]
                  + [vec_spec, x_spec, vec_spec, _const_spec(wup.shape), _const_spec(cw.shape),
                     _const_spec(wdn.shape), vec_spec]),
        out_specs=x_spec,
        out_shape=jax.ShapeDtypeStruct(x.shape, F32),
        scratch_shapes=([pltpu.VMEM((tm, D_MODEL), BF16),
                         pltpu.VMEM((nc, SUBLANES, FFN_CHUNK), F32),
                         pltpu.VMEM((tm + SUBLANES, 2 * FFN_CHUNK), F32),
                         pltpu.VMEM((tm + SUBLANES, 2 * FFN_CHUNK), F32)]
                        + [pltpu.VMEM((nc, tm, FFN_CHUNK), BF16)] * FFN_TILES_PER_STEP),
        compiler_params=_params("arbitrary"),
        name="mix_ffn",
    )(*acts, *w_outs, g_mix.reshape(1, -1), x, g_pre.reshape(1, -1), wup, cw, wdn,
      g_post.reshape(1, -1))


def kernel(x, norm_mix_pre, norm_mix_post, norm_ffn_pre, norm_ffn_post, even_w_in, even_b_forget, even_q_norm, even_w_uq, even_kv_norm, even_w_ukv, even_w_out, odd_w_in, odd_lambda_q1, odd_lambda_k1, odd_lambda_q2, odd_lambda_k2, odd_subln, odd_w_out, ffn_w_up, ffn_conv_w, ffn_conv_b, ffn_w_down):
    bsz, seq, d = x.shape
    assert d == D_MODEL and seq % ATTN_TILE == 0 and seq % FFN_ROWS == 0
    t = bsz * seq
    rope = _rope_tables(seq)
    xf = x.reshape(t, d)
    for layer in range(DEPTH):
        i = layer // 2
        if layer % 2 == 0:
            fq, fk, fvt, mq, mk, mvt = _even_proj(
                xf, norm_mix_pre[layer], even_w_in[i], even_b_forget[i], even_q_norm[i],
                even_w_uq[i], even_kv_norm[i], even_w_ukv[i], rope, bsz, seq)
            shp = (bsz, seq, HEAD_COLS)
            fo = _flash_attention(fq.reshape(shp), fk.reshape(shp), fvt)
            mo = _flash_attention(mq.reshape(shp), mk.reshape(shp), mvt)
            w_out = even_w_out[i].astype(BF16)
            acts = [fo.reshape(t, -1), mo.reshape(t, -1)]
            w_outs = [w_out[:FOX_WIDTH], w_out[FOX_WIDTH:]]
        else:
            lambda_init = 0.8 - 0.6 * math.exp(-0.3 * layer)
            q, k, vt = _odd_proj(xf, norm_mix_pre[layer], odd_w_in[i], bsz, seq)
            o = _diff_attention(q.reshape(bsz, seq, -1), k.reshape(bsz, seq, -1), vt,
                                odd_lambda_q1[i], odd_lambda_k1[i],
                                odd_lambda_q2[i], odd_lambda_k2[i], odd_subln[i], lambda_init)
            acts, w_outs = [o.reshape(t, -1)], [odd_w_out[i].astype(BF16)]
        xf = _mix_ffn(acts, w_outs, norm_mix_post[layer], xf, norm_ffn_pre[layer], ffn_w_up[layer],
                      ffn_conv_w[layer], ffn_conv_b[layer], ffn_w_down[layer],
                      norm_ffn_post[layer], seq)
    return xf.reshape(bsz, seq, d)
```

```python
import functools
import math

import numpy as np
import jax
import jax.numpy as jnp
from jax import lax
from jax.experimental import pallas as pl
from jax.experimental.pallas import tpu as pltpu

D_MODEL = 1024
DEPTH = 4
RMS_EPS = 1e-6
NEG_INF = -1e30
HEADS = 8
FOX_HEAD_DIM = 64
FOX_WIDTH = HEADS * FOX_HEAD_DIM
MLA_NOPE_DIM = 64
MLA_ROPE_DIM = 32
MLA_V_DIM = 64
MLA_Q_RANK = 384
MLA_KV_RANK = 256
ROPE_THETA = 10000.0
DIFF_HEAD_DIM = 64
DIFF_V_DIM = 2 * DIFF_HEAD_DIM
D_FF = 2816
CONV_WIDTH = 3

LANES = 128
SUBLANES = 8
BF16_ROWS = 16
LOG2E = 1.4426950408889634
HEAD_COLS = HEADS * LANES
VMEM_LIMIT_BYTES = 56 * 1024 * 1024

PROJ_TILES_PER_STEP = 2
FFN_ROWS = 512
FFN_TILES_PER_STEP = 2
FFN_CHUNK = 256
ATTN_TILE = 512
FLASH_HEADS_PER_STEP, FLASH_KEYS, FLASH_LAG = 2, 256, 3
DIFF_HEADS_PER_STEP, DIFF_KEYS, DIFF_LAG = 1, 256, 3
F32 = jnp.float32
BF16 = jnp.bfloat16


def _params(*sem):
    return pltpu.CompilerParams(dimension_semantics=sem, vmem_limit_bytes=VMEM_LIMIT_BYTES)


def _const_spec(shape):
    nd = len(shape)
    return pl.BlockSpec(shape, lambda *_: (0,) * nd, pipeline_mode=pl.Buffered(1))


def _rms(x, g):
    return x * lax.rsqrt(jnp.mean(x * x, axis=-1, keepdims=True) + RMS_EPS) * g


def _split3(x):
    hi = x.astype(BF16)
    r1 = x - hi.astype(F32)
    mid = r1.astype(BF16)
    lo = (r1 - mid.astype(F32)).astype(BF16)
    return hi, mid, lo


def _rope(blk, ra, rb, rc):
    return blk * ra + pltpu.roll(blk, 16, 1) * rb + pltpu.roll(blk, LANES - 16, 1) * rc


def _vt_rows(dv):
    return dv + BF16_ROWS


def _store_vt(vt_ref, v, dv, chunk):
    tk = v.shape[0]
    per = LANES // dv
    for blk in range(v.shape[1] // LANES):
        vt = v[:, blk * LANES:(blk + 1) * LANES].T.astype(BF16)
        for s in range(per):
            vt_ref[0, blk * per + s, chunk, 0:dv, :] = vt[s * dv:(s + 1) * dv, :]
    row = lax.broadcasted_iota(jnp.int32, (BF16_ROWS, tk), 0)
    ones_row = jnp.where(row == 0, 1.0, 0.0).astype(BF16)
    for hd in range(HEADS):
        vt_ref[0, hd, chunk, dv:dv + BF16_ROWS, :] = ones_row


def _vt_spec(dv, chunks_per_step, tk, steps_per_seq):
    return pl.BlockSpec((1, HEADS, chunks_per_step, _vt_rows(dv), tk),
                        lambda i: (i // steps_per_seq, 0, i % steps_per_seq, 0, 0))


_EV_Q, _EV_K, _EV_V = 0, FOX_WIDTH, 2 * FOX_WIDTH
_EV_G = _EV_V + FOX_WIDTH
_EV_CQ = _EV_G + LANES
_EV_CKV = _EV_CQ + MLA_Q_RANK
_EV_KR = _EV_CKV + MLA_KV_RANK
_EV_WIDTH = _EV_KR + LANES


def _even_proj_kernel(x_ref, g_ref, w_ref, bf_ref, qn_ref, wuq_ref, kvn_ref, wuk_ref, wuv_ref,
                      ec_ref, ra_ref, rb_ref, rc_ref, cst_ref,
                      fq_ref, fk_ref, fv_ref, mq_ref, mk_ref, mv_ref, carry_ref,
                      *, steps_per_seq):
    tm = fv_ref.shape[4]
    n_tiles = x_ref.shape[0] // tm
    i = pl.program_id(0)

    @pl.when(i % steps_per_seq == 0)
    def _():
        carry_ref[...] = jnp.zeros_like(carry_ref)

    hs = [_rms(x_ref[s * tm:(s + 1) * tm, :], g_ref[...]).astype(BF16) for s in range(n_tiles)]
    for s, h in enumerate(hs):
        _even_proj_tile(h, s, slice(s * tm, (s + 1) * tm), w_ref, bf_ref, qn_ref, wuq_ref, kvn_ref,
                        wuk_ref, wuv_ref, ec_ref, ra_ref, rb_ref, rc_ref, cst_ref,
                        fq_ref, fk_ref, fv_ref, mq_ref, mk_ref, mv_ref, carry_ref)


def _even_proj_tile(h, s, rows, w_ref, bf_ref, qn_ref, wuq_ref, kvn_ref, wuk_ref, wuv_ref,
                    ec_ref, ra_ref, rb_ref, rc_ref, cst_ref,
                    fq_ref, fk_ref, fv_ref, mq_ref, mk_ref, mv_ref, carry_ref):
    tm = h.shape[0]
    q_aug = cst_ref[0:1, :]

    def proj(lo, hi):
        return jnp.dot(h, w_ref[:, lo:hi], preferred_element_type=F32)

    lane = lax.broadcasted_iota(jnp.int32, (tm, LANES), 1)

    def spread_heads(o_ref, packed, aug):
        for pair in range(HEADS // 2):
            blk = packed[:, pair * LANES:(pair + 1) * LANES]
            for odd, src in enumerate((blk, pltpu.roll(blk, FOX_HEAD_DIM, 1))):
                sl = slice((2 * pair + odd) * LANES, (2 * pair + odd + 1) * LANES)
                o_ref[rows, sl] = jnp.where(lane < FOX_HEAD_DIM, src, aug[:, sl]).astype(BF16)

    ra, rb, rc = ra_ref[rows, :], rb_ref[rows, :], rc_ref[rows, :]
    fox_scale = FOX_HEAD_DIM ** -0.5 * LOG2E
    mla_scale = (MLA_NOPE_DIM + MLA_ROPE_DIM) ** -0.5 * LOG2E

    z = proj(_EV_G, _EV_CQ) + bf_ref[...]
    cq_raw = proj(_EV_CQ, _EV_CKV)
    ckv_raw = proj(_EV_CKV, _EV_KR)
    kr_raw = proj(_EV_KR, _EV_WIDTH)

    logf = (jnp.minimum(z, 0.0) - jnp.log1p(jnp.exp(-jnp.abs(z)))) * LOG2E
    spread_heads(fq_ref, proj(_EV_Q, _EV_K) * fox_scale, q_aug)

    row = lax.broadcasted_iota(jnp.int32, (tm, tm), 0)
    col = lax.broadcasted_iota(jnp.int32, (tm, tm), 1)
    tril = jnp.where(row >= col, 1.0, 0.0).astype(BF16)
    parts = jnp.dot(tril, jnp.concatenate(_split3(logf), axis=1), preferred_element_type=F32)
    _store_vt(fv_ref, proj(_EV_V, _EV_G), FOX_HEAD_DIM, s)
    c = (parts[:, :LANES] + parts[:, LANES:2 * LANES] + parts[:, 2 * LANES:]
         + carry_ref[SUBLANES - 1:SUBLANES, :])
    carry_ref[...] = c[tm - SUBLANES:, :]

    cq = _rms(cq_raw, qn_ref[...]).astype(BF16)
    q = jnp.dot(cq, wuq_ref[...], preferred_element_type=F32) * mla_scale
    for hd in range(HEADS):
        sl = slice(hd * LANES, (hd + 1) * LANES)
        mq_ref[rows, sl] = _rope(q[:, sl], ra, rb, rc).astype(BF16)

    ckv = _rms(ckv_raw, kvn_ref[...]).astype(BF16)
    kr = _rope(kr_raw, ra, rb, rc)
    kn = jnp.dot(ckv, wuk_ref[...], preferred_element_type=F32)
    for hd in range(HEADS):
        sl = slice(hd * LANES, (hd + 1) * LANES)
        mk_ref[rows, sl] = (kn[:, sl] + kr).astype(BF16)
    _store_vt(mv_ref, jnp.dot(ckv, wuv_ref[...], preferred_element_type=F32), MLA_V_DIM, s)

    c_aug = jnp.dot(jnp.concatenate(_split3(c), axis=1), ec_ref[...], preferred_element_type=F32)
    spread_heads(fk_ref, proj(_EV_K, _EV_V), c_aug)


def _pad_heads(w, dh):
    k = w.shape[0]
    w = w.reshape(k, HEADS, dh)
    return jnp.pad(w, ((0, 0), (0, 0), (0, LANES - dh))).reshape(k, HEAD_COLS)


def _even_consts():
    cst = np.zeros((SUBLANES, HEAD_COLS), np.float32)
    ec = np.zeros((3 * LANES, HEAD_COLS), np.float32)
    for hd in range(HEADS):
        cst[0, hd * LANES + 64: hd * LANES + 67] = 1.0
        for p in range(3):
            ec[p * LANES + hd, hd * LANES + 64 + p] = -1.0
    return jnp.asarray(cst), jnp.asarray(ec, dtype=BF16)


def _rope_tables(seq):
    half = MLA_ROPE_DIM // 2
    inv = ROPE_THETA ** (-jnp.arange(0, MLA_ROPE_DIM, 2, dtype=F32) / MLA_ROPE_DIM)
    ang = jnp.arange(seq, dtype=F32)[:, None] * inv[None, :]
    cos, sin = jnp.cos(ang), jnp.sin(ang)
    zeros = jnp.zeros((seq, half), F32)
    ra = jnp.concatenate([jnp.ones((seq, 64), F32), cos, cos, jnp.zeros((seq, 32), F32)], axis=1)
    rb = jnp.concatenate([jnp.zeros((seq, 64), F32), zeros, sin, jnp.zeros((seq, 32), F32)], axis=1)
    rc = jnp.concatenate([jnp.zeros((seq, 64), F32), -sin, zeros, jnp.zeros((seq, 32), F32)], axis=1)
    return ra, rb, rc


def _even_proj(x, g, w_in, b_forget, q_norm, w_uq, kv_norm, w_ukv, rope, bsz, seq):
    t = x.shape[0]
    tk = ATTN_TILE
    tm = tk * PROJ_TILES_PER_STEP
    cuts = np.cumsum([FOX_WIDTH, FOX_WIDTH, FOX_WIDTH, HEADS, MLA_Q_RANK, MLA_KV_RANK]).tolist()
    wfq, wfk, wfv, wg, wcq, wckv, wkr = jnp.split(w_in, cuts, axis=1)
    w1 = jnp.concatenate([
        wfq, wfk, wfv,
        jnp.pad(wg, ((0, 0), (0, LANES - HEADS))), wcq, wckv,
        jnp.pad(wkr, ((0, 0), (64, LANES - 64 - MLA_ROPE_DIM)))], axis=1).astype(BF16)
    assert w1.shape[1] == _EV_WIDTH
    bf = jnp.pad(b_forget, (0, LANES - HEADS)).reshape(1, LANES)
    wuq = _pad_heads(w_uq, MLA_NOPE_DIM + MLA_ROPE_DIM).astype(BF16)
    wukv = w_ukv.reshape(MLA_KV_RANK, HEADS, MLA_NOPE_DIM + MLA_V_DIM)
    wuk = _pad_heads(wukv[:, :, :MLA_NOPE_DIM].reshape(MLA_KV_RANK, -1), MLA_NOPE_DIM).astype(BF16)
    wuv = wukv[:, :, MLA_NOPE_DIM:].reshape(MLA_KV_RANK, -1).astype(BF16)
    cst, ec = _even_consts()
    ra, rb, rc = rope
    steps_per_seq = seq // tm
    rope_spec = pl.BlockSpec((tm, LANES), lambda i: (i % steps_per_seq, 0))
    row_spec = pl.BlockSpec((tm, HEAD_COLS), lambda i: (i, 0))
    out_sds = jax.ShapeDtypeStruct((t, HEAD_COLS), BF16)
    vt_spec = _vt_spec(FOX_HEAD_DIM, PROJ_TILES_PER_STEP, tk, steps_per_seq)
    vt_sds = jax.ShapeDtypeStruct((bsz, HEADS, seq // tk, _vt_rows(FOX_HEAD_DIM), tk), BF16)
    return pl.pallas_call(
        functools.partial(_even_proj_kernel, steps_per_seq=steps_per_seq),
        grid=(t // tm,),
        in_specs=[pl.BlockSpec((tm, D_MODEL), lambda i: (i, 0)),
                  _const_spec((1, D_MODEL)), _const_spec(w1.shape), _const_spec((1, LANES)),
                  _const_spec((1, MLA_Q_RANK)), _const_spec(wuq.shape),
                  _const_spec((1, MLA_KV_RANK)), _const_spec(wuk.shape), _const_spec(wuv.shape),
                  _const_spec(ec.shape), rope_spec, rope_spec, rope_spec, _const_spec(cst.shape)],
        out_specs=[row_spec, row_spec, vt_spec, row_spec, row_spec, vt_spec],
        out_shape=[out_sds, out_sds, vt_sds, out_sds, out_sds, vt_sds],
        scratch_shapes=[pltpu.VMEM((SUBLANES, LANES), F32)],
        compiler_params=_params("arbitrary"),
        name="even_proj",
    )(x, g.reshape(1, -1), w1, bf, q_norm.reshape(1, -1), wuq, kv_norm.reshape(1, -1), wuk, wuv,
      ec, ra, rb, rc, cst)


def _alibi_slope(hd):
    return 2.0 ** (-8.0 * (hd + 1) / HEADS)


def _odd_proj_kernel(x_ref, g_ref, w_ref, q_ref, k_ref, v_ref, *, steps_per_seq):
    tm = v_ref.shape[4]
    n_tiles = x_ref.shape[0] // tm
    i = pl.program_id(0)
    scale = DIFF_HEAD_DIM ** -0.5 * LOG2E
    d = DIFF_HEAD_DIM
    lane = lax.broadcasted_iota(jnp.int32, (tm, LANES), 1)
    q_aug = jnp.where((lane >= d) & (lane < d + 3), 1.0, 0.0)
    hs = [_rms(x_ref[s * tm:(s + 1) * tm, :], g_ref[...]).astype(BF16) for s in range(n_tiles)]
    for s, h in enumerate(hs):
        rows = slice(s * tm, (s + 1) * tm)
        v = jnp.dot(h, w_ref[:, 2 * HEAD_COLS:], preferred_element_type=F32)
        k = jnp.dot(h, w_ref[:, HEAD_COLS:2 * HEAD_COLS], preferred_element_type=F32)
        _store_vt(v_ref, v, DIFF_V_DIM, s)
        q = jnp.dot(h, w_ref[:, :HEAD_COLS], preferred_element_type=F32)
        pos = (((i % steps_per_seq) * n_tiles + s) * tm
               + lax.broadcasted_iota(jnp.int32, (tm, 1), 0)).astype(F32)
        for hd in range(HEADS):
            sl = slice(hd * LANES, (hd + 1) * LANES)
            b_hi, b_mid, b_lo = (p.astype(F32)
                                 for p in _split3(pos * (_alibi_slope(hd) * LOG2E)))
            k_aug = jnp.where(lane == d, b_hi,
                              jnp.where(lane == d + 1, b_mid, jnp.where(lane == d + 2, b_lo, 0.0)))
            qb, kb = q[:, sl] * scale, k[:, sl]
            for br, (qs, ks) in enumerate(((qb, kb),
                                           (pltpu.roll(qb, d, 1), pltpu.roll(kb, d, 1)))):
                so = slice((2 * hd + br) * LANES, (2 * hd + br + 1) * LANES)
                q_ref[rows, so] = jnp.where(lane < d, qs, q_aug).astype(BF16)
                k_ref[rows, so] = jnp.where(lane < d, ks, k_aug).astype(BF16)


def _odd_proj(x, g, w_in, bsz, seq):
    t = x.shape[0]
    tk = ATTN_TILE
    tm = tk * PROJ_TILES_PER_STEP
    w = w_in.astype(BF16)
    steps_per_seq = seq // tm
    return pl.pallas_call(
        functools.partial(_odd_proj_kernel, steps_per_seq=steps_per_seq),
        grid=(t // tm,),
        in_specs=[pl.BlockSpec((tm, D_MODEL), lambda i: (i, 0)),
                  _const_spec((1, D_MODEL)), _const_spec(w.shape)],
        out_specs=[pl.BlockSpec((tm, 2 * HEAD_COLS), lambda i: (i, 0)),
                   pl.BlockSpec((tm, 2 * HEAD_COLS), lambda i: (i, 0)),
                   _vt_spec(DIFF_V_DIM, PROJ_TILES_PER_STEP, tk, steps_per_seq)],
        out_shape=[jax.ShapeDtypeStruct((t, 2 * HEAD_COLS), BF16),
                   jax.ShapeDtypeStruct((t, 2 * HEAD_COLS), BF16),
                   jax.ShapeDtypeStruct((bsz, HEADS, seq // tk, _vt_rows(DIFF_V_DIM), tk), BF16)],
        compiler_params=_params("arbitrary"),
        name="odd_proj",
    )(x, g.reshape(1, -1), w)


_NT = (((1,), (1,)), ((), ()))


def _attend(s, vt, m, acc, key0=None):
    if key0 is None:
        m_new = jnp.maximum(m, jnp.max(s, axis=0, keepdims=True))
        p = jnp.exp2(s - m_new).astype(BF16)
    else:
        nr, nc = s.shape[0] // LANES, s.shape[1] // LANES
        key = lax.broadcasted_iota(jnp.int32, (LANES, LANES), 0)
        qry = lax.broadcasted_iota(jnp.int32, (LANES, LANES), 1)
        sq = [[s[a * LANES:(a + 1) * LANES, b * LANES:(b + 1) * LANES] for b in range(nc)]
              for a in range(nr)]
        k0 = key0 // LANES
        for a in range(nr):
            if k0 + a < nc:
                sq[a][k0 + a] = jnp.where(key <= qry, sq[a][k0 + a], NEG_INF)
        col_max = []
        for b in range(nc):
            vis = [sq[a][b] for a in range(nr) if k0 + a <= b]
            col_max.append(functools.reduce(jnp.maximum, [jnp.max(x, axis=0, keepdims=True)
                                                          for x in vis])
                           if vis else jnp.full((1, LANES), NEG_INF, F32))
        m_new = jnp.maximum(m, jnp.concatenate(col_max, axis=1))
        p = jnp.concatenate(
            [jnp.concatenate(
                [jnp.exp2(sq[a][b] - m_new[:, b * LANES:(b + 1) * LANES]).astype(BF16)
                 if k0 + a <= b else jnp.zeros((LANES, LANES), BF16) for b in range(nc)], axis=1)
             for a in range(nr)], axis=0)
    acc = jnp.exp2(m - m_new) * acc + jnp.dot(vt, p, preferred_element_type=F32)
    return m_new, acc


def _stream_attention(qk, vts, finish, q_tiles, t, tk, rows, n, lag):
    r = t // tk
    items = [(qi, j, u, k) for qi in range(q_tiles) for j in range(qi + 1) for u in range(r)
             for k in range(n)]
    def first_query(qi, j, u):
        return u * tk if j == qi else 0

    pending, issued = [], 0
    state = None
    for i, (qi, j, u, k) in enumerate(items):
        while issued < min(i + lag + 1, len(items)):
            nqi, nj, nu, nk = items[issued]
            pending.append(qk(nk, nqi, nj, nu, first_query(nqi, nj, nu)))
            issued += 1
        if j == 0 and u == 0 and k == 0:
            state = [(jnp.full((1, t), NEG_INF, F32), jnp.zeros((rows, t), F32))] * n
        q0 = first_query(qi, j, u)
        m, acc = state[k]
        m_hi, acc_hi = _attend(pending.pop(0), vts(k, j, u), m[:, q0:], acc[:, q0:],
                               0 if j == qi else None)
        state[k] = ((jnp.concatenate([m[:, :q0], m_hi], axis=1),
                     jnp.concatenate([acc[:, :q0], acc_hi], axis=1)) if q0 else (m_hi, acc_hi))
        if j == qi and u == r - 1 and k == n - 1:
            finish(qi, [acc for _, acc in state])


def _flash_kernel(q_ref, k_ref, vt_ref, o_ref):
    n, chunks, rows, t = vt_ref.shape[1:]
    tk = FLASH_KEYS
    dv = rows - BF16_ROWS

    def qk(hd, qi, j, u, q0):
        kc = k_ref[0, j * t + u * tk:j * t + (u + 1) * tk, hd * LANES:(hd + 1) * LANES]
        q = q_ref[0, qi * t + q0:(qi + 1) * t, hd * LANES:(hd + 1) * LANES]
        return lax.dot_general(kc, q, _NT, preferred_element_type=F32)

    def vts(hd, j, u):
        return vt_ref[0, hd, j, :, u * tk:(u + 1) * tk]

    def finish(qi, accs):
        o_t = jnp.concatenate([a[:dv] / a[dv:dv + 1] for a in accs], axis=0)
        o_ref[0, qi * t:(qi + 1) * t, :] = o_t.T.astype(BF16)

    _stream_attention(qk, vts, finish, chunks, t, tk, rows, n, FLASH_LAG)


def _flash_attention(q, k, vt):
    b, s, _ = q.shape
    _, _, chunks, rows, t = vt.shape
    n = FLASH_HEADS_PER_STEP
    dv = rows - BF16_ROWS
    return pl.pallas_call(
        _flash_kernel,
        grid=(b, HEADS // n),
        in_specs=[pl.BlockSpec((1, s, n * LANES), lambda bi, hp: (bi, 0, hp)),
                  pl.BlockSpec((1, s, n * LANES), lambda bi, hp: (bi, 0, hp)),
                  pl.BlockSpec((1, n, chunks, rows, t), lambda bi, hp: (bi, hp, 0, 0, 0))],
        out_specs=pl.BlockSpec((1, s, n * dv), lambda bi, hp: (bi, 0, hp)),
        out_shape=jax.ShapeDtypeStruct((b, s, HEADS * dv), BF16),
        compiler_params=_params("arbitrary", "arbitrary"),
        name="flash_attention",
    )(q, k, vt)


def _diff_kernel(q_ref, k_ref, vt_ref, lq1_ref, lk1_ref, lq2_ref, lk2_ref, sub_ref, o_ref,
                 *, lambda_init):
    n, chunks, rows, t = vt_ref.shape[1:]
    tk = DIFF_KEYS
    dv = rows - BF16_ROWS

    def qk(st, qi, j, u, q0):
        kc = k_ref[0, j * t + u * tk:j * t + (u + 1) * tk, st * LANES:(st + 1) * LANES]
        q = q_ref[0, qi * t + q0:(qi + 1) * t, st * LANES:(st + 1) * LANES]
        return lax.dot_general(kc, q, _NT, preferred_element_type=F32)

    def vts(st, j, u):
        return vt_ref[0, st // 2, j, :, u * tk:(u + 1) * tk]

    lam = (jnp.exp(jnp.sum(lq1_ref[...] * lk1_ref[...], axis=1, keepdims=True))
           - jnp.exp(jnp.sum(lq2_ref[...] * lk2_ref[...], axis=1, keepdims=True)) + lambda_init)

    def finish(qi, accs):
        for hd in range(n):
            a1, a2 = accs[2 * hd], accs[2 * hd + 1]
            o = (a1[:dv] / a1[dv:dv + 1] - lam * (a2[:dv] / a2[dv:dv + 1])).T
            o_ref[0, qi * t:(qi + 1) * t, hd * dv:(hd + 1) * dv] = (
                _rms(o, sub_ref[...]) * (1.0 - lambda_init)).astype(BF16)

    _stream_attention(qk, vts, finish, chunks, t, tk, rows, 2 * n, DIFF_LAG)


def _diff_attention(q, k, vt, lq1, lk1, lq2, lk2, subln, lambda_init):
    b, s, _ = q.shape
    _, _, chunks, rows, t = vt.shape
    vec = lambda a: a.reshape(1, -1)
    n = DIFF_HEADS_PER_STEP
    return pl.pallas_call(
        functools.partial(_diff_kernel, lambda_init=lambda_init),
        grid=(b, HEADS // n),
        in_specs=[pl.BlockSpec((1, s, 2 * n * LANES), lambda bi, hd: (bi, 0, hd)),
                  pl.BlockSpec((1, s, 2 * n * LANES), lambda bi, hd: (bi, 0, hd)),
                  pl.BlockSpec((1, n, chunks, rows, t), lambda bi, hd: (bi, hd, 0, 0, 0)),
                  _const_spec((1, DIFF_HEAD_DIM)), _const_spec((1, DIFF_HEAD_DIM)),
                  _const_spec((1, DIFF_HEAD_DIM)), _const_spec((1, DIFF_HEAD_DIM)),
                  _const_spec((1, DIFF_V_DIM))],
        out_specs=pl.BlockSpec((1, s, n * LANES), lambda bi, hd: (bi, 0, hd)),
        out_shape=jax.ShapeDtypeStruct((b, s, HEADS * (rows - BF16_ROWS)), BF16),
        compiler_params=_params("arbitrary", "arbitrary"),
        name="diff_attention",
    )(q, k, vt, vec(lq1), vec(lk1), vec(lq2), vec(lk2), vec(subln))


_N_FFN_CHUNKS = D_FF // FFN_CHUNK
_GELU_C = math.sqrt(2.0 / math.pi)


def _mix_ffn_kernel(*refs, n_in, steps_per_seq):
    a_refs, wo_refs = refs[:n_in], refs[n_in:2 * n_in]
    (gmix_ref, x_ref, gpre_ref, wup_ref, cw_ref, wdn_ref, gpost_ref, o_ref,
     h_ref, hist_ref, gva_ref, gvb_ref, *mid_refs) = refs[2 * n_in:]
    tm = h_ref.shape[0]
    nc = _N_FFN_CHUNKS
    i = pl.program_id(0)

    @pl.when(i % steps_per_seq == 0)
    def _():
        hist_ref[...] = jnp.zeros_like(hist_ref)

    def out_proj(rows):
        mix = jnp.dot(a_refs[0][rows, :], wo_refs[0][...], preferred_element_type=F32)
        for a_ref, wo_ref in zip(a_refs[1:], wo_refs[1:]):
            mix = mix + jnp.dot(a_ref[rows, :], wo_ref[...], preferred_element_type=F32)
        return mix

    def pre_norms(rows, mix):
        x1 = x_ref[rows, :] + _rms(mix, gmix_ref[...])
        o_ref[rows, :] = x1
        h_ref[...] = _rms(x1, gpre_ref[...]).astype(BF16)

    def up(c):
        return jnp.dot(h_ref[...], wup_ref[c], preferred_element_type=F32)

    def down(mid_ref):
        f = jnp.dot(mid_ref[0], wdn_ref[0], preferred_element_type=F32)
        for c in range(1, nc):
            f = f + jnp.dot(mid_ref[c], wdn_ref[c], preferred_element_type=F32)
        return f

    def gated(c, gv_ref, mid_ref):
        gv_ref[0:SUBLANES, :FFN_CHUNK] = hist_ref[c]
        gate = gv_ref[SUBLANES:, :FFN_CHUNK]
        val = gv_ref[SUBLANES:, FFN_CHUNK:]
        hist_ref[c] = gv_ref[tm:, :FFN_CHUNK]
        cw = cw_ref[c]
        pre = (gv_ref[SUBLANES - 2:SUBLANES - 2 + tm, :FFN_CHUNK] * cw[0:1, :]
               + gv_ref[SUBLANES - 1:SUBLANES - 1 + tm, :FFN_CHUNK] * cw[1:2, :]
               + gate * cw[2:3, :] + cw[3:4, :])
        pre = pre.astype(BF16)
        act = 0.5 * pre * (1.0 + jnp.tanh(_GELU_C * (pre + 0.044715 * (pre * pre * pre))))
        mid_ref[c] = act * val.astype(BF16)

    def chunks_after_first(mid_ref):
        def body(k, _):
            gated(2 * k, gva_ref, mid_ref)
            gvb_ref[SUBLANES:, :] = up(2 * k + 1)
            gated(2 * k + 1, gvb_ref, mid_ref)
            gva_ref[SUBLANES:, :] = up(2 * k + 2)
            return 0

        for k in range(nc // 2):
            body(k, 0)
        gated(nc - 1, gva_ref, mid_ref)

    assert nc % 2 == 1
    n_tiles = len(mid_refs)
    rows = [slice(s * tm, (s + 1) * tm) for s in range(n_tiles)]
    pre_norms(rows[0], out_proj(rows[0]))
    gva_ref[SUBLANES:, :] = up(0)
    for s in range(n_tiles):
        chunks_after_first(mid_refs[s])
        if s + 1 < n_tiles:
            mix_next = out_proj(rows[s + 1])
        f = down(mid_refs[s])
        if s + 1 < n_tiles:
            pre_norms(rows[s + 1], mix_next)
            gva_ref[SUBLANES:, :] = up(0)
        o_ref[rows[s], :] = o_ref[rows[s], :] + _rms(f, gpost_ref[...])


def _mix_ffn(acts, w_outs, g_mix, x, g_pre, w_up, conv_w, conv_b, w_down, g_post, seq):
    t = x.shape[0]
    tm = FFN_ROWS
    nc = _N_FFN_CHUNKS
    n_in = len(acts)
    wg = w_up[:, :D_FF].reshape(D_MODEL, nc, FFN_CHUNK)
    wv = w_up[:, D_FF:].reshape(D_MODEL, nc, FFN_CHUNK)
    wup = jnp.concatenate([wg, wv], axis=2).transpose(1, 0, 2).astype(BF16)
    wdn = w_down.reshape(nc, FFN_CHUNK, D_MODEL).astype(BF16)
    cw = jnp.concatenate([conv_w, conv_b[None, :],
                          jnp.zeros((SUBLANES - CONV_WIDTH - 1, D_FF), F32)], axis=0)
    cw = cw.reshape(SUBLANES, nc, FFN_CHUNK).transpose(1, 0, 2)
    step_rows = tm * FFN_TILES_PER_STEP
    assert seq % step_rows == 0
    x_spec = pl.BlockSpec((step_rows, D_MODEL), lambda i: (i, 0))
    vec_spec = _const_spec((1, D_MODEL))
    return pl.pallas_call(
        functools.partial(_mix_ffn_kernel, n_in=n_in, steps_per_seq=seq // step_rows),
        grid=(t // step_rows,),
        in_specs=([pl.BlockSpec((step_rows, a.shape[1]), lambda i: (i, 0)) for a in acts]
                  + [_const_spec(w.shape) for w in w_outs]
                  + [vec_spec, x_spec, vec_spec, _const_spec(wup.shape), _const_spec(cw.shape),
                     _const_spec(wdn.shape), vec_spec]),
        out_specs=x_spec,
        out_shape=jax.ShapeDtypeStruct(x.shape, F32),
        scratch_shapes=([pltpu.VMEM((tm, D_MODEL), BF16),
                         pltpu.VMEM((nc, SUBLANES, FFN_CHUNK), F32),
                         pltpu.VMEM((tm + SUBLANES, 2 * FFN_CHUNK), F32),
                         pltpu.VMEM((tm + SUBLANES, 2 * FFN_CHUNK), F32)]
                        + [pltpu.VMEM((nc, tm, FFN_CHUNK), BF16)] * FFN_TILES_PER_STEP),
        compiler_params=_params("arbitrary"),
        name="mix_ffn",
    )(*acts, *w_outs, g_mix.reshape(1, -1), x, g_pre.reshape(1, -1), wup, cw, wdn,
      g_post.reshape(1, -1))


def kernel(x, norm_mix_pre, norm_mix_post, norm_ffn_pre, norm_ffn_post, even_w_in, even_b_forget, even_q_norm, even_w_uq, even_kv_norm, even_w_ukv, even_w_out, odd_w_in, odd_lambda_q1, odd_lambda_k1, odd_lambda_q2, odd_lambda_k2, odd_subln, odd_w_out, ffn_w_up, ffn_conv_w, ffn_conv_b, ffn_w_down):
    bsz, seq, d = x.shape
    assert d == D_MODEL and seq % ATTN_TILE == 0 and seq % FFN_ROWS == 0
    t = bsz * seq
    rope = _rope_tables(seq)
    xf = x.reshape(t, d)
    for layer in range(DEPTH):
        i = layer // 2
        if layer % 2 == 0:
            fq, fk, fvt, mq, mk, mvt = _even_proj(
                xf, norm_mix_pre[layer], even_w_in[i], even_b_forget[i], even_q_norm[i],
                even_w_uq[i], even_kv_norm[i], even_w_ukv[i], rope, bsz, seq)
            shp = (bsz, seq, HEAD_COLS)
            fo = _flash_attention(fq.reshape(shp), fk.reshape(shp), fvt)
            mo = _flash_attention(mq.reshape(shp), mk.reshape(shp), mvt)
            w_out = even_w_out[i].astype(BF16)
            acts = [fo.reshape(t, -1), mo.reshape(t, -1)]
            w_outs = [w_out[:FOX_WIDTH], w_out[FOX_WIDTH:]]
        else:
            lambda_init = 0.8 - 0.6 * math.exp(-0.3 * layer)
            q, k, vt = _odd_proj(xf, norm_mix_pre[layer], odd_w_in[i], bsz, seq)
            o = _diff_attention(q.reshape(bsz, seq, -1), k.reshape(bsz, seq, -1), vt,
                                odd_lambda_q1[i], odd_lambda_k1[i],
                                odd_lambda_q2[i], odd_lambda_k2[i], odd_subln[i], lambda_init)
            acts, w_outs = [o.reshape(t, -1)], [odd_w_out[i].astype(BF16)]
        xf = _mix_ffn(acts, w_outs, norm_mix_post[layer], xf, norm_ffn_pre[layer], ffn_w_up[layer],
                      ffn_conv_w[layer], ffn_conv_b[layer], ffn_w_down[layer],
                      norm_ffn_post[layer], seq)
    return xf.reshape(bsz, seq, d)
```

```python
import functools
import math

import numpy as np
import jax
import jax.numpy as jnp
from jax import lax
from jax.experimental import pallas as pl
from jax.experimental.pallas import tpu as pltpu

D_MODEL = 1024
DEPTH = 4
RMS_EPS = 1e-6
NEG_INF = -1e30
HEADS = 8
FOX_HEAD_DIM = 64
FOX_WIDTH = HEADS * FOX_HEAD_DIM
MLA_NOPE_DIM = 64
MLA_ROPE_DIM = 32
MLA_V_DIM = 64
MLA_Q_RANK = 384
MLA_KV_RANK = 256
ROPE_THETA = 10000.0
DIFF_HEAD_DIM = 64
DIFF_V_DIM = 2 * DIFF_HEAD_DIM
D_FF = 2816
CONV_WIDTH = 3

LANES = 128
SUBLANES = 8
BF16_ROWS = 16
LOG2E = 1.4426950408889634
HEAD_COLS = HEADS * LANES
VMEM_LIMIT_BYTES = 56 * 1024 * 1024

PROJ_TILES_PER_STEP = 2
FFN_ROWS = 512
FFN_TILES_PER_STEP = 2
FFN_CHUNK = 256
ATTN_TILE = 512
FLASH_HEADS_PER_STEP, FLASH_KEYS, FLASH_LAG = 2, 256, 3
DIFF_HEADS_PER_STEP, DIFF_KEYS, DIFF_LAG = 1, 256, 3
F32 = jnp.float32
BF16 = jnp.bfloat16


def _params(*sem):
    return pltpu.CompilerParams(dimension_semantics=sem, vmem_limit_bytes=VMEM_LIMIT_BYTES)


def _const_spec(shape):
    nd = len(shape)
    return pl.BlockSpec(shape, lambda *_: (0,) * nd, pipeline_mode=pl.Buffered(1))


def _rms(x, g):
    return x * lax.rsqrt(jnp.mean(x * x, axis=-1, keepdims=True) + RMS_EPS) * g


def _split3(x):
    hi = x.astype(BF16)
    r1 = x - hi.astype(F32)
    mid = r1.astype(BF16)
    lo = (r1 - mid.astype(F32)).astype(BF16)
    return hi, mid, lo


def _rope(blk, ra, rb, rc):
    return blk * ra + pltpu.roll(blk, 16, 1) * rb + pltpu.roll(blk, LANES - 16, 1) * rc


def _vt_rows(dv):
    return dv + BF16_ROWS


def _store_vt(vt_ref, v, dv, chunk):
    tk = v.shape[0]
    per = LANES // dv
    for blk in range(v.shape[1] // LANES):
        vt = v[:, blk * LANES:(blk + 1) * LANES].T.astype(BF16)
        for s in range(per):
            vt_ref[0, blk * per + s, chunk, 0:dv, :] = vt[s * dv:(s + 1) * dv, :]
    row = lax.broadcasted_iota(jnp.int32, (BF16_ROWS, tk), 0)
    ones_row = jnp.where(row == 0, 1.0, 0.0).astype(BF16)
    for hd in range(HEADS):
        vt_ref[0, hd, chunk, dv:dv + BF16_ROWS, :] = ones_row


def _vt_spec(dv, chunks_per_step, tk, steps_per_seq):
    return pl.BlockSpec((1, HEADS, chunks_per_step, _vt_rows(dv), tk),
                        lambda i: (i // steps_per_seq, 0, i % steps_per_seq, 0, 0))


_EV_Q, _EV_K, _EV_V = 0, FOX_WIDTH, 2 * FOX_WIDTH
_EV_G = _EV_V + FOX_WIDTH
_EV_CQ = _EV_G + LANES
_EV_CKV = _EV_CQ + MLA_Q_RANK
_EV_KR = _EV_CKV + MLA_KV_RANK
_EV_WIDTH = _EV_KR + LANES


def _even_proj_kernel(x_ref, g_ref, w_ref, bf_ref, qn_ref, wuq_ref, kvn_ref, wuk_ref, wuv_ref,
                      ec_ref, ra_ref, rb_ref, rc_ref, cst_ref,
                      fq_ref, fk_ref, fv_ref, mq_ref, mk_ref, mv_ref, carry_ref,
                      *, steps_per_seq):
    tm = fv_ref.shape[4]
    n_tiles = x_ref.shape[0] // tm
    i = pl.program_id(0)

    @pl.when(i % steps_per_seq == 0)
    def _():
        carry_ref[...] = jnp.zeros_like(carry_ref)

    hs = [_rms(x_ref[s * tm:(s + 1) * tm, :], g_ref[...]).astype(BF16) for s in range(n_tiles)]
    for s, h in enumerate(hs):
        _even_proj_tile(h, s, slice(s * tm, (s + 1) * tm), w_ref, bf_ref, qn_ref, wuq_ref, kvn_ref,
                        wuk_ref, wuv_ref, ec_ref, ra_ref, rb_ref, rc_ref, cst_ref,
                        fq_ref, fk_ref, fv_ref, mq_ref, mk_ref, mv_ref, carry_ref)


def _even_proj_tile(h, s, rows, w_ref, bf_ref, qn_ref, wuq_ref, kvn_ref, wuk_ref, wuv_ref,
                    ec_ref, ra_ref, rb_ref, rc_ref, cst_ref,
                    fq_ref, fk_ref, fv_ref, mq_ref, mk_ref, mv_ref, carry_ref):
    tm = h.shape[0]
    q_aug = cst_ref[0:1, :]

    def proj(lo, hi):
        return jnp.dot(h, w_ref[:, lo:hi], preferred_element_type=F32)

    lane = lax.broadcasted_iota(jnp.int32, (tm, LANES), 1)

    def spread_heads(o_ref, packed, aug):
        for pair in range(HEADS // 2):
            blk = packed[:, pair * LANES:(pair + 1) * LANES]
            for odd, src in enumerate((blk, pltpu.roll(blk, FOX_HEAD_DIM, 1))):
                sl = slice((2 * pair + odd) * LANES, (2 * pair + odd + 1) * LANES)
                o_ref[rows, sl] = jnp.where(lane < FOX_HEAD_DIM, src, aug[:, sl]).astype(BF16)

    ra, rb, rc = ra_ref[rows, :], rb_ref[rows, :], rc_ref[rows, :]
    fox_scale = FOX_HEAD_DIM ** -0.5 * LOG2E
    mla_scale = (MLA_NOPE_DIM + MLA_ROPE_DIM) ** -0.5 * LOG2E

    z = proj(_EV_G, _EV_CQ) + bf_ref[...]
    cq_raw = proj(_EV_CQ, _EV_CKV)
    ckv_raw = proj(_EV_CKV, _EV_KR)
    kr_raw = proj(_EV_KR, _EV_WIDTH)

    logf = (jnp.minimum(z, 0.0) - jnp.log1p(jnp.exp(-jnp.abs(z)))) * LOG2E
    spread_heads(fq_ref, proj(_EV_Q, _EV_K) * fox_scale, q_aug)

    row = lax.broadcasted_iota(jnp.int32, (tm, tm), 0)
    col = lax.broadcasted_iota(jnp.int32, (tm, tm), 1)
    tril = jnp.where(row >= col, 1.0, 0.0).astype(BF16)
    parts = jnp.dot(tril, jnp.concatenate(_split3(logf), axis=1), preferred_element_type=F32)
    _store_vt(fv_ref, proj(_EV_V, _EV_G), FOX_HEAD_DIM, s)
    c = (parts[:, :LANES] + parts[:, LANES:2 * LANES] + parts[:, 2 * LANES:]
         + carry_ref[SUBLANES - 1:SUBLANES, :])
    carry_ref[...] = c[tm - SUBLANES:, :]

    cq = _rms(cq_raw, qn_ref[...]).astype(BF16)
    q = jnp.dot(cq, wuq_ref[...], preferred_element_type=F32) * mla_scale
    for hd in range(HEADS):
        sl = slice(hd * LANES, (hd + 1) * LANES)
        mq_ref[rows, sl] = _rope(q[:, sl], ra, rb, rc).astype(BF16)

    ckv = _rms(ckv_raw, kvn_ref[...]).astype(BF16)
    kr = _rope(kr_raw, ra, rb, rc)
    kn = jnp.dot(ckv, wuk_ref[...], preferred_element_type=F32)
    for hd in range(HEADS):
        sl = slice(hd * LANES, (hd + 1) * LANES)
        mk_ref[rows, sl] = (kn[:, sl] + kr).astype(BF16)
    _store_vt(mv_ref, jnp.dot(ckv, wuv_ref[...], preferred_element_type=F32), MLA_V_DIM, s)

    c_aug = jnp.dot(jnp.concatenate(_split3(c), axis=1), ec_ref[...], preferred_element_type=F32)
    spread_heads(fk_ref, proj(_EV_K, _EV_V), c_aug)


def _pad_heads(w, dh):
    k = w.shape[0]
    w = w.reshape(k, HEADS, dh)
    return jnp.pad(w, ((0, 0), (0, 0), (0, LANES - dh))).reshape(k, HEAD_COLS)


def _even_consts():
    cst = np.zeros((SUBLANES, HEAD_COLS), np.float32)
    ec = np.zeros((3 * LANES, HEAD_COLS), np.float32)
    for hd in range(HEADS):
        cst[0, hd * LANES + 64: hd * LANES + 67] = 1.0
        for p in range(3):
            ec[p * LANES + hd, hd * LANES + 64 + p] = -1.0
    return jnp.asarray(cst), jnp.asarray(ec, dtype=BF16)


def _rope_tables(seq):
    half = MLA_ROPE_DIM // 2
    inv = ROPE_THETA ** (-jnp.arange(0, MLA_ROPE_DIM, 2, dtype=F32) / MLA_ROPE_DIM)
    ang = jnp.arange(seq, dtype=F32)[:, None] * inv[None, :]
    cos, sin = jnp.cos(ang), jnp.sin(ang)
    zeros = jnp.zeros((seq, half), F32)
    ra = jnp.concatenate([jnp.ones((seq, 64), F32), cos, cos, jnp.zeros((seq, 32), F32)], axis=1)
    rb = jnp.concatenate([jnp.zeros((seq, 64), F32), zeros, sin, jnp.zeros((seq, 32), F32)], axis=1)
    rc = jnp.concatenate([jnp.zeros((seq, 64), F32), -sin, zeros, jnp.zeros((seq, 32), F32)], axis=1)
    return ra, rb, rc


def _even_proj(x, g, w_in, b_forget, q_norm, w_uq, kv_norm, w_ukv, rope, bsz, seq):
    t = x.shape[0]
    tk = ATTN_TILE
    tm = tk * PROJ_TILES_PER_STEP
    cuts = np.cumsum([FOX_WIDTH, FOX_WIDTH, FOX_WIDTH, HEADS, MLA_Q_RANK, MLA_KV_RANK]).tolist()
    wfq, wfk, wfv, wg, wcq, wckv, wkr = jnp.split(w_in, cuts, axis=1)
    w1 = jnp.concatenate([
        wfq, wfk, wfv,
        jnp.pad(wg, ((0, 0), (0, LANES - HEADS))), wcq, wckv,
        jnp.pad(wkr, ((0, 0), (64, LANES - 64 - MLA_ROPE_DIM)))], axis=1).astype(BF16)
    assert w1.shape[1] == _EV_WIDTH
    bf = jnp.pad(b_forget, (0, LANES - HEADS)).reshape(1, LANES)
    wuq = _pad_heads(w_uq, MLA_NOPE_DIM + MLA_ROPE_DIM).astype(BF16)
    wukv = w_ukv.reshape(MLA_KV_RANK, HEADS, MLA_NOPE_DIM + MLA_V_DIM)
    wuk = _pad_heads(wukv[:, :, :MLA_NOPE_DIM].reshape(MLA_KV_RANK, -1), MLA_NOPE_DIM).astype(BF16)
    wuv = wukv[:, :, MLA_NOPE_DIM:].reshape(MLA_KV_RANK, -1).astype(BF16)
    cst, ec = _even_consts()
    ra, rb, rc = rope
    steps_per_seq = seq // tm
    rope_spec = pl.BlockSpec((tm, LANES), lambda i: (i % steps_per_seq, 0))
    row_spec = pl.BlockSpec((tm, HEAD_COLS), lambda i: (i, 0))
    out_sds = jax.ShapeDtypeStruct((t, HEAD_COLS), BF16)
    vt_spec = _vt_spec(FOX_HEAD_DIM, PROJ_TILES_PER_STEP, tk, steps_per_seq)
    vt_sds = jax.ShapeDtypeStruct((bsz, HEADS, seq // tk, _vt_rows(FOX_HEAD_DIM), tk), BF16)
    return pl.pallas_call(
        functools.partial(_even_proj_kernel, steps_per_seq=steps_per_seq),
        grid=(t // tm,),
        in_specs=[pl.BlockSpec((tm, D_MODEL), lambda i: (i, 0)),
                  _const_spec((1, D_MODEL)), _const_spec(w1.shape), _const_spec((1, LANES)),
                  _const_spec((1, MLA_Q_RANK)), _const_spec(wuq.shape),
                  _const_spec((1, MLA_KV_RANK)), _const_spec(wuk.shape), _const_spec(wuv.shape),
                  _const_spec(ec.shape), rope_spec, rope_spec, rope_spec, _const_spec(cst.shape)],
        out_specs=[row_spec, row_spec, vt_spec, row_spec, row_spec, vt_spec],
        out_shape=[out_sds, out_sds, vt_sds, out_sds, out_sds, vt_sds],
        scratch_shapes=[pltpu.VMEM((SUBLANES, LANES), F32)],
        compiler_params=_params("arbitrary"),
        name="even_proj",
    )(x, g.reshape(1, -1), w1, bf, q_norm.reshape(1, -1), wuq, kv_norm.reshape(1, -1), wuk, wuv,
      ec, ra, rb, rc, cst)


def _alibi_slope(hd):
    return 2.0 ** (-8.0 * (hd + 1) / HEADS)


def _odd_proj_kernel(x_ref, g_ref, w_ref, q_ref, k_ref, v_ref, *, steps_per_seq):
    tm = v_ref.shape[4]
    n_tiles = x_ref.shape[0] // tm
    i = pl.program_id(0)
    scale = DIFF_HEAD_DIM ** -0.5 * LOG2E
    d = DIFF_HEAD_DIM
    lane = lax.broadcasted_iota(jnp.int32, (tm, LANES), 1)
    q_aug = jnp.where((lane >= d) & (lane < d + 3), 1.0, 0.0)
    hs = [_rms(x_ref[s * tm:(s + 1) * tm, :], g_ref[...]).astype(BF16) for s in range(n_tiles)]
    for s, h in enumerate(hs):
        rows = slice(s * tm, (s + 1) * tm)
        v = jnp.dot(h, w_ref[:, 2 * HEAD_COLS:], preferred_element_type=F32)
        k = jnp.dot(h, w_ref[:, HEAD_COLS:2 * HEAD_COLS], preferred_element_type=F32)
        _store_vt(v_ref, v, DIFF_V_DIM, s)
        q = jnp.dot(h, w_ref[:, :HEAD_COLS], preferred_element_type=F32)
        pos = (((i % steps_per_seq) * n_tiles + s) * tm
               + lax.broadcasted_iota(jnp.int32, (tm, 1), 0)).astype(F32)
        for hd in range(HEADS):
            sl = slice(hd * LANES, (hd + 1) * LANES)
            b_hi, b_mid, b_lo = (p.astype(F32)
                                 for p in _split3(pos * (_alibi_slope(hd) * LOG2E)))
            k_aug = jnp.where(lane == d, b_hi,
                              jnp.where(lane == d + 1, b_mid, jnp.where(lane == d + 2, b_lo, 0.0)))
            qb, kb = q[:, sl] * scale, k[:, sl]
            for br, (qs, ks) in enumerate(((qb, kb),
                                           (pltpu.roll(qb, d, 1), pltpu.roll(kb, d, 1)))):
                so = slice((2 * hd + br) * LANES, (2 * hd + br + 1) * LANES)
                q_ref[rows, so] = jnp.where(lane < d, qs, q_aug).astype(BF16)
                k_ref[rows, so] = jnp.where(lane < d, ks, k_aug).astype(BF16)


def _odd_proj(x, g, w_in, bsz, seq):
    t = x.shape[0]
    tk = ATTN_TILE
    tm = tk * PROJ_TILES_PER_STEP
    w = w_in.astype(BF16)
    steps_per_seq = seq // tm
    return pl.pallas_call(
        functools.partial(_odd_proj_kernel, steps_per_seq=steps_per_seq),
        grid=(t // tm,),
        in_specs=[pl.BlockSpec((tm, D_MODEL), lambda i: (i, 0)),
                  _const_spec((1, D_MODEL)), _const_spec(w.shape)],
        out_specs=[pl.BlockSpec((tm, 2 * HEAD_COLS), lambda i: (i, 0)),
                   pl.BlockSpec((tm, 2 * HEAD_COLS), lambda i: (i, 0)),
                   _vt_spec(DIFF_V_DIM, PROJ_TILES_PER_STEP, tk, steps_per_seq)],
        out_shape=[jax.ShapeDtypeStruct((t, 2 * HEAD_COLS), BF16),
                   jax.ShapeDtypeStruct((t, 2 * HEAD_COLS), BF16),
                   jax.ShapeDtypeStruct((bsz, HEADS, seq // tk, _vt_rows(DIFF_V_DIM), tk), BF16)],
        compiler_params=_params("arbitrary"),
        name="odd_proj",
    )(x, g.reshape(1, -1), w)


_NT = (((1,), (1,)), ((), ()))


def _attend(s, vt, m, acc, causal):
    if not causal:
        m_new = jnp.maximum(m, jnp.max(s, axis=0, keepdims=True))
        p = jnp.exp2(s - m_new).astype(BF16)
    else:
        nr, nc = s.shape[0] // LANES, s.shape[1] // LANES
        key = lax.broadcasted_iota(jnp.int32, (LANES, LANES), 0)
        qry = lax.broadcasted_iota(jnp.int32, (LANES, LANES), 1)
        sq = [[s[a * LANES:(a + 1) * LANES, b * LANES:(b + 1) * LANES] for b in range(nc)]
              for a in range(nr)]
        for a in range(nr):
            sq[a][a] = jnp.where(key <= qry, sq[a][a], NEG_INF)
        col_max = [functools.reduce(jnp.maximum, [jnp.max(sq[a][b], axis=0, keepdims=True)
                                                  for a in range(min(b + 1, nr))])
                   for b in range(nc)]
        m_new = jnp.maximum(m, jnp.concatenate(col_max, axis=1))
        p = jnp.concatenate(
            [jnp.concatenate(
                [jnp.exp2(sq[a][b] - m_new[:, b * LANES:(b + 1) * LANES]).astype(BF16)
                 if a <= b else jnp.zeros((LANES, LANES), BF16) for b in range(nc)], axis=1)
             for a in range(nr)], axis=0)
    acc = jnp.exp2(m - m_new) * acc + jnp.dot(vt, p, preferred_element_type=F32)
    return m_new, acc


def _stream_attention(qk, vts, finish, q_tiles, t, tk, rows, n, lag):
    r = t // tk
    items = [(qi, j, u, k) for qi in range(q_tiles) for j in range(qi + 1) for u in range(r)
             for k in range(n)]

    def first_query(qi, j, u):
        return u * tk if j == qi else 0

    pending, issued = [], 0
    state = None
    for i, (qi, j, u, k) in enumerate(items):
        while issued < min(i + lag + 1, len(items)):
            nqi, nj, nu, nk = items[issued]
            pending.append(qk(nk, nqi, nj, nu, first_query(nqi, nj, nu)))
            issued += 1
        if j == 0 and u == 0 and k == 0:
            state = [(jnp.full((1, t), NEG_INF, F32), jnp.zeros((rows, t), F32))] * n
        q0 = first_query(qi, j, u)
        m, acc = state[k]
        m_hi, acc_hi = _attend(pending.pop(0), vts(k, j, u), m[:, q0:], acc[:, q0:], j == qi)
        state[k] = ((jnp.concatenate([m[:, :q0], m_hi], axis=1),
                     jnp.concatenate([acc[:, :q0], acc_hi], axis=1)) if q0 else (m_hi, acc_hi))
        if j == qi and u == r - 1 and k == n - 1:
            finish(qi, [acc for _, acc in state])


def _flash_kernel(q_ref, k_ref, vt_ref, o_ref):
    n, chunks, rows, t = vt_ref.shape[1:]
    tk = FLASH_KEYS
    dv = rows - BF16_ROWS

    def qk(hd, qi, j, u, q0):
        kc = k_ref[0, j * t + u * tk:j * t + (u + 1) * tk, hd * LANES:(hd + 1) * LANES]
        q = q_ref[0, qi * t + q0:(qi + 1) * t, hd * LANES:(hd + 1) * LANES]
        return lax.dot_general(kc, q, _NT, preferred_element_type=F32)

    def vts(hd, j, u):
        return vt_ref[0, hd, j, :, u * tk:(u + 1) * tk]

    def finish(qi, accs):
        o_t = jnp.concatenate([a[:dv] / a[dv:dv + 1] for a in accs], axis=0)
        o_ref[0, qi * t:(qi + 1) * t, :] = o_t.T.astype(BF16)

    _stream_attention(qk, vts, finish, chunks, t, tk, rows, n, FLASH_LAG)


def _flash_attention(q, k, vt):
    b, s, _ = q.shape
    _, _, chunks, rows, t = vt.shape
    n = FLASH_HEADS_PER_STEP
    dv = rows - BF16_ROWS
    return pl.pallas_call(
        _flash_kernel,
        grid=(b, HEADS // n),
        in_specs=[pl.BlockSpec((1, s, n * LANES), lambda bi, hp: (bi, 0, hp)),
                  pl.BlockSpec((1, s, n * LANES), lambda bi, hp: (bi, 0, hp)),
                  pl.BlockSpec((1, n, chunks, rows, t), lambda bi, hp: (bi, hp, 0, 0, 0))],
        out_specs=pl.BlockSpec((1, s, n * dv), lambda bi, hp: (bi, 0, hp)),
        out_shape=jax.ShapeDtypeStruct((b, s, HEADS * dv), BF16),
        compiler_params=_params("arbitrary", "arbitrary"),
        name="flash_attention",
    )(q, k, vt)


def _diff_kernel(q_ref, k_ref, vt_ref, lq1_ref, lk1_ref, lq2_ref, lk2_ref, sub_ref, o_ref,
                 *, lambda_init):
    n, chunks, rows, t = vt_ref.shape[1:]
    tk = DIFF_KEYS
    dv = rows - BF16_ROWS

    def qk(st, qi, j, u, q0):
        kc = k_ref[0, j * t + u * tk:j * t + (u + 1) * tk, st * LANES:(st + 1) * LANES]
        q = q_ref[0, qi * t + q0:(qi + 1) * t, st * LANES:(st + 1) * LANES]
        return lax.dot_general(kc, q, _NT, preferred_element_type=F32)

    def vts(st, j, u):
        return vt_ref[0, st // 2, j, :, u * tk:(u + 1) * tk]

    lam = (jnp.exp(jnp.sum(lq1_ref[...] * lk1_ref[...], axis=1, keepdims=True))
           - jnp.exp(jnp.sum(lq2_ref[...] * lk2_ref[...], axis=1, keepdims=True)) + lambda_init)

    def finish(qi, accs):
        for hd in range(n):
            a1, a2 = accs[2 * hd], accs[2 * hd + 1]
            o = (a1[:dv] / a1[dv:dv + 1] - lam * (a2[:dv] / a2[dv:dv + 1])).T
            o_ref[0, qi * t:(qi + 1) * t, hd * dv:(hd + 1) * dv] = (
                _rms(o, sub_ref[...]) * (1.0 - lambda_init)).astype(BF16)

    _stream_attention(qk, vts, finish, chunks, t, tk, rows, 2 * n, DIFF_LAG)


def _diff_attention(q, k, vt, lq1, lk1, lq2, lk2, subln, lambda_init):
    b, s, _ = q.shape
    _, _, chunks, rows, t = vt.shape
    vec = lambda a: a.reshape(1, -1)
    n = DIFF_HEADS_PER_STEP
    return pl.pallas_call(
        functools.partial(_diff_kernel, lambda_init=lambda_init),
        grid=(b, HEADS // n),
        in_specs=[pl.BlockSpec((1, s, 2 * n * LANES), lambda bi, hd: (bi, 0, hd)),
                  pl.BlockSpec((1, s, 2 * n * LANES), lambda bi, hd: (bi, 0, hd)),
                  pl.BlockSpec((1, n, chunks, rows, t), lambda bi, hd: (bi, hd, 0, 0, 0)),
                  _const_spec((1, DIFF_HEAD_DIM)), _const_spec((1, DIFF_HEAD_DIM)),
                  _const_spec((1, DIFF_HEAD_DIM)), _const_spec((1, DIFF_HEAD_DIM)),
                  _const_spec((1, DIFF_V_DIM))],
        out_specs=pl.BlockSpec((1, s, n * LANES), lambda bi, hd: (bi, 0, hd)),
        out_shape=jax.ShapeDtypeStruct((b, s, HEADS * (rows - BF16_ROWS)), BF16),
        compiler_params=_params("arbitrary", "arbitrary"),
        name="diff_attention",
    )(q, k, vt, vec(lq1), vec(lk1), vec(lq2), vec(lk2), vec(subln))


_N_FFN_CHUNKS = D_FF // FFN_CHUNK
_GELU_C = math.sqrt(2.0 / math.pi)


def _mix_ffn_kernel(*refs, n_in, steps_per_seq):
    a_refs, wo_refs = refs[:n_in], refs[n_in:2 * n_in]
    (gmix_ref, x_ref, gpre_ref, wup_ref, cw_ref, wdn_ref, gpost_ref, o_ref,
     h_ref, hist_ref, gva_ref, gvb_ref, *mid_refs) = refs[2 * n_in:]
    tm = h_ref.shape[0]
    nc = _N_FFN_CHUNKS
    i = pl.program_id(0)

    @pl.when(i % steps_per_seq == 0)
    def _():
        hist_ref[...] = jnp.zeros_like(hist_ref)

    def out_proj(rows):
        mix = jnp.dot(a_refs[0][rows, :], wo_refs[0][...], preferred_element_type=F32)
        for a_ref, wo_ref in zip(a_refs[1:], wo_refs[1:]):
            mix = mix + jnp.dot(a_ref[rows, :], wo_ref[...], preferred_element_type=F32)
        return mix

    def pre_norms(rows, mix):
        x1 = x_ref[rows, :] + _rms(mix, gmix_ref[...])
        o_ref[rows, :] = x1
        h_ref[...] = _rms(x1, gpre_ref[...]).astype(BF16)

    def up(c):
        return jnp.dot(h_ref[...], wup_ref[c], preferred_element_type=F32)

    def down(mid_ref):
        f = jnp.dot(mid_ref[0], wdn_ref[0], preferred_element_type=F32)
        for c in range(1, nc):
            f = f + jnp.dot(mid_ref[c], wdn_ref[c], preferred_element_type=F32)
        return f

    def gated(c, gv_ref, mid_ref):
        gv_ref[0:SUBLANES, :FFN_CHUNK] = hist_ref[c]
        gate = gv_ref[SUBLANES:, :FFN_CHUNK]
        val = gv_ref[SUBLANES:, FFN_CHUNK:]
        hist_ref[c] = gv_ref[tm:, :FFN_CHUNK]
        cw = cw_ref[c]
        pre = (gv_ref[SUBLANES - 2:SUBLANES - 2 + tm, :FFN_CHUNK] * cw[0:1, :]
               + gv_ref[SUBLANES - 1:SUBLANES - 1 + tm, :FFN_CHUNK] * cw[1:2, :]
               + gate * cw[2:3, :] + cw[3:4, :])
        pre = pre.astype(BF16)
        act = 0.5 * pre * (1.0 + jnp.tanh(_GELU_C * (pre + 0.044715 * (pre * pre * pre))))
        mid_ref[c] = act * val.astype(BF16)

    def chunks_after_first(mid_ref):
        def body(k, _):
            gated(2 * k, gva_ref, mid_ref)
            gvb_ref[SUBLANES:, :] = up(2 * k + 1)
            gated(2 * k + 1, gvb_ref, mid_ref)
            gva_ref[SUBLANES:, :] = up(2 * k + 2)
            return 0

        for k in range(nc // 2):
            body(k, 0)
        gated(nc - 1, gva_ref, mid_ref)

    assert nc % 2 == 1
    n_tiles = len(mid_refs)
    rows = [slice(s * tm, (s + 1) * tm) for s in range(n_tiles)]
    pre_norms(rows[0], out_proj(rows[0]))
    gva_ref[SUBLANES:, :] = up(0)
    for s in range(n_tiles):
        chunks_after_first(mid_refs[s])
        if s + 1 < n_tiles:
            mix_next = out_proj(rows[s + 1])
        f = down(mid_refs[s])
        if s + 1 < n_tiles:
            pre_norms(rows[s + 1], mix_next)
            gva_ref[SUBLANES:, :] = up(0)
        o_ref[rows[s], :] = o_ref[rows[s], :] + _rms(f, gpost_ref[...])


def _mix_ffn(acts, w_outs, g_mix, x, g_pre, w_up, conv_w, conv_b, w_down, g_post, seq):
    t = x.shape[0]
    tm = FFN_ROWS
    nc = _N_FFN_CHUNKS
    n_in = len(acts)
    wg = w_up[:, :D_FF].reshape(D_MODEL, nc, FFN_CHUNK)
    wv = w_up[:, D_FF:].reshape(D_MODEL, nc, FFN_CHUNK)
    wup = jnp.concatenate([wg, wv], axis=2).transpose(1, 0, 2).astype(BF16)
    wdn = w_down.reshape(nc, FFN_CHUNK, D_MODEL).astype(BF16)
    cw = jnp.concatenate([conv_w, conv_b[None, :],
                          jnp.zeros((SUBLANES - CONV_WIDTH - 1, D_FF), F32)], axis=0)
    cw = cw.reshape(SUBLANES, nc, FFN_CHUNK).transpose(1, 0, 2)
    step_rows = tm * FFN_TILES_PER_STEP
    assert seq % step_rows == 0
    x_spec = pl.BlockSpec((step_rows, D_MODEL), lambda i: (i, 0))
    vec_spec = _const_spec((1, D_MODEL))
    return pl.pallas_call(
        functools.partial(_mix_ffn_kernel, n_in=n_in, steps_per_seq=seq // step_rows),
        grid=(t // step_rows,),
        in_specs=([pl.BlockSpec((step_rows, a.shape[1]), lambda i: (i, 0)) for a in acts]
                  + [_const_spec(w.shape) for w in w_outs]
                  + [vec_spec, x_spec, vec_spec, _const_spec(wup.shape), _const_spec(cw.shape),
                     _const_spec(wdn.shape), vec_spec]),
        out_specs=x_spec,
        out_shape=jax.ShapeDtypeStruct(x.shape, F32),
        scratch_shapes=([pltpu.VMEM((tm, D_MODEL), BF16),
                         pltpu.VMEM((nc, SUBLANES, FFN_CHUNK), F32),
                         pltpu.VMEM((tm + SUBLANES, 2 * FFN_CHUNK), F32),
                         pltpu.VMEM((tm + SUBLANES, 2 * FFN_CHUNK), F32)]
                        + [pltpu.VMEM((nc, tm, FFN_CHUNK), BF16)] * FFN_TILES_PER_STEP),
        compiler_params=_params("arbitrary"),
        name="mix_ffn",
    )(*acts, *w_outs, g_mix.reshape(1, -1), x, g_pre.reshape(1, -1), wup, cw, wdn,
      g_post.reshape(1, -1))


def kernel(x, norm_mix_pre, norm_mix_post, norm_ffn_pre, norm_ffn_post, even_w_in, even_b_forget, even_q_norm, even_w_uq, even_kv_norm, even_w_ukv, even_w_out, odd_w_in, odd_lambda_q1, odd_lambda_k1, odd_lambda_q2, odd_lambda_k2, odd_subln, odd_w_out, ffn_w_up, ffn_conv_w, ffn_conv_b, ffn_w_down):
    bsz, seq, d = x.shape
    assert d == D_MODEL and seq % (ATTN_TILE * PROJ_TILES_PER_STEP) == 0
    t = bsz * seq
    rope = _rope_tables(seq)
    xf = x.reshape(t, d)
    for layer in range(DEPTH):
        i = layer // 2
        if layer % 2 == 0:
            fq, fk, fvt, mq, mk, mvt = _even_proj(
                xf, norm_mix_pre[layer], even_w_in[i], even_b_forget[i], even_q_norm[i],
                even_w_uq[i], even_kv_norm[i], even_w_ukv[i], rope, bsz, seq)
            shp = (bsz, seq, HEAD_COLS)
            fo = _flash_attention(fq.reshape(shp), fk.reshape(shp), fvt)
            mo = _flash_attention(mq.reshape(shp), mk.reshape(shp), mvt)
            w_out = even_w_out[i].astype(BF16)
            acts = [fo.reshape(t, -1), mo.reshape(t, -1)]
            w_outs = [w_out[:FOX_WIDTH], w_out[FOX_WIDTH:]]
        else:
            lambda_init = 0.8 - 0.6 * math.exp(-0.3 * layer)
            q, k, vt = _odd_proj(xf, norm_mix_pre[layer], odd_w_in[i], bsz, seq)
            o = _diff_attention(q.reshape(bsz, seq, -1), k.reshape(bsz, seq, -1), vt,
                                odd_lambda_q1[i], odd_lambda_k1[i],
                                odd_lambda_q2[i], odd_lambda_k2[i], odd_subln[i], lambda_init)
            acts, w_outs = [o.reshape(t, -1)], [odd_w_out[i].astype(BF16)]
        xf = _mix_ffn(acts, w_outs, norm_mix_post[layer], xf, norm_ffn_pre[layer], ffn_w_up[layer],
                      ffn_conv_w[layer], ffn_conv_b[layer], ffn_w_down[layer],
                      norm_ffn_post[layer], seq)
    return xf.reshape(bsz, seq, d)
```

```python
import functools
import math

import numpy as np
import jax
import jax.numpy as jnp
from jax import lax
from jax.experimental import pallas as pl
from jax.experimental.pallas import tpu as pltpu

D_MODEL = 1024
DEPTH = 4
RMS_EPS = 1e-6
NEG_INF = -1e30
HEADS = 8
FOX_HEAD_DIM = 64
FOX_WIDTH = HEADS * FOX_HEAD_DIM
MLA_NOPE_DIM = 64
MLA_ROPE_DIM = 32
MLA_V_DIM = 64
MLA_Q_RANK = 384
MLA_KV_RANK = 256
ROPE_THETA = 10000.0
DIFF_HEAD_DIM = 64
DIFF_V_DIM = 2 * DIFF_HEAD_DIM
D_FF = 2816
CONV_WIDTH = 3

LANES = 128
SUBLANES = 8
BF16_ROWS = 16
LOG2E = 1.4426950408889634
HEAD_COLS = HEADS * LANES
VMEM_LIMIT_BYTES = 56 * 1024 * 1024

PROJ_TILES_PER_STEP = 2
FFN_ROWS = 512
FFN_TILES_PER_STEP = 2
FFN_CHUNK = 256
ATTN_TILE = 512
FLASH_HEADS_PER_STEP, FLASH_KEYS, FLASH_LAG = 2, 256, 2
DIFF_HEADS_PER_STEP, DIFF_KEYS, DIFF_LAG = 1, 256, 2
F32 = jnp.float32
BF16 = jnp.bfloat16


def _params(*sem):
    return pltpu.CompilerParams(dimension_semantics=sem, vmem_limit_bytes=VMEM_LIMIT_BYTES)


def _const_spec(shape):
    nd = len(shape)
    return pl.BlockSpec(shape, lambda *_: (0,) * nd, pipeline_mode=pl.Buffered(1))


def _rms(x, g):
    return x * lax.rsqrt(jnp.mean(x * x, axis=-1, keepdims=True) + RMS_EPS) * g


def _split3(x):
    hi = x.astype(BF16)
    r1 = x - hi.astype(F32)
    mid = r1.astype(BF16)
    lo = (r1 - mid.astype(F32)).astype(BF16)
    return hi, mid, lo


def _rope(blk, ra, rb, rc):
    return blk * ra + pltpu.roll(blk, 16, 1) * rb + pltpu.roll(blk, LANES - 16, 1) * rc


def _vt_rows(dv):
    return dv + BF16_ROWS


def _store_vt(vt_ref, v, dv, chunk):
    tk = v.shape[0]
    per = LANES // dv
    for blk in range(v.shape[1] // LANES):
        vt = v[:, blk * LANES:(blk + 1) * LANES].T.astype(BF16)
        for s in range(per):
            vt_ref[0, blk * per + s, chunk, 0:dv, :] = vt[s * dv:(s + 1) * dv, :]
    row = lax.broadcasted_iota(jnp.int32, (BF16_ROWS, tk), 0)
    ones_row = jnp.where(row == 0, 1.0, 0.0).astype(BF16)
    for hd in range(HEADS):
        vt_ref[0, hd, chunk, dv:dv + BF16_ROWS, :] = ones_row


def _vt_spec(dv, chunks_per_step, tk, steps_per_seq):
    return pl.BlockSpec((1, HEADS, chunks_per_step, _vt_rows(dv), tk),
                        lambda i: (i // steps_per_seq, 0, i % steps_per_seq, 0, 0))


_EV_Q, _EV_K, _EV_V = 0, FOX_WIDTH, 2 * FOX_WIDTH
_EV_G = _EV_V + FOX_WIDTH
_EV_CQ = _EV_G + LANES
_EV_CKV = _EV_CQ + MLA_Q_RANK
_EV_KR = _EV_CKV + MLA_KV_RANK
_EV_WIDTH = _EV_KR + LANES


def _even_proj_kernel(x_ref, g_ref, w_ref, bf_ref, qn_ref, wuq_ref, kvn_ref, wuk_ref, wuv_ref,
                      ec_ref, ra_ref, rb_ref, rc_ref, cst_ref,
                      fq_ref, fk_ref, fv_ref, mq_ref, mk_ref, mv_ref, carry_ref,
                      *, steps_per_seq):
    tm = fv_ref.shape[4]
    n_tiles = x_ref.shape[0] // tm
    i = pl.program_id(0)

    @pl.when(i % steps_per_seq == 0)
    def _():
        carry_ref[...] = jnp.zeros_like(carry_ref)

    hs = [_rms(x_ref[s * tm:(s + 1) * tm, :], g_ref[...]).astype(BF16) for s in range(n_tiles)]
    for s, h in enumerate(hs):
        _even_proj_tile(h, s, slice(s * tm, (s + 1) * tm), w_ref, bf_ref, qn_ref, wuq_ref, kvn_ref,
                        wuk_ref, wuv_ref, ec_ref, ra_ref, rb_ref, rc_ref, cst_ref,
                        fq_ref, fk_ref, fv_ref, mq_ref, mk_ref, mv_ref, carry_ref)


def _even_proj_tile(h, s, rows, w_ref, bf_ref, qn_ref, wuq_ref, kvn_ref, wuk_ref, wuv_ref,
                    ec_ref, ra_ref, rb_ref, rc_ref, cst_ref,
                    fq_ref, fk_ref, fv_ref, mq_ref, mk_ref, mv_ref, carry_ref):
    tm = h.shape[0]
    q_aug = cst_ref[0:1, :]

    def proj(lo, hi):
        return jnp.dot(h, w_ref[:, lo:hi], preferred_element_type=F32)

    lane = lax.broadcasted_iota(jnp.int32, (tm, LANES), 1)

    def spread_heads(o_ref, packed, aug):
        for pair in range(HEADS // 2):
            blk = packed[:, pair * LANES:(pair + 1) * LANES]
            for odd, src in enumerate((blk, pltpu.roll(blk, FOX_HEAD_DIM, 1))):
                sl = slice((2 * pair + odd) * LANES, (2 * pair + odd + 1) * LANES)
                o_ref[rows, sl] = jnp.where(lane < FOX_HEAD_DIM, src, aug[:, sl]).astype(BF16)

    ra, rb, rc = ra_ref[rows, :], rb_ref[rows, :], rc_ref[rows, :]
    fox_scale = FOX_HEAD_DIM ** -0.5 * LOG2E
    mla_scale = (MLA_NOPE_DIM + MLA_ROPE_DIM) ** -0.5 * LOG2E

    z = proj(_EV_G, _EV_CQ) + bf_ref[...]
    cq_raw = proj(_EV_CQ, _EV_CKV)
    ckv_raw = proj(_EV_CKV, _EV_KR)
    kr_raw = proj(_EV_KR, _EV_WIDTH)

    logf = (jnp.minimum(z, 0.0) - jnp.log1p(jnp.exp(-jnp.abs(z)))) * LOG2E
    spread_heads(fq_ref, proj(_EV_Q, _EV_K) * fox_scale, q_aug)

    row = lax.broadcasted_iota(jnp.int32, (tm, tm), 0)
    col = lax.broadcasted_iota(jnp.int32, (tm, tm), 1)
    tril = jnp.where(row >= col, 1.0, 0.0).astype(BF16)
    parts = jnp.dot(tril, jnp.concatenate(_split3(logf), axis=1), preferred_element_type=F32)
    _store_vt(fv_ref, proj(_EV_V, _EV_G), FOX_HEAD_DIM, s)
    c = (parts[:, :LANES] + parts[:, LANES:2 * LANES] + parts[:, 2 * LANES:]
         + carry_ref[SUBLANES - 1:SUBLANES, :])
    carry_ref[...] = c[tm - SUBLANES:, :]

    cq = _rms(cq_raw, qn_ref[...]).astype(BF16)
    q = jnp.dot(cq, wuq_ref[...], preferred_element_type=F32) * mla_scale
    for hd in range(HEADS):
        sl = slice(hd * LANES, (hd + 1) * LANES)
        mq_ref[rows, sl] = _rope(q[:, sl], ra, rb, rc).astype(BF16)

    ckv = _rms(ckv_raw, kvn_ref[...]).astype(BF16)
    kr = _rope(kr_raw, ra, rb, rc)
    kn = jnp.dot(ckv, wuk_ref[...], preferred_element_type=F32)
    for hd in range(HEADS):
        sl = slice(hd * LANES, (hd + 1) * LANES)
        mk_ref[rows, sl] = (kn[:, sl] + kr).astype(BF16)
    _store_vt(mv_ref, jnp.dot(ckv, wuv_ref[...], preferred_element_type=F32), MLA_V_DIM, s)

    c_aug = jnp.dot(jnp.concatenate(_split3(c), axis=1), ec_ref[...], preferred_element_type=F32)
    spread_heads(fk_ref, proj(_EV_K, _EV_V), c_aug)


def _pad_heads(w, dh):
    k = w.shape[0]
    w = w.reshape(k, HEADS, dh)
    return jnp.pad(w, ((0, 0), (0, 0), (0, LANES - dh))).reshape(k, HEAD_COLS)


def _even_consts():
    cst = np.zeros((SUBLANES, HEAD_COLS), np.float32)
    ec = np.zeros((3 * LANES, HEAD_COLS), np.float32)
    for hd in range(HEADS):
        cst[0, hd * LANES + 64: hd * LANES + 67] = 1.0
        for p in range(3):
            ec[p * LANES + hd, hd * LANES + 64 + p] = -1.0
    return jnp.asarray(cst), jnp.asarray(ec, dtype=BF16)


def _rope_tables(seq):
    half = MLA_ROPE_DIM // 2
    inv = ROPE_THETA ** (-jnp.arange(0, MLA_ROPE_DIM, 2, dtype=F32) / MLA_ROPE_DIM)
    ang = jnp.arange(seq, dtype=F32)[:, None] * inv[None, :]
    cos, sin = jnp.cos(ang), jnp.sin(ang)
    zeros = jnp.zeros((seq, half), F32)
    ra = jnp.concatenate([jnp.ones((seq, 64), F32), cos, cos, jnp.zeros((seq, 32), F32)], axis=1)
    rb = jnp.concatenate([jnp.zeros((seq, 64), F32), zeros, sin, jnp.zeros((seq, 32), F32)], axis=1)
    rc = jnp.concatenate([jnp.zeros((seq, 64), F32), -sin, zeros, jnp.zeros((seq, 32), F32)], axis=1)
    return ra, rb, rc


def _even_proj(x, g, w_in, b_forget, q_norm, w_uq, kv_norm, w_ukv, rope, bsz, seq):
    t = x.shape[0]
    tk = ATTN_TILE
    tm = tk * PROJ_TILES_PER_STEP
    cuts = np.cumsum([FOX_WIDTH, FOX_WIDTH, FOX_WIDTH, HEADS, MLA_Q_RANK, MLA_KV_RANK]).tolist()
    wfq, wfk, wfv, wg, wcq, wckv, wkr = jnp.split(w_in, cuts, axis=1)
    w1 = jnp.concatenate([
        wfq, wfk, wfv,
        jnp.pad(wg, ((0, 0), (0, LANES - HEADS))), wcq, wckv,
        jnp.pad(wkr, ((0, 0), (64, LANES - 64 - MLA_ROPE_DIM)))], axis=1).astype(BF16)
    assert w1.shape[1] == _EV_WIDTH
    bf = jnp.pad(b_forget, (0, LANES - HEADS)).reshape(1, LANES)
    wuq = _pad_heads(w_uq, MLA_NOPE_DIM + MLA_ROPE_DIM).astype(BF16)
    wukv = w_ukv.reshape(MLA_KV_RANK, HEADS, MLA_NOPE_DIM + MLA_V_DIM)
    wuk = _pad_heads(wukv[:, :, :MLA_NOPE_DIM].reshape(MLA_KV_RANK, -1), MLA_NOPE_DIM).astype(BF16)
    wuv = wukv[:, :, MLA_NOPE_DIM:].reshape(MLA_KV_RANK, -1).astype(BF16)
    cst, ec = _even_consts()
    ra, rb, rc = rope
    steps_per_seq = seq // tm
    rope_spec = pl.BlockSpec((tm, LANES), lambda i: (i % steps_per_seq, 0))
    row_spec = pl.BlockSpec((tm, HEAD_COLS), lambda i: (i, 0))
    out_sds = jax.ShapeDtypeStruct((t, HEAD_COLS), BF16)
    vt_spec = _vt_spec(FOX_HEAD_DIM, PROJ_TILES_PER_STEP, tk, steps_per_seq)
    vt_sds = jax.ShapeDtypeStruct((bsz, HEADS, seq // tk, _vt_rows(FOX_HEAD_DIM), tk), BF16)
    return pl.pallas_call(
        functools.partial(_even_proj_kernel, steps_per_seq=steps_per_seq),
        grid=(t // tm,),
        in_specs=[pl.BlockSpec((tm, D_MODEL), lambda i: (i, 0)),
                  _const_spec((1, D_MODEL)), _const_spec(w1.shape), _const_spec((1, LANES)),
                  _const_spec((1, MLA_Q_RANK)), _const_spec(wuq.shape),
                  _const_spec((1, MLA_KV_RANK)), _const_spec(wuk.shape), _const_spec(wuv.shape),
                  _const_spec(ec.shape), rope_spec, rope_spec, rope_spec, _const_spec(cst.shape)],
        out_specs=[row_spec, row_spec, vt_spec, row_spec, row_spec, vt_spec],
        out_shape=[out_sds, out_sds, vt_sds, out_sds, out_sds, vt_sds],
        scratch_shapes=[pltpu.VMEM((SUBLANES, LANES), F32)],
        compiler_params=_params("arbitrary"),
        name="even_proj",
    )(x, g.reshape(1, -1), w1, bf, q_norm.reshape(1, -1), wuq, kv_norm.reshape(1, -1), wuk, wuv,
      ec, ra, rb, rc, cst)


def _alibi_slope(hd):
    return 2.0 ** (-8.0 * (hd + 1) / HEADS)


def _odd_proj_kernel(x_ref, g_ref, w_ref, q_ref, k_ref, v_ref, *, steps_per_seq):
    tm = v_ref.shape[4]
    n_tiles = x_ref.shape[0] // tm
    i = pl.program_id(0)
    scale = DIFF_HEAD_DIM ** -0.5 * LOG2E
    d = DIFF_HEAD_DIM
    lane = lax.broadcasted_iota(jnp.int32, (tm, LANES), 1)
    q_aug = jnp.where((lane >= d) & (lane < d + 3), 1.0, 0.0)
    hs = [_rms(x_ref[s * tm:(s + 1) * tm, :], g_ref[...]).astype(BF16) for s in range(n_tiles)]
    for s, h in enumerate(hs):
        rows = slice(s * tm, (s + 1) * tm)
        v = jnp.dot(h, w_ref[:, 2 * HEAD_COLS:], preferred_element_type=F32)
        k = jnp.dot(h, w_ref[:, HEAD_COLS:2 * HEAD_COLS], preferred_element_type=F32)
        _store_vt(v_ref, v, DIFF_V_DIM, s)
        q = jnp.dot(h, w_ref[:, :HEAD_COLS], preferred_element_type=F32)
        pos = (((i % steps_per_seq) * n_tiles + s) * tm
               + lax.broadcasted_iota(jnp.int32, (tm, 1), 0)).astype(F32)
        for hd in range(HEADS):
            sl = slice(hd * LANES, (hd + 1) * LANES)
            b_hi, b_mid, b_lo = (p.astype(F32)
                                 for p in _split3(pos * (_alibi_slope(hd) * LOG2E)))
            k_aug = jnp.where(lane == d, b_hi,
                              jnp.where(lane == d + 1, b_mid, jnp.where(lane == d + 2, b_lo, 0.0)))
            qb, kb = q[:, sl] * scale, k[:, sl]
            for br, (qs, ks) in enumerate(((qb, kb),
                                           (pltpu.roll(qb, d, 1), pltpu.roll(kb, d, 1)))):
                so = slice((2 * hd + br) * LANES, (2 * hd + br + 1) * LANES)
                q_ref[rows, so] = jnp.where(lane < d, qs, q_aug).astype(BF16)
                k_ref[rows, so] = jnp.where(lane < d, ks, k_aug).astype(BF16)


def _odd_proj(x, g, w_in, bsz, seq):
    t = x.shape[0]
    tk = ATTN_TILE
    tm = tk * PROJ_TILES_PER_STEP
    w = w_in.astype(BF16)
    steps_per_seq = seq // tm
    return pl.pallas_call(
        functools.partial(_odd_proj_kernel, steps_per_seq=steps_per_seq),
        grid=(t // tm,),
        in_specs=[pl.BlockSpec((tm, D_MODEL), lambda i: (i, 0)),
                  _const_spec((1, D_MODEL)), _const_spec(w.shape)],
        out_specs=[pl.BlockSpec((tm, 2 * HEAD_COLS), lambda i: (i, 0)),
                   pl.BlockSpec((tm, 2 * HEAD_COLS), lambda i: (i, 0)),
                   _vt_spec(DIFF_V_DIM, PROJ_TILES_PER_STEP, tk, steps_per_seq)],
        out_shape=[jax.ShapeDtypeStruct((t, 2 * HEAD_COLS), BF16),
                   jax.ShapeDtypeStruct((t, 2 * HEAD_COLS), BF16),
                   jax.ShapeDtypeStruct((bsz, HEADS, seq // tk, _vt_rows(DIFF_V_DIM), tk), BF16)],
        compiler_params=_params("arbitrary"),
        name="odd_proj",
    )(x, g.reshape(1, -1), w)


_NT = (((1,), (1,)), ((), ()))


def _attend(s, vt, m, acc, causal):
    if not causal:
        m_new = jnp.maximum(m, jnp.max(s, axis=0, keepdims=True))
        p = jnp.exp2(s - m_new).astype(BF16)
    else:
        nr, nc = s.shape[0] // LANES, s.shape[1] // LANES
        key = lax.broadcasted_iota(jnp.int32, (LANES, LANES), 0)
        qry = lax.broadcasted_iota(jnp.int32, (LANES, LANES), 1)
        sq = [[s[a * LANES:(a + 1) * LANES, b * LANES:(b + 1) * LANES] for b in range(nc)]
              for a in range(nr)]
        for a in range(nr):
            sq[a][a] = jnp.where(key <= qry, sq[a][a], NEG_INF)
        col_max = [functools.reduce(jnp.maximum, [jnp.max(sq[a][b], axis=0, keepdims=True)
                                                  for a in range(min(b + 1, nr))])
                   for b in range(nc)]
        m_new = jnp.maximum(m, jnp.concatenate(col_max, axis=1))
        p = jnp.concatenate(
            [jnp.concatenate(
                [jnp.exp2(sq[a][b] - m_new[:, b * LANES:(b + 1) * LANES]).astype(BF16)
                 if a <= b else jnp.zeros((LANES, LANES), BF16) for b in range(nc)], axis=1)
             for a in range(nr)], axis=0)
    acc = jnp.exp2(m - m_new) * acc + jnp.dot(vt, p, preferred_element_type=F32)
    return m_new, acc


def _stream_attention(qk, vts, finish, q_tiles, t, tk, rows, n, lag):
    r = t // tk
    items = [(qi, j, u, k) for qi in range(q_tiles) for j in range(qi + 1) for u in range(r)
             for k in range(n)]

    def first_query(qi, j, u):
        return u * tk if j == qi else 0

    pending, issued = [], 0
    state = None
    for i, (qi, j, u, k) in enumerate(items):
        while issued < min(i + lag + 1, len(items)):
            nqi, nj, nu, nk = items[issued]
            pending.append(qk(nk, nqi, nj, nu, first_query(nqi, nj, nu)))
            issued += 1
        if j == 0 and u == 0 and k == 0:
            state = [(jnp.full((1, t), NEG_INF, F32), jnp.zeros((rows, t), F32))] * n
        q0 = first_query(qi, j, u)
        m, acc = state[k]
        m_hi, acc_hi = _attend(pending.pop(0), vts(k, j, u), m[:, q0:], acc[:, q0:], j == qi)
        state[k] = ((jnp.concatenate([m[:, :q0], m_hi], axis=1),
                     jnp.concatenate([acc[:, :q0], acc_hi], axis=1)) if q0 else (m_hi, acc_hi))
        if j == qi and u == r - 1 and k == n - 1:
            finish(qi, [acc for _, acc in state])


def _flash_kernel(q_ref, k_ref, vt_ref, o_ref):
    n, chunks, rows, t = vt_ref.shape[1:]
    tk = FLASH_KEYS
    dv = rows - BF16_ROWS

    def qk(hd, qi, j, u, q0):
        kc = k_ref[0, j * t + u * tk:j * t + (u + 1) * tk, hd * LANES:(hd + 1) * LANES]
        q = q_ref[0, qi * t + q0:(qi + 1) * t, hd * LANES:(hd + 1) * LANES]
        return lax.dot_general(kc, q, _NT, preferred_element_type=F32)

    def vts(hd, j, u):
        return vt_ref[0, hd, j, :, u * tk:(u + 1) * tk]

    def finish(qi, accs):
        o_t = jnp.concatenate([a[:dv] / a[dv:dv + 1] for a in accs], axis=0)
        o_ref[0, qi * t:(qi + 1) * t, :] = o_t.T.astype(BF16)

    _stream_attention(qk, vts, finish, chunks, t, tk, rows, n, FLASH_LAG)


def _flash_attention(q, k, vt):
    b, s, _ = q.shape
    _, _, chunks, rows, t = vt.shape
    n = FLASH_HEADS_PER_STEP
    dv = rows - BF16_ROWS
    return pl.pallas_call(
        _flash_kernel,
        grid=(b, HEADS // n),
        in_specs=[pl.BlockSpec((1, s, n * LANES), lambda bi, hp: (bi, 0, hp)),
                  pl.BlockSpec((1, s, n * LANES), lambda bi, hp: (bi, 0, hp)),
                  pl.BlockSpec((1, n, chunks, rows, t), lambda bi, hp: (bi, hp, 0, 0, 0))],
        out_specs=pl.BlockSpec((1, s, n * dv), lambda bi, hp: (bi, 0, hp)),
        out_shape=jax.ShapeDtypeStruct((b, s, HEADS * dv), BF16),
        compiler_params=_params("arbitrary", "arbitrary"),
        name="flash_attention",
    )(q, k, vt)


def _diff_kernel(q_ref, k_ref, vt_ref, lq1_ref, lk1_ref, lq2_ref, lk2_ref, sub_ref, o_ref,
                 *, lambda_init):
    n, chunks, rows, t = vt_ref.shape[1:]
    tk = DIFF_KEYS
    dv = rows - BF16_ROWS

    def qk(st, qi, j, u, q0):
        kc = k_ref[0, j * t + u * tk:j * t + (u + 1) * tk, st * LANES:(st + 1) * LANES]
        q = q_ref[0, qi * t + q0:(qi + 1) * t, st * LANES:(st + 1) * LANES]
        return lax.dot_general(kc, q, _NT, preferred_element_type=F32)

    def vts(st, j, u):
        return vt_ref[0, st // 2, j, :, u * tk:(u + 1) * tk]

    lam = (jnp.exp(jnp.sum(lq1_ref[...] * lk1_ref[...], axis=1, keepdims=True))
           - jnp.exp(jnp.sum(lq2_ref[...] * lk2_ref[...], axis=1, keepdims=True)) + lambda_init)

    def finish(qi, accs):
        for hd in range(n):
            a1, a2 = accs[2 * hd], accs[2 * hd + 1]
            o = (a1[:dv] / a1[dv:dv + 1] - lam * (a2[:dv] / a2[dv:dv + 1])).T
            o_ref[0, qi * t:(qi + 1) * t, hd * dv:(hd + 1) * dv] = (
                _rms(o, sub_ref[...]) * (1.0 - lambda_init)).astype(BF16)

    _stream_attention(qk, vts, finish, chunks, t, tk, rows, 2 * n, DIFF_LAG)


def _diff_attention(q, k, vt, lq1, lk1, lq2, lk2, subln, lambda_init):
    b, s, _ = q.shape
    _, _, chunks, rows, t = vt.shape
    vec = lambda a: a.reshape(1, -1)
    n = DIFF_HEADS_PER_STEP
    return pl.pallas_call(
        functools.partial(_diff_kernel, lambda_init=lambda_init),
        grid=(b, HEADS // n),
        in_specs=[pl.BlockSpec((1, s, 2 * n * LANES), lambda bi, hd: (bi, 0, hd)),
                  pl.BlockSpec((1, s, 2 * n * LANES), lambda bi, hd: (bi, 0, hd)),
                  pl.BlockSpec((1, n, chunks, rows, t), lambda bi, hd: (bi, hd, 0, 0, 0)),
                  _const_spec((1, DIFF_HEAD_DIM)), _const_spec((1, DIFF_HEAD_DIM)),
                  _const_spec((1, DIFF_HEAD_DIM)), _const_spec((1, DIFF_HEAD_DIM)),
                  _const_spec((1, DIFF_V_DIM))],
        out_specs=pl.BlockSpec((1, s, n * LANES), lambda bi, hd: (bi, 0, hd)),
        out_shape=jax.ShapeDtypeStruct((b, s, HEADS * (rows - BF16_ROWS)), BF16),
        compiler_params=_params("arbitrary", "arbitrary"),
        name="diff_attention",
    )(q, k, vt, vec(lq1), vec(lk1), vec(lq2), vec(lk2), vec(subln))


_N_FFN_CHUNKS = D_FF // FFN_CHUNK
_GELU_C = math.sqrt(2.0 / math.pi)


def _mix_ffn_kernel(*refs, n_in, steps_per_seq):
    a_refs, wo_refs = refs[:n_in], refs[n_in:2 * n_in]
    (gmix_ref, x_ref, gpre_ref, wup_ref, cw_ref, wdn_ref, gpost_ref, o_ref,
     h_ref, hist_ref, gva_ref, gvb_ref, *mid_refs) = refs[2 * n_in:]
    tm = h_ref.shape[0]
    nc = _N_FFN_CHUNKS
    i = pl.program_id(0)

    @pl.when(i % steps_per_seq == 0)
    def _():
        hist_ref[...] = jnp.zeros_like(hist_ref)

    def out_proj(rows):
        mix = jnp.dot(a_refs[0][rows, :], wo_refs[0][...], preferred_element_type=F32)
        for a_ref, wo_ref in zip(a_refs[1:], wo_refs[1:]):
            mix = mix + jnp.dot(a_ref[rows, :], wo_ref[...], preferred_element_type=F32)
        return mix

    def pre_norms(rows, mix):
        x1 = x_ref[rows, :] + _rms(mix, gmix_ref[...])
        o_ref[rows, :] = x1
        h_ref[...] = _rms(x1, gpre_ref[...]).astype(BF16)

    def up(c):
        return jnp.dot(h_ref[...], wup_ref[c], preferred_element_type=F32)

    def down(mid_ref):
        f = jnp.dot(mid_ref[0], wdn_ref[0], preferred_element_type=F32)
        for c in range(1, nc):
            f = f + jnp.dot(mid_ref[c], wdn_ref[c], preferred_element_type=F32)
        return f

    def gated(c, gv_ref, mid_ref):
        gv_ref[0:SUBLANES, :FFN_CHUNK] = hist_ref[c]
        gate = gv_ref[SUBLANES:, :FFN_CHUNK]
        val = gv_ref[SUBLANES:, FFN_CHUNK:]
        hist_ref[c] = gv_ref[tm:, :FFN_CHUNK]
        cw = cw_ref[c]
        pre = (gv_ref[SUBLANES - 2:SUBLANES - 2 + tm, :FFN_CHUNK] * cw[0:1, :]
               + gv_ref[SUBLANES - 1:SUBLANES - 1 + tm, :FFN_CHUNK] * cw[1:2, :]
               + gate * cw[2:3, :] + cw[3:4, :])
        pre = pre.astype(BF16)
        act = 0.5 * pre * (1.0 + jnp.tanh(_GELU_C * (pre + 0.044715 * (pre * pre * pre))))
        mid_ref[c] = act * val.astype(BF16)

    def chunks_after_first(mid_ref):
        def body(k, _):
            gated(2 * k, gva_ref, mid_ref)
            gvb_ref[SUBLANES:, :] = up(2 * k + 1)
            gated(2 * k + 1, gvb_ref, mid_ref)
            gva_ref[SUBLANES:, :] = up(2 * k + 2)
            return 0

        for k in range(nc // 2):
            body(k, 0)
        gated(nc - 1, gva_ref, mid_ref)

    assert nc % 2 == 1
    n_tiles = len(mid_refs)
    rows = [slice(s * tm, (s + 1) * tm) for s in range(n_tiles)]
    pre_norms(rows[0], out_proj(rows[0]))
    gva_ref[SUBLANES:, :] = up(0)
    for s in range(n_tiles):
        chunks_after_first(mid_refs[s])
        if s + 1 < n_tiles:
            mix_next = out_proj(rows[s + 1])
        f = down(mid_refs[s])
        if s + 1 < n_tiles:
            pre_norms(rows[s + 1], mix_next)
            gva_ref[SUBLANES:, :] = up(0)
        o_ref[rows[s], :] = o_ref[rows[s], :] + _rms(f, gpost_ref[...])


def _mix_ffn(acts, w_outs, g_mix, x, g_pre, w_up, conv_w, conv_b, w_down, g_post, seq):
    t = x.shape[0]
    tm = FFN_ROWS
    nc = _N_FFN_CHUNKS
    n_in = len(acts)
    wg = w_up[:, :D_FF].reshape(D_MODEL, nc, FFN_CHUNK)
    wv = w_up[:, D_FF:].reshape(D_MODEL, nc, FFN_CHUNK)
    wup = jnp.concatenate([wg, wv], axis=2).transpose(1, 0, 2).astype(BF16)
    wdn = w_down.reshape(nc, FFN_CHUNK, D_MODEL).astype(BF16)
    cw = jnp.concatenate([conv_w, conv_b[None, :],
                          jnp.zeros((SUBLANES - CONV_WIDTH - 1, D_FF), F32)], axis=0)
    cw = cw.reshape(SUBLANES, nc, FFN_CHUNK).transpose(1, 0, 2)
    step_rows = tm * FFN_TILES_PER_STEP
    assert seq % step_rows == 0
    x_spec = pl.BlockSpec((step_rows, D_MODEL), lambda i: (i, 0))
    vec_spec = _const_spec((1, D_MODEL))
    return pl.pallas_call(
        functools.partial(_mix_ffn_kernel, n_in=n_in, steps_per_seq=seq // step_rows),
        grid=(t // step_rows,),
        in_specs=([pl.BlockSpec((step_rows, a.shape[1]), lambda i: (i, 0)) for a in acts]
                  + [_const_spec(w.shape) for w in w_outs]
                  + [vec_spec, x_spec, vec_spec, _const_spec(wup.shape), _const_spec(cw.shape),
                     _const_spec(wdn.shape), vec_spec]),
        out_specs=x_spec,
        out_shape=jax.ShapeDtypeStruct(x.shape, F32),
        scratch_shapes=([pltpu.VMEM((tm, D_MODEL), BF16),
                         pltpu.VMEM((nc, SUBLANES, FFN_CHUNK), F32),
                         pltpu.VMEM((tm + SUBLANES, 2 * FFN_CHUNK), F32),
                         pltpu.VMEM((tm + SUBLANES, 2 * FFN_CHUNK), F32)]
                        + [pltpu.VMEM((nc, tm, FFN_CHUNK), BF16)] * FFN_TILES_PER_STEP),
        compiler_params=_params("arbitrary"),
        name="mix_ffn",
    )(*acts, *w_outs, g_mix.reshape(1, -1), x, g_pre.reshape(1, -1), wup, cw, wdn,
      g_post.reshape(1, -1))


def kernel(x, norm_mix_pre, norm_mix_post, norm_ffn_pre, norm_ffn_post, even_w_in, even_b_forget, even_q_norm, even_w_uq, even_kv_norm, even_w_ukv, even_w_out, odd_w_in, odd_lambda_q1, odd_lambda_k1, odd_lambda_q2, odd_lambda_k2, odd_subln, odd_w_out, ffn_w_up, ffn_conv_w, ffn_conv_b, ffn_w_down):
    bsz, seq, d = x.shape
    assert d == D_MODEL and seq % (ATTN_TILE * PROJ_TILES_PER_STEP) == 0
    t = bsz * seq
    rope = _rope_tables(seq)
    xf = x.reshape(t, d)
    for layer in range(DEPTH):
        i = layer // 2
        if layer % 2 == 0:
            fq, fk, fvt, mq, mk, mvt = _even_proj(
                xf, norm_mix_pre[layer], even_w_in[i], even_b_forget[i], even_q_norm[i],
                even_w_uq[i], even_kv_norm[i], even_w_ukv[i], rope, bsz, seq)
            shp = (bsz, seq, HEAD_COLS)
            fo = _flash_attention(fq.reshape(shp), fk.reshape(shp), fvt)
            mo = _flash_attention(mq.reshape(shp), mk.reshape(shp), mvt)
            w_out = even_w_out[i].astype(BF16)
            acts = [fo.reshape(t, -1), mo.reshape(t, -1)]
            w_outs = [w_out[:FOX_WIDTH], w_out[FOX_WIDTH:]]
        else:
            lambda_init = 0.8 - 0.6 * math.exp(-0.3 * layer)
            q, k, vt = _odd_proj(xf, norm_mix_pre[layer], odd_w_in[i], bsz, seq)
            o = _diff_attention(q.reshape(bsz, seq, -1), k.reshape(bsz, seq, -1), vt,
                                odd_lambda_q1[i], odd_lambda_k1[i],
                                odd_lambda_q2[i], odd_lambda_k2[i], odd_subln[i], lambda_init)
            acts, w_outs = [o.reshape(t, -1)], [odd_w_out[i].astype(BF16)]
        xf = _mix_ffn(acts, w_outs, norm_mix_post[layer], xf, norm_ffn_pre[layer], ffn_w_up[layer],
                      ffn_conv_w[layer], ffn_conv_b[layer], ffn_w_down[layer],
                      norm_ffn_post[layer], seq)
    return xf.reshape(bsz, seq, d)
```

```python
import functools
import math

import numpy as np
import jax
import jax.numpy as jnp
from jax import lax
from jax.experimental import pallas as pl
from jax.experimental.pallas import tpu as pltpu

D_MODEL = 1024
DEPTH = 4
RMS_EPS = 1e-6
NEG_INF = -1e30
HEADS = 8
FOX_HEAD_DIM = 64
FOX_WIDTH = HEADS * FOX_HEAD_DIM
MLA_NOPE_DIM = 64
MLA_ROPE_DIM = 32
MLA_V_DIM = 64
MLA_Q_RANK = 384
MLA_KV_RANK = 256
ROPE_THETA = 10000.0
DIFF_HEAD_DIM = 64
DIFF_V_DIM = 2 * DIFF_HEAD_DIM
D_FF = 2816
CONV_WIDTH = 3

LANES = 128
SUBLANES = 8
BF16_ROWS = 16
LOG2E = 1.4426950408889634
HEAD_COLS = HEADS * LANES
VMEM_LIMIT_BYTES = 56 * 1024 * 1024

PROJ_TILES_PER_STEP = 2
FFN_ROWS = 512
FFN_TILES_PER_STEP = 2
FFN_CHUNK = 256
ATTN_TILE = 512
FLASH_HEADS_PER_STEP, FLASH_KEYS, FLASH_LAG = 2, 256, 1
DIFF_HEADS_PER_STEP, DIFF_KEYS, DIFF_LAG = 1, 256, 1
F32 = jnp.float32
BF16 = jnp.bfloat16


def _params(*sem):
    return pltpu.CompilerParams(dimension_semantics=sem, vmem_limit_bytes=VMEM_LIMIT_BYTES)


def _const_spec(shape):
    nd = len(shape)
    return pl.BlockSpec(shape, lambda *_: (0,) * nd, pipeline_mode=pl.Buffered(1))


def _rms(x, g):
    return x * lax.rsqrt(jnp.mean(x * x, axis=-1, keepdims=True) + RMS_EPS) * g


def _split3(x):
    hi = x.astype(BF16)
    r1 = x - hi.astype(F32)
    mid = r1.astype(BF16)
    lo = (r1 - mid.astype(F32)).astype(BF16)
    return hi, mid, lo


def _rope(blk, ra, rb, rc):
    return blk * ra + pltpu.roll(blk, 16, 1) * rb + pltpu.roll(blk, LANES - 16, 1) * rc


def _vt_rows(dv):
    return dv + BF16_ROWS


def _store_vt(vt_ref, v, dv, chunk):
    tk = v.shape[0]
    per = LANES // dv
    for blk in range(v.shape[1] // LANES):
        vt = v[:, blk * LANES:(blk + 1) * LANES].T.astype(BF16)
        for s in range(per):
            vt_ref[0, blk * per + s, chunk, 0:dv, :] = vt[s * dv:(s + 1) * dv, :]
    row = lax.broadcasted_iota(jnp.int32, (BF16_ROWS, tk), 0)
    ones_row = jnp.where(row == 0, 1.0, 0.0).astype(BF16)
    for hd in range(HEADS):
        vt_ref[0, hd, chunk, dv:dv + BF16_ROWS, :] = ones_row


def _vt_spec(dv, chunks_per_step, tk, steps_per_seq):
    return pl.BlockSpec((1, HEADS, chunks_per_step, _vt_rows(dv), tk),
                        lambda i: (i // steps_per_seq, 0, i % steps_per_seq, 0, 0))


_EV_Q, _EV_K, _EV_V = 0, FOX_WIDTH, 2 * FOX_WIDTH
_EV_G = _EV_V + FOX_WIDTH
_EV_CQ = _EV_G + LANES
_EV_CKV = _EV_CQ + MLA_Q_RANK
_EV_KR = _EV_CKV + MLA_KV_RANK
_EV_WIDTH = _EV_KR + LANES


def _even_proj_kernel(x_ref, g_ref, w_ref, bf_ref, qn_ref, wuq_ref, kvn_ref, wuk_ref, wuv_ref,
                      ec_ref, ra_ref, rb_ref, rc_ref, cst_ref,
                      fq_ref, fk_ref, fv_ref, mq_ref, mk_ref, mv_ref, carry_ref,
                      *, steps_per_seq):
    tm = fv_ref.shape[4]
    n_tiles = x_ref.shape[0] // tm
    i = pl.program_id(0)

    @pl.when(i % steps_per_seq == 0)
    def _():
        carry_ref[...] = jnp.zeros_like(carry_ref)

    hs = [_rms(x_ref[s * tm:(s + 1) * tm, :], g_ref[...]).astype(BF16) for s in range(n_tiles)]
    for s, h in enumerate(hs):
        _even_proj_tile(h, s, slice(s * tm, (s + 1) * tm), w_ref, bf_ref, qn_ref, wuq_ref, kvn_ref,
                        wuk_ref, wuv_ref, ec_ref, ra_ref, rb_ref, rc_ref, cst_ref,
                        fq_ref, fk_ref, fv_ref, mq_ref, mk_ref, mv_ref, carry_ref)


def _even_proj_tile(h, s, rows, w_ref, bf_ref, qn_ref, wuq_ref, kvn_ref, wuk_ref, wuv_ref,
                    ec_ref, ra_ref, rb_ref, rc_ref, cst_ref,
                    fq_ref, fk_ref, fv_ref, mq_ref, mk_ref, mv_ref, carry_ref):
    tm = h.shape[0]
    q_aug = cst_ref[0:1, :]

    def proj(lo, hi):
        return jnp.dot(h, w_ref[:, lo:hi], preferred_element_type=F32)

    lane = lax.broadcasted_iota(jnp.int32, (tm, LANES), 1)

    def spread_heads(o_ref, packed, aug):
        for pair in range(HEADS // 2):
            blk = packed[:, pair * LANES:(pair + 1) * LANES]
            for odd, src in enumerate((blk, pltpu.roll(blk, FOX_HEAD_DIM, 1))):
                sl = slice((2 * pair + odd) * LANES, (2 * pair + odd + 1) * LANES)
                o_ref[rows, sl] = jnp.where(lane < FOX_HEAD_DIM, src, aug[:, sl]).astype(BF16)

    ra, rb, rc = ra_ref[rows, :], rb_ref[rows, :], rc_ref[rows, :]
    fox_scale = FOX_HEAD_DIM ** -0.5 * LOG2E
    mla_scale = (MLA_NOPE_DIM + MLA_ROPE_DIM) ** -0.5 * LOG2E

    z = proj(_EV_G, _EV_CQ) + bf_ref[...]
    cq_raw = proj(_EV_CQ, _EV_CKV)
    ckv_raw = proj(_EV_CKV, _EV_KR)
    kr_raw = proj(_EV_KR, _EV_WIDTH)

    logf = (jnp.minimum(z, 0.0) - jnp.log1p(jnp.exp(-jnp.abs(z)))) * LOG2E
    spread_heads(fq_ref, proj(_EV_Q, _EV_K) * fox_scale, q_aug)

    row = lax.broadcasted_iota(jnp.int32, (tm, tm), 0)
    col = lax.broadcasted_iota(jnp.int32, (tm, tm), 1)
    tril = jnp.where(row >= col, 1.0, 0.0).astype(BF16)
    parts = jnp.dot(tril, jnp.concatenate(_split3(logf), axis=1), preferred_element_type=F32)
    _store_vt(fv_ref, proj(_EV_V, _EV_G), FOX_HEAD_DIM, s)
    c = (parts[:, :LANES] + parts[:, LANES:2 * LANES] + parts[:, 2 * LANES:]
         + carry_ref[SUBLANES - 1:SUBLANES, :])
    carry_ref[...] = c[tm - SUBLANES:, :]

    cq = _rms(cq_raw, qn_ref[...]).astype(BF16)
    q = jnp.dot(cq, wuq_ref[...], preferred_element_type=F32) * mla_scale
    for hd in range(HEADS):
        sl = slice(hd * LANES, (hd + 1) * LANES)
        mq_ref[rows, sl] = _rope(q[:, sl], ra, rb, rc).astype(BF16)

    ckv = _rms(ckv_raw, kvn_ref[...]).astype(BF16)
    kr = _rope(kr_raw, ra, rb, rc)
    kn = jnp.dot(ckv, wuk_ref[...], preferred_element_type=F32)
    for hd in range(HEADS):
        sl = slice(hd * LANES, (hd + 1) * LANES)
        mk_ref[rows, sl] = (kn[:, sl] + kr).astype(BF16)
    _store_vt(mv_ref, jnp.dot(ckv, wuv_ref[...], preferred_element_type=F32), MLA_V_DIM, s)

    c_aug = jnp.dot(jnp.concatenate(_split3(c), axis=1), ec_ref[...], preferred_element_type=F32)
    spread_heads(fk_ref, proj(_EV_K, _EV_V), c_aug)


def _pad_heads(w, dh):
    k = w.shape[0]
    w = w.reshape(k, HEADS, dh)
    return jnp.pad(w, ((0, 0), (0, 0), (0, LANES - dh))).reshape(k, HEAD_COLS)


def _even_consts():
    cst = np.zeros((SUBLANES, HEAD_COLS), np.float32)
    ec = np.zeros((3 * LANES, HEAD_COLS), np.float32)
    for hd in range(HEADS):
        cst[0, hd * LANES + 64: hd * LANES + 67] = 1.0
        for p in range(3):
            ec[p * LANES + hd, hd * LANES + 64 + p] = -1.0
    return jnp.asarray(cst), jnp.asarray(ec, dtype=BF16)


def _rope_tables(seq):
    half = MLA_ROPE_DIM // 2
    inv = ROPE_THETA ** (-jnp.arange(0, MLA_ROPE_DIM, 2, dtype=F32) / MLA_ROPE_DIM)
    ang = jnp.arange(seq, dtype=F32)[:, None] * inv[None, :]
    cos, sin = jnp.cos(ang), jnp.sin(ang)
    zeros = jnp.zeros((seq, half), F32)
    ra = jnp.concatenate([jnp.ones((seq, 64), F32), cos, cos, jnp.zeros((seq, 32), F32)], axis=1)
    rb = jnp.concatenate([jnp.zeros((seq, 64), F32), zeros, sin, jnp.zeros((seq, 32), F32)], axis=1)
    rc = jnp.concatenate([jnp.zeros((seq, 64), F32), -sin, zeros, jnp.zeros((seq, 32), F32)], axis=1)
    return ra, rb, rc


def _even_proj(x, g, w_in, b_forget, q_norm, w_uq, kv_norm, w_ukv, rope, bsz, seq):
    t = x.shape[0]
    tk = ATTN_TILE
    tm = tk * PROJ_TILES_PER_STEP
    cuts = np.cumsum([FOX_WIDTH, FOX_WIDTH, FOX_WIDTH, HEADS, MLA_Q_RANK, MLA_KV_RANK]).tolist()
    wfq, wfk, wfv, wg, wcq, wckv, wkr = jnp.split(w_in, cuts, axis=1)
    w1 = jnp.concatenate([
        wfq, wfk, wfv,
        jnp.pad(wg, ((0, 0), (0, LANES - HEADS))), wcq, wckv,
        jnp.pad(wkr, ((0, 0), (64, LANES - 64 - MLA_ROPE_DIM)))], axis=1).astype(BF16)
    assert w1.shape[1] == _EV_WIDTH
    bf = jnp.pad(b_forget, (0, LANES - HEADS)).reshape(1, LANES)
    wuq = _pad_heads(w_uq, MLA_NOPE_DIM + MLA_ROPE_DIM).astype(BF16)
    wukv = w_ukv.reshape(MLA_KV_RANK, HEADS, MLA_NOPE_DIM + MLA_V_DIM)
    wuk = _pad_heads(wukv[:, :, :MLA_NOPE_DIM].reshape(MLA_KV_RANK, -1), MLA_NOPE_DIM).astype(BF16)
    wuv = wukv[:, :, MLA_NOPE_DIM:].reshape(MLA_KV_RANK, -1).astype(BF16)
    cst, ec = _even_consts()
    ra, rb, rc = rope
    steps_per_seq = seq // tm
    rope_spec = pl.BlockSpec((tm, LANES), lambda i: (i % steps_per_seq, 0))
    row_spec = pl.BlockSpec((tm, HEAD_COLS), lambda i: (i, 0))
    out_sds = jax.ShapeDtypeStruct((t, HEAD_COLS), BF16)
    vt_spec = _vt_spec(FOX_HEAD_DIM, PROJ_TILES_PER_STEP, tk, steps_per_seq)
    vt_sds = jax.ShapeDtypeStruct((bsz, HEADS, seq // tk, _vt_rows(FOX_HEAD_DIM), tk), BF16)
    return pl.pallas_call(
        functools.partial(_even_proj_kernel, steps_per_seq=steps_per_seq),
        grid=(t // tm,),
        in_specs=[pl.BlockSpec((tm, D_MODEL), lambda i: (i, 0)),
                  _const_spec((1, D_MODEL)), _const_spec(w1.shape), _const_spec((1, LANES)),
                  _const_spec((1, MLA_Q_RANK)), _const_spec(wuq.shape),
                  _const_spec((1, MLA_KV_RANK)), _const_spec(wuk.shape), _const_spec(wuv.shape),
                  _const_spec(ec.shape), rope_spec, rope_spec, rope_spec, _const_spec(cst.shape)],
        out_specs=[row_spec, row_spec, vt_spec, row_spec, row_spec, vt_spec],
        out_shape=[out_sds, out_sds, vt_sds, out_sds, out_sds, vt_sds],
        scratch_shapes=[pltpu.VMEM((SUBLANES, LANES), F32)],
        compiler_params=_params("arbitrary"),
        name="even_proj",
    )(x, g.reshape(1, -1), w1, bf, q_norm.reshape(1, -1), wuq, kv_norm.reshape(1, -1), wuk, wuv,
      ec, ra, rb, rc, cst)


def _alibi_slope(hd):
    return 2.0 ** (-8.0 * (hd + 1) / HEADS)


def _odd_proj_kernel(x_ref, g_ref, w_ref, q_ref, k_ref, v_ref, *, steps_per_seq):
    tm = v_ref.shape[4]
    n_tiles = x_ref.shape[0] // tm
    i = pl.program_id(0)
    scale = DIFF_HEAD_DIM ** -0.5 * LOG2E
    d = DIFF_HEAD_DIM
    lane = lax.broadcasted_iota(jnp.int32, (tm, LANES), 1)
    q_aug = jnp.where((lane >= d) & (lane < d + 3), 1.0, 0.0)
    hs = [_rms(x_ref[s * tm:(s + 1) * tm, :], g_ref[...]).astype(BF16) for s in range(n_tiles)]
    for s, h in enumerate(hs):
        rows = slice(s * tm, (s + 1) * tm)
        v = jnp.dot(h, w_ref[:, 2 * HEAD_COLS:], preferred_element_type=F32)
        k = jnp.dot(h, w_ref[:, HEAD_COLS:2 * HEAD_COLS], preferred_element_type=F32)
        _store_vt(v_ref, v, DIFF_V_DIM, s)
        q = jnp.dot(h, w_ref[:, :HEAD_COLS], preferred_element_type=F32)
        pos = (((i % steps_per_seq) * n_tiles + s) * tm
               + lax.broadcasted_iota(jnp.int32, (tm, 1), 0)).astype(F32)
        for hd in range(HEADS):
            sl = slice(hd * LANES, (hd + 1) * LANES)
            b_hi, b_mid, b_lo = (p.astype(F32)
                                 for p in _split3(pos * (_alibi_slope(hd) * LOG2E)))
            k_aug = jnp.where(lane == d, b_hi,
                              jnp.where(lane == d + 1, b_mid, jnp.where(lane == d + 2, b_lo, 0.0)))
            qb, kb = q[:, sl] * scale, k[:, sl]
            for br, (qs, ks) in enumerate(((qb, kb),
                                           (pltpu.roll(qb, d, 1), pltpu.roll(kb, d, 1)))):
                so = slice((2 * hd + br) * LANES, (2 * hd + br + 1) * LANES)
                q_ref[rows, so] = jnp.where(lane < d, qs, q_aug).astype(BF16)
                k_ref[rows, so] = jnp.where(lane < d, ks, k_aug).astype(BF16)


def _odd_proj(x, g, w_in, bsz, seq):
    t = x.shape[0]
    tk = ATTN_TILE
    tm = tk * PROJ_TILES_PER_STEP
    w = w_in.astype(BF16)
    steps_per_seq = seq // tm
    return pl.pallas_call(
        functools.partial(_odd_proj_kernel, steps_per_seq=steps_per_seq),
        grid=(t // tm,),
        in_specs=[pl.BlockSpec((tm, D_MODEL), lambda i: (i, 0)),
                  _const_spec((1, D_MODEL)), _const_spec(w.shape)],
        out_specs=[pl.BlockSpec((tm, 2 * HEAD_COLS), lambda i: (i, 0)),
                   pl.BlockSpec((tm, 2 * HEAD_COLS), lambda i: (i, 0)),
                   _vt_spec(DIFF_V_DIM, PROJ_TILES_PER_STEP, tk, steps_per_seq)],
        out_shape=[jax.ShapeDtypeStruct((t, 2 * HEAD_COLS), BF16),
                   jax.ShapeDtypeStruct((t, 2 * HEAD_COLS), BF16),
                   jax.ShapeDtypeStruct((bsz, HEADS, seq // tk, _vt_rows(DIFF_V_DIM), tk), BF16)],
        compiler_params=_params("arbitrary"),
        name="odd_proj",
    )(x, g.reshape(1, -1), w)


_NT = (((1,), (1,)), ((), ()))


def _attend(s, vt, m, acc, causal):
    if not causal:
        m_new = jnp.maximum(m, jnp.max(s, axis=0, keepdims=True))
        p = jnp.exp2(s - m_new).astype(BF16)
    else:
        nr, nc = s.shape[0] // LANES, s.shape[1] // LANES
        key = lax.broadcasted_iota(jnp.int32, (LANES, LANES), 0)
        qry = lax.broadcasted_iota(jnp.int32, (LANES, LANES), 1)
        sq = [[s[a * LANES:(a + 1) * LANES, b * LANES:(b + 1) * LANES] for b in range(nc)]
              for a in range(nr)]
        for a in range(nr):
            sq[a][a] = jnp.where(key <= qry, sq[a][a], NEG_INF)
        col_max = [functools.reduce(jnp.maximum, [jnp.max(sq[a][b], axis=0, keepdims=True)
                                                  for a in range(min(b + 1, nr))])
                   for b in range(nc)]
        m_new = jnp.maximum(m, jnp.concatenate(col_max, axis=1))
        p = jnp.concatenate(
            [jnp.concatenate(
                [jnp.exp2(sq[a][b] - m_new[:, b * LANES:(b + 1) * LANES]).astype(BF16)
                 if a <= b else jnp.zeros((LANES, LANES), BF16) for b in range(nc)], axis=1)
             for a in range(nr)], axis=0)
    acc = jnp.exp2(m - m_new) * acc + jnp.dot(vt, p, preferred_element_type=F32)
    return m_new, acc


def _stream_attention(qk, vts, finish, q_tiles, t, tk, rows, n, lag):
    r = t // tk
    items = [(qi, j, u, k) for qi in range(q_tiles) for j in range(qi + 1) for u in range(r)
             for k in range(n)]

    def first_query(qi, j, u):
        return u * tk if j == qi else 0

    pending, issued = [], 0
    state = None
    for i, (qi, j, u, k) in enumerate(items):
        while issued < min(i + lag + 1, len(items)):
            nqi, nj, nu, nk = items[issued]
            pending.append(qk(nk, nqi, nj, nu, first_query(nqi, nj, nu)))
            issued += 1
        if j == 0 and u == 0 and k == 0:
            state = [(jnp.full((1, t), NEG_INF, F32), jnp.zeros((rows, t), F32))] * n
        q0 = first_query(qi, j, u)
        m, acc = state[k]
        m_hi, acc_hi = _attend(pending.pop(0), vts(k, j, u), m[:, q0:], acc[:, q0:], j == qi)
        state[k] = ((jnp.concatenate([m[:, :q0], m_hi], axis=1),
                     jnp.concatenate([acc[:, :q0], acc_hi], axis=1)) if q0 else (m_hi, acc_hi))
        if j == qi and u == r - 1 and k == n - 1:
            finish(qi, [acc for _, acc in state])


def _flash_kernel(q_ref, k_ref, vt_ref, o_ref):
    n, chunks, rows, t = vt_ref.shape[1:]
    tk = FLASH_KEYS
    dv = rows - BF16_ROWS

    def qk(hd, qi, j, u, q0):
        kc = k_ref[0, j * t + u * tk:j * t + (u + 1) * tk, hd * LANES:(hd + 1) * LANES]
        q = q_ref[0, qi * t + q0:(qi + 1) * t, hd * LANES:(hd + 1) * LANES]
        return lax.dot_general(kc, q, _NT, preferred_element_type=F32)

    def vts(hd, j, u):
        return vt_ref[0, hd, j, :, u * tk:(u + 1) * tk]

    def finish(qi, accs):
        o_t = jnp.concatenate([a[:dv] / a[dv:dv + 1] for a in accs], axis=0)
        o_ref[0, qi * t:(qi + 1) * t, :] = o_t.T.astype(BF16)

    _stream_attention(qk, vts, finish, chunks, t, tk, rows, n, FLASH_LAG)


def _flash_attention(q, k, vt):
    b, s, _ = q.shape
    _, _, chunks, rows, t = vt.shape
    n = FLASH_HEADS_PER_STEP
    dv = rows - BF16_ROWS
    return pl.pallas_call(
        _flash_kernel,
        grid=(b, HEADS // n),
        in_specs=[pl.BlockSpec((1, s, n * LANES), lambda bi, hp: (bi, 0, hp)),
                  pl.BlockSpec((1, s, n * LANES), lambda bi, hp: (bi, 0, hp)),
                  pl.BlockSpec((1, n, chunks, rows, t), lambda bi, hp: (bi, hp, 0, 0, 0))],
        out_specs=pl.BlockSpec((1, s, n * dv), lambda bi, hp: (bi, 0, hp)),
        out_shape=jax.ShapeDtypeStruct((b, s, HEADS * dv), BF16),
        compiler_params=_params("arbitrary", "arbitrary"),
        name="flash_attention",
    )(q, k, vt)


def _diff_kernel(q_ref, k_ref, vt_ref, lq1_ref, lk1_ref, lq2_ref, lk2_ref, sub_ref, o_ref,
                 *, lambda_init):
    n, chunks, rows, t = vt_ref.shape[1:]
    tk = DIFF_KEYS
    dv = rows - BF16_ROWS

    def qk(st, qi, j, u, q0):
        kc = k_ref[0, j * t + u * tk:j * t + (u + 1) * tk, st * LANES:(st + 1) * LANES]
        q = q_ref[0, qi * t + q0:(qi + 1) * t, st * LANES:(st + 1) * LANES]
        return lax.dot_general(kc, q, _NT, preferred_element_type=F32)

    def vts(st, j, u):
        return vt_ref[0, st // 2, j, :, u * tk:(u + 1) * tk]

    lam = (jnp.exp(jnp.sum(lq1_ref[...] * lk1_ref[...], axis=1, keepdims=True))
           - jnp.exp(jnp.sum(lq2_ref[...] * lk2_ref[...], axis=1, keepdims=True)) + lambda_init)

    def finish(qi, accs):
        for hd in range(n):
            a1, a2 = accs[2 * hd], accs[2 * hd + 1]
            o = (a1[:dv] / a1[dv:dv + 1] - lam * (a2[:dv] / a2[dv:dv + 1])).T
            o_ref[0, qi * t:(qi + 1) * t, hd * dv:(hd + 1) * dv] = (
                _rms(o, sub_ref[...]) * (1.0 - lambda_init)).astype(BF16)

    _stream_attention(qk, vts, finish, chunks, t, tk, rows, 2 * n, DIFF_LAG)


def _diff_attention(q, k, vt, lq1, lk1, lq2, lk2, subln, lambda_init):
    b, s, _ = q.shape
    _, _, chunks, rows, t = vt.shape
    vec = lambda a: a.reshape(1, -1)
    n = DIFF_HEADS_PER_STEP
    return pl.pallas_call(
        functools.partial(_diff_kernel, lambda_init=lambda_init),
        grid=(b, HEADS // n),
        in_specs=[pl.BlockSpec((1, s, 2 * n * LANES), lambda bi, hd: (bi, 0, hd)),
                  pl.BlockSpec((1, s, 2 * n * LANES), lambda bi, hd: (bi, 0, hd)),
                  pl.BlockSpec((1, n, chunks, rows, t), lambda bi, hd: (bi, hd, 0, 0, 0)),
                  _const_spec((1, DIFF_HEAD_DIM)), _const_spec((1, DIFF_HEAD_DIM)),
                  _const_spec((1, DIFF_HEAD_DIM)), _const_spec((1, DIFF_HEAD_DIM)),
                  _const_spec((1, DIFF_V_DIM))],
        out_specs=pl.BlockSpec((1, s, n * LANES), lambda bi, hd: (bi, 0, hd)),
        out_shape=jax.ShapeDtypeStruct((b, s, HEADS * (rows - BF16_ROWS)), BF16),
        compiler_params=_params("arbitrary", "arbitrary"),
        name="diff_attention",
    )(q, k, vt, vec(lq1), vec(lk1), vec(lq2), vec(lk2), vec(subln))


_N_FFN_CHUNKS = D_FF // FFN_CHUNK
_GELU_C = math.sqrt(2.0 / math.pi)


def _mix_ffn_kernel(*refs, n_in, steps_per_seq):
    a_refs, wo_refs = refs[:n_in], refs[n_in:2 * n_in]
    (gmix_ref, x_ref, gpre_ref, wup_ref, cw_ref, wdn_ref, gpost_ref, o_ref,
     h_ref, hist_ref, gva_ref, gvb_ref, *mid_refs) = refs[2 * n_in:]
    tm = h_ref.shape[0]
    nc = _N_FFN_CHUNKS
    i = pl.program_id(0)

    @pl.when(i % steps_per_seq == 0)
    def _():
        hist_ref[...] = jnp.zeros_like(hist_ref)

    def out_proj(rows):
        mix = jnp.dot(a_refs[0][rows, :], wo_refs[0][...], preferred_element_type=F32)
        for a_ref, wo_ref in zip(a_refs[1:], wo_refs[1:]):
            mix = mix + jnp.dot(a_ref[rows, :], wo_ref[...], preferred_element_type=F32)
        return mix

    def pre_norms(rows, mix):
        x1 = x_ref[rows, :] + _rms(mix, gmix_ref[...])
        o_ref[rows, :] = x1
        h_ref[...] = _rms(x1, gpre_ref[...]).astype(BF16)

    def up(c):
        return jnp.dot(h_ref[...], wup_ref[c], preferred_element_type=F32)

    def down(mid_ref):
        f = jnp.dot(mid_ref[0], wdn_ref[0], preferred_element_type=F32)
        for c in range(1, nc):
            f = f + jnp.dot(mid_ref[c], wdn_ref[c], preferred_element_type=F32)
        return f

    def gated(c, gv_ref, mid_ref):
        gv_ref[0:SUBLANES, :FFN_CHUNK] = hist_ref[c]
        gate = gv_ref[SUBLANES:, :FFN_CHUNK]
        val = gv_ref[SUBLANES:, FFN_CHUNK:]
        hist_ref[c] = gv_ref[tm:, :FFN_CHUNK]
        cw = cw_ref[c]
        pre = (gv_ref[SUBLANES - 2:SUBLANES - 2 + tm, :FFN_CHUNK] * cw[0:1, :]
               + gv_ref[SUBLANES - 1:SUBLANES - 1 + tm, :FFN_CHUNK] * cw[1:2, :]
               + gate * cw[2:3, :] + cw[3:4, :])
        pre = pre.astype(BF16)
        act = 0.5 * pre * (1.0 + jnp.tanh(_GELU_C * (pre + 0.044715 * (pre * pre * pre))))
        mid_ref[c] = act * val.astype(BF16)

    def chunks_after_first(mid_ref):
        def body(k, _):
            gated(2 * k, gva_ref, mid_ref)
            gvb_ref[SUBLANES:, :] = up(2 * k + 1)
            gated(2 * k + 1, gvb_ref, mid_ref)
            gva_ref[SUBLANES:, :] = up(2 * k + 2)
            return 0

        for k in range(nc // 2):
            body(k, 0)
        gated(nc - 1, gva_ref, mid_ref)

    assert nc % 2 == 1
    n_tiles = len(mid_refs)
    rows = [slice(s * tm, (s + 1) * tm) for s in range(n_tiles)]
    pre_norms(rows[0], out_proj(rows[0]))
    gva_ref[SUBLANES:, :] = up(0)
    for s in range(n_tiles):
        chunks_after_first(mid_refs[s])
        if s + 1 < n_tiles:
            mix_next = out_proj(rows[s + 1])
        f = down(mid_refs[s])
        if s + 1 < n_tiles:
            pre_norms(rows[s + 1], mix_next)
            gva_ref[SUBLANES:, :] = up(0)
        o_ref[rows[s], :] = o_ref[rows[s], :] + _rms(f, gpost_ref[...])


def _mix_ffn(acts, w_outs, g_mix, x, g_pre, w_up, conv_w, conv_b, w_down, g_post, seq):
    t = x.shape[0]
    tm = FFN_ROWS
    nc = _N_FFN_CHUNKS
    n_in = len(acts)
    wg = w_up[:, :D_FF].reshape(D_MODEL, nc, FFN_CHUNK)
    wv = w_up[:, D_FF:].reshape(D_MODEL, nc, FFN_CHUNK)
    wup = jnp.concatenate([wg, wv], axis=2).transpose(1, 0, 2).astype(BF16)
    wdn = w_down.reshape(nc, FFN_CHUNK, D_MODEL).astype(BF16)
    cw = jnp.concatenate([conv_w, conv_b[None, :],
                          jnp.zeros((SUBLANES - CONV_WIDTH - 1, D_FF), F32)], axis=0)
    cw = cw.reshape(SUBLANES, nc, FFN_CHUNK).transpose(1, 0, 2)
    step_rows = tm * FFN_TILES_PER_STEP
    assert seq % step_rows == 0
    x_spec = pl.BlockSpec((step_rows, D_MODEL), lambda i: (i, 0))
    vec_spec = _const_spec((1, D_MODEL))
    return pl.pallas_call(
        functools.partial(_mix_ffn_kernel, n_in=n_in, steps_per_seq=seq // step_rows),
        grid=(t // step_rows,),
        in_specs=([pl.BlockSpec((step_rows, a.shape[1]), lambda i: (i, 0)) for a in acts]
                  + [_const_spec(w.shape) for w in w_outs]
                  + [vec_spec, x_spec, vec_spec, _const_spec(wup.shape), _const_spec(cw.shape),
                     _const_spec(wdn.shape), vec_spec]),
        out_specs=x_spec,
        out_shape=jax.ShapeDtypeStruct(x.shape, F32),
        scratch_shapes=([pltpu.VMEM((tm, D_MODEL), BF16),
                         pltpu.VMEM((nc, SUBLANES, FFN_CHUNK), F32),
                         pltpu.VMEM((tm + SUBLANES, 2 * FFN_CHUNK), F32),
                         pltpu.VMEM((tm + SUBLANES, 2 * FFN_CHUNK), F32)]
                        + [pltpu.VMEM((nc, tm, FFN_CHUNK), BF16)] * FFN_TILES_PER_STEP),
        compiler_params=_params("arbitrary"),
        name="mix_ffn",
    )(*acts, *w_outs, g_mix.reshape(1, -1), x, g_pre.reshape(1, -1), wup, cw, wdn,
      g_post.reshape(1, -1))


def kernel(x, norm_mix_pre, norm_mix_post, norm_ffn_pre, norm_ffn_post, even_w_in, even_b_forget, even_q_norm, even_w_uq, even_kv_norm, even_w_ukv, even_w_out, odd_w_in, odd_lambda_q1, odd_lambda_k1, odd_lambda_q2, odd_lambda_k2, odd_subln, odd_w_out, ffn_w_up, ffn_conv_w, ffn_conv_b, ffn_w_down):
    bsz, seq, d = x.shape
    assert d == D_MODEL and seq % (ATTN_TILE * PROJ_TILES_PER_STEP) == 0
    t = bsz * seq
    rope = _rope_tables(seq)
    xf = x.reshape(t, d)
    for layer in range(DEPTH):
        i = layer // 2
        if layer % 2 == 0:
            fq, fk, fvt, mq, mk, mvt = _even_proj(
                xf, norm_mix_pre[layer], even_w_in[i], even_b_forget[i], even_q_norm[i],
                even_w_uq[i], even_kv_norm[i], even_w_ukv[i], rope, bsz, seq)
            shp = (bsz, seq, HEAD_COLS)
            fo = _flash_attention(fq.reshape(shp), fk.reshape(shp), fvt)
            mo = _flash_attention(mq.reshape(shp), mk.reshape(shp), mvt)
            w_out = even_w_out[i].astype(BF16)
            acts = [fo.reshape(t, -1), mo.reshape(t, -1)]
            w_outs = [w_out[:FOX_WIDTH], w_out[FOX_WIDTH:]]
        else:
            lambda_init = 0.8 - 0.6 * math.exp(-0.3 * layer)
            q, k, vt = _odd_proj(xf, norm_mix_pre[layer], odd_w_in[i], bsz, seq)
            o = _diff_attention(q.reshape(bsz, seq, -1), k.reshape(bsz, seq, -1), vt,
                                odd_lambda_q1[i], odd_lambda_k1[i],
                                odd_lambda_q2[i], odd_lambda_k2[i], odd_subln[i], lambda_init)
            acts, w_outs = [o.reshape(t, -1)], [odd_w_out[i].astype(BF16)]
        xf = _mix_ffn(acts, w_outs, norm_mix_post[layer], xf, norm_ffn_pre[layer], ffn_w_up[layer],
                      ffn_conv_w[layer], ffn_conv_b[layer], ffn_w_down[layer],
                      norm_ffn_post[layer], seq)
    return xf.reshape(bsz, seq, d)
```

```python
import functools
import math

import numpy as np
import jax
import jax.numpy as jnp
from jax import lax
from jax.experimental import pallas as pl
from jax.experimental.pallas import tpu as pltpu

D_MODEL = 1024
DEPTH = 4
RMS_EPS = 1e-6
NEG_INF = -1e30
HEADS = 8
FOX_HEAD_DIM = 64
FOX_WIDTH = HEADS * FOX_HEAD_DIM
MLA_NOPE_DIM = 64
MLA_ROPE_DIM = 32
MLA_V_DIM = 64
MLA_Q_RANK = 384
MLA_KV_RANK = 256
ROPE_THETA = 10000.0
DIFF_HEAD_DIM = 64
DIFF_V_DIM = 2 * DIFF_HEAD_DIM
D_FF = 2816
CONV_WIDTH = 3

LANES = 128
SUBLANES = 8
BF16_ROWS = 16
LOG2E = 1.4426950408889634
HEAD_COLS = HEADS * LANES
VMEM_LIMIT_BYTES = 56 * 1024 * 1024

PROJ_TILES_PER_STEP = 2
FFN_ROWS = 512
FFN_TILES_PER_STEP = 2
FFN_CHUNK = 256
ATTN_TILE = 512
FLASH_HEADS_PER_STEP, FLASH_KEYS, FLASH_LAG = 2, 512, 2
DIFF_HEADS_PER_STEP, DIFF_KEYS, DIFF_LAG = 1, 256, 2
F32 = jnp.float32
BF16 = jnp.bfloat16


def _params(*sem):
    return pltpu.CompilerParams(dimension_semantics=sem, vmem_limit_bytes=VMEM_LIMIT_BYTES)


def _const_spec(shape):
    nd = len(shape)
    return pl.BlockSpec(shape, lambda *_: (0,) * nd, pipeline_mode=pl.Buffered(1))


def _rms(x, g):
    return x * lax.rsqrt(jnp.mean(x * x, axis=-1, keepdims=True) + RMS_EPS) * g


def _split3(x):
    hi = x.astype(BF16)
    r1 = x - hi.astype(F32)
    mid = r1.astype(BF16)
    lo = (r1 - mid.astype(F32)).astype(BF16)
    return hi, mid, lo


def _rope(blk, ra, rb, rc):
    return blk * ra + pltpu.roll(blk, 16, 1) * rb + pltpu.roll(blk, LANES - 16, 1) * rc


def _vt_rows(dv):
    return dv + BF16_ROWS


def _store_vt(vt_ref, v, dv, chunk):
    tk = v.shape[0]
    per = LANES // dv
    for blk in range(v.shape[1] // LANES):
        vt = v[:, blk * LANES:(blk + 1) * LANES].T.astype(BF16)
        for s in range(per):
            vt_ref[0, blk * per + s, chunk, 0:dv, :] = vt[s * dv:(s + 1) * dv, :]
    row = lax.broadcasted_iota(jnp.int32, (BF16_ROWS, tk), 0)
    ones_row = jnp.where(row == 0, 1.0, 0.0).astype(BF16)
    for hd in range(HEADS):
        vt_ref[0, hd, chunk, dv:dv + BF16_ROWS, :] = ones_row


def _vt_spec(dv, chunks_per_step, tk, steps_per_seq):
    return pl.BlockSpec((1, HEADS, chunks_per_step, _vt_rows(dv), tk),
                        lambda i: (i // steps_per_seq, 0, i % steps_per_seq, 0, 0))


_EV_Q, _EV_K, _EV_V = 0, FOX_WIDTH, 2 * FOX_WIDTH
_EV_G = _EV_V + FOX_WIDTH
_EV_CQ = _EV_G + LANES
_EV_CKV = _EV_CQ + MLA_Q_RANK
_EV_KR = _EV_CKV + MLA_KV_RANK
_EV_WIDTH = _EV_KR + LANES


def _even_proj_kernel(x_ref, g_ref, w_ref, bf_ref, qn_ref, wuq_ref, kvn_ref, wuk_ref, wuv_ref,
                      ec_ref, ra_ref, rb_ref, rc_ref, cst_ref,
                      fq_ref, fk_ref, fv_ref, mq_ref, mk_ref, mv_ref, carry_ref,
                      *, steps_per_seq):
    tm = fv_ref.shape[4]
    n_tiles = x_ref.shape[0] // tm
    i = pl.program_id(0)

    @pl.when(i % steps_per_seq == 0)
    def _():
        carry_ref[...] = jnp.zeros_like(carry_ref)

    hs = [_rms(x_ref[s * tm:(s + 1) * tm, :], g_ref[...]).astype(BF16) for s in range(n_tiles)]
    for s, h in enumerate(hs):
        _even_proj_tile(h, s, slice(s * tm, (s + 1) * tm), w_ref, bf_ref, qn_ref, wuq_ref, kvn_ref,
                        wuk_ref, wuv_ref, ec_ref, ra_ref, rb_ref, rc_ref, cst_ref,
                        fq_ref, fk_ref, fv_ref, mq_ref, mk_ref, mv_ref, carry_ref)


def _even_proj_tile(h, s, rows, w_ref, bf_ref, qn_ref, wuq_ref, kvn_ref, wuk_ref, wuv_ref,
                    ec_ref, ra_ref, rb_ref, rc_ref, cst_ref,
                    fq_ref, fk_ref, fv_ref, mq_ref, mk_ref, mv_ref, carry_ref):
    tm = h.shape[0]
    q_aug = cst_ref[0:1, :]

    def proj(lo, hi):
        return jnp.dot(h, w_ref[:, lo:hi], preferred_element_type=F32)

    lane = lax.broadcasted_iota(jnp.int32, (tm, LANES), 1)

    def spread_heads(o_ref, packed, aug):
        for pair in range(HEADS // 2):
            blk = packed[:, pair * LANES:(pair + 1) * LANES]
            for odd, src in enumerate((blk, pltpu.roll(blk, FOX_HEAD_DIM, 1))):
                sl = slice((2 * pair + odd) * LANES, (2 * pair + odd + 1) * LANES)
                o_ref[rows, sl] = jnp.where(lane < FOX_HEAD_DIM, src, aug[:, sl]).astype(BF16)

    ra, rb, rc = ra_ref[rows, :], rb_ref[rows, :], rc_ref[rows, :]
    fox_scale = FOX_HEAD_DIM ** -0.5 * LOG2E
    mla_scale = (MLA_NOPE_DIM + MLA_ROPE_DIM) ** -0.5 * LOG2E

    z = proj(_EV_G, _EV_CQ) + bf_ref[...]
    cq_raw = proj(_EV_CQ, _EV_CKV)
    ckv_raw = proj(_EV_CKV, _EV_KR)
    kr_raw = proj(_EV_KR, _EV_WIDTH)

    logf = (jnp.minimum(z, 0.0) - jnp.log1p(jnp.exp(-jnp.abs(z)))) * LOG2E
    spread_heads(fq_ref, proj(_EV_Q, _EV_K) * fox_scale, q_aug)

    row = lax.broadcasted_iota(jnp.int32, (tm, tm), 0)
    col = lax.broadcasted_iota(jnp.int32, (tm, tm), 1)
    tril = jnp.where(row >= col, 1.0, 0.0).astype(BF16)
    parts = jnp.dot(tril, jnp.concatenate(_split3(logf), axis=1), preferred_element_type=F32)
    _store_vt(fv_ref, proj(_EV_V, _EV_G), FOX_HEAD_DIM, s)
    c = (parts[:, :LANES] + parts[:, LANES:2 * LANES] + parts[:, 2 * LANES:]
         + carry_ref[SUBLANES - 1:SUBLANES, :])
    carry_ref[...] = c[tm - SUBLANES:, :]

    cq = _rms(cq_raw, qn_ref[...]).astype(BF16)
    q = jnp.dot(cq, wuq_ref[...], preferred_element_type=F32) * mla_scale
    for hd in range(HEADS):
        sl = slice(hd * LANES, (hd + 1) * LANES)
        mq_ref[rows, sl] = _rope(q[:, sl], ra, rb, rc).astype(BF16)

    ckv = _rms(ckv_raw, kvn_ref[...]).astype(BF16)
    kr = _rope(kr_raw, ra, rb, rc)
    kn = jnp.dot(ckv, wuk_ref[...], preferred_element_type=F32)
    for hd in range(HEADS):
        sl = slice(hd * LANES, (hd + 1) * LANES)
        mk_ref[rows, sl] = (kn[:, sl] + kr).astype(BF16)
    _store_vt(mv_ref, jnp.dot(ckv, wuv_ref[...], preferred_element_type=F32), MLA_V_DIM, s)

    c_aug = jnp.dot(jnp.concatenate(_split3(c), axis=1), ec_ref[...], preferred_element_type=F32)
    spread_heads(fk_ref, proj(_EV_K, _EV_V), c_aug)


def _pad_heads(w, dh):
    k = w.shape[0]
    w = w.reshape(k, HEADS, dh)
    return jnp.pad(w, ((0, 0), (0, 0), (0, LANES - dh))).reshape(k, HEAD_COLS)


def _even_consts():
    cst = np.zeros((SUBLANES, HEAD_COLS), np.float32)
    ec = np.zeros((3 * LANES, HEAD_COLS), np.float32)
    for hd in range(HEADS):
        cst[0, hd * LANES + 64: hd * LANES + 67] = 1.0
        for p in range(3):
            ec[p * LANES + hd, hd * LANES + 64 + p] = -1.0
    return jnp.asarray(cst), jnp.asarray(ec, dtype=BF16)


def _rope_tables(seq):
    half = MLA_ROPE_DIM // 2
    inv = ROPE_THETA ** (-jnp.arange(0, MLA_ROPE_DIM, 2, dtype=F32) / MLA_ROPE_DIM)
    ang = jnp.arange(seq, dtype=F32)[:, None] * inv[None, :]
    cos, sin = jnp.cos(ang), jnp.sin(ang)
    zeros = jnp.zeros((seq, half), F32)
    ra = jnp.concatenate([jnp.ones((seq, 64), F32), cos, cos, jnp.zeros((seq, 32), F32)], axis=1)
    rb = jnp.concatenate([jnp.zeros((seq, 64), F32), zeros, sin, jnp.zeros((seq, 32), F32)], axis=1)
    rc = jnp.concatenate([jnp.zeros((seq, 64), F32), -sin, zeros, jnp.zeros((seq, 32), F32)], axis=1)
    return ra, rb, rc


def _even_proj(x, g, w_in, b_forget, q_norm, w_uq, kv_norm, w_ukv, rope, bsz, seq):
    t = x.shape[0]
    tk = ATTN_TILE
    tm = tk * PROJ_TILES_PER_STEP
    cuts = np.cumsum([FOX_WIDTH, FOX_WIDTH, FOX_WIDTH, HEADS, MLA_Q_RANK, MLA_KV_RANK]).tolist()
    wfq, wfk, wfv, wg, wcq, wckv, wkr = jnp.split(w_in, cuts, axis=1)
    w1 = jnp.concatenate([
        wfq, wfk, wfv,
        jnp.pad(wg, ((0, 0), (0, LANES - HEADS))), wcq, wckv,
        jnp.pad(wkr, ((0, 0), (64, LANES - 64 - MLA_ROPE_DIM)))], axis=1).astype(BF16)
    assert w1.shape[1] == _EV_WIDTH
    bf = jnp.pad(b_forget, (0, LANES - HEADS)).reshape(1, LANES)
    wuq = _pad_heads(w_uq, MLA_NOPE_DIM + MLA_ROPE_DIM).astype(BF16)
    wukv = w_ukv.reshape(MLA_KV_RANK, HEADS, MLA_NOPE_DIM + MLA_V_DIM)
    wuk = _pad_heads(wukv[:, :, :MLA_NOPE_DIM].reshape(MLA_KV_RANK, -1), MLA_NOPE_DIM).astype(BF16)
    wuv = wukv[:, :, MLA_NOPE_DIM:].reshape(MLA_KV_RANK, -1).astype(BF16)
    cst, ec = _even_consts()
    ra, rb, rc = rope
    steps_per_seq = seq // tm
    rope_spec = pl.BlockSpec((tm, LANES), lambda i: (i % steps_per_seq, 0))
    row_spec = pl.BlockSpec((tm, HEAD_COLS), lambda i: (i, 0))
    out_sds = jax.ShapeDtypeStruct((t, HEAD_COLS), BF16)
    vt_spec = _vt_spec(FOX_HEAD_DIM, PROJ_TILES_PER_STEP, tk, steps_per_seq)
    vt_sds = jax.ShapeDtypeStruct((bsz, HEADS, seq // tk, _vt_rows(FOX_HEAD_DIM), tk), BF16)
    return pl.pallas_call(
        functools.partial(_even_proj_kernel, steps_per_seq=steps_per_seq),
        grid=(t // tm,),
        in_specs=[pl.BlockSpec((tm, D_MODEL), lambda i: (i, 0)),
                  _const_spec((1, D_MODEL)), _const_spec(w1.shape), _const_spec((1, LANES)),
                  _const_spec((1, MLA_Q_RANK)), _const_spec(wuq.shape),
                  _const_spec((1, MLA_KV_RANK)), _const_spec(wuk.shape), _const_spec(wuv.shape),
                  _const_spec(ec.shape), rope_spec, rope_spec, rope_spec, _const_spec(cst.shape)],
        out_specs=[row_spec, row_spec, vt_spec, row_spec, row_spec, vt_spec],
        out_shape=[out_sds, out_sds, vt_sds, out_sds, out_sds, vt_sds],
        scratch_shapes=[pltpu.VMEM((SUBLANES, LANES), F32)],
        compiler_params=_params("arbitrary"),
        name="even_proj",
    )(x, g.reshape(1, -1), w1, bf, q_norm.reshape(1, -1), wuq, kv_norm.reshape(1, -1), wuk, wuv,
      ec, ra, rb, rc, cst)


def _alibi_slope(hd):
    return 2.0 ** (-8.0 * (hd + 1) / HEADS)


def _odd_proj_kernel(x_ref, g_ref, w_ref, q_ref, k_ref, v_ref, *, steps_per_seq):
    tm = v_ref.shape[4]
    n_tiles = x_ref.shape[0] // tm
    i = pl.program_id(0)
    scale = DIFF_HEAD_DIM ** -0.5 * LOG2E
    d = DIFF_HEAD_DIM
    lane = lax.broadcasted_iota(jnp.int32, (tm, LANES), 1)
    q_aug = jnp.where((lane >= d) & (lane < d + 3), 1.0, 0.0)
    hs = [_rms(x_ref[s * tm:(s + 1) * tm, :], g_ref[...]).astype(BF16) for s in range(n_tiles)]
    for s, h in enumerate(hs):
        rows = slice(s * tm, (s + 1) * tm)
        v = jnp.dot(h, w_ref[:, 2 * HEAD_COLS:], preferred_element_type=F32)
        k = jnp.dot(h, w_ref[:, HEAD_COLS:2 * HEAD_COLS], preferred_element_type=F32)
        _store_vt(v_ref, v, DIFF_V_DIM, s)
        q = jnp.dot(h, w_ref[:, :HEAD_COLS], preferred_element_type=F32)
        pos = (((i % steps_per_seq) * n_tiles + s) * tm
               + lax.broadcasted_iota(jnp.int32, (tm, 1), 0)).astype(F32)
        for hd in range(HEADS):
            sl = slice(hd * LANES, (hd + 1) * LANES)
            b_hi, b_mid, b_lo = (p.astype(F32)
                                 for p in _split3(pos * (_alibi_slope(hd) * LOG2E)))
            k_aug = jnp.where(lane == d, b_hi,
                              jnp.where(lane == d + 1, b_mid, jnp.where(lane == d + 2, b_lo, 0.0)))
            qb, kb = q[:, sl] * scale, k[:, sl]
            for br, (qs, ks) in enumerate(((qb, kb),
                                           (pltpu.roll(qb, d, 1), pltpu.roll(kb, d, 1)))):
                so = slice((2 * hd + br) * LANES, (2 * hd + br + 1) * LANES)
                q_ref[rows, so] = jnp.where(lane < d, qs, q_aug).astype(BF16)
                k_ref[rows, so] = jnp.where(lane < d, ks, k_aug).astype(BF16)


def _odd_proj(x, g, w_in, bsz, seq):
    t = x.shape[0]
    tk = ATTN_TILE
    tm = tk * PROJ_TILES_PER_STEP
    w = w_in.astype(BF16)
    steps_per_seq = seq // tm
    return pl.pallas_call(
        functools.partial(_odd_proj_kernel, steps_per_seq=steps_per_seq),
        grid=(t // tm,),
        in_specs=[pl.BlockSpec((tm, D_MODEL), lambda i: (i, 0)),
                  _const_spec((1, D_MODEL)), _const_spec(w.shape)],
        out_specs=[pl.BlockSpec((tm, 2 * HEAD_COLS), lambda i: (i, 0)),
                   pl.BlockSpec((tm, 2 * HEAD_COLS), lambda i: (i, 0)),
                   _vt_spec(DIFF_V_DIM, PROJ_TILES_PER_STEP, tk, steps_per_seq)],
        out_shape=[jax.ShapeDtypeStruct((t, 2 * HEAD_COLS), BF16),
                   jax.ShapeDtypeStruct((t, 2 * HEAD_COLS), BF16),
                   jax.ShapeDtypeStruct((bsz, HEADS, seq // tk, _vt_rows(DIFF_V_DIM), tk), BF16)],
        compiler_params=_params("arbitrary"),
        name="odd_proj",
    )(x, g.reshape(1, -1), w)


_NT = (((1,), (1,)), ((), ()))


def _attend(s, vt, m, acc, causal):
    if not causal:
        m_new = jnp.maximum(m, jnp.max(s, axis=0, keepdims=True))
        p = jnp.exp2(s - m_new).astype(BF16)
    else:
        nr, nc = s.shape[0] // LANES, s.shape[1] // LANES
        key = lax.broadcasted_iota(jnp.int32, (LANES, LANES), 0)
        qry = lax.broadcasted_iota(jnp.int32, (LANES, LANES), 1)
        sq = [[s[a * LANES:(a + 1) * LANES, b * LANES:(b + 1) * LANES] for b in range(nc)]
              for a in range(nr)]
        for a in range(nr):
            sq[a][a] = jnp.where(key <= qry, sq[a][a], NEG_INF)
        col_max = [functools.reduce(jnp.maximum, [jnp.max(sq[a][b], axis=0, keepdims=True)
                                                  for a in range(min(b + 1, nr))])
                   for b in range(nc)]
        m_new = jnp.maximum(m, jnp.concatenate(col_max, axis=1))
        p = jnp.concatenate(
            [jnp.concatenate(
                [jnp.exp2(sq[a][b] - m_new[:, b * LANES:(b + 1) * LANES]).astype(BF16)
                 if a <= b else jnp.zeros((LANES, LANES), BF16) for b in range(nc)], axis=1)
             for a in range(nr)], axis=0)
    acc = jnp.exp2(m - m_new) * acc + jnp.dot(vt, p, preferred_element_type=F32)
    return m_new, acc


def _stream_attention(qk, vts, finish, q_tiles, t, tk, rows, n, lag):
    r = t // tk
    items = [(qi, j, u, k) for qi in range(q_tiles) for j in range(qi + 1) for u in range(r)
             for k in range(n)]

    def first_query(qi, j, u):
        return u * tk if j == qi else 0

    pending, issued = [], 0
    state = None
    for i, (qi, j, u, k) in enumerate(items):
        while issued < min(i + lag + 1, len(items)):
            nqi, nj, nu, nk = items[issued]
            pending.append(qk(nk, nqi, nj, nu, first_query(nqi, nj, nu)))
            issued += 1
        if j == 0 and u == 0 and k == 0:
            state = [(jnp.full((1, t), NEG_INF, F32), jnp.zeros((rows, t), F32))] * n
        q0 = first_query(qi, j, u)
        m, acc = state[k]
        m_hi, acc_hi = _attend(pending.pop(0), vts(k, j, u), m[:, q0:], acc[:, q0:], j == qi)
        state[k] = ((jnp.concatenate([m[:, :q0], m_hi], axis=1),
                     jnp.concatenate([acc[:, :q0], acc_hi], axis=1)) if q0 else (m_hi, acc_hi))
        if j == qi and u == r - 1 and k == n - 1:
            finish(qi, [acc for _, acc in state])


def _flash_kernel(q_ref, k_ref, vt_ref, o_ref):
    n, chunks, rows, t = vt_ref.shape[1:]
    tk = FLASH_KEYS
    dv = rows - BF16_ROWS

    def qk(hd, qi, j, u, q0):
        kc = k_ref[0, j * t + u * tk:j * t + (u + 1) * tk, hd * LANES:(hd + 1) * LANES]
        q = q_ref[0, qi * t + q0:(qi + 1) * t, hd * LANES:(hd + 1) * LANES]
        return lax.dot_general(kc, q, _NT, preferred_element_type=F32)

    def vts(hd, j, u):
        return vt_ref[0, hd, j, :, u * tk:(u + 1) * tk]

    def finish(qi, accs):
        o_t = jnp.concatenate([a[:dv] / a[dv:dv + 1] for a in accs], axis=0)
        o_ref[0, qi * t:(qi + 1) * t, :] = o_t.T.astype(BF16)

    _stream_attention(qk, vts, finish, chunks, t, tk, rows, n, FLASH_LAG)


def _flash_attention(q, k, vt):
    b, s, _ = q.shape
    _, _, chunks, rows, t = vt.shape
    n = FLASH_HEADS_PER_STEP
    dv = rows - BF16_ROWS
    return pl.pallas_call(
        _flash_kernel,
        grid=(b, HEADS // n),
        in_specs=[pl.BlockSpec((1, s, n * LANES), lambda bi, hp: (bi, 0, hp)),
                  pl.BlockSpec((1, s, n * LANES), lambda bi, hp: (bi, 0, hp)),
                  pl.BlockSpec((1, n, chunks, rows, t), lambda bi, hp: (bi, hp, 0, 0, 0))],
        out_specs=pl.BlockSpec((1, s, n * dv), lambda bi, hp: (bi, 0, hp)),
        out_shape=jax.ShapeDtypeStruct((b, s, HEADS * dv), BF16),
        compiler_params=_params("arbitrary", "arbitrary"),
        name="flash_attention",
    )(q, k, vt)


def _diff_kernel(q_ref, k_ref, vt_ref, lq1_ref, lk1_ref, lq2_ref, lk2_ref, sub_ref, o_ref,
                 *, lambda_init):
    n, chunks, rows, t = vt_ref.shape[1:]
    tk = DIFF_KEYS
    dv = rows - BF16_ROWS

    def qk(st, qi, j, u, q0):
        kc = k_ref[0, j * t + u * tk:j * t + (u + 1) * tk, st * LANES:(st + 1) * LANES]
        q = q_ref[0, qi * t + q0:(qi + 1) * t, st * LANES:(st + 1) * LANES]
        return lax.dot_general(kc, q, _NT, preferred_element_type=F32)

    def vts(st, j, u):
        return vt_ref[0, st // 2, j, :, u * tk:(u + 1) * tk]

    lam = (jnp.exp(jnp.sum(lq1_ref[...] * lk1_ref[...], axis=1, keepdims=True))
           - jnp.exp(jnp.sum(lq2_ref[...] * lk2_ref[...], axis=1, keepdims=True)) + lambda_init)

    def finish(qi, accs):
        for hd in range(n):
            a1, a2 = accs[2 * hd], accs[2 * hd + 1]
            o = (a1[:dv] / a1[dv:dv + 1] - lam * (a2[:dv] / a2[dv:dv + 1])).T
            o_ref[0, qi * t:(qi + 1) * t, hd * dv:(hd + 1) * dv] = (
                _rms(o, sub_ref[...]) * (1.0 - lambda_init)).astype(BF16)

    _stream_attention(qk, vts, finish, chunks, t, tk, rows, 2 * n, DIFF_LAG)


def _diff_attention(q, k, vt, lq1, lk1, lq2, lk2, subln, lambda_init):
    b, s, _ = q.shape
    _, _, chunks, rows, t = vt.shape
    vec = lambda a: a.reshape(1, -1)
    n = DIFF_HEADS_PER_STEP
    return pl.pallas_call(
        functools.partial(_diff_kernel, lambda_init=lambda_init),
        grid=(b, HEADS // n),
        in_specs=[pl.BlockSpec((1, s, 2 * n * LANES), lambda bi, hd: (bi, 0, hd)),
                  pl.BlockSpec((1, s, 2 * n * LANES), lambda bi, hd: (bi, 0, hd)),
                  pl.BlockSpec((1, n, chunks, rows, t), lambda bi, hd: (bi, hd, 0, 0, 0)),
                  _const_spec((1, DIFF_HEAD_DIM)), _const_spec((1, DIFF_HEAD_DIM)),
                  _const_spec((1, DIFF_HEAD_DIM)), _const_spec((1, DIFF_HEAD_DIM)),
                  _const_spec((1, DIFF_V_DIM))],
        out_specs=pl.BlockSpec((1, s, n * LANES), lambda bi, hd: (bi, 0, hd)),
        out_shape=jax.ShapeDtypeStruct((b, s, HEADS * (rows - BF16_ROWS)), BF16),
        compiler_params=_params("arbitrary", "arbitrary"),
        name="diff_attention",
    )(q, k, vt, vec(lq1), vec(lk1), vec(lq2), vec(lk2), vec(subln))


_N_FFN_CHUNKS = D_FF // FFN_CHUNK
_GELU_C = math.sqrt(2.0 / math.pi)


def _mix_ffn_kernel(*refs, n_in, steps_per_seq):
    a_refs, wo_refs = refs[:n_in], refs[n_in:2 * n_in]
    (gmix_ref, x_ref, gpre_ref, wup_ref, cw_ref, wdn_ref, gpost_ref, o_ref,
     h_ref, hist_ref, gva_ref, gvb_ref, *mid_refs) = refs[2 * n_in:]
    tm = h_ref.shape[0]
    nc = _N_FFN_CHUNKS
    i = pl.program_id(0)

    @pl.when(i % steps_per_seq == 0)
    def _():
        hist_ref[...] = jnp.zeros_like(hist_ref)

    def out_proj(rows):
        mix = jnp.dot(a_refs[0][rows, :], wo_refs[0][...], preferred_element_type=F32)
        for a_ref, wo_ref in zip(a_refs[1:], wo_refs[1:]):
            mix = mix + jnp.dot(a_ref[rows, :], wo_ref[...], preferred_element_type=F32)
        return mix

    def pre_norms(rows, mix):
        x1 = x_ref[rows, :] + _rms(mix, gmix_ref[...])
        o_ref[rows, :] = x1
        h_ref[...] = _rms(x1, gpre_ref[...]).astype(BF16)

    def up(c):
        return jnp.dot(h_ref[...], wup_ref[c], preferred_element_type=F32)

    def down(mid_ref):
        f = jnp.dot(mid_ref[0], wdn_ref[0], preferred_element_type=F32)
        for c in range(1, nc):
            f = f + jnp.dot(mid_ref[c], wdn_ref[c], preferred_element_type=F32)
        return f

    def gated(c, gv_ref, mid_ref):
        gv_ref[0:SUBLANES, :FFN_CHUNK] = hist_ref[c]
        gate = gv_ref[SUBLANES:, :FFN_CHUNK]
        val = gv_ref[SUBLANES:, FFN_CHUNK:]
        hist_ref[c] = gv_ref[tm:, :FFN_CHUNK]
        cw = cw_ref[c]
        pre = (gv_ref[SUBLANES - 2:SUBLANES - 2 + tm, :FFN_CHUNK] * cw[0:1, :]
               + gv_ref[SUBLANES - 1:SUBLANES - 1 + tm, :FFN_CHUNK] * cw[1:2, :]
               + gate * cw[2:3, :] + cw[3:4, :])
        pre = pre.astype(BF16)
        act = 0.5 * pre * (1.0 + jnp.tanh(_GELU_C * (pre + 0.044715 * (pre * pre * pre))))
        mid_ref[c] = act * val.astype(BF16)

    def chunks_after_first(mid_ref):
        def body(k, _):
            gated(2 * k, gva_ref, mid_ref)
            gvb_ref[SUBLANES:, :] = up(2 * k + 1)
            gated(2 * k + 1, gvb_ref, mid_ref)
            gva_ref[SUBLANES:, :] = up(2 * k + 2)
            return 0

        for k in range(nc // 2):
            body(k, 0)
        gated(nc - 1, gva_ref, mid_ref)

    assert nc % 2 == 1
    n_tiles = len(mid_refs)
    rows = [slice(s * tm, (s + 1) * tm) for s in range(n_tiles)]
    pre_norms(rows[0], out_proj(rows[0]))
    gva_ref[SUBLANES:, :] = up(0)
    for s in range(n_tiles):
        chunks_after_first(mid_refs[s])
        if s + 1 < n_tiles:
            mix_next = out_proj(rows[s + 1])
        f = down(mid_refs[s])
        if s + 1 < n_tiles:
            pre_norms(rows[s + 1], mix_next)
            gva_ref[SUBLANES:, :] = up(0)
        o_ref[rows[s], :] = o_ref[rows[s], :] + _rms(f, gpost_ref[...])


def _mix_ffn(acts, w_outs, g_mix, x, g_pre, w_up, conv_w, conv_b, w_down, g_post, seq):
    t = x.shape[0]
    tm = FFN_ROWS
    nc = _N_FFN_CHUNKS
    n_in = len(acts)
    wg = w_up[:, :D_FF].reshape(D_MODEL, nc, FFN_CHUNK)
    wv = w_up[:, D_FF:].reshape(D_MODEL, nc, FFN_CHUNK)
    wup = jnp.concatenate([wg, wv], axis=2).transpose(1, 0, 2).astype(BF16)
    wdn = w_down.reshape(nc, FFN_CHUNK, D_MODEL).astype(BF16)
    cw = jnp.concatenate([conv_w, conv_b[None, :],
                          jnp.zeros((SUBLANES - CONV_WIDTH - 1, D_FF), F32)], axis=0)
    cw = cw.reshape(SUBLANES, nc, FFN_CHUNK).transpose(1, 0, 2)
    step_rows = tm * FFN_TILES_PER_STEP
    assert seq % step_rows == 0
    x_spec = pl.BlockSpec((step_rows, D_MODEL), lambda i: (i, 0))
    vec_spec = _const_spec((1, D_MODEL))
    return pl.pallas_call(
        functools.partial(_mix_ffn_kernel, n_in=n_in, steps_per_seq=seq // step_rows),
        grid=(t // step_rows,),
        in_specs=([pl.BlockSpec((step_rows, a.shape[1]), lambda i: (i, 0)) for a in acts]
                  + [_const_spec(w.shape) for w in w_outs]
                  + [vec_spec, x_spec, vec_spec, _const_spec(wup.shape), _const_spec(cw.shape),
                     _const_spec(wdn.shape), vec_spec]),
        out_specs=x_spec,
        out_shape=jax.ShapeDtypeStruct(x.shape, F32),
        scratch_shapes=([pltpu.VMEM((tm, D_MODEL), BF16),
                         pltpu.VMEM((nc, SUBLANES, FFN_CHUNK), F32),
                         pltpu.VMEM((tm + SUBLANES, 2 * FFN_CHUNK), F32),
                         pltpu.VMEM((tm + SUBLANES, 2 * FFN_CHUNK), F32)]
                        + [pltpu.VMEM((nc, tm, FFN_CHUNK), BF16)] * FFN_TILES_PER_STEP),
        compiler_params=_params("arbitrary"),
        name="mix_ffn",
    )(*acts, *w_outs, g_mix.reshape(1, -1), x, g_pre.reshape(1, -1), wup, cw, wdn,
      g_post.reshape(1, -1))


def kernel(x, norm_mix_pre, norm_mix_post, norm_ffn_pre, norm_ffn_post, even_w_in, even_b_forget, even_q_norm, even_w_uq, even_kv_norm, even_w_ukv, even_w_out, odd_w_in, odd_lambda_q1, odd_lambda_k1, odd_lambda_q2, odd_lambda_k2, odd_subln, odd_w_out, ffn_w_up, ffn_conv_w, ffn_conv_b, ffn_w_down):
    bsz, seq, d = x.shape
    assert d == D_MODEL and seq % (ATTN_TILE * PROJ_TILES_PER_STEP) == 0
    t = bsz * seq
    rope = _rope_tables(seq)
    xf = x.reshape(t, d)
    for layer in range(DEPTH):
        i = layer // 2
        if layer % 2 == 0:
            fq, fk, fvt, mq, mk, mvt = _even_proj(
                xf, norm_mix_pre[layer], even_w_in[i], even_b_forget[i], even_q_norm[i],
                even_w_uq[i], even_kv_norm[i], even_w_ukv[i], rope, bsz, seq)
            shp = (bsz, seq, HEAD_COLS)
            fo = _flash_attention(fq.reshape(shp), fk.reshape(shp), fvt)
            mo = _flash_attention(mq.reshape(shp), mk.reshape(shp), mvt)
            w_out = even_w_out[i].astype(BF16)
            acts = [fo.reshape(t, -1), mo.reshape(t, -1)]
            w_outs = [w_out[:FOX_WIDTH], w_out[FOX_WIDTH:]]
        else:
            lambda_init = 0.8 - 0.6 * math.exp(-0.3 * layer)
            q, k, vt = _odd_proj(xf, norm_mix_pre[layer], odd_w_in[i], bsz, seq)
            o = _diff_attention(q.reshape(bsz, seq, -1), k.reshape(bsz, seq, -1), vt,
                                odd_lambda_q1[i], odd_lambda_k1[i],
                                odd_lambda_q2[i], odd_lambda_k2[i], odd_subln[i], lambda_init)
            acts, w_outs = [o.reshape(t, -1)], [odd_w_out[i].astype(BF16)]
        xf = _mix_ffn(acts, w_outs, norm_mix_post[layer], xf, norm_ffn_pre[layer], ffn_w_up[layer],
                      ffn_conv_w[layer], ffn_conv_b[layer], ffn_w_down[layer],
                      norm_ffn_post[layer], seq)
    return xf.reshape(bsz, seq, d)
```

```python
import functools
import math

import numpy as np
import jax
import jax.numpy as jnp
from jax import lax
from jax.experimental import pallas as pl
from jax.experimental.pallas import tpu as pltpu

D_MODEL = 1024
DEPTH = 4
RMS_EPS = 1e-6
NEG_INF = -1e30
HEADS = 8
FOX_HEAD_DIM = 64
FOX_WIDTH = HEADS * FOX_HEAD_DIM
MLA_NOPE_DIM = 64
MLA_ROPE_DIM = 32
MLA_V_DIM = 64
MLA_Q_RANK = 384
MLA_KV_RANK = 256
ROPE_THETA = 10000.0
DIFF_HEAD_DIM = 64
DIFF_V_DIM = 2 * DIFF_HEAD_DIM
D_FF = 2816
CONV_WIDTH = 3

LANES = 128
SUBLANES = 8
LOG2E = 1.4426950408889634
HEAD_COLS = HEADS * LANES
VMEM_LIMIT_BYTES = 56 * 1024 * 1024

PROJ_TILES_PER_STEP = 2
FFN_ROWS = 512
FFN_TILES_PER_STEP = 2
FFN_CHUNK = 256
ATTN_TILE = 512
FLASH_HEADS_PER_STEP, FLASH_KEYS, FLASH_LAG = 2, 256, 2
DIFF_HEADS_PER_STEP, DIFF_KEYS, DIFF_LAG = 1, 256, 2
F32 = jnp.float32
BF16 = jnp.bfloat16


def _params(*sem):
    return pltpu.CompilerParams(dimension_semantics=sem, vmem_limit_bytes=VMEM_LIMIT_BYTES)


def _const_spec(shape):
    nd = len(shape)
    return pl.BlockSpec(shape, lambda *_: (0,) * nd, pipeline_mode=pl.Buffered(1))


def _rms(x, g):
    return x * lax.rsqrt(jnp.mean(x * x, axis=-1, keepdims=True) + RMS_EPS) * g


def _split3(x):
    hi = x.astype(BF16)
    r1 = x - hi.astype(F32)
    mid = r1.astype(BF16)
    lo = (r1 - mid.astype(F32)).astype(BF16)
    return hi, mid, lo


def _rope(blk, ra, rb, rc):
    return blk * ra + pltpu.roll(blk, 16, 1) * rb + pltpu.roll(blk, LANES - 16, 1) * rc


def _vt_rows(dv):
    return dv


def _store_vt(vt_ref, v, dv, chunk):
    per = LANES // dv
    for blk in range(v.shape[1] // LANES):
        vt = v[:, blk * LANES:(blk + 1) * LANES].T.astype(BF16)
        for s in range(per):
            vt_ref[0, blk * per + s, chunk, :, :] = vt[s * dv:(s + 1) * dv, :]


def _vt_spec(dv, chunks_per_step, tk, steps_per_seq):
    return pl.BlockSpec((1, HEADS, chunks_per_step, _vt_rows(dv), tk),
                        lambda i: (i // steps_per_seq, 0, i % steps_per_seq, 0, 0))


_EV_Q, _EV_K, _EV_V = 0, FOX_WIDTH, 2 * FOX_WIDTH
_EV_G = _EV_V + FOX_WIDTH
_EV_CQ = _EV_G + LANES
_EV_CKV = _EV_CQ + MLA_Q_RANK
_EV_KR = _EV_CKV + MLA_KV_RANK
_EV_WIDTH = _EV_KR + LANES


def _even_proj_kernel(x_ref, g_ref, w_ref, bf_ref, qn_ref, wuq_ref, kvn_ref, wuk_ref, wuv_ref,
                      ec_ref, ra_ref, rb_ref, rc_ref, cst_ref,
                      fq_ref, fk_ref, fv_ref, mq_ref, mk_ref, mv_ref, carry_ref,
                      *, steps_per_seq):
    tm = fv_ref.shape[4]
    n_tiles = x_ref.shape[0] // tm
    i = pl.program_id(0)

    @pl.when(i % steps_per_seq == 0)
    def _():
        carry_ref[...] = jnp.zeros_like(carry_ref)

    hs = [_rms(x_ref[s * tm:(s + 1) * tm, :], g_ref[...]).astype(BF16) for s in range(n_tiles)]
    for s, h in enumerate(hs):
        _even_proj_tile(h, s, slice(s * tm, (s + 1) * tm), w_ref, bf_ref, qn_ref, wuq_ref, kvn_ref,
                        wuk_ref, wuv_ref, ec_ref, ra_ref, rb_ref, rc_ref, cst_ref,
                        fq_ref, fk_ref, fv_ref, mq_ref, mk_ref, mv_ref, carry_ref)


def _even_proj_tile(h, s, rows, w_ref, bf_ref, qn_ref, wuq_ref, kvn_ref, wuk_ref, wuv_ref,
                    ec_ref, ra_ref, rb_ref, rc_ref, cst_ref,
                    fq_ref, fk_ref, fv_ref, mq_ref, mk_ref, mv_ref, carry_ref):
    tm = h.shape[0]
    q_aug = cst_ref[0:1, :]

    def proj(lo, hi):
        return jnp.dot(h, w_ref[:, lo:hi], preferred_element_type=F32)

    lane = lax.broadcasted_iota(jnp.int32, (tm, LANES), 1)

    def spread_heads(o_ref, packed, aug):
        for pair in range(HEADS // 2):
            blk = packed[:, pair * LANES:(pair + 1) * LANES]
            for odd, src in enumerate((blk, pltpu.roll(blk, FOX_HEAD_DIM, 1))):
                sl = slice((2 * pair + odd) * LANES, (2 * pair + odd + 1) * LANES)
                o_ref[rows, sl] = jnp.where(lane < FOX_HEAD_DIM, src, aug[:, sl]).astype(BF16)

    ra, rb, rc = ra_ref[rows, :], rb_ref[rows, :], rc_ref[rows, :]
    fox_scale = FOX_HEAD_DIM ** -0.5 * LOG2E
    mla_scale = (MLA_NOPE_DIM + MLA_ROPE_DIM) ** -0.5 * LOG2E

    z = proj(_EV_G, _EV_CQ) + bf_ref[...]
    cq_raw = proj(_EV_CQ, _EV_CKV)
    ckv_raw = proj(_EV_CKV, _EV_KR)
    kr_raw = proj(_EV_KR, _EV_WIDTH)

    logf = (jnp.minimum(z, 0.0) - jnp.log1p(jnp.exp(-jnp.abs(z)))) * LOG2E
    spread_heads(fq_ref, proj(_EV_Q, _EV_K) * fox_scale, q_aug)

    row = lax.broadcasted_iota(jnp.int32, (tm, tm), 0)
    col = lax.broadcasted_iota(jnp.int32, (tm, tm), 1)
    tril = jnp.where(row >= col, 1.0, 0.0).astype(BF16)
    parts = jnp.dot(tril, jnp.concatenate(_split3(logf), axis=1), preferred_element_type=F32)
    _store_vt(fv_ref, proj(_EV_V, _EV_G), FOX_HEAD_DIM, s)
    c = (parts[:, :LANES] + parts[:, LANES:2 * LANES] + parts[:, 2 * LANES:]
         + carry_ref[SUBLANES - 1:SUBLANES, :])
    carry_ref[...] = c[tm - SUBLANES:, :]

    cq = _rms(cq_raw, qn_ref[...]).astype(BF16)
    q = jnp.dot(cq, wuq_ref[...], preferred_element_type=F32) * mla_scale
    for hd in range(HEADS):
        sl = slice(hd * LANES, (hd + 1) * LANES)
        mq_ref[rows, sl] = _rope(q[:, sl], ra, rb, rc).astype(BF16)

    ckv = _rms(ckv_raw, kvn_ref[...]).astype(BF16)
    kr = _rope(kr_raw, ra, rb, rc)
    kn = jnp.dot(ckv, wuk_ref[...], preferred_element_type=F32)
    for hd in range(HEADS):
        sl = slice(hd * LANES, (hd + 1) * LANES)
        mk_ref[rows, sl] = (kn[:, sl] + kr).astype(BF16)
    _store_vt(mv_ref, jnp.dot(ckv, wuv_ref[...], preferred_element_type=F32), MLA_V_DIM, s)

    c_aug = jnp.dot(jnp.concatenate(_split3(c), axis=1), ec_ref[...], preferred_element_type=F32)
    spread_heads(fk_ref, proj(_EV_K, _EV_V), c_aug)


def _pad_heads(w, dh):
    k = w.shape[0]
    w = w.reshape(k, HEADS, dh)
    return jnp.pad(w, ((0, 0), (0, 0), (0, LANES - dh))).reshape(k, HEAD_COLS)


def _even_consts():
    cst = np.zeros((SUBLANES, HEAD_COLS), np.float32)
    ec = np.zeros((3 * LANES, HEAD_COLS), np.float32)
    for hd in range(HEADS):
        cst[0, hd * LANES + 64: hd * LANES + 67] = 1.0
        for p in range(3):
            ec[p * LANES + hd, hd * LANES + 64 + p] = -1.0
    return jnp.asarray(cst), jnp.asarray(ec, dtype=BF16)


def _rope_tables(seq):
    half = MLA_ROPE_DIM // 2
    inv = ROPE_THETA ** (-jnp.arange(0, MLA_ROPE_DIM, 2, dtype=F32) / MLA_ROPE_DIM)
    ang = jnp.arange(seq, dtype=F32)[:, None] * inv[None, :]
    cos, sin = jnp.cos(ang), jnp.sin(ang)
    zeros = jnp.zeros((seq, half), F32)
    ra = jnp.concatenate([jnp.ones((seq, 64), F32), cos, cos, jnp.zeros((seq, 32), F32)], axis=1)
    rb = jnp.concatenate([jnp.zeros((seq, 64), F32), zeros, sin, jnp.zeros((seq, 32), F32)], axis=1)
    rc = jnp.concatenate([jnp.zeros((seq, 64), F32), -sin, zeros, jnp.zeros((seq, 32), F32)], axis=1)
    return ra, rb, rc


def _even_proj(x, g, w_in, b_forget, q_norm, w_uq, kv_norm, w_ukv, rope, bsz, seq):
    t = x.shape[0]
    tk = ATTN_TILE
    tm = tk * PROJ_TILES_PER_STEP
    cuts = np.cumsum([FOX_WIDTH, FOX_WIDTH, FOX_WIDTH, HEADS, MLA_Q_RANK, MLA_KV_RANK]).tolist()
    wfq, wfk, wfv, wg, wcq, wckv, wkr = jnp.split(w_in, cuts, axis=1)
    w1 = jnp.concatenate([
        wfq, wfk, wfv,
        jnp.pad(wg, ((0, 0), (0, LANES - HEADS))), wcq, wckv,
        jnp.pad(wkr, ((0, 0), (64, LANES - 64 - MLA_ROPE_DIM)))], axis=1).astype(BF16)
    assert w1.shape[1] == _EV_WIDTH
    bf = jnp.pad(b_forget, (0, LANES - HEADS)).reshape(1, LANES)
    wuq = _pad_heads(w_uq, MLA_NOPE_DIM + MLA_ROPE_DIM).astype(BF16)
    wukv = w_ukv.reshape(MLA_KV_RANK, HEADS, MLA_NOPE_DIM + MLA_V_DIM)
    wuk = _pad_heads(wukv[:, :, :MLA_NOPE_DIM].reshape(MLA_KV_RANK, -1), MLA_NOPE_DIM).astype(BF16)
    wuv = wukv[:, :, MLA_NOPE_DIM:].reshape(MLA_KV_RANK, -1).astype(BF16)
    cst, ec = _even_consts()
    ra, rb, rc = rope
    steps_per_seq = seq // tm
    rope_spec = pl.BlockSpec((tm, LANES), lambda i: (i % steps_per_seq, 0))
    row_spec = pl.BlockSpec((tm, HEAD_COLS), lambda i: (i, 0))
    out_sds = jax.ShapeDtypeStruct((t, HEAD_COLS), BF16)
    vt_spec = _vt_spec(FOX_HEAD_DIM, PROJ_TILES_PER_STEP, tk, steps_per_seq)
    vt_sds = jax.ShapeDtypeStruct((bsz, HEADS, seq // tk, _vt_rows(FOX_HEAD_DIM), tk), BF16)
    return pl.pallas_call(
        functools.partial(_even_proj_kernel, steps_per_seq=steps_per_seq),
        grid=(t // tm,),
        in_specs=[pl.BlockSpec((tm, D_MODEL), lambda i: (i, 0)),
                  _const_spec((1, D_MODEL)), _const_spec(w1.shape), _const_spec((1, LANES)),
                  _const_spec((1, MLA_Q_RANK)), _const_spec(wuq.shape),
                  _const_spec((1, MLA_KV_RANK)), _const_spec(wuk.shape), _const_spec(wuv.shape),
                  _const_spec(ec.shape), rope_spec, rope_spec, rope_spec, _const_spec(cst.shape)],
        out_specs=[row_spec, row_spec, vt_spec, row_spec, row_spec, vt_spec],
        out_shape=[out_sds, out_sds, vt_sds, out_sds, out_sds, vt_sds],
        scratch_shapes=[pltpu.VMEM((SUBLANES, LANES), F32)],
        compiler_params=_params("arbitrary"),
        name="even_proj",
    )(x, g.reshape(1, -1), w1, bf, q_norm.reshape(1, -1), wuq, kv_norm.reshape(1, -1), wuk, wuv,
      ec, ra, rb, rc, cst)


def _alibi_slope(hd):
    return 2.0 ** (-8.0 * (hd + 1) / HEADS)


def _odd_proj_kernel(x_ref, g_ref, w_ref, q_ref, k_ref, v_ref, *, steps_per_seq):
    tm = v_ref.shape[4]
    n_tiles = x_ref.shape[0] // tm
    i = pl.program_id(0)
    scale = DIFF_HEAD_DIM ** -0.5 * LOG2E
    d = DIFF_HEAD_DIM
    lane = lax.broadcasted_iota(jnp.int32, (tm, LANES), 1)
    q_aug = jnp.where((lane >= d) & (lane < d + 3), 1.0, 0.0)
    hs = [_rms(x_ref[s * tm:(s + 1) * tm, :], g_ref[...]).astype(BF16) for s in range(n_tiles)]
    for s, h in enumerate(hs):
        rows = slice(s * tm, (s + 1) * tm)
        v = jnp.dot(h, w_ref[:, 2 * HEAD_COLS:], preferred_element_type=F32)
        k = jnp.dot(h, w_ref[:, HEAD_COLS:2 * HEAD_COLS], preferred_element_type=F32)
        _store_vt(v_ref, v, DIFF_V_DIM, s)
        q = jnp.dot(h, w_ref[:, :HEAD_COLS], preferred_element_type=F32)
        pos = (((i % steps_per_seq) * n_tiles + s) * tm
               + lax.broadcasted_iota(jnp.int32, (tm, 1), 0)).astype(F32)
        for hd in range(HEADS):
            sl = slice(hd * LANES, (hd + 1) * LANES)
            b_hi, b_mid, b_lo = (p.astype(F32)
                                 for p in _split3(pos * (_alibi_slope(hd) * LOG2E)))
            k_aug = jnp.where(lane == d, b_hi,
                              jnp.where(lane == d + 1, b_mid, jnp.where(lane == d + 2, b_lo, 0.0)))
            qb, kb = q[:, sl] * scale, k[:, sl]
            for br, (qs, ks) in enumerate(((qb, kb),
                                           (pltpu.roll(qb, d, 1), pltpu.roll(kb, d, 1)))):
                so = slice((2 * hd + br) * LANES, (2 * hd + br + 1) * LANES)
                q_ref[rows, so] = jnp.where(lane < d, qs, q_aug).astype(BF16)
                k_ref[rows, so] = jnp.where(lane < d, ks, k_aug).astype(BF16)


def _odd_proj(x, g, w_in, bsz, seq):
    t = x.shape[0]
    tk = ATTN_TILE
    tm = tk * PROJ_TILES_PER_STEP
    w = w_in.astype(BF16)
    steps_per_seq = seq // tm
    return pl.pallas_call(
        functools.partial(_odd_proj_kernel, steps_per_seq=steps_per_seq),
        grid=(t // tm,),
        in_specs=[pl.BlockSpec((tm, D_MODEL), lambda i: (i, 0)),
                  _const_spec((1, D_MODEL)), _const_spec(w.shape)],
        out_specs=[pl.BlockSpec((tm, 2 * HEAD_COLS), lambda i: (i, 0)),
                   pl.BlockSpec((tm, 2 * HEAD_COLS), lambda i: (i, 0)),
                   _vt_spec(DIFF_V_DIM, PROJ_TILES_PER_STEP, tk, steps_per_seq)],
        out_shape=[jax.ShapeDtypeStruct((t, 2 * HEAD_COLS), BF16),
                   jax.ShapeDtypeStruct((t, 2 * HEAD_COLS), BF16),
                   jax.ShapeDtypeStruct((bsz, HEADS, seq // tk, _vt_rows(DIFF_V_DIM), tk), BF16)],
        compiler_params=_params("arbitrary"),
        name="odd_proj",
    )(x, g.reshape(1, -1), w)


_NT = (((1,), (1,)), ((), ()))


def _attend(s, vt, m, acc, causal):
    if not causal:
        m_new = jnp.maximum(m, jnp.max(s, axis=0, keepdims=True))
        e = jnp.exp2(s - m_new)
        p, p_sum = e.astype(BF16), jnp.sum(e, axis=0, keepdims=True)
    else:
        nr, nc = s.shape[0] // LANES, s.shape[1] // LANES
        key = lax.broadcasted_iota(jnp.int32, (LANES, LANES), 0)
        qry = lax.broadcasted_iota(jnp.int32, (LANES, LANES), 1)
        sq = [[s[a * LANES:(a + 1) * LANES, b * LANES:(b + 1) * LANES] for b in range(nc)]
              for a in range(nr)]
        for a in range(nr):
            sq[a][a] = jnp.where(key <= qry, sq[a][a], NEG_INF)
        col_max = [functools.reduce(jnp.maximum, [jnp.max(sq[a][b], axis=0, keepdims=True)
                                                  for a in range(min(b + 1, nr))])
                   for b in range(nc)]
        m_new = jnp.maximum(m, jnp.concatenate(col_max, axis=1))
        e = [[jnp.exp2(sq[a][b] - m_new[:, b * LANES:(b + 1) * LANES]) if a <= b else None
              for b in range(nc)] for a in range(nr)]
        p = jnp.concatenate(
            [jnp.concatenate(
                [e[a][b].astype(BF16) if a <= b else jnp.zeros((LANES, LANES), BF16)
                 for b in range(nc)], axis=1) for a in range(nr)], axis=0)
        p_sum = jnp.concatenate(
            [functools.reduce(jnp.add, [jnp.sum(e[a][b], axis=0, keepdims=True)
                                        for a in range(min(b + 1, nr))]) for b in range(nc)],
            axis=1)
    upd = jnp.concatenate([jnp.dot(vt, p, preferred_element_type=F32),
                           jnp.broadcast_to(p_sum, (SUBLANES, p_sum.shape[1]))], axis=0)
    return m_new, jnp.exp2(m - m_new) * acc + upd


def _stream_attention(qk, vts, finish, q_tiles, t, tk, rows, n, lag):
    r = t // tk
    items = [(qi, j, u, k) for qi in range(q_tiles) for j in range(qi + 1) for u in range(r)
             for k in range(n)]

    def first_query(qi, j, u):
        return u * tk if j == qi else 0

    pending, issued = [], 0
    state = None
    for i, (qi, j, u, k) in enumerate(items):
        while issued < min(i + lag + 1, len(items)):
            nqi, nj, nu, nk = items[issued]
            pending.append(qk(nk, nqi, nj, nu, first_query(nqi, nj, nu)))
            issued += 1
        if j == 0 and u == 0 and k == 0:
            state = [(jnp.full((1, t), NEG_INF, F32), jnp.zeros((rows + SUBLANES, t), F32))] * n
        q0 = first_query(qi, j, u)
        m, acc = state[k]
        m_hi, acc_hi = _attend(pending.pop(0), vts(k, j, u), m[:, q0:], acc[:, q0:], j == qi)
        state[k] = ((jnp.concatenate([m[:, :q0], m_hi], axis=1),
                     jnp.concatenate([acc[:, :q0], acc_hi], axis=1)) if q0 else (m_hi, acc_hi))
        if j == qi and u == r - 1 and k == n - 1:
            finish(qi, [acc for _, acc in state])


def _flash_kernel(q_ref, k_ref, vt_ref, o_ref):
    n, chunks, rows, t = vt_ref.shape[1:]
    tk = FLASH_KEYS
    dv = rows

    def qk(hd, qi, j, u, q0):
        kc = k_ref[0, j * t + u * tk:j * t + (u + 1) * tk, hd * LANES:(hd + 1) * LANES]
        q = q_ref[0, qi * t + q0:(qi + 1) * t, hd * LANES:(hd + 1) * LANES]
        return lax.dot_general(kc, q, _NT, preferred_element_type=F32)

    def vts(hd, j, u):
        return vt_ref[0, hd, j, :, u * tk:(u + 1) * tk]

    def finish(qi, accs):
        o_t = jnp.concatenate([a[:dv] / a[dv:dv + 1] for a in accs], axis=0)
        o_ref[0, qi * t:(qi + 1) * t, :] = o_t.T.astype(BF16)

    _stream_attention(qk, vts, finish, chunks, t, tk, rows, n, FLASH_LAG)


def _flash_attention(q, k, vt):
    b, s, _ = q.shape
    _, _, chunks, rows, t = vt.shape
    n = FLASH_HEADS_PER_STEP
    dv = rows
    return pl.pallas_call(
        _flash_kernel,
        grid=(b, HEADS // n),
        in_specs=[pl.BlockSpec((1, s, n * LANES), lambda bi, hp: (bi, 0, hp)),
                  pl.BlockSpec((1, s, n * LANES), lambda bi, hp: (bi, 0, hp)),
                  pl.BlockSpec((1, n, chunks, rows, t), lambda bi, hp: (bi, hp, 0, 0, 0))],
        out_specs=pl.BlockSpec((1, s, n * dv), lambda bi, hp: (bi, 0, hp)),
        out_shape=jax.ShapeDtypeStruct((b, s, HEADS * dv), BF16),
        compiler_params=_params("arbitrary", "arbitrary"),
        name="flash_attention",
    )(q, k, vt)


def _diff_kernel(q_ref, k_ref, vt_ref, lq1_ref, lk1_ref, lq2_ref, lk2_ref, sub_ref, o_ref,
                 *, lambda_init):
    n, chunks, rows, t = vt_ref.shape[1:]
    tk = DIFF_KEYS
    dv = rows

    def qk(st, qi, j, u, q0):
        kc = k_ref[0, j * t + u * tk:j * t + (u + 1) * tk, st * LANES:(st + 1) * LANES]
        q = q_ref[0, qi * t + q0:(qi + 1) * t, st * LANES:(st + 1) * LANES]
        return lax.dot_general(kc, q, _NT, preferred_element_type=F32)

    def vts(st, j, u):
        return vt_ref[0, st // 2, j, :, u * tk:(u + 1) * tk]

    lam = (jnp.exp(jnp.sum(lq1_ref[...] * lk1_ref[...], axis=1, keepdims=True))
           - jnp.exp(jnp.sum(lq2_ref[...] * lk2_ref[...], axis=1, keepdims=True)) + lambda_init)

    def finish(qi, accs):
        for hd in range(n):
            a1, a2 = accs[2 * hd], accs[2 * hd + 1]
            o = (a1[:dv] / a1[dv:dv + 1] - lam * (a2[:dv] / a2[dv:dv + 1])).T
            o_ref[0, qi * t:(qi + 1) * t, hd * dv:(hd + 1) * dv] = (
                _rms(o, sub_ref[...]) * (1.0 - lambda_init)).astype(BF16)

    _stream_attention(qk, vts, finish, chunks, t, tk, rows, 2 * n, DIFF_LAG)


def _diff_attention(q, k, vt, lq1, lk1, lq2, lk2, subln, lambda_init):
    b, s, _ = q.shape
    _, _, chunks, rows, t = vt.shape
    vec = lambda a: a.reshape(1, -1)
    n = DIFF_HEADS_PER_STEP
    return pl.pallas_call(
        functools.partial(_diff_kernel, lambda_init=lambda_init),
        grid=(b, HEADS // n),
        in_specs=[pl.BlockSpec((1, s, 2 * n * LANES), lambda bi, hd: (bi, 0, hd)),
                  pl.BlockSpec((1, s, 2 * n * LANES), lambda bi, hd: (bi, 0, hd)),
                  pl.BlockSpec((1, n, chunks, rows, t), lambda bi, hd: (bi, hd, 0, 0, 0)),
                  _const_spec((1, DIFF_HEAD_DIM)), _const_spec((1, DIFF_HEAD_DIM)),
                  _const_spec((1, DIFF_HEAD_DIM)), _const_spec((1, DIFF_HEAD_DIM)),
                  _const_spec((1, DIFF_V_DIM))],
        out_specs=pl.BlockSpec((1, s, n * LANES), lambda bi, hd: (bi, 0, hd)),
        out_shape=jax.ShapeDtypeStruct((b, s, HEADS * rows), BF16),
        compiler_params=_params("arbitrary", "arbitrary"),
        name="diff_attention",
    )(q, k, vt, vec(lq1), vec(lk1), vec(lq2), vec(lk2), vec(subln))


_N_FFN_CHUNKS = D_FF // FFN_CHUNK
_GELU_C = math.sqrt(2.0 / math.pi)


def _mix_ffn_kernel(*refs, n_in, steps_per_seq):
    a_refs, wo_refs = refs[:n_in], refs[n_in:2 * n_in]
    (gmix_ref, x_ref, gpre_ref, wup_ref, cw_ref, wdn_ref, gpost_ref, o_ref,
     h_ref, hist_ref, gva_ref, gvb_ref, *mid_refs) = refs[2 * n_in:]
    tm = h_ref.shape[0]
    nc = _N_FFN_CHUNKS
    i = pl.program_id(0)

    @pl.when(i % steps_per_seq == 0)
    def _():
        hist_ref[...] = jnp.zeros_like(hist_ref)

    def out_proj(rows):
        mix = jnp.dot(a_refs[0][rows, :], wo_refs[0][...], preferred_element_type=F32)
        for a_ref, wo_ref in zip(a_refs[1:], wo_refs[1:]):
            mix = mix + jnp.dot(a_ref[rows, :], wo_ref[...], preferred_element_type=F32)
        return mix

    def pre_norms(rows, mix):
        x1 = x_ref[rows, :] + _rms(mix, gmix_ref[...])
        o_ref[rows, :] = x1
        h_ref[...] = _rms(x1, gpre_ref[...]).astype(BF16)

    def up(c):
        return jnp.dot(h_ref[...], wup_ref[c], preferred_element_type=F32)

    def down(mid_ref):
        f = jnp.dot(mid_ref[0], wdn_ref[0], preferred_element_type=F32)
        for c in range(1, nc):
            f = f + jnp.dot(mid_ref[c], wdn_ref[c], preferred_element_type=F32)
        return f

    def gated(c, gv_ref, mid_ref):
        gv_ref[0:SUBLANES, :FFN_CHUNK] = hist_ref[c]
        gate = gv_ref[SUBLANES:, :FFN_CHUNK]
        val = gv_ref[SUBLANES:, FFN_CHUNK:]
        hist_ref[c] = gv_ref[tm:, :FFN_CHUNK]
        cw = cw_ref[c]
        pre = (gv_ref[SUBLANES - 2:SUBLANES - 2 + tm, :FFN_CHUNK] * cw[0:1, :]
               + gv_ref[SUBLANES - 1:SUBLANES - 1 + tm, :FFN_CHUNK] * cw[1:2, :]
               + gate * cw[2:3, :] + cw[3:4, :])
        pre = pre.astype(BF16)
        act = 0.5 * pre * (1.0 + jnp.tanh(_GELU_C * (pre + 0.044715 * (pre * pre * pre))))
        mid_ref[c] = act * val.astype(BF16)

    def chunks_after_first(mid_ref):
        def body(k, _):
            gated(2 * k, gva_ref, mid_ref)
            gvb_ref[SUBLANES:, :] = up(2 * k + 1)
            gated(2 * k + 1, gvb_ref, mid_ref)
            gva_ref[SUBLANES:, :] = up(2 * k + 2)
            return 0

        for k in range(nc // 2):
            body(k, 0)
        gated(nc - 1, gva_ref, mid_ref)

    assert nc % 2 == 1
    n_tiles = len(mid_refs)
    rows = [slice(s * tm, (s + 1) * tm) for s in range(n_tiles)]
    pre_norms(rows[0], out_proj(rows[0]))
    gva_ref[SUBLANES:, :] = up(0)
    for s in range(n_tiles):
        chunks_after_first(mid_refs[s])
        if s + 1 < n_tiles:
            mix_next = out_proj(rows[s + 1])
        f = down(mid_refs[s])
        if s + 1 < n_tiles:
            pre_norms(rows[s + 1], mix_next)
            gva_ref[SUBLANES:, :] = up(0)
        o_ref[rows[s], :] = o_ref[rows[s], :] + _rms(f, gpost_ref[...])


def _mix_ffn(acts, w_outs, g_mix, x, g_pre, w_up, conv_w, conv_b, w_down, g_post, seq):
    t = x.shape[0]
    tm = FFN_ROWS
    nc = _N_FFN_CHUNKS
    n_in = len(acts)
    wg = w_up[:, :D_FF].reshape(D_MODEL, nc, FFN_CHUNK)
    wv = w_up[:, D_FF:].reshape(D_MODEL, nc, FFN_CHUNK)
    wup = jnp.concatenate([wg, wv], axis=2).transpose(1, 0, 2).astype(BF16)
    wdn = w_down.reshape(nc, FFN_CHUNK, D_MODEL).astype(BF16)
    cw = jnp.concatenate([conv_w, conv_b[None, :],
                          jnp.zeros((SUBLANES - CONV_WIDTH - 1, D_FF), F32)], axis=0)
    cw = cw.reshape(SUBLANES, nc, FFN_CHUNK).transpose(1, 0, 2)
    step_rows = tm * FFN_TILES_PER_STEP
    assert seq % step_rows == 0
    x_spec = pl.BlockSpec((step_rows, D_MODEL), lambda i: (i, 0))
    vec_spec = _const_spec((1, D_MODEL))
    return pl.pallas_call(
        functools.partial(_mix_ffn_kernel, n_in=n_in, steps_per_seq=seq // step_rows),
        grid=(t // step_rows,),
        in_specs=([pl.BlockSpec((step_rows, a.shape[1]), lambda i: (i, 0)) for a in acts]
                  + [_const_spec(w.shape) for w in w_outs]
                  + [vec_spec, x_spec, vec_spec, _const_spec(wup.shape), _const_spec(cw.shape),
                     _const_spec(wdn.shape), vec_spec]),
        out_specs=x_spec,
        out_shape=jax.ShapeDtypeStruct(x.shape, F32),
        scratch_shapes=([pltpu.VMEM((tm, D_MODEL), BF16),
                         pltpu.VMEM((nc, SUBLANES, FFN_CHUNK), F32),
                         pltpu.VMEM((tm + SUBLANES, 2 * FFN_CHUNK), F32),
                         pltpu.VMEM((tm + SUBLANES, 2 * FFN_CHUNK), F32)]
                        + [pltpu.VMEM((nc, tm, FFN_CHUNK), BF16)] * FFN_TILES_PER_STEP),
        compiler_params=_params("arbitrary"),
        name="mix_ffn",
    )(*acts, *w_outs, g_mix.reshape(1, -1), x, g_pre.reshape(1, -1), wup, cw, wdn,
      g_post.reshape(1, -1))


def kernel(x, norm_mix_pre, norm_mix_post, norm_ffn_pre, norm_ffn_post, even_w_in, even_b_forget, even_q_norm, even_w_uq, even_kv_norm, even_w_ukv, even_w_out, odd_w_in, odd_lambda_q1, odd_lambda_k1, odd_lambda_q2, odd_lambda_k2, odd_subln, odd_w_out, ffn_w_up, ffn_conv_w, ffn_conv_b, ffn_w_down):
    bsz, seq, d = x.shape
    assert d == D_MODEL and seq % (ATTN_TILE * PROJ_TILES_PER_STEP) == 0
    t = bsz * seq
    rope = _rope_tables(seq)
    xf = x.reshape(t, d)
    for layer in range(DEPTH):
        i = layer // 2
        if layer % 2 == 0:
            fq, fk, fvt, mq, mk, mvt = _even_proj(
                xf, norm_mix_pre[layer], even_w_in[i], even_b_forget[i], even_q_norm[i],
                even_w_uq[i], even_kv_norm[i], even_w_ukv[i], rope, bsz, seq)
            shp = (bsz, seq, HEAD_COLS)
            fo = _flash_attention(fq.reshape(shp), fk.reshape(shp), fvt)
            mo = _flash_attention(mq.reshape(shp), mk.reshape(shp), mvt)
            w_out = even_w_out[i].astype(BF16)
            acts = [fo.reshape(t, -1), mo.reshape(t, -1)]
            w_outs = [w_out[:FOX_WIDTH], w_out[FOX_WIDTH:]]
        else:
            lambda_init = 0.8 - 0.6 * math.exp(-0.3 * layer)
            q, k, vt = _odd_proj(xf, norm_mix_pre[layer], odd_w_in[i], bsz, seq)
            o = _diff_attention(q.reshape(bsz, seq, -1), k.reshape(bsz, seq, -1), vt,
                                odd_lambda_q1[i], odd_lambda_k1[i],
                                odd_lambda_q2[i], odd_lambda_k2[i], odd_subln[i], lambda_init)
            acts, w_outs = [o.reshape(t, -1)], [odd_w_out[i].astype(BF16)]
        xf = _mix_ffn(acts, w_outs, norm_mix_post[layer], xf, norm_ffn_pre[layer], ffn_w_up[layer],
                      ffn_conv_w[layer], ffn_conv_b[layer], ffn_w_down[layer],
                      norm_ffn_post[layer], seq)
    return xf.reshape(bsz, seq, d)
```

```python
import functools
import math

import numpy as np
import jax
import jax.numpy as jnp
from jax import lax
from jax.experimental import pallas as pl
from jax.experimental.pallas import tpu as pltpu

D_MODEL = 1024
DEPTH = 4
RMS_EPS = 1e-6
NEG_INF = -1e30
HEADS = 8
FOX_HEAD_DIM = 64
FOX_WIDTH = HEADS * FOX_HEAD_DIM
MLA_NOPE_DIM = 64
MLA_ROPE_DIM = 32
MLA_V_DIM = 64
MLA_Q_RANK = 384
MLA_KV_RANK = 256
ROPE_THETA = 10000.0
DIFF_HEAD_DIM = 64
DIFF_V_DIM = 2 * DIFF_HEAD_DIM
D_FF = 2816
CONV_WIDTH = 3

LANES = 128
SUBLANES = 8
BF16_ROWS = 16
LOG2E = 1.4426950408889634
HEAD_COLS = HEADS * LANES
VMEM_LIMIT_BYTES = 56 * 1024 * 1024

PROJ_TILES_PER_STEP = 2
FFN_ROWS = 512
FFN_TILES_PER_STEP = 2
FFN_CHUNK = 256
ATTN_TILE = 512
FLASH_HEADS_PER_STEP, FLASH_KEYS, FLASH_LAG = 2, 256, 2
DIFF_HEADS_PER_STEP, DIFF_KEYS, DIFF_LAG = 1, 256, 2
F32 = jnp.float32
BF16 = jnp.bfloat16


def _params(*sem):
    return pltpu.CompilerParams(dimension_semantics=sem, vmem_limit_bytes=VMEM_LIMIT_BYTES)


def _const_spec(shape):
    nd = len(shape)
    return pl.BlockSpec(shape, lambda *_: (0,) * nd, pipeline_mode=pl.Buffered(1))


def _rms(x, g):
    return x * lax.rsqrt(jnp.mean(x * x, axis=-1, keepdims=True) + RMS_EPS) * g


def _split3(x):
    hi = x.astype(BF16)
    r1 = x - hi.astype(F32)
    mid = r1.astype(BF16)
    lo = (r1 - mid.astype(F32)).astype(BF16)
    return hi, mid, lo


def _rope(blk, ra, rb, rc):
    return blk * ra + pltpu.roll(blk, 16, 1) * rb + pltpu.roll(blk, LANES - 16, 1) * rc


def _vt_rows(dv):
    return dv + BF16_ROWS


def _store_vt(vt_ref, v, dv, chunk):
    tk = v.shape[0]
    per = LANES // dv
    for blk in range(v.shape[1] // LANES):
        vt = v[:, blk * LANES:(blk + 1) * LANES].T.astype(BF16)
        for s in range(per):
            vt_ref[0, blk * per + s, chunk, 0:dv, :] = vt[s * dv:(s + 1) * dv, :]
    row = lax.broadcasted_iota(jnp.int32, (BF16_ROWS, tk), 0)
    ones_row = jnp.where(row == 0, 1.0, 0.0).astype(BF16)
    for hd in range(HEADS):
        vt_ref[0, hd, chunk, dv:dv + BF16_ROWS, :] = ones_row


def _vt_spec(dv, chunks_per_step, tk, steps_per_seq):
    return pl.BlockSpec((1, HEADS, chunks_per_step, _vt_rows(dv), tk),
                        lambda i: (i // steps_per_seq, 0, i % steps_per_seq, 0, 0))


_EV_Q, _EV_K, _EV_V = 0, FOX_WIDTH, 2 * FOX_WIDTH
_EV_G = _EV_V + FOX_WIDTH
_EV_CQ = _EV_G + LANES
_EV_CKV = _EV_CQ + MLA_Q_RANK
_EV_KR = _EV_CKV + MLA_KV_RANK
_EV_WIDTH = _EV_KR + LANES


def _even_proj_kernel(x_ref, g_ref, w_ref, bf_ref, qn_ref, wuq_ref, kvn_ref, wuk_ref, wuv_ref,
                      ec_ref, ra_ref, rb_ref, rc_ref, cst_ref,
                      fq_ref, fk_ref, fv_ref, mq_ref, mk_ref, mv_ref, carry_ref,
                      *, steps_per_seq):
    tm = fv_ref.shape[4]
    n_tiles = x_ref.shape[0] // tm
    i = pl.program_id(0)

    @pl.when(i % steps_per_seq == 0)
    def _():
        carry_ref[...] = jnp.zeros_like(carry_ref)

    hs = [_rms(x_ref[s * tm:(s + 1) * tm, :], g_ref[...]).astype(BF16) for s in range(n_tiles)]
    for s, h in enumerate(hs):
        _even_proj_tile(h, s, slice(s * tm, (s + 1) * tm), w_ref, bf_ref, qn_ref, wuq_ref, kvn_ref,
                        wuk_ref, wuv_ref, ec_ref, ra_ref, rb_ref, rc_ref, cst_ref,
                        fq_ref, fk_ref, fv_ref, mq_ref, mk_ref, mv_ref, carry_ref)


def _even_proj_tile(h, s, rows, w_ref, bf_ref, qn_ref, wuq_ref, kvn_ref, wuk_ref, wuv_ref,
                    ec_ref, ra_ref, rb_ref, rc_ref, cst_ref,
                    fq_ref, fk_ref, fv_ref, mq_ref, mk_ref, mv_ref, carry_ref):
    tm = h.shape[0]
    q_aug = cst_ref[0:1, :]

    def proj(lo, hi):
        return jnp.dot(h, w_ref[:, lo:hi], preferred_element_type=F32)

    lane = lax.broadcasted_iota(jnp.int32, (tm, LANES), 1)

    def spread_heads(o_ref, packed, aug):
        for pair in range(HEADS // 2):
            blk = packed[:, pair * LANES:(pair + 1) * LANES]
            for odd, src in enumerate((blk, pltpu.roll(blk, FOX_HEAD_DIM, 1))):
                sl = slice((2 * pair + odd) * LANES, (2 * pair + odd + 1) * LANES)
                o_ref[rows, sl] = jnp.where(lane < FOX_HEAD_DIM, src, aug[:, sl]).astype(BF16)

    ra, rb, rc = ra_ref[rows, :], rb_ref[rows, :], rc_ref[rows, :]
    fox_scale = FOX_HEAD_DIM ** -0.5 * LOG2E
    mla_scale = (MLA_NOPE_DIM + MLA_ROPE_DIM) ** -0.5 * LOG2E

    z = proj(_EV_G, _EV_CQ) + bf_ref[...]
    cq_raw = proj(_EV_CQ, _EV_CKV)
    ckv_raw = proj(_EV_CKV, _EV_KR)
    kr_raw = proj(_EV_KR, _EV_WIDTH)

    logf = (jnp.minimum(z, 0.0) - jnp.log1p(jnp.exp(-jnp.abs(z)))) * LOG2E
    spread_heads(fq_ref, proj(_EV_Q, _EV_K) * fox_scale, q_aug)

    row = lax.broadcasted_iota(jnp.int32, (tm, tm), 0)
    col = lax.broadcasted_iota(jnp.int32, (tm, tm), 1)
    tril = jnp.where(row >= col, 1.0, 0.0).astype(BF16)
    parts = jnp.dot(tril, jnp.concatenate(_split3(logf), axis=1), preferred_element_type=F32)
    _store_vt(fv_ref, proj(_EV_V, _EV_G), FOX_HEAD_DIM, s)
    c = (parts[:, :LANES] + parts[:, LANES:2 * LANES] + parts[:, 2 * LANES:]
         + carry_ref[SUBLANES - 1:SUBLANES, :])
    carry_ref[...] = c[tm - SUBLANES:, :]

    cq = _rms(cq_raw, qn_ref[...]).astype(BF16)
    q = jnp.dot(cq, wuq_ref[...], preferred_element_type=F32) * mla_scale
    for hd in range(HEADS):
        sl = slice(hd * LANES, (hd + 1) * LANES)
        mq_ref[rows, sl] = _rope(q[:, sl], ra, rb, rc).astype(BF16)

    ckv = _rms(ckv_raw, kvn_ref[...]).astype(BF16)
    kr = _rope(kr_raw, ra, rb, rc)
    kn = jnp.dot(ckv, wuk_ref[...], preferred_element_type=F32)
    for hd in range(HEADS):
        sl = slice(hd * LANES, (hd + 1) * LANES)
        mk_ref[rows, sl] = (kn[:, sl] + kr).astype(BF16)
    _store_vt(mv_ref, jnp.dot(ckv, wuv_ref[...], preferred_element_type=F32), MLA_V_DIM, s)

    c_aug = jnp.dot(jnp.concatenate(_split3(c), axis=1), ec_ref[...], preferred_element_type=F32)
    spread_heads(fk_ref, proj(_EV_K, _EV_V), c_aug)


def _pad_heads(w, dh):
    k = w.shape[0]
    w = w.reshape(k, HEADS, dh)
    return jnp.pad(w, ((0, 0), (0, 0), (0, LANES - dh))).reshape(k, HEAD_COLS)


def _even_consts():
    cst = np.zeros((SUBLANES, HEAD_COLS), np.float32)
    ec = np.zeros((3 * LANES, HEAD_COLS), np.float32)
    for hd in range(HEADS):
        cst[0, hd * LANES + 64: hd * LANES + 67] = 1.0
        for p in range(3):
            ec[p * LANES + hd, hd * LANES + 64 + p] = -1.0
    return jnp.asarray(cst), jnp.asarray(ec, dtype=BF16)


def _rope_tables(seq):
    half = MLA_ROPE_DIM // 2
    inv = ROPE_THETA ** (-jnp.arange(0, MLA_ROPE_DIM, 2, dtype=F32) / MLA_ROPE_DIM)
    ang = jnp.arange(seq, dtype=F32)[:, None] * inv[None, :]
    cos, sin = jnp.cos(ang), jnp.sin(ang)
    zeros = jnp.zeros((seq, half), F32)
    ra = jnp.concatenate([jnp.ones((seq, 64), F32), cos, cos, jnp.zeros((seq, 32), F32)], axis=1)
    rb = jnp.concatenate([jnp.zeros((seq, 64), F32), zeros, sin, jnp.zeros((seq, 32), F32)], axis=1)
    rc = jnp.concatenate([jnp.zeros((seq, 64), F32), -sin, zeros, jnp.zeros((seq, 32), F32)], axis=1)
    return ra, rb, rc


def _even_proj(x, g, w_in, b_forget, q_norm, w_uq, kv_norm, w_ukv, rope, bsz, seq):
    t = x.shape[0]
    tk = ATTN_TILE
    tm = tk * PROJ_TILES_PER_STEP
    cuts = np.cumsum([FOX_WIDTH, FOX_WIDTH, FOX_WIDTH, HEADS, MLA_Q_RANK, MLA_KV_RANK]).tolist()
    wfq, wfk, wfv, wg, wcq, wckv, wkr = jnp.split(w_in, cuts, axis=1)
    w1 = jnp.concatenate([
        wfq, wfk, wfv,
        jnp.pad(wg, ((0, 0), (0, LANES - HEADS))), wcq, wckv,
        jnp.pad(wkr, ((0, 0), (64, LANES - 64 - MLA_ROPE_DIM)))], axis=1).astype(BF16)
    assert w1.shape[1] == _EV_WIDTH
    bf = jnp.pad(b_forget, (0, LANES - HEADS)).reshape(1, LANES)
    wuq = _pad_heads(w_uq, MLA_NOPE_DIM + MLA_ROPE_DIM).astype(BF16)
    wukv = w_ukv.reshape(MLA_KV_RANK, HEADS, MLA_NOPE_DIM + MLA_V_DIM)
    wuk = _pad_heads(wukv[:, :, :MLA_NOPE_DIM].reshape(MLA_KV_RANK, -1), MLA_NOPE_DIM).astype(BF16)
    wuv = wukv[:, :, MLA_NOPE_DIM:].reshape(MLA_KV_RANK, -1).astype(BF16)
    cst, ec = _even_consts()
    ra, rb, rc = rope
    steps_per_seq = seq // tm
    rope_spec = pl.BlockSpec((tm, LANES), lambda i: (i % steps_per_seq, 0))
    row_spec = pl.BlockSpec((tm, HEAD_COLS), lambda i: (i, 0))
    out_sds = jax.ShapeDtypeStruct((t, HEAD_COLS), BF16)
    vt_spec = _vt_spec(FOX_HEAD_DIM, PROJ_TILES_PER_STEP, tk, steps_per_seq)
    vt_sds = jax.ShapeDtypeStruct((bsz, HEADS, seq // tk, _vt_rows(FOX_HEAD_DIM), tk), BF16)
    return pl.pallas_call(
        functools.partial(_even_proj_kernel, steps_per_seq=steps_per_seq),
        grid=(t // tm,),
        in_specs=[pl.BlockSpec((tm, D_MODEL), lambda i: (i, 0)),
                  _const_spec((1, D_MODEL)), _const_spec(w1.shape), _const_spec((1, LANES)),
                  _const_spec((1, MLA_Q_RANK)), _const_spec(wuq.shape),
                  _const_spec((1, MLA_KV_RANK)), _const_spec(wuk.shape), _const_spec(wuv.shape),
                  _const_spec(ec.shape), rope_spec, rope_spec, rope_spec, _const_spec(cst.shape)],
        out_specs=[row_spec, row_spec, vt_spec, row_spec, row_spec, vt_spec],
        out_shape=[out_sds, out_sds, vt_sds, out_sds, out_sds, vt_sds],
        scratch_shapes=[pltpu.VMEM((SUBLANES, LANES), F32)],
        compiler_params=_params("arbitrary"),
        name="even_proj",
    )(x, g.reshape(1, -1), w1, bf, q_norm.reshape(1, -1), wuq, kv_norm.reshape(1, -1), wuk, wuv,
      ec, ra, rb, rc, cst)


def _alibi_slope(hd):
    return 2.0 ** (-8.0 * (hd + 1) / HEADS)


def _odd_proj_kernel(x_ref, g_ref, w_ref, q_ref, k_ref, v_ref, *, steps_per_seq):
    tm = v_ref.shape[4]
    n_tiles = x_ref.shape[0] // tm
    i = pl.program_id(0)
    scale = DIFF_HEAD_DIM ** -0.5 * LOG2E
    d = DIFF_HEAD_DIM
    lane = lax.broadcasted_iota(jnp.int32, (tm, LANES), 1)
    q_aug = jnp.where((lane >= d) & (lane < d + 3), 1.0, 0.0)
    hs = [_rms(x_ref[s * tm:(s + 1) * tm, :], g_ref[...]).astype(BF16) for s in range(n_tiles)]
    for s, h in enumerate(hs):
        rows = slice(s * tm, (s + 1) * tm)
        v = jnp.dot(h, w_ref[:, 2 * HEAD_COLS:], preferred_element_type=F32)
        k = jnp.dot(h, w_ref[:, HEAD_COLS:2 * HEAD_COLS], preferred_element_type=F32)
        _store_vt(v_ref, v, DIFF_V_DIM, s)
        q = jnp.dot(h, w_ref[:, :HEAD_COLS], preferred_element_type=F32)
        pos = (((i % steps_per_seq) * n_tiles + s) * tm
               + lax.broadcasted_iota(jnp.int32, (tm, 1), 0)).astype(F32)
        for hd in range(HEADS):
            sl = slice(hd * LANES, (hd + 1) * LANES)
            b_hi, b_mid, b_lo = (p.astype(F32)
                                 for p in _split3(pos * (_alibi_slope(hd) * LOG2E)))
            k_aug = jnp.where(lane == d, b_hi,
                              jnp.where(lane == d + 1, b_mid, jnp.where(lane == d + 2, b_lo, 0.0)))
            qb, kb = q[:, sl] * scale, k[:, sl]
            for br, (qs, ks) in enumerate(((qb, kb),
                                           (pltpu.roll(qb, d, 1), pltpu.roll(kb, d, 1)))):
                so = slice((2 * hd + br) * LANES, (2 * hd + br + 1) * LANES)
                q_ref[rows, so] = jnp.where(lane < d, qs, q_aug).astype(BF16)
                k_ref[rows, so] = jnp.where(lane < d, ks, k_aug).astype(BF16)


def _odd_proj(x, g, w_in, bsz, seq):
    t = x.shape[0]
    tk = ATTN_TILE
    tm = tk * PROJ_TILES_PER_STEP
    w = w_in.astype(BF16)
    steps_per_seq = seq // tm
    return pl.pallas_call(
        functools.partial(_odd_proj_kernel, steps_per_seq=steps_per_seq),
        grid=(t // tm,),
        in_specs=[pl.BlockSpec((tm, D_MODEL), lambda i: (i, 0)),
                  _const_spec((1, D_MODEL)), _const_spec(w.shape)],
        out_specs=[pl.BlockSpec((tm, 2 * HEAD_COLS), lambda i: (i, 0)),
                   pl.BlockSpec((tm, 2 * HEAD_COLS), lambda i: (i, 0)),
                   _vt_spec(DIFF_V_DIM, PROJ_TILES_PER_STEP, tk, steps_per_seq)],
        out_shape=[jax.ShapeDtypeStruct((t, 2 * HEAD_COLS), BF16),
                   jax.ShapeDtypeStruct((t, 2 * HEAD_COLS), BF16),
                   jax.ShapeDtypeStruct((bsz, HEADS, seq // tk, _vt_rows(DIFF_V_DIM), tk), BF16)],
        compiler_params=_params("arbitrary"),
        name="odd_proj",
    )(x, g.reshape(1, -1), w)


_NT = (((1,), (1,)), ((), ()))


def _attend(s, vt, m, acc, causal):
    if not causal:
        m_new = jnp.maximum(m, jnp.max(s, axis=0, keepdims=True))
        p = jnp.exp2(s - m_new).astype(BF16)
    else:
        nr, nc = s.shape[0] // LANES, s.shape[1] // LANES
        key = lax.broadcasted_iota(jnp.int32, (LANES, LANES), 0)
        qry = lax.broadcasted_iota(jnp.int32, (LANES, LANES), 1)
        sq = [[s[a * LANES:(a + 1) * LANES, b * LANES:(b + 1) * LANES] for b in range(nc)]
              for a in range(nr)]
        for a in range(nr):
            sq[a][a] = jnp.where(key <= qry, sq[a][a], NEG_INF)
        col_max = [functools.reduce(jnp.maximum, [jnp.max(sq[a][b], axis=0, keepdims=True)
                                                  for a in range(min(b + 1, nr))])
                   for b in range(nc)]
        m_new = jnp.maximum(m, jnp.concatenate(col_max, axis=1))
        p = jnp.concatenate(
            [jnp.concatenate(
                [jnp.exp2(sq[a][b] - m_new[:, b * LANES:(b + 1) * LANES]).astype(BF16)
                 if a <= b else jnp.zeros((LANES, LANES), BF16) for b in range(nc)], axis=1)
             for a in range(nr)], axis=0)
    acc = jnp.exp2(m - m_new) * acc + jnp.dot(vt, p, preferred_element_type=F32)
    return m_new, acc


def _stream_attention(qk, vts, finish, q_tiles, t, tk, rows, n, lag):
    r = t // tk
    items = [(qi, j, u, k) for qi in range(q_tiles) for j in range(qi + 1) for u in range(r)
             for k in range(n)]

    def first_query(qi, j, u):
        return u * tk if j == qi else 0

    pending, issued = [], 0
    state = None
    for i, (qi, j, u, k) in enumerate(items):
        while issued < min(i + lag + 1, len(items)):
            nqi, nj, nu, nk = items[issued]
            pending.append(qk(nk, nqi, nj, nu, first_query(nqi, nj, nu)))
            issued += 1
        if j == 0 and u == 0 and k == 0:
            state = [(jnp.full((1, t), NEG_INF, F32), jnp.zeros((rows, t), F32))] * n
        q0 = first_query(qi, j, u)
        m, acc = state[k]
        m_hi, acc_hi = _attend(pending.pop(0), vts(k, j, u), m[:, q0:], acc[:, q0:], j == qi)
        state[k] = ((jnp.concatenate([m[:, :q0], m_hi], axis=1),
                     jnp.concatenate([acc[:, :q0], acc_hi], axis=1)) if q0 else (m_hi, acc_hi))
        if j == qi and u == r - 1 and k == n - 1:
            finish(qi, [acc for _, acc in state])


def _flash_kernel(q_ref, k_ref, vt_ref, o_ref):
    n, chunks, rows, t = vt_ref.shape[1:]
    tk = FLASH_KEYS
    dv = rows - BF16_ROWS

    def qk(hd, qi, j, u, q0):
        kc = k_ref[0, j * t + u * tk:j * t + (u + 1) * tk, hd * LANES:(hd + 1) * LANES]
        q = q_ref[0, qi * t + q0:(qi + 1) * t, hd * LANES:(hd + 1) * LANES]
        return lax.dot_general(kc, q, _NT, preferred_element_type=F32)

    def vts(hd, j, u):
        return vt_ref[0, hd, j, :, u * tk:(u + 1) * tk]

    def finish(qi, accs):
        o_t = jnp.concatenate([a[:dv] / a[dv:dv + 1] for a in accs], axis=0)
        o_ref[0, qi * t:(qi + 1) * t, :] = o_t.T.astype(BF16)

    _stream_attention(qk, vts, finish, chunks, t, tk, rows, n, FLASH_LAG)


def _flash_attention(q, k, vt):
    b, s, _ = q.shape
    _, _, chunks, rows, t = vt.shape
    n = FLASH_HEADS_PER_STEP
    dv = rows - BF16_ROWS
    return pl.pallas_call(
        _flash_kernel,
        grid=(b, HEADS // n),
        in_specs=[pl.BlockSpec((1, s, n * LANES), lambda bi, hp: (bi, 0, hp)),
                  pl.BlockSpec((1, s, n * LANES), lambda bi, hp: (bi, 0, hp)),
                  pl.BlockSpec((1, n, chunks, rows, t), lambda bi, hp: (bi, hp, 0, 0, 0))],
        out_specs=pl.BlockSpec((1, s, n * dv), lambda bi, hp: (bi, 0, hp)),
        out_shape=jax.ShapeDtypeStruct((b, s, HEADS * dv), BF16),
        compiler_params=_params("arbitrary", "arbitrary"),
        name="flash_attention",
    )(q, k, vt)


def _diff_kernel(q_ref, k_ref, vt_ref, lq1_ref, lk1_ref, lq2_ref, lk2_ref, sub_ref, o_ref,
                 *, lambda_init):
    n, chunks, rows, t = vt_ref.shape[1:]
    tk = DIFF_KEYS
    dv = rows - BF16_ROWS

    def qk(st, qi, j, u, q0):
        kc = k_ref[0, j * t + u * tk:j * t + (u + 1) * tk, st * LANES:(st + 1) * LANES]
        q = q_ref[0, qi * t + q0:(qi + 1) * t, st * LANES:(st + 1) * LANES]
        return lax.dot_general(kc, q, _NT, preferred_element_type=F32)

    def vts(st, j, u):
        return vt_ref[0, st // 2, j, :, u * tk:(u + 1) * tk]

    lam = (jnp.exp(jnp.sum(lq1_ref[...] * lk1_ref[...], axis=1, keepdims=True))
           - jnp.exp(jnp.sum(lq2_ref[...] * lk2_ref[...], axis=1, keepdims=True)) + lambda_init)

    def finish(qi, accs):
        for hd in range(n):
            a1, a2 = accs[2 * hd], accs[2 * hd + 1]
            o = (a1[:dv] / a1[dv:dv + 1] - lam * (a2[:dv] / a2[dv:dv + 1])).T
            o_ref[0, qi * t:(qi + 1) * t, hd * dv:(hd + 1) * dv] = (
                _rms(o, sub_ref[...]) * (1.0 - lambda_init)).astype(BF16)

    _stream_attention(qk, vts, finish, chunks, t, tk, rows, 2 * n, DIFF_LAG)


def _diff_attention(q, k, vt, lq1, lk1, lq2, lk2, subln, lambda_init):
    b, s, _ = q.shape
    _, _, chunks, rows, t = vt.shape
    vec = lambda a: a.reshape(1, -1)
    n = DIFF_HEADS_PER_STEP
    return pl.pallas_call(
        functools.partial(_diff_kernel, lambda_init=lambda_init),
        grid=(b, HEADS // n),
        in_specs=[pl.BlockSpec((1, s, 2 * n * LANES), lambda bi, hd: (bi, 0, hd)),
                  pl.BlockSpec((1, s, 2 * n * LANES), lambda bi, hd: (bi, 0, hd)),
                  pl.BlockSpec((1, n, chunks, rows, t), lambda bi, hd: (bi, hd, 0, 0, 0)),
                  _const_spec((1, DIFF_HEAD_DIM)), _const_spec((1, DIFF_HEAD_DIM)),
                  _const_spec((1, DIFF_HEAD_DIM)), _const_spec((1, DIFF_HEAD_DIM)),
                  _const_spec((1, DIFF_V_DIM))],
        out_specs=pl.BlockSpec((1, s, n * LANES), lambda bi, hd: (bi, 0, hd)),
        out_shape=jax.ShapeDtypeStruct((b, s, HEADS * (rows - BF16_ROWS)), BF16),
        compiler_params=_params("arbitrary", "arbitrary"),
        name="diff_attention",
    )(q, k, vt, vec(lq1), vec(lk1), vec(lq2), vec(lk2), vec(subln))


_N_FFN_CHUNKS = D_FF // FFN_CHUNK
_GELU_C = math.sqrt(2.0 / math.pi)


def _mix_ffn_kernel(*refs, n_in, steps_per_seq):
    a_refs, wo_refs = refs[:n_in], refs[n_in:2 * n_in]
    (gmix_ref, x_ref, gpre_ref, wup_ref, cw_ref, wdn_ref, gpost_ref, o_ref,
     h_ref, hist_ref, gva_ref, gvb_ref, *mid_refs) = refs[2 * n_in:]
    tm = h_ref.shape[0]
    nc = _N_FFN_CHUNKS
    i = pl.program_id(0)

    @pl.when(i % steps_per_seq == 0)
    def _():
        hist_ref[...] = jnp.zeros_like(hist_ref)

    def out_proj(rows):
        mix = jnp.dot(a_refs[0][rows, :], wo_refs[0][...], preferred_element_type=F32)
        for a_ref, wo_ref in zip(a_refs[1:], wo_refs[1:]):
            mix = mix + jnp.dot(a_ref[rows, :], wo_ref[...], preferred_element_type=F32)
        return mix

    def pre_norms(rows, mix):
        x1 = x_ref[rows, :] + _rms(mix, gmix_ref[...])
        o_ref[rows, :] = x1
        h_ref[...] = _rms(x1, gpre_ref[...]).astype(BF16)

    def up(c):
        return jnp.dot(h_ref[...], wup_ref[c], preferred_element_type=F32)

    def down(mid_ref):
        f = jnp.dot(mid_ref[0], wdn_ref[0], preferred_element_type=F32)
        for c in range(1, nc):
            f = f + jnp.dot(mid_ref[c], wdn_ref[c], preferred_element_type=F32)
        return f

    def gated(c, gv_ref, mid_ref):
        gv_ref[0:SUBLANES, :FFN_CHUNK] = hist_ref[c]
        hist_ref[c] = gv_ref[tm:, :FFN_CHUNK]
        cw = cw_ref[c]
        half = tm // 2
        for r0 in (0, half):
            lo = SUBLANES + r0
            gate = gv_ref[lo:lo + half, :FFN_CHUNK]
            val = gv_ref[lo:lo + half, FFN_CHUNK:]
            pre = (gv_ref[lo - 2:lo - 2 + half, :FFN_CHUNK] * cw[0:1, :]
                   + gv_ref[lo - 1:lo - 1 + half, :FFN_CHUNK] * cw[1:2, :]
                   + gate * cw[2:3, :] + cw[3:4, :])
            pre = pre.astype(BF16)
            act = 0.5 * pre * (1.0 + jnp.tanh(_GELU_C * (pre + 0.044715 * (pre * pre * pre))))
            mid_ref[c, r0:r0 + half, :] = act * val.astype(BF16)

    def chunks_after_first(mid_ref):
        def body(k, _):
            gated(2 * k, gva_ref, mid_ref)
            gvb_ref[SUBLANES:, :] = up(2 * k + 1)
            gated(2 * k + 1, gvb_ref, mid_ref)
            gva_ref[SUBLANES:, :] = up(2 * k + 2)
            return 0

        for k in range(nc // 2):
            body(k, 0)
        gated(nc - 1, gva_ref, mid_ref)

    assert nc % 2 == 1
    n_tiles = len(mid_refs)
    rows = [slice(s * tm, (s + 1) * tm) for s in range(n_tiles)]
    pre_norms(rows[0], out_proj(rows[0]))
    gva_ref[SUBLANES:, :] = up(0)
    for s in range(n_tiles):
        chunks_after_first(mid_refs[s])
        if s + 1 < n_tiles:
            mix_next = out_proj(rows[s + 1])
        f = down(mid_refs[s])
        if s + 1 < n_tiles:
            pre_norms(rows[s + 1], mix_next)
            gva_ref[SUBLANES:, :] = up(0)
        o_ref[rows[s], :] = o_ref[rows[s], :] + _rms(f, gpost_ref[...])


def _mix_ffn(acts, w_outs, g_mix, x, g_pre, w_up, conv_w, conv_b, w_down, g_post, seq):
    t = x.shape[0]
    tm = FFN_ROWS
    nc = _N_FFN_CHUNKS
    n_in = len(acts)
    wg = w_up[:, :D_FF].reshape(D_MODEL, nc, FFN_CHUNK)
    wv = w_up[:, D_FF:].reshape(D_MODEL, nc, FFN_CHUNK)
    wup = jnp.concatenate([wg, wv], axis=2).transpose(1, 0, 2).astype(BF16)
    wdn = w_down.reshape(nc, FFN_CHUNK, D_MODEL).astype(BF16)
    cw = jnp.concatenate([conv_w, conv_b[None, :],
                          jnp.zeros((SUBLANES - CONV_WIDTH - 1, D_FF), F32)], axis=0)
    cw = cw.reshape(SUBLANES, nc, FFN_CHUNK).transpose(1, 0, 2)
    step_rows = tm * FFN_TILES_PER_STEP
    assert seq % step_rows == 0
    x_spec = pl.BlockSpec((step_rows, D_MODEL), lambda i: (i, 0))
    vec_spec = _const_spec((1, D_MODEL))
    return pl.pallas_call(
        functools.partial(_mix_ffn_kernel, n_in=n_in, steps_per_seq=seq // step_rows),
        grid=(t // step_rows,),
        in_specs=([pl.BlockSpec((step_rows, a.shape[1]), lambda i: (i, 0)) for a in acts]
                  + [_const_spec(w.shape) for w in w_outs]
                  + [vec_spec, x_spec, vec_spec, _const_spec(wup.shape), _const_spec(cw.shape),
                     _const_spec(wdn.shape), vec_spec]),
        out_specs=x_spec,
        out_shape=jax.ShapeDtypeStruct(x.shape, F32),
        scratch_shapes=([pltpu.VMEM((tm, D_MODEL), BF16),
                         pltpu.VMEM((nc, SUBLANES, FFN_CHUNK), F32),
                         pltpu.VMEM((tm + SUBLANES, 2 * FFN_CHUNK), F32),
                         pltpu.VMEM((tm + SUBLANES, 2 * FFN_CHUNK), F32)]
                        + [pltpu.VMEM((nc, tm, FFN_CHUNK), BF16)] * FFN_TILES_PER_STEP),
        compiler_params=_params("arbitrary"),
        name="mix_ffn",
    )(*acts, *w_outs, g_mix.reshape(1, -1), x, g_pre.reshape(1, -1), wup, cw, wdn,
      g_post.reshape(1, -1))


def kernel(x, norm_mix_pre, norm_mix_post, norm_ffn_pre, norm_ffn_post, even_w_in, even_b_forget, even_q_norm, even_w_uq, even_kv_norm, even_w_ukv, even_w_out, odd_w_in, odd_lambda_q1, odd_lambda_k1, odd_lambda_q2, odd_lambda_k2, odd_subln, odd_w_out, ffn_w_up, ffn_conv_w, ffn_conv_b, ffn_w_down):
    bsz, seq, d = x.shape
    assert d == D_MODEL and seq % (ATTN_TILE * PROJ_TILES_PER_STEP) == 0
    t = bsz * seq
    rope = _rope_tables(seq)
    xf = x.reshape(t, d)
    for layer in range(DEPTH):
        i = layer // 2
        if layer % 2 == 0:
            fq, fk, fvt, mq, mk, mvt = _even_proj(
                xf, norm_mix_pre[layer], even_w_in[i], even_b_forget[i], even_q_norm[i],
                even_w_uq[i], even_kv_norm[i], even_w_ukv[i], rope, bsz, seq)
            shp = (bsz, seq, HEAD_COLS)
            fo = _flash_attention(fq.reshape(shp), fk.reshape(shp), fvt)
            mo = _flash_attention(mq.reshape(shp), mk.reshape(shp), mvt)
            w_out = even_w_out[i].astype(BF16)
            acts = [fo.reshape(t, -1), mo.reshape(t, -1)]
            w_outs = [w_out[:FOX_WIDTH], w_out[FOX_WIDTH:]]
        else:
            lambda_init = 0.8 - 0.6 * math.exp(-0.3 * layer)
            q, k, vt = _odd_proj(xf, norm_mix_pre[layer], odd_w_in[i], bsz, seq)
            o = _diff_attention(q.reshape(bsz, seq, -1), k.reshape(bsz, seq, -1), vt,
                                odd_lambda_q1[i], odd_lambda_k1[i],
                                odd_lambda_q2[i], odd_lambda_k2[i], odd_subln[i], lambda_init)
            acts, w_outs = [o.reshape(t, -1)], [odd_w_out[i].astype(BF16)]
        xf = _mix_ffn(acts, w_outs, norm_mix_post[layer], xf, norm_ffn_pre[layer], ffn_w_up[layer],
                      ffn_conv_w[layer], ffn_conv_b[layer], ffn_w_down[layer],
                      norm_ffn_post[layer], seq)
    return xf.reshape(bsz, seq, d)
```
